```python
import jax
import jax.numpy as jnp
from jax import lax
import numpy as np


D_MODEL = 1024
BATCH = 1
SEQ = 16384
DEPTH = 4

GRID_W = 64
CTX_LEN = 256
MIX_WIDTH = D_MODEL
MLSTM_WIDTH = MIX_WIDTH // 4
MLSTM_DH = 64
MLSTM_HEADS = MLSTM_WIDTH // MLSTM_DH
MLSTM_GATE_COLS = 2 * 2 * MLSTM_HEADS
HEAD_DIM = 64
ATTN_Q_WIDTH = MIX_WIDTH // 2
ATTN_HEADS = ATTN_Q_WIDTH // HEAD_DIM
ATTN_KV_HEADS = 2
ATTN_GROUP = ATTN_HEADS // ATTN_KV_HEADS
ATTN_KV_WIDTH = ATTN_KV_HEADS * HEAD_DIM
WINDOW = 128
WIN_BLOCK = 128
ROPE_BASE = 10000.0
ROPE_PAIRS = HEAD_DIM // 4
GLA_WIDTH = MIX_WIDTH // 4
GLA_DV = 64
GLA_HEADS = GLA_WIDTH // GLA_DV
GLA_DK = GLA_DV // 2
GLA_KEY_WIDTH = GLA_HEADS * GLA_DK
GLA_RANK = 16
GLA_TAU = 16.0
GLA_GATE_COLS = 2 * GLA_RANK
SCAN_CHUNK = 64
COL_SIZES = (MLSTM_WIDTH, MLSTM_WIDTH, MLSTM_WIDTH, MLSTM_WIDTH, MLSTM_GATE_COLS,
             ATTN_Q_WIDTH, ATTN_KV_WIDTH, ATTN_KV_WIDTH,
             GLA_KEY_WIDTH, GLA_KEY_WIDTH, GLA_WIDTH, GLA_WIDTH, GLA_GATE_COLS)
COL_OFFSETS = tuple(int(o) for o in np.cumsum(COL_SIZES)[:-1])
IN_COLS = sum(COL_SIZES)
D_FF = 2816
N_EXPERTS = 8
TOP_K = 2
D_EXPERT = 3584
MOE_BLOCK = 128
N_DENSE = (DEPTH + 1) // 2
N_MOE = DEPTH // 2
DEEPNORM_ALPHA = (2 * DEPTH) ** 0.25
DEEPNORM_BETA = (8 * DEPTH) ** -0.25
ADA_INIT = 0.5
LN_EPS = 1e-5

kernel_name = 'hymba_mlstm_swa_gla_moe_dit'


def layer_norm(x, w, b):
    xf = x.astype(jnp.float32)
    mu = jnp.mean(xf, -1, keepdims=True)
    var = jnp.mean(jnp.square(xf - mu), -1, keepdims=True)
    return ((xf - mu) * lax.rsqrt(var + LN_EPS) * w + b).astype(x.dtype)


def head_norm(h, w, n_heads):
    hh = h.reshape(h.shape[:-1] + (n_heads, -1))
    mu = jnp.mean(hh, -1, keepdims=True)
    var = jnp.mean(jnp.square(hh - mu), -1, keepdims=True)
    return ((hh - mu) * lax.rsqrt(var + LN_EPS)).reshape(h.shape) * w


def to_heads(a, n):
    b, t, w = a.shape
    return a.reshape(b, t, n, w // n).transpose(0, 2, 1, 3)


def from_heads(a):
    b, n, t, d = a.shape
    return a.transpose(0, 2, 1, 3).reshape(b, t, n * d)


def axial_rope(x, row, col):
    inv = ROPE_BASE ** (-jnp.arange(ROPE_PAIRS, dtype=jnp.float32) / ROPE_PAIRS)

    def rot(xa, pos):
        ang = pos.astype(jnp.float32)[:, None] * inv
        cos, sin = jnp.cos(ang).astype(x.dtype), jnp.sin(ang).astype(x.dtype)
        x1, x2 = xa[..., :ROPE_PAIRS], xa[..., ROPE_PAIRS:]
        return jnp.concatenate([x1 * cos - x2 * sin, x2 * cos + x1 * sin], -1)

    half = HEAD_DIM // 2
    return jnp.concatenate([rot(x[..., :half], row), rot(x[..., half:], col)], -1)


def _chunks(a):
    nc = a.shape[2] // SCAN_CHUNK
    return jnp.moveaxis(a.reshape(a.shape[:2] + (nc, SCAN_CHUNK) + a.shape[3:]), 2, 0)


def _unchunk(hs):
    nc, b, h, l, d = hs.shape
    return jnp.moveaxis(hs, 0, 2).reshape(b, h, nc * l, d)


def mlstm_scan(state, q, k, v, ig, lf):
    tri = jnp.tril(jnp.ones((SCAN_CHUNK, SCAN_CHUNK), bool))

    def step(carry, inp):
        cmat, nvec, m = carry
        qc, kc, vc, ic, fc = inp
        b = jnp.cumsum(fc, axis=-1)
        log_d = jnp.where(tri, b[..., :, None] - b[..., None, :] + ic[..., None, :], -jnp.inf)
        log_inter = b + m[..., None]
        m_t = jnp.maximum(log_inter, jnp.max(log_d, axis=-1))
        d = jnp.exp(log_d - m_t[..., None])
        w_inter = jnp.exp(log_inter - m_t)
        s = jnp.einsum('bhtd,bhsd->bhts', qc, kc) * d
        num = jnp.einsum('bhts,bhsv->bhtv', s, vc) + w_inter[..., None] * jnp.einsum('bhtd,bhdv->bhtv', qc, cmat)
        den = jnp.sum(s, -1) + w_inter * jnp.einsum('bhtd,bhd->bht', qc, nvec)
        h = num / jnp.maximum(jnp.abs(den), jnp.exp(-m_t))[..., None]
        m_new = m_t[..., -1]
        w_state = jnp.exp(b[..., -1:] - b + ic - m_new[..., None])
        decay = jnp.exp(b[..., -1] + m - m_new)
        cmat = decay[..., None, None] * cmat + jnp.einsum('bhs,bhsd,bhsv->bhdv', w_state, kc, vc)
        nvec = decay[..., None] * nvec + jnp.einsum('bhs,bhsd->bhd', w_state, kc)
        return (cmat, nvec, m_new), h

    state, hs = lax.scan(step, state, tuple(_chunks(a) for a in (q, k, v, ig, lf)))
    return _unchunk(hs), state


def gla_scan(state, q, k, v, lg):
    tri = jnp.tril(jnp.ones((SCAN_CHUNK, SCAN_CHUNK), bool))

    def step(smat, inp):
        qc, kc, vc, gc = inp
        b = jnp.cumsum(gc, axis=2)
        q_t = qc * jnp.exp(b)
        k_t = kc * jnp.exp(-b)
        a = jnp.where(tri, jnp.einsum('bhtd,bhsd->bhts', q_t, k_t), 0.0)
        o = jnp.einsum('bhts,bhsv->bhtv', a, vc) + jnp.einsum('bhtd,bhdv->bhtv', q_t, smat)
        b_last = b[:, :, -1:, :]
        smat = jnp.exp(b_last[:, :, 0])[..., None] * smat + jnp.einsum('bhsd,bhsv->bhdv', kc * jnp.exp(b_last - b), vc)
        return smat, o

    state, os_ = lax.scan(step, state, tuple(_chunks(a) for a in (q, k, v, lg)))
    return _unchunk(os_), state


def bidirectional(scan, init, ctx_fwd, ctx_bwd, lat_fwd, lat_bwd):
    flip = lambda ts: tuple(jnp.flip(t, axis=2) for t in ts)
    c_f, s_f = scan(init, *ctx_fwd)
    l_f, _ = scan(s_f, *lat_fwd)
    c_b, s_b = scan(init, *flip(ctx_bwd))
    l_b, _ = scan(s_b, *flip(lat_bwd))
    return c_f + jnp.flip(c_b, 2), l_f + jnp.flip(l_b, 2)


def window_attention(q, k, v, kc, vc, sink):
    b_, _, s_, hd = q.shape
    nb = s_ // WIN_BLOCK
    qg = (q * hd ** -0.5).reshape(b_, ATTN_KV_HEADS, ATTN_GROUP, nb, WIN_BLOCK, hd)

    def band(a):
        ap = jnp.pad(a, ((0, 0), (0, 0), (WIN_BLOCK, WIN_BLOCK), (0, 0))).reshape(b_, ATTN_KV_HEADS, nb + 2, WIN_BLOCK, hd)
        return jnp.concatenate([ap[:, :, :-2], ap[:, :, 1:-1], ap[:, :, 2:]], axis=3)

    kb, vb = band(k), band(v)
    qpos = jnp.arange(nb)[:, None, None] * WIN_BLOCK + jnp.arange(WIN_BLOCK)[None, :, None]
    kpos = jnp.arange(nb)[:, None, None] * WIN_BLOCK - WIN_BLOCK + jnp.arange(3 * WIN_BLOCK)[None, None, :]
    valid = (jnp.abs(kpos - qpos) <= WINDOW) & (kpos >= 0) & (kpos < s_)
    s_loc = jnp.where(valid, jnp.einsum('bkgnqd,bknjd->bkgnqj', qg, kb).astype(jnp.float32), -jnp.inf)
    s_ctx = jnp.einsum('bkgnqd,bkcd->bkgnqc', qg, kc).astype(jnp.float32)
    s_sink = jnp.broadcast_to(sink.astype(jnp.float32).reshape(ATTN_KV_HEADS, ATTN_GROUP, 1, 1, 1), s_ctx.shape[:-1] + (1,))
    p = jax.nn.softmax(jnp.concatenate([s_loc, s_ctx, s_sink], -1), axis=-1).astype(v.dtype)
    n_loc, n_ctx = 3 * WIN_BLOCK, kc.shape[2]
    out = (jnp.einsum('bkgnqj,bknjd->bkgnqd', p[..., :n_loc], vb)
           + jnp.einsum('bkgnqc,bkcd->bkgnqd', p[..., n_loc:n_loc + n_ctx], vc))
    return out.reshape(b_, ATTN_HEADS, s_, hd)


def context_attention(qc, kc, vc, sink):
    b_, _, n_ctx, hd = qc.shape
    qg = (qc * hd ** -0.5).reshape(b_, ATTN_KV_HEADS, ATTN_GROUP, n_ctx, hd)
    s = jnp.einsum('bkgqd,bkcd->bkgqc', qg, kc).astype(jnp.float32)
    s_sink = jnp.broadcast_to(sink.astype(jnp.float32).reshape(ATTN_KV_HEADS, ATTN_GROUP, 1, 1), s.shape[:-1] + (1,))
    p = jax.nn.softmax(jnp.concatenate([s, s_sink], -1), axis=-1).astype(vc.dtype)
    out = jnp.einsum('bkgqc,bkcd->bkgqd', p[..., :n_ctx], vc)
    return out.reshape(b_, ATTN_HEADS, n_ctx, hd)


def mixer(h, hc, w_in, gate_b, mnorm_w, sink, g_up, g_b, gnorm_w, w_out, row, col, with_ctx_out):
    f32 = jnp.float32
    pl = jnp.split(h @ w_in, COL_OFFSETS, axis=-1)
    pc = jnp.split(hc @ w_in, COL_OFFSETS, axis=-1)
    b_ = h.shape[0]

    def mlstm_in(p):
        t_ = p[0].shape[1]
        q = to_heads(p[0], MLSTM_HEADS).astype(f32)
        k = to_heads(p[1], MLSTM_HEADS).astype(f32) * MLSTM_DH ** -0.5
        v = to_heads(p[2], MLSTM_HEADS).astype(f32)
        g = p[4].astype(f32).reshape(b_, t_, 2, 2, MLSTM_HEADS) + gate_b.astype(f32)
        g = jnp.transpose(g, (2, 3, 0, 4, 1))
        ig, lf = g[:, 0], jax.nn.log_sigmoid(g[:, 1])
        return [(q, k, v, ig[d], lf[d]) for d in range(2)]

    mc, ml = mlstm_in(pc), mlstm_in(pl)
    m_init = (jnp.zeros((b_, MLSTM_HEADS, MLSTM_DH, MLSTM_DH), f32),
              jnp.zeros((b_, MLSTM_HEADS, MLSTM_DH), f32),
              jnp.zeros((b_, MLSTM_HEADS), f32))
    m_ctx, m_lat = bidirectional(mlstm_scan, m_init, mc[0], mc[1], ml[0], ml[1])
    mlstm_out = lambda hs, p: (head_norm(from_heads(hs), mnorm_w, MLSTM_HEADS) * jax.nn.sigmoid(p[3].astype(f32))).astype(h.dtype)

    ql = axial_rope(to_heads(pl[5], ATTN_HEADS), row, col)
    kl = axial_rope(to_heads(pl[6], ATTN_KV_HEADS), row, col)
    vl = to_heads(pl[7], ATTN_KV_HEADS)
    kc, vc = to_heads(pc[6], ATTN_KV_HEADS), to_heads(pc[7], ATTN_KV_HEADS)
    a_lat = from_heads(window_attention(ql, kl, vl, kc, vc, sink))

    def gla_in(p):
        t_ = p[8].shape[1]
        q = to_heads(p[8], GLA_HEADS).astype(f32) * GLA_DK ** -0.5
        k = to_heads(p[9], GLA_HEADS).astype(f32)
        v = to_heads(p[10], GLA_HEADS).astype(f32)
        z = jnp.einsum('btzr,zrk->zbtk', p[12].astype(f32).reshape(b_, t_, 2, GLA_RANK), g_up.astype(f32)) + g_b.astype(f32)[:, None, None, :]
        lg = jax.nn.log_sigmoid(z) / GLA_TAU
        lg = jnp.transpose(lg.reshape(2, b_, t_, GLA_HEADS, GLA_DK), (0, 1, 3, 2, 4))
        return [(q, k, v, lg[d]) for d in range(2)]

    gc, gl = gla_in(pc), gla_in(pl)
    g_init = jnp.zeros((b_, GLA_HEADS, GLA_DK, GLA_DV), f32)
    g_ctx, g_lat = bidirectional(gla_scan, g_init, gc[0], gc[1], gl[0], gl[1])
    gla_out = lambda o, p: (head_norm(from_heads(o), gnorm_w, GLA_HEADS) * jax.nn.silu(p[11].astype(f32))).astype(h.dtype)

    lat = jnp.concatenate([mlstm_out(m_lat, pl), a_lat, gla_out(g_lat, pl)], -1) @ w_out
    if not with_ctx_out:
        return lat, None
    a_ctx = from_heads(context_attention(to_heads(pc[5], ATTN_HEADS), kc, vc, sink))
    ctx_o = jnp.concatenate([mlstm_out(m_ctx, pc), a_ctx, gla_out(g_ctx, pc)], -1) @ w_out
    return lat, ctx_o


def swiglu(h, w_gu, w_down):
    g, u = jnp.split(h @ w_gu, 2, axis=-1)
    return (jax.nn.silu(g) * u) @ w_down


def moe_swiglu(h, w_router, b_router, w_gu, w_down):
    n, d = h.shape
    logits = jnp.dot(h, w_router).astype(jnp.float32) + b_router.astype(jnp.float32)
    top_logits, top_e = lax.top_k(logits, TOP_K)
    top_w = jax.nn.softmax(top_logits, axis=-1)
    n_assign = n * TOP_K
    e_flat = top_e.reshape(-1)
    tok_flat = jnp.repeat(jnp.arange(n, dtype=jnp.int32), TOP_K)
    w_flat = top_w.reshape(-1)
    order = jnp.argsort(e_flat)
    e_sorted = e_flat[order]
    counts = jnp.bincount(e_flat, length=N_EXPERTS)
    padded = (counts + MOE_BLOCK - 1) // MOE_BLOCK * MOE_BLOCK
    start = jnp.cumsum(counts) - counts
    pend = jnp.cumsum(padded)
    pstart = pend - padded
    slot = pstart[e_sorted] + jnp.arange(n_assign) - start[e_sorted]
    n_blocks = -(-n_assign // MOE_BLOCK) + N_EXPERTS
    slot_tok = jnp.full((n_blocks * MOE_BLOCK,), n, jnp.int32).at[slot].set(tok_flat[order])
    slot_w = jnp.zeros((n_blocks * MOE_BLOCK,), jnp.float32).at[slot].set(w_flat[order])
    block_e = jnp.minimum(jnp.searchsorted(pend, jnp.arange(n_blocks) * MOE_BLOCK, side='right'), N_EXPERTS - 1)
    h_pad = jnp.concatenate([h, jnp.zeros((1, d), h.dtype)], 0)
    xb = h_pad[slot_tok].reshape(n_blocks, MOE_BLOCK, d)

    def expert_block(args):
        xblk, e = args
        g, u = jnp.split(xblk @ w_gu[e], 2, axis=-1)
        return (jax.nn.silu(g) * u) @ w_down[e]

    yb = lax.map(expert_block, (xb, block_e))
    y = jnp.zeros((n + 1, d), jnp.float32).at[slot_tok].add(yb.reshape(-1, d).astype(jnp.float32) * slot_w[:, None])
    return y[:n].astype(h.dtype)


def setup_inputs(seed: int = 0) -> dict:
    key = jax.random.key(seed)
    ks = jax.random.split(key, 24)
    nrm = lambda k, shape, s=1.0: jax.random.normal(k, shape, jnp.float32) * s
    D = D_MODEL
    gate_base = jnp.stack([jnp.zeros((MLSTM_HEADS,), jnp.float32), jnp.linspace(3.0, 6.0, MLSTM_HEADS)])
    return {
        'x': nrm(ks[0], (BATCH, SEQ, D)),
        'c': nrm(ks[1], (BATCH, D)),
        'ctx': nrm(ks[2], (BATCH, CTX_LEN, D)),
        'c_ctx': nrm(ks[3], (D,)),
        'w_ada': nrm(ks[4], (DEPTH, D, 6 * D), ADA_INIT * D ** -0.5),
        'b_ada': nrm(ks[5], (DEPTH, 6 * D), 0.02),
        'w_in': nrm(ks[6], (DEPTH, D, IN_COLS), D ** -0.5),
        'mlstm_gate_b': gate_base[None, None] + nrm(ks[7], (DEPTH, 2, 2, MLSTM_HEADS), 0.1),
        'mlstm_norm_w': 1.0 + nrm(ks[8], (DEPTH, MLSTM_WIDTH), 0.02),
        'attn_sink': nrm(ks[9], (DEPTH, ATTN_HEADS), 0.5),
        'gla_gate_up': nrm(ks[10], (DEPTH, 2, GLA_RANK, GLA_KEY_WIDTH), GLA_RANK ** -0.5),
        'gla_gate_b': nrm(ks[11], (DEPTH, 2, GLA_KEY_WIDTH), 0.1),
        'gla_norm_w': 1.0 + nrm(ks[12], (DEPTH, GLA_WIDTH), 0.02),
        'w_out': nrm(ks[13], (DEPTH, MIX_WIDTH, D), DEEPNORM_BETA * MIX_WIDTH ** -0.5),
        'ln_w': 1.0 + nrm(ks[14], (DEPTH, 2, D), 0.02),
        'ln_b': nrm(ks[15], (DEPTH, 2, D), 0.02),
        'ffn_w_gu': nrm(ks[16], (N_DENSE, D, 2 * D_FF), D ** -0.5),
        'ffn_w_down': nrm(ks[17], (N_DENSE, D_FF, D), DEEPNORM_BETA * D_FF ** -0.5),
        'router_w': nrm(ks[18], (N_MOE, D, N_EXPERTS), D ** -0.5),
        'router_b': nrm(ks[19], (N_MOE, N_EXPERTS), 0.01),
        'moe_w_gu': nrm(ks[20], (N_MOE, N_EXPERTS, D, 2 * D_EXPERT), D ** -0.5),
        'moe_w_down': nrm(ks[21], (N_MOE, N_EXPERTS, D_EXPERT, D), DEEPNORM_BETA * D_EXPERT ** -0.5),
    }


def reference(x, c, ctx, c_ctx, w_ada, b_ada, w_in, mlstm_gate_b, mlstm_norm_w, attn_sink,
              gla_gate_up, gla_gate_b, gla_norm_w, w_out, ln_w, ln_b, ffn_w_gu, ffn_w_down,
              router_w, router_b, moe_w_gu, moe_w_down):
    b_, seq, d = x.shape
    n_ctx = ctx.shape[1]
    rows = seq // GRID_W
    row = jnp.repeat(jnp.arange(rows), GRID_W)
    col = jnp.tile(jnp.arange(GRID_W), rows)
    sc, scc = jax.nn.silu(c), jax.nn.silu(c_ctx)
    xc = ctx
    for l in range(DEPTH):
        last = l == DEPTH - 1
        mod = (sc @ w_ada[l] + b_ada[l]).reshape(b_, 6, d)[:, :, None, :]
        modc = (scc @ w_ada[l] + b_ada[l]).reshape(6, d)
        h = x * (1.0 + mod[:, 1]) + mod[:, 0]
        hc = xc * (1.0 + modc[1]) + modc[0]
        mix, mix_c = mixer(h, hc, w_in[l], mlstm_gate_b[l], mlstm_norm_w[l], attn_sink[l],
                           gla_gate_up[l], gla_gate_b[l], gla_norm_w[l], w_out[l], row, col, not last)
        x = layer_norm(DEEPNORM_ALPHA * x + mod[:, 2] * mix, ln_w[l, 0], ln_b[l, 0])
        h = x * (1.0 + mod[:, 4]) + mod[:, 3]
        if last:
            tokens = h
        else:
            xc = layer_norm(DEEPNORM_ALPHA * xc + modc[2] * mix_c, ln_w[l, 0], ln_b[l, 0])
            hc = xc * (1.0 + modc[4]) + modc[3]
            tokens = jnp.concatenate([hc, h], axis=1)
        if l % 2 == 0:
            y = swiglu(tokens, ffn_w_gu[l // 2], ffn_w_down[l // 2])
        else:
            y = moe_swiglu(tokens.reshape(-1, d), router_w[l // 2], router_b[l // 2],
                           moe_w_gu[l // 2], moe_w_down[l // 2]).reshape(tokens.shape)
        if not last:
            y_c, y = y[:, :n_ctx], y[:, n_ctx:]
            xc = layer_norm(DEEPNORM_ALPHA * xc + modc[5] * y_c, ln_w[l, 1], ln_b[l, 1])
        x = layer_norm(DEEPNORM_ALPHA * x + mod[:, 5] * y, ln_w[l, 1], ln_b[l, 1])
    return x
```

```python
import functools

import jax
import jax.numpy as jnp
import numpy as np
from jax import lax
from jax.experimental import pallas as pl
from jax.experimental.pallas import tpu as pltpu

F32 = jnp.float32
BF16 = jnp.bfloat16
HIGHEST = lax.Precision.HIGHEST

D_MODEL = 1024
SEQ = 16384
DEPTH = 4
GRID_W = 64
CTX_LEN = 256
MLSTM_HEADS = 4
MLSTM_DH = 64
MLSTM_WIDTH = 256
HEAD_DIM = 64
ATTN_HEADS = 8
ATTN_KV_HEADS = 2
ATTN_Q_WIDTH = 512
ATTN_KV_WIDTH = 128
WINDOW = 128
ROPE_BASE = 10000.0
ROPE_PAIRS = 16
GLA_HEADS = 4
GLA_DK = 32
GLA_DV = 64
GLA_WIDTH = 256
GLA_KEY_WIDTH = 128
GLA_RANK = 16
GLA_TAU = 16.0
CHUNK = 64
D_FF = 2816
N_EXPERTS = 8
D_EXPERT = 3584
DEEPNORM_ALPHA = (2 * DEPTH) ** 0.25
LN_EPS = 1e-5

LANES = 128
SUBLANES = 8
VMEM_LIMIT = 56 * 1024 * 1024

TB = 256
N_CHUNKS = TB // CHUNK
FFN_TM = 640
FFN_TF = 256
MOE_G = 512
MOE_TF = 512

C_MQ, C_MK, C_MV, C_MO = 0, 256, 512, 768
C_AQ, C_AK, C_AV = 1024, 1536, 1664
C_GQ, C_GK, C_GV, C_GO = 1792, 1920, 2048, 2304
N_MAIN = 2560
N_GATE = 128
N_PROJ = N_MAIN + N_GATE
GLA_GATE_OFF = 16


def _cparams(n_axes=1):
    return pltpu.CompilerParams(dimension_semantics=("arbitrary",) * n_axes,
                                vmem_limit_bytes=VMEM_LIMIT)


def _sigmoid(x):
    return 1.0 / (1.0 + jnp.exp(-x))


def _log_sigmoid(x):
    return jnp.minimum(x, 0.0) - jnp.log(1.0 + jnp.exp(-jnp.abs(x)))


def _split3(x):
    hi = x.astype(BF16)
    r1 = x - hi.astype(F32)
    mid = r1.astype(BF16)
    lo = (r1 - mid.astype(F32)).astype(BF16)
    return hi, mid, lo


def _dot_exact_rhs(x, m_bf16):
    hi, mid, lo = _split3(x)
    d = lambda a: jnp.dot(a, m_bf16, preferred_element_type=F32)
    return d(hi) + d(mid) + d(lo)


def _dot_exact_lhs(m_bf16, x):
    hi, mid, lo = _split3(x)
    d = lambda a: jnp.dot(m_bf16, a, preferred_element_type=F32)
    return d(hi) + d(mid) + d(lo)


def _layer_norm(v, w, b):
    mu = jnp.mean(v, axis=-1, keepdims=True)
    cen = v - mu
    var = jnp.mean(cen * cen, axis=-1, keepdims=True)
    return cen * lax.rsqrt(var + LN_EPS) * w + b


def _mod_kernel(c_ref, w_ref, b_ref, o_ref):
    c = c_ref[...]
    sc = c * _sigmoid(c)
    o_ref[...] = jnp.dot(sc, w_ref[...], preferred_element_type=F32, precision=HIGHEST) + b_ref[...]


def _mod_call(cvec, w_ada, b_ada):
    depth, d, n = w_ada.shape
    tn = 1536
    return pl.pallas_call(
        _mod_kernel,
        grid=(depth, n // tn),
        in_specs=[pl.BlockSpec((SUBLANES, d), lambda l, j: (0, 0)),
                  pl.BlockSpec((None, d, tn), lambda l, j: (l, 0, j)),
                  pl.BlockSpec((None, 1, tn), lambda l, j: (l, 0, j))],
        out_specs=pl.BlockSpec((None, SUBLANES, tn), lambda l, j: (l, 0, j)),
        out_shape=jax.ShapeDtypeStruct((depth, SUBLANES, n), F32),
        compiler_params=_cparams(2),
        name="ada_mod",
    )(cvec, w_ada, b_ada.reshape(depth, 1, n))


def _who(i):
    return jnp.minimum(i, 1)


def _rope(x, cos, sin_signed, first_half):
    swapped = jnp.where(first_half, pltpu.roll(x, LANES - ROPE_PAIRS, 1), pltpu.roll(x, ROPE_PAIRS, 1))
    return x * cos + swapped * sin_signed


def _inproj_kernel(x_ref, mod_ref, w_ref, cos_ref, sin_ref, main_ref, gate_ref):
    m = mod_ref[...]
    h = x_ref[...] * (1.0 + m[1:2]) + m[0:1]
    p = jnp.dot(h.astype(BF16), w_ref[...], preferred_element_type=F32)
    cos, sin = cos_ref[...], sin_ref[...]
    lane = lax.broadcasted_iota(jnp.int32, (TB, LANES), 1)
    first_half = (lane % (2 * ROPE_PAIRS)) < ROPE_PAIRS
    main_ref[:, :C_AQ] = p[:, :C_AQ].astype(BF16)
    for j in range(ATTN_Q_WIDTH // LANES):
        lo = C_AQ + j * LANES
        main_ref[:, lo:lo + LANES] = (_rope(p[:, lo:lo + LANES], cos, sin, first_half) * HEAD_DIM ** -0.5).astype(BF16)
    main_ref[:, C_AK:C_AV] = _rope(p[:, C_AK:C_AV], cos, sin, first_half).astype(BF16)
    main_ref[:, C_AV:] = p[:, C_AV:N_MAIN].astype(BF16)
    gate_ref[...] = p[:, N_MAIN:]


def _inproj_call(x, mod, w, cos_t, sin_t):
    t, d = x.shape
    nb = t // TB
    return pl.pallas_call(
        _inproj_kernel,
        grid=(nb,),
        in_specs=[pl.BlockSpec((TB, d), lambda i: (i, 0)),
                  pl.BlockSpec((None, SUBLANES, d), lambda i: (_who(i), 0, 0)),
                  pl.BlockSpec((d, N_PROJ), lambda i: (0, 0)),
                  pl.BlockSpec((TB, LANES), lambda i: (i, 0)),
                  pl.BlockSpec((TB, LANES), lambda i: (i, 0))],
        out_specs=[pl.BlockSpec((TB, N_MAIN), lambda i: (i, 0)),
                   pl.BlockSpec((TB, N_GATE), lambda i: (i, 0))],
        out_shape=[jax.ShapeDtypeStruct((t, N_MAIN), BF16),
                   jax.ShapeDtypeStruct((t, N_GATE), F32)],
        compiler_params=_cparams(1),
        name="in_proj",
    )(x, mod, w, cos_t, sin_t)


def _scan_block(i, nb, direction):
    if direction == 0:
        return i
    return jnp.where(i == 0, 0, nb - i)


def _tri_masks(direction):
    r = lax.broadcasted_iota(jnp.int32, (CHUNK, CHUNK), 0)
    c = lax.broadcasted_iota(jnp.int32, (CHUNK, CHUNK), 1)
    if direction == 0:
        return c <= r, r <= c
    return c >= r, r >= c


def _mlstm_kernel(q_ref, k_ref, v_ref, g_ref, gb_ref, o_ref, c_scr, m_scr, *, direction):
    @pl.when(pl.program_id(0) == 0)
    def _():
        c_scr[...] = jnp.zeros_like(c_scr)
        m_scr[...] = jnp.zeros_like(m_scr)

    g = g_ref[...] + gb_ref[...]
    g_t = g.T
    q = q_ref[...].astype(F32)
    k = k_ref[...].astype(F32) * MLSTM_DH ** -0.5
    v = v_ref[...].astype(F32)
    k_t = k.T
    mask, mask_t = _tri_masks(direction)
    ones_col = (lax.broadcasted_iota(jnp.int32, (CHUNK, MLSTM_DH), 1) == 0).astype(F32)
    order = range(N_CHUNKS) if direction == 0 else range(N_CHUNKS - 1, -1, -1)
    neg_inf = jnp.float32(-jnp.inf)

    for h in range(MLSTM_HEADS):
        ji = direction * 2 * MLSTM_HEADS + h
        jf = ji + MLSTM_HEADS
        hs = slice(h * MLSTM_DH, (h + 1) * MLSTM_DH)
        c_aug = c_scr[h]
        m_prev = m_scr[h][0:1, 0:1]
        for c in order:
            ts = slice(c * CHUNK, (c + 1) * CHUNK)
            ig_row = g_t[ji:ji + 1, ts]
            lf_row = _log_sigmoid(g_t[jf:jf + 1, ts])
            ig_col = g[ts, ji:ji + 1]
            lf_col = _log_sigmoid(g[ts, jf:jf + 1])
            b_col = jnp.sum(jnp.where(mask, lf_row, 0.0), axis=1, keepdims=True)
            b_row = jnp.sum(jnp.where(mask_t, lf_col, 0.0), axis=0, keepdims=True)
            gr = ig_row - b_row
            gc = ig_col - b_col
            a_col = jnp.maximum(m_prev, jnp.max(jnp.where(mask, gr, neg_inf), axis=1, keepdims=True))
            d_mat = jnp.exp(jnp.where(mask, gr - a_col, neg_inf))
            w_inter = jnp.exp(m_prev - a_col)
            m_col = b_col + a_col
            qh = q[ts, hs].astype(BF16)
            vh = v[ts, hs]
            kth = k_t[hs, ts].astype(BF16)
            s = jnp.dot(qh, kth, preferred_element_type=F32) * d_mat
            r1 = jnp.dot(s.astype(BF16), vh.astype(BF16), preferred_element_type=F32)
            r2 = jnp.dot(qh, c_aug.astype(BF16), preferred_element_type=F32)
            num = r1 + w_inter * r2[:, :MLSTM_DH]
            den = jnp.sum(s, axis=1, keepdims=True) + w_inter * r2[:, MLSTM_DH:MLSTM_DH + 1]
            o_ref[ts, hs] = num / jnp.maximum(jnp.abs(den), jnp.exp(-m_col))
            a_last = jnp.maximum(m_prev, jnp.max(gr, axis=1, keepdims=True))
            w_state = jnp.exp(gc - a_last)
            decay = jnp.exp(m_prev - a_last)
            v_aug = jnp.concatenate([vh, ones_col], axis=1)
            upd = jnp.dot(kth, (w_state * v_aug).astype(BF16), preferred_element_type=F32)
            c_aug = decay * c_aug + upd
            m_prev = jnp.sum(lf_row, axis=1, keepdims=True) + a_last
        c_scr[h] = c_aug
        m_scr[h] = jnp.broadcast_to(m_prev, (SUBLANES, LANES))


def _mlstm_call(main, gates, gate_bias, direction):
    t = main.shape[0]
    nb = t // TB
    blk = lambda i: _scan_block(i, nb, direction)
    wq = MLSTM_WIDTH
    return pl.pallas_call(
        functools.partial(_mlstm_kernel, direction=direction),
        grid=(nb,),
        in_specs=[pl.BlockSpec((TB, wq), lambda i: (blk(i), C_MQ // wq)),
                  pl.BlockSpec((TB, wq), lambda i: (blk(i), C_MK // wq)),
                  pl.BlockSpec((TB, wq), lambda i: (blk(i), C_MV // wq)),
                  pl.BlockSpec((TB, N_GATE), lambda i: (blk(i), 0)),
                  pl.BlockSpec((1, N_GATE), lambda i: (0, 0))],
        out_specs=pl.BlockSpec((TB, wq), lambda i: (blk(i), 0)),
        out_shape=jax.ShapeDtypeStruct((t, wq), F32),
        scratch_shapes=[pltpu.VMEM((MLSTM_HEADS, MLSTM_DH, LANES), F32),
                        pltpu.VMEM((MLSTM_HEADS, SUBLANES, LANES), F32)],
        compiler_params=_cparams(1),
        name=f"mlstm_dir{direction}",
    )(main, main, main, gates, gate_bias)


def _gla_kernel(q_ref, k_ref, v_ref, g_ref, gup_ref, gb_ref, tri_ref, o_ref, s_scr, *, direction):
    @pl.when(pl.program_id(0) == 0)
    def _():
        s_scr[...] = jnp.zeros_like(s_scr)

    z = jnp.dot(g_ref[...], gup_ref[...], preferred_element_type=F32, precision=HIGHEST) + gb_ref[...]
    lg = _log_sigmoid(z) * (1.0 / GLA_TAU)
    b = _dot_exact_lhs(tri_ref[...], lg)
    q = q_ref[...].astype(F32) * GLA_DK ** -0.5
    k = k_ref[...].astype(F32)
    v = v_ref[...]
    qt = q * jnp.exp(b)
    kt_t = (k * jnp.exp(-b)).T
    b_t = b.T
    kdec_parts = []
    for c in range(N_CHUNKS):
        last = c * CHUNK + (CHUNK - 1 if direction == 0 else 0)
        ts = slice(c * CHUNK, (c + 1) * CHUNK)
        kdec_parts.append(k[ts] * jnp.exp(b[last:last + 1] - b[ts]))
    kdec_t = jnp.concatenate(kdec_parts, axis=0).T
    mask, _ = _tri_masks(direction)
    order = range(N_CHUNKS) if direction == 0 else range(N_CHUNKS - 1, -1, -1)

    for h in range(GLA_HEADS):
        ks = slice(h * GLA_DK, (h + 1) * GLA_DK)
        vs = slice(h * GLA_DV, (h + 1) * GLA_DV)
        s_h = s_scr[h]
        for c in order:
            ts = slice(c * CHUNK, (c + 1) * CHUNK)
            last = c * CHUNK + (CHUNK - 1 if direction == 0 else 0)
            qth = qt[ts, ks].astype(BF16)
            vh = v[ts, vs]
            a = jnp.dot(qth, kt_t[ks, ts].astype(BF16), preferred_element_type=F32)
            a = jnp.where(mask, a, 0.0)
            o = (jnp.dot(a.astype(BF16), vh, preferred_element_type=F32)
                 + jnp.dot(qth, s_h.astype(BF16), preferred_element_type=F32))
            o_ref[ts, vs] = o
            dec_col = jnp.exp(b_t[ks, last:last + 1])
            s_h = dec_col * s_h + jnp.dot(kdec_t[ks, ts].astype(BF16), vh, preferred_element_type=F32)
        s_scr[h] = s_h


def _gla_call(main, gates, gup_pad, gb, tri, direction):
    t = main.shape[0]
    nb = t // TB
    blk = lambda i: _scan_block(i, nb, direction)
    return pl.pallas_call(
        functools.partial(_gla_kernel, direction=direction),
        grid=(nb,),
        in_specs=[pl.BlockSpec((TB, GLA_KEY_WIDTH), lambda i: (blk(i), C_GQ // GLA_KEY_WIDTH)),
                  pl.BlockSpec((TB, GLA_KEY_WIDTH), lambda i: (blk(i), C_GK // GLA_KEY_WIDTH)),
                  pl.BlockSpec((TB, GLA_WIDTH), lambda i: (blk(i), C_GV // GLA_WIDTH)),
                  pl.BlockSpec((TB, N_GATE), lambda i: (blk(i), 0)),
                  pl.BlockSpec((N_GATE, GLA_KEY_WIDTH), lambda i: (0, 0)),
                  pl.BlockSpec((1, GLA_KEY_WIDTH), lambda i: (0, 0)),
                  pl.BlockSpec((TB, TB), lambda i: (0, 0))],
        out_specs=pl.BlockSpec((TB, GLA_WIDTH), lambda i: (blk(i), 0)),
        out_shape=jax.ShapeDtypeStruct((t, GLA_WIDTH), F32),
        scratch_shapes=[pltpu.VMEM((GLA_HEADS, GLA_DK, GLA_DV), F32)],
        compiler_params=_cparams(1),
        name=f"gla_dir{direction}",
    )(main, main, main, gates, gup_pad, gb, tri)


def _attn_kernel(sink_ref, q_ref, kp_ref, kc_ref, kn_ref, kx_ref, vp_ref, vc_ref, vn_ref, vx_ref, o_ref, *, seq):
    i = pl.program_id(0)
    half = TB // 2
    n_loc = 2 * TB
    keys = jnp.concatenate([kp_ref[...], kc_ref[...], kn_ref[...], kx_ref[...]], axis=0)
    vals = jnp.concatenate([vp_ref[...], vc_ref[...], vn_ref[...], vx_ref[...]], axis=0)
    n_keys = keys.shape[0]
    lane = lax.broadcasted_iota(jnp.int32, (n_keys, LANES), 1)
    zero = jnp.zeros_like(keys)
    keys_g = [jnp.where(lane < HEAD_DIM, keys, zero), jnp.where(lane >= HEAD_DIM, keys, zero)]
    r = lax.broadcasted_iota(jnp.int32, (TB, n_keys), 0)
    c = lax.broadcasted_iota(jnp.int32, (TB, n_keys), 1)
    rel = c - half - r
    kpos = (i - 1) * TB + c - half
    local_ok = (jnp.abs(rel) <= WINDOW) & (kpos >= 0) & (kpos < seq) & (i > 0)
    valid = local_ok | (c >= n_loc)
    out_lane = lax.broadcasted_iota(jnp.int32, (TB, LANES), 1)
    neg_inf = jnp.float32(-jnp.inf)
    dn = (((1,), (1,)), ((), ()))
    for m in range(ATTN_Q_WIDTH // LANES):
        qm = q_ref[:, m * LANES:(m + 1) * LANES]
        res = []
        for gidx in range(ATTN_KV_HEADS):
            sink = sink_ref[gidx * (ATTN_HEADS // ATTN_KV_HEADS) + m]
            s = lax.dot_general(qm, keys_g[gidx], dn, preferred_element_type=F32)
            s = jnp.where(valid, s, neg_inf)
            mx = jnp.maximum(jnp.max(s, axis=1, keepdims=True), sink)
            p = jnp.exp(s - mx)
            denom = jnp.sum(p, axis=1, keepdims=True) + jnp.exp(sink - mx)
            pv = jnp.dot(p.astype(BF16), vals, preferred_element_type=F32)
            res.append(pv / denom)
        o_ref[:, m * LANES:(m + 1) * LANES] = jnp.where(out_lane < HEAD_DIM, res[0], res[1]).astype(BF16)


def _attn_call(main, sink, seq):
    t = main.shape[0]
    nb = t // TB
    half = TB // 2
    nhb = t // half
    kcol, vcol = C_AK // LANES, C_AV // LANES
    prev_i = lambda i: jnp.maximum(2 * i - 1, 0)
    next_i = lambda i: jnp.minimum(2 * i + 2, nhb - 1)
    grid_spec = pltpu.PrefetchScalarGridSpec(
        num_scalar_prefetch=1,
        grid=(nb,),
        in_specs=[pl.BlockSpec((TB, ATTN_Q_WIDTH), lambda i, s: (i, C_AQ // ATTN_Q_WIDTH)),
                  pl.BlockSpec((half, LANES), lambda i, s: (prev_i(i), kcol)),
                  pl.BlockSpec((TB, LANES), lambda i, s: (i, kcol)),
                  pl.BlockSpec((half, LANES), lambda i, s: (next_i(i), kcol)),
                  pl.BlockSpec((TB, LANES), lambda i, s: (0, kcol)),
                  pl.BlockSpec((half, LANES), lambda i, s: (prev_i(i), vcol)),
                  pl.BlockSpec((TB, LANES), lambda i, s: (i, vcol)),
                  pl.BlockSpec((half, LANES), lambda i, s: (next_i(i), vcol)),
                  pl.BlockSpec((TB, LANES), lambda i, s: (0, vcol))],
        out_specs=pl.BlockSpec((TB, ATTN_Q_WIDTH), lambda i, s: (i, 0)),
    )
    return pl.pallas_call(
        functools.partial(_attn_kernel, seq=seq),
        grid_spec=grid_spec,
        out_shape=jax.ShapeDtypeStruct((t, ATTN_Q_WIDTH), BF16),
        compiler_params=_cparams(1),
        name="window_attn",
    )(sink, main, main, main, main, main, main, main, main, main)


def _head_norm(x, avg_bf16, w):
    mu = _dot_exact_rhs(x, avg_bf16)
    cen = x - mu
    var = _dot_exact_rhs(cen * cen, avg_bf16)
    return cen * lax.rsqrt(var + LN_EPS) * w


def _outproj_kernel(mf_ref, mb_ref, mo_ref, at_ref, gf_ref, gbk_ref, go_ref, mnw_ref, gnw_ref, avg_ref,
                    w_ref, x_ref, mod_ref, lnw_ref, lnb_ref, x1_ref, h2_ref):
    avg = avg_ref[...]
    hm = _head_norm(mf_ref[...] + mb_ref[...], avg, mnw_ref[...]) * _sigmoid(mo_ref[...].astype(F32))
    gate = go_ref[...].astype(F32)
    hg = _head_norm(gf_ref[...] + gbk_ref[...], avg, gnw_ref[...]) * (gate * _sigmoid(gate))
    a0, a1, a2 = MLSTM_WIDTH, MLSTM_WIDTH + ATTN_Q_WIDTH, D_MODEL
    mix = (jnp.dot(hm.astype(BF16), w_ref[:a0], preferred_element_type=F32)
           + jnp.dot(at_ref[...], w_ref[a0:a1], preferred_element_type=F32)
           + jnp.dot(hg.astype(BF16), w_ref[a1:a2], preferred_element_type=F32))
    m = mod_ref[...]
    x1 = _layer_norm(DEEPNORM_ALPHA * x_ref[...] + m[2:3] * mix, lnw_ref[...], lnb_ref[...])
    x1_ref[...] = x1
    h2_ref[...] = (x1 * (1.0 + m[4:5]) + m[3:4]).astype(h2_ref.dtype)


def _outproj_call(mf, mb, main, attn, gf, gbk, mnw, gnw, avg, w_out, x, mod, lnw, lnb, h2_dtype):
    t, d = x.shape
    nb = t // TB
    row = lambda i: (i, 0)
    const = lambda i: (0, 0)
    return pl.pallas_call(
        _outproj_kernel,
        grid=(nb,),
        in_specs=[pl.BlockSpec((TB, MLSTM_WIDTH), row),
                  pl.BlockSpec((TB, MLSTM_WIDTH), row),
                  pl.BlockSpec((TB, MLSTM_WIDTH), lambda i: (i, C_MO // MLSTM_WIDTH)),
                  pl.BlockSpec((TB, ATTN_Q_WIDTH), row),
                  pl.BlockSpec((TB, GLA_WIDTH), row),
                  pl.BlockSpec((TB, GLA_WIDTH), row),
                  pl.BlockSpec((TB, GLA_WIDTH), lambda i: (i, C_GO // GLA_WIDTH)),
                  pl.BlockSpec((1, MLSTM_WIDTH), const),
                  pl.BlockSpec((1, GLA_WIDTH), const),
                  pl.BlockSpec((MLSTM_WIDTH, MLSTM_WIDTH), const),
                  pl.BlockSpec((d, d), const),
                  pl.BlockSpec((TB, d), row),
                  pl.BlockSpec((None, SUBLANES, d), lambda i: (_who(i), 0, 0)),
                  pl.BlockSpec((1, d), const),
                  pl.BlockSpec((1, d), const)],
        out_specs=[pl.BlockSpec((TB, d), row), pl.BlockSpec((TB, d), row)],
        out_shape=[jax.ShapeDtypeStruct((t, d), F32), jax.ShapeDtypeStruct((t, d), h2_dtype)],
        compiler_params=_cparams(1),
        name="out_proj",
    )(mf, mb, main, attn, gf, gbk, main, mnw, gnw, avg, w_out, x, mod, lnw, lnb)


def _select_mod(mod_ref, k, i, tm):
    rows = i * tm + lax.broadcasted_iota(jnp.int32, (tm, 1), 0)
    return jnp.where(rows < CTX_LEN, mod_ref[0, k:k + 1, :], mod_ref[1, k:k + 1, :])


def _ffn_kernel(h_ref, wg_ref, wu_ref, wd_ref, x_ref, mod_ref, lnw_ref, lnb_ref, o_ref, acc_ref):
    f = pl.program_id(1)
    h = h_ref[...]
    g = jnp.dot(h, wg_ref[...], preferred_element_type=F32)
    u = jnp.dot(h, wu_ref[...], preferred_element_type=F32)
    part = jnp.dot((g * _sigmoid(g) * u).astype(BF16), wd_ref[...], preferred_element_type=F32)

    @pl.when(f == 0)
    def _():
        acc_ref[...] = part

    @pl.when(f > 0)
    def _():
        acc_ref[...] += part

    @pl.when(f == pl.num_programs(1) - 1)
    def _():
        m5 = _select_mod(mod_ref, 5, pl.program_id(0), h_ref.shape[0])
        o_ref[...] = _layer_norm(DEEPNORM_ALPHA * x_ref[...] + m5 * acc_ref[...], lnw_ref[...], lnb_ref[...])


def _ffn_call(h2, w_gu, w_down, x1, mod, lnw, lnb):
    t, d = x1.shape
    nf = D_FF // FFN_TF
    return pl.pallas_call(
        _ffn_kernel,
        grid=(t // FFN_TM, nf),
        in_specs=[pl.BlockSpec((FFN_TM, d), lambda i, f: (i, 0)),
                  pl.BlockSpec((d, FFN_TF), lambda i, f: (0, f)),
                  pl.BlockSpec((d, FFN_TF), lambda i, f: (0, nf + f)),
                  pl.BlockSpec((FFN_TF, d), lambda i, f: (f, 0)),
                  pl.BlockSpec((FFN_TM, d), lambda i, f: (i, 0)),
                  pl.BlockSpec((2, SUBLANES, d), lambda i, f: (0, 0, 0)),
                  pl.BlockSpec((1, d), lambda i, f: (0, 0)),
                  pl.BlockSpec((1, d), lambda i, f: (0, 0))],
        out_specs=pl.BlockSpec((FFN_TM, d), lambda i, f: (i, 0)),
        out_shape=jax.ShapeDtypeStruct((t, d), F32),
        scratch_shapes=[pltpu.VMEM((FFN_TM, d), F32)],
        compiler_params=_cparams(2),
        name="dense_ffn",
    )(h2, w_gu, w_gu, w_down, x1, mod, lnw, lnb)


def _router_kernel(h_ref, wr_ref, br_ref, su_ref, route_ref, cnt_ref, base_scr):
    @pl.when(pl.program_id(0) == 0)
    def _():
        base_scr[...] = jnp.zeros_like(base_scr)

    logits = jnp.dot(h_ref[...], wr_ref[...], preferred_element_type=F32, precision=HIGHEST)
    lt = logits.T[:N_EXPERTS] + br_ref[...]
    idx = lax.broadcasted_iota(jnp.int32, lt.shape, 0)
    neg_inf = jnp.float32(-jnp.inf)
    m1 = jnp.max(lt, axis=0, keepdims=True)
    e1 = jnp.min(jnp.where(lt == m1, idx, N_EXPERTS), axis=0, keepdims=True)
    lt2 = jnp.where(idx == e1, neg_inf, lt)
    m2 = jnp.max(lt2, axis=0, keepdims=True)
    e2 = jnp.min(jnp.where(lt2 == m2, idx, N_EXPERTS), axis=0, keepdims=True)
    t2 = jnp.exp(m2 - m1)
    w1 = 1.0 / (1.0 + t2)
    w2 = t2 / (1.0 + t2)
    su = su_ref[...]
    base = base_scr[...][:, 0:1]
    oh1 = (idx == e1).astype(F32)
    cum1 = jnp.dot(oh1.astype(BF16), su, preferred_element_type=F32)
    rank1 = jnp.sum(oh1 * (base + cum1), axis=0, keepdims=True)
    base = base + jnp.sum(oh1, axis=1, keepdims=True)
    oh2 = (idx == e2).astype(F32)
    cum2 = jnp.dot(oh2.astype(BF16), su, preferred_element_type=F32)
    rank2 = jnp.sum(oh2 * (base + cum2), axis=0, keepdims=True)
    base = base + jnp.sum(oh2, axis=1, keepdims=True)
    base_scr[...] = jnp.broadcast_to(base, base_scr.shape)
    cnt_ref[...] = jnp.broadcast_to(base, cnt_ref.shape)
    zero = jnp.zeros_like(w1)
    route_ref[...] = jnp.concatenate(
        [e1.astype(F32), e2.astype(F32), w1, w2, rank1, rank2, zero, zero], axis=0)


def _router_call(h2, wr_pad, br, su):
    t, d = h2.shape
    nb = t // TB
    return pl.pallas_call(
        _router_kernel,
        grid=(nb,),
        in_specs=[pl.BlockSpec((TB, d), lambda i: (i, 0)),
                  pl.BlockSpec((d, LANES), lambda i: (0, 0)),
                  pl.BlockSpec((N_EXPERTS, 1), lambda i: (0, 0)),
                  pl.BlockSpec((TB, TB), lambda i: (0, 0))],
        out_specs=[pl.BlockSpec((SUBLANES, TB), lambda i: (0, i)),
                   pl.BlockSpec((N_EXPERTS, LANES), lambda i: (0, 0))],
        out_shape=[jax.ShapeDtypeStruct((SUBLANES, t), F32),
                   jax.ShapeDtypeStruct((N_EXPERTS, LANES), F32)],
        scratch_shapes=[pltpu.VMEM((N_EXPERTS, LANES), F32)],
        compiler_params=_cparams(1),
        name="moe_router",
    )(h2, wr_pad, br, su)


def _dispatch_kernel(s1_ref, s2_ref, h_ref, xs_in_ref, xs_ref, sem):
    del xs_in_ref
    i = pl.program_id(0)

    def copy(r, slot_ref):
        return pltpu.make_async_copy(h_ref.at[pl.ds(r, 1)], xs_ref.at[pl.ds(slot_ref[i * TB + r], 1)], sem)

    def start(r, carry):
        copy(r, s1_ref).start()
        copy(r, s2_ref).start()
        return carry

    def wait(r, carry):
        copy(r, s1_ref).wait()
        copy(r, s2_ref).wait()
        return carry

    lax.fori_loop(0, TB, start, 0)
    lax.fori_loop(0, TB, wait, 0)


def _dispatch_call(slot1, slot2, h2, xs_init):
    t, d = h2.shape
    grid_spec = pltpu.PrefetchScalarGridSpec(
        num_scalar_prefetch=2,
        grid=(t // TB,),
        in_specs=[pl.BlockSpec((TB, d), lambda i, a, b: (i, 0)),
                  pl.BlockSpec(memory_space=pl.ANY)],
        out_specs=pl.BlockSpec(memory_space=pl.ANY),
        scratch_shapes=[pltpu.SemaphoreType.DMA(())],
    )
    return pl.pallas_call(
        _dispatch_kernel,
        grid_spec=grid_spec,
        out_shape=jax.ShapeDtypeStruct(xs_init.shape, xs_init.dtype),
        input_output_aliases={3: 0},
        compiler_params=_cparams(1),
        name="moe_dispatch",
    )(slot1, slot2, h2, xs_init)


def _expert_kernel(be_ref, used_ref, x_ref, wg_ref, wu_ref, wd_ref, y_ref, xb_scr, acc_scr):
    b, f = pl.program_id(0), pl.program_id(1)
    live = b < used_ref[0]

    @pl.when(live & (f == 0))
    def _():
        xb_scr[...] = x_ref[...].astype(BF16)

    @pl.when(live)
    def _():
        x = xb_scr[...]
        g = jnp.dot(x, wg_ref[...], preferred_element_type=F32)
        u = jnp.dot(x, wu_ref[...], preferred_element_type=F32)
        part = jnp.dot((g * _sigmoid(g) * u).astype(BF16), wd_ref[...], preferred_element_type=F32)

        @pl.when(f == 0)
        def _():
            acc_scr[...] = part

        @pl.when(f > 0)
        def _():
            acc_scr[...] += part

    @pl.when(f == pl.num_programs(1) - 1)
    def _():
        @pl.when(live)
        def _():
            y_ref[...] = acc_scr[...]

        @pl.when(jnp.logical_not(live))
        def _():
            y_ref[...] = jnp.zeros_like(y_ref)


def _expert_call(block_e, used, xs, w_gu, w_down):
    ns, d = xs.shape
    nblk = ns // MOE_G
    nf = D_EXPERT // MOE_TF

    def f_eff(b, f, used_ref):
        return jnp.where(b < used_ref[0], f, nf - 1)

    grid_spec = pltpu.PrefetchScalarGridSpec(
        num_scalar_prefetch=2,
        grid=(nblk, nf),
        in_specs=[pl.BlockSpec((MOE_G, d), lambda b, f, be, us: (b, 0)),
                  pl.BlockSpec((None, d, MOE_TF), lambda b, f, be, us: (be[b], 0, f_eff(b, f, us))),
                  pl.BlockSpec((None, d, MOE_TF), lambda b, f, be, us: (be[b], 0, nf + f_eff(b, f, us))),
                  pl.BlockSpec((None, MOE_TF, d), lambda b, f, be, us: (be[b], f_eff(b, f, us), 0))],
        out_specs=pl.BlockSpec((MOE_G, d), lambda b, f, be, us: (b, 0)),
        scratch_shapes=[pltpu.VMEM((MOE_G, d), BF16), pltpu.VMEM((MOE_G, d), F32)],
    )
    return pl.pallas_call(
        _expert_kernel,
        grid_spec=grid_spec,
        out_shape=jax.ShapeDtypeStruct((ns, d), F32),
        compiler_params=_cparams(2),
        name="moe_experts",
    )(block_e, used, xs, w_gu, w_gu, w_down)


def _combine_kernel(s1_ref, s2_ref, yb_ref, route_ref, x_ref, mod_ref, lnw_ref, lnb_ref, o_ref, buf1, buf2, sem):
    i = pl.program_id(0)

    def copy(r, slot_ref, buf):
        return pltpu.make_async_copy(yb_ref.at[pl.ds(slot_ref[i * TB + r], 1)], buf.at[pl.ds(r, 1)], sem)

    def start(r, carry):
        copy(r, s1_ref, buf1).start()
        copy(r, s2_ref, buf2).start()
        return carry

    def wait(r, carry):
        copy(r, s1_ref, buf1).wait()
        copy(r, s2_ref, buf2).wait()
        return carry

    lax.fori_loop(0, TB, start, 0)
    lax.fori_loop(0, TB, wait, 0)
    rt = route_ref[...].T
    y = rt[:, 2:3] * buf1[...] + rt[:, 3:4] * buf2[...]
    m = mod_ref[...]
    o_ref[...] = _layer_norm(DEEPNORM_ALPHA * x_ref[...] + m[5:6] * y, lnw_ref[...], lnb_ref[...])


def _combine_call(slot1, slot2, yb, route, x1, mod, lnw, lnb):
    t, d = x1.shape
    grid_spec = pltpu.PrefetchScalarGridSpec(
        num_scalar_prefetch=2,
        grid=(t // TB,),
        in_specs=[pl.BlockSpec(memory_space=pl.ANY),
                  pl.BlockSpec((SUBLANES, TB), lambda i, a, b: (0, i)),
                  pl.BlockSpec((TB, d), lambda i, a, b: (i, 0)),
                  pl.BlockSpec((None, SUBLANES, d), lambda i, a, b: (_who(i), 0, 0)),
                  pl.BlockSpec((1, d), lambda i, a, b: (0, 0)),
                  pl.BlockSpec((1, d), lambda i, a, b: (0, 0))],
        out_specs=pl.BlockSpec((TB, d), lambda i, a, b: (i, 0)),
        scratch_shapes=[pltpu.VMEM((TB, d), F32), pltpu.VMEM((TB, d), F32), pltpu.SemaphoreType.DMA(())],
    )
    return pl.pallas_call(
        _combine_kernel,
        grid_spec=grid_spec,
        out_shape=jax.ShapeDtypeStruct((t, d), F32),
        compiler_params=_cparams(1),
        name="moe_combine",
    )(slot1, slot2, yb, route, x1, mod, lnw, lnb)


def _moe_layer(h2, x1, mod, lnw, lnb, w_router, b_router, w_gu, w_down, su):
    t, d = h2.shape
    wr_pad = jnp.pad(w_router, ((0, 0), (0, LANES - N_EXPERTS)))
    route, cnt = _router_call(h2, wr_pad, b_router.reshape(N_EXPERTS, 1), su)
    counts = cnt[:, 0].astype(jnp.int32)
    padded = (counts + MOE_G - 1) // MOE_G * MOE_G
    pend = jnp.cumsum(padded)
    pstart = pend - padded
    e1, e2 = route[0].astype(jnp.int32), route[1].astype(jnp.int32)
    slot1 = pstart[e1] + route[4].astype(jnp.int32)
    slot2 = pstart[e2] + route[5].astype(jnp.int32)
    nblk = -(-(2 * t) // MOE_G) + N_EXPERTS
    block_e = jnp.minimum(jnp.searchsorted(pend, jnp.arange(nblk, dtype=jnp.int32) * MOE_G, side='right'),
                          N_EXPERTS - 1).astype(jnp.int32)
    used = (pend[-1:] // MOE_G).astype(jnp.int32)
    xs = _dispatch_call(slot1, slot2, h2, jnp.zeros((nblk * MOE_G, d), F32))
    yb = _expert_call(block_e, used, xs, w_gu, w_down)
    return _combine_call(slot1, slot2, yb, route, x1, mod, lnw, lnb)


def _attn_head_perm():
    cols = []
    for m in range(ATTN_HEADS // ATTN_KV_HEADS):
        for hq in (m, m + ATTN_HEADS // ATTN_KV_HEADS):
            cols.extend(range(hq * HEAD_DIM, (hq + 1) * HEAD_DIM))
    return np.asarray(cols, np.int32)


def _in_col_perm():
    n_m = 4 * MLSTM_WIDTH
    mg = 2 * 2 * MLSTM_HEADS
    a0 = n_m + mg
    perm = list(range(n_m))
    perm += [a0 + int(j) for j in _attn_head_perm()]
    perm += list(range(a0 + ATTN_Q_WIDTH, a0 + ATTN_Q_WIDTH + 2 * ATTN_KV_WIDTH))
    g0 = a0 + ATTN_Q_WIDTH + 2 * ATTN_KV_WIDTH
    perm += list(range(g0, g0 + 2 * GLA_KEY_WIDTH + 2 * GLA_WIDTH))
    perm += list(range(n_m, n_m + mg))
    perm += list(range(g0 + 2 * GLA_KEY_WIDTH + 2 * GLA_WIDTH, g0 + 2 * GLA_KEY_WIDTH + 2 * GLA_WIDTH + 2 * GLA_RANK))
    return np.asarray(perm, np.int32)


def _rope_tables(seq):
    inv = ROPE_BASE ** (-jnp.arange(ROPE_PAIRS, dtype=F32) / ROPE_PAIRS)
    rows = seq // GRID_W
    row = jnp.repeat(jnp.arange(rows), GRID_W).astype(F32)[:, None] * inv
    col = jnp.tile(jnp.arange(GRID_W), rows).astype(F32)[:, None] * inv
    cos64 = jnp.concatenate([jnp.cos(row), jnp.cos(row), jnp.cos(col), jnp.cos(col)], -1)
    sin64 = jnp.concatenate([-jnp.sin(row), jnp.sin(row), -jnp.sin(col), jnp.sin(col)], -1)
    cos_l = jnp.concatenate([cos64, cos64], -1)
    sin_l = jnp.concatenate([sin64, sin64], -1)
    cos_t = jnp.concatenate([jnp.ones((CTX_LEN, LANES), F32), cos_l], 0)
    sin_t = jnp.concatenate([jnp.zeros((CTX_LEN, LANES), F32), sin_l], 0)
    return cos_t, sin_t


def _block_tri(direction):
    r = np.arange(TB)[:, None]
    c = np.arange(TB)[None, :]
    same = (r // CHUNK) == (c // CHUNK)
    tri = (c <= r) if direction == 0 else (c >= r)
    return jnp.asarray(same & tri, BF16)


def kernel(x, c, ctx, c_ctx, w_ada, b_ada, w_in, mlstm_gate_b, mlstm_norm_w, attn_sink, gla_gate_up, gla_gate_b,
           gla_norm_w, w_out, ln_w, ln_b, ffn_w_gu, ffn_w_down, router_w, router_b, moe_w_gu, moe_w_down):
    seq, d = x.shape[1], x.shape[2]
    depth = w_in.shape[0]
    xt = jnp.concatenate([ctx[0], x[0]], axis=0)

    cvec = jnp.zeros((SUBLANES, d), F32).at[0].set(c_ctx).at[1].set(c[0])
    mods = _mod_call(cvec, w_ada, b_ada)[:, :2].reshape(depth, 2, 6, d)
    mods = jnp.pad(mods, ((0, 0), (0, 0), (0, SUBLANES - 6), (0, 0)))

    perm = _in_col_perm()
    w_in_p = jnp.pad(w_in[:, :, perm], ((0, 0), (0, 0), (0, N_PROJ - perm.shape[0]))).astype(BF16)
    out_perm = np.concatenate([np.arange(MLSTM_WIDTH), MLSTM_WIDTH + _attn_head_perm(),
                               np.arange(MLSTM_WIDTH + ATTN_Q_WIDTH, d)]).astype(np.int32)
    w_out_p = w_out[:, out_perm, :].astype(BF16)
    gate_bias = jnp.pad(mlstm_gate_b.reshape(depth, 1, -1), ((0, 0), (0, 0), (0, N_GATE - 4 * MLSTM_HEADS)))
    gup_pad = jnp.zeros((depth, 2, N_GATE, GLA_KEY_WIDTH), F32)
    for dr in range(2):
        lo = GLA_GATE_OFF + dr * GLA_RANK
        gup_pad = gup_pad.at[:, dr, lo:lo + GLA_RANK, :].set(gla_gate_up[:, dr])
    cos_t, sin_t = _rope_tables(seq)
    tri = [_block_tri(0), _block_tri(1)]
    su = jnp.asarray(np.arange(TB)[:, None] < np.arange(TB)[None, :], BF16)
    hh = np.arange(MLSTM_WIDTH) // MLSTM_DH
    avg = jnp.asarray((hh[:, None] == hh[None, :]) / MLSTM_DH, BF16)
    ffn_gu, ffn_dn = ffn_w_gu.astype(BF16), ffn_w_down.astype(BF16)
    moe_gu, moe_dn = moe_w_gu.astype(BF16), moe_w_down.astype(BF16)

    for l in range(depth):
        is_moe = l % 2 == 1
        main, gates = _inproj_call(xt, mods[l], w_in_p[l], cos_t, sin_t)
        mf = _mlstm_call(main, gates, gate_bias[l], 0)
        mb = _mlstm_call(main, gates, gate_bias[l], 1)
        gf = _gla_call(main, gates, gup_pad[l, 0], gla_gate_b[l, 0:1], tri[0], 0)
        gbk = _gla_call(main, gates, gup_pad[l, 1], gla_gate_b[l, 1:2], tri[1], 1)
        attn = _attn_call(main, attn_sink[l], seq)
        x1, h2 = _outproj_call(mf, mb, main, attn, gf, gbk, mlstm_norm_w[l:l + 1], gla_norm_w[l:l + 1], avg,
                               w_out_p[l], xt, mods[l], ln_w[l, 0:1], ln_b[l, 0:1], F32 if is_moe else BF16)
        if is_moe:
            xt = _moe_layer(h2, x1, mods[l], ln_w[l, 1:2], ln_b[l, 1:2], router_w[l // 2], router_b[l // 2],
                            moe_gu[l // 2], moe_dn[l // 2], su)
        else:
            xt = _ffn_call(h2, ffn_gu[l // 2], ffn_dn[l // 2], x1, mods[l], ln_w[l, 1:2], ln_b[l, 1:2])
    return xt[CTX_LEN:][None]
```

```python
import functools

import jax
import jax.numpy as jnp
import numpy as np
from jax import lax
from jax.experimental import pallas as pl
from jax.experimental.pallas import tpu as pltpu

F32 = jnp.float32
BF16 = jnp.bfloat16
HIGHEST = lax.Precision.HIGHEST

D_MODEL = 1024
SEQ = 16384
DEPTH = 4
GRID_W = 64
CTX_LEN = 256
MLSTM_HEADS = 4
MLSTM_DH = 64
MLSTM_WIDTH = 256
HEAD_DIM = 64
ATTN_HEADS = 8
ATTN_KV_HEADS = 2
ATTN_Q_WIDTH = 512
ATTN_KV_WIDTH = 128
WINDOW = 128
ROPE_BASE = 10000.0
ROPE_PAIRS = 16
GLA_HEADS = 4
GLA_DK = 32
GLA_DV = 64
GLA_WIDTH = 256
GLA_KEY_WIDTH = 128
GLA_RANK = 16
GLA_TAU = 16.0
CHUNK = 64
D_FF = 2816
N_EXPERTS = 8
D_EXPERT = 3584
DEEPNORM_ALPHA = (2 * DEPTH) ** 0.25
LN_EPS = 1e-5

LANES = 128
SUBLANES = 8
VMEM_LIMIT = 56 * 1024 * 1024

TB = 256
N_CHUNKS = TB // CHUNK
FFN_TM = 640
FFN_TF = 256
MOE_G = 512
MOE_TF = 512

C_MQ, C_MK, C_MV, C_MO = 0, 256, 512, 768
C_AQ, C_AK, C_AV = 1024, 1536, 1664
C_GQ, C_GK, C_GV, C_GO = 1792, 1920, 2048, 2304
N_MAIN = 2560
N_GATE = 128
N_PROJ = N_MAIN + N_GATE
GLA_GATE_OFF = 16


def _cparams(n_axes=1):
    return pltpu.CompilerParams(dimension_semantics=("arbitrary",) * n_axes,
                                vmem_limit_bytes=VMEM_LIMIT)


def _sigmoid(x):
    return 1.0 / (1.0 + jnp.exp(-x))


def _log_sigmoid(x):
    return jnp.minimum(x, 0.0) - jnp.log(1.0 + jnp.exp(-jnp.abs(x)))


def _split3(x):
    hi = x.astype(BF16)
    r1 = x - hi.astype(F32)
    mid = r1.astype(BF16)
    lo = (r1 - mid.astype(F32)).astype(BF16)
    return hi, mid, lo


def _dot_exact_rhs(x, m_bf16):
    hi, mid, lo = _split3(x)
    d = lambda a: jnp.dot(a, m_bf16, preferred_element_type=F32)
    return d(hi) + d(mid) + d(lo)


def _dot_exact_lhs(m_bf16, x):
    hi, mid, lo = _split3(x)
    d = lambda a: jnp.dot(m_bf16, a, preferred_element_type=F32)
    return d(hi) + d(mid) + d(lo)


def _layer_norm(v, w, b):
    mu = jnp.mean(v, axis=-1, keepdims=True)
    cen = v - mu
    var = jnp.mean(cen * cen, axis=-1, keepdims=True)
    return cen * lax.rsqrt(var + LN_EPS) * w + b


def _mod_kernel(c_ref, w_ref, b_ref, o_ref):
    c = c_ref[...]
    sc = c * _sigmoid(c)
    o_ref[...] = jnp.dot(sc, w_ref[...], preferred_element_type=F32, precision=HIGHEST) + b_ref[...]


def _mod_call(cvec, w_ada, b_ada):
    depth, d, n = w_ada.shape
    tn = 1536
    return pl.pallas_call(
        _mod_kernel,
        grid=(depth, n // tn),
        in_specs=[pl.BlockSpec((SUBLANES, d), lambda l, j: (0, 0)),
                  pl.BlockSpec((None, d, tn), lambda l, j: (l, 0, j)),
                  pl.BlockSpec((None, 1, tn), lambda l, j: (l, 0, j))],
        out_specs=pl.BlockSpec((None, SUBLANES, tn), lambda l, j: (l, 0, j)),
        out_shape=jax.ShapeDtypeStruct((depth, SUBLANES, n), F32),
        compiler_params=_cparams(2),
        name="ada_mod",
    )(cvec, w_ada, b_ada.reshape(depth, 1, n))


def _who(i):
    return jnp.minimum(i, 1)


def _rope(x, cos, sin_signed, first_half):
    swapped = jnp.where(first_half, pltpu.roll(x, LANES - ROPE_PAIRS, 1), pltpu.roll(x, ROPE_PAIRS, 1))
    return x * cos + swapped * sin_signed


def _inproj_kernel(x_ref, mod_ref, w_ref, cos_ref, sin_ref, main_ref, gate_ref):
    m = mod_ref[...]
    h = x_ref[...] * (1.0 + m[1:2]) + m[0:1]
    p = jnp.dot(h.astype(BF16), w_ref[...], preferred_element_type=F32)
    cos, sin = cos_ref[...], sin_ref[...]
    lane = lax.broadcasted_iota(jnp.int32, (TB, LANES), 1)
    first_half = (lane % (2 * ROPE_PAIRS)) < ROPE_PAIRS
    main_ref[:, :C_AQ] = p[:, :C_AQ].astype(BF16)
    for j in range(ATTN_Q_WIDTH // LANES):
        lo = C_AQ + j * LANES
        main_ref[:, lo:lo + LANES] = (_rope(p[:, lo:lo + LANES], cos, sin, first_half) * HEAD_DIM ** -0.5).astype(BF16)
    main_ref[:, C_AK:C_AV] = _rope(p[:, C_AK:C_AV], cos, sin, first_half).astype(BF16)
    main_ref[:, C_AV:] = p[:, C_AV:N_MAIN].astype(BF16)
    gate_ref[...] = p[:, N_MAIN:]


def _inproj_call(x, mod, w, layer, cos_t, sin_t):
    t, d = x.shape
    nb = t // TB
    return pl.pallas_call(
        _inproj_kernel,
        grid=(nb,),
        in_specs=[pl.BlockSpec((TB, d), lambda i: (i, 0)),
                  pl.BlockSpec((None, SUBLANES, d), lambda i: (_who(i), 0, 0)),
                  pl.BlockSpec((None, d, N_PROJ), lambda i: (layer, 0, 0)),
                  pl.BlockSpec((TB, LANES), lambda i: (i, 0)),
                  pl.BlockSpec((TB, LANES), lambda i: (i, 0))],
        out_specs=[pl.BlockSpec((TB, N_MAIN), lambda i: (i, 0)),
                   pl.BlockSpec((TB, N_GATE), lambda i: (i, 0))],
        out_shape=[jax.ShapeDtypeStruct((t, N_MAIN), BF16),
                   jax.ShapeDtypeStruct((t, N_GATE), F32)],
        compiler_params=_cparams(1),
        name="in_proj",
    )(x, mod, w, cos_t, sin_t)


def _scan_block(i, nb, direction):
    if direction == 0:
        return i
    return jnp.where(i == 0, 0, nb - i)


def _tri_masks(direction):
    r = lax.broadcasted_iota(jnp.int32, (CHUNK, CHUNK), 0)
    c = lax.broadcasted_iota(jnp.int32, (CHUNK, CHUNK), 1)
    if direction == 0:
        return c <= r, r <= c
    return c >= r, r >= c


def _mlstm_kernel(q_ref, k_ref, v_ref, g_ref, gb_ref, tri_ref, exp_ref, o_ref, c_scr, n_scr, m_scr, *, direction):
    @pl.when(pl.program_id(0) == 0)
    def _():
        c_scr[...] = jnp.zeros_like(c_scr)
        n_scr[...] = jnp.zeros_like(n_scr)
        m_scr[...] = jnp.zeros_like(m_scr)

    fwd = direction == 0
    neg_inf = jnp.float32(-jnp.inf)
    order = range(N_CHUNKS) if fwd else range(N_CHUNKS - 1, -1, -1)
    last_of = lambda c: c * CHUNK + (CHUNK - 1 if fwd else 0)

    g = g_ref[...] + gb_ref[...]
    b_f = _dot_exact_lhs(tri_ref[...], _log_sigmoid(g))
    b_i = pltpu.roll(b_f, LANES - MLSTM_HEADS, 1)
    gr = g - b_i
    tok = lax.broadcasted_iota(jnp.int32, (TB, LANES), 0) % CHUNK
    aloc = gr
    for sh in (1, 2, 4, 8, 16, 32):
        if fwd:
            shifted, ok = pltpu.roll(aloc, sh, 0), tok >= sh
        else:
            shifted, ok = pltpu.roll(aloc, TB - sh, 0), tok < CHUNK - sh
        aloc = jnp.maximum(aloc, jnp.where(ok, shifted, neg_inf))
    chunk_max = jnp.concatenate(
        [jnp.broadcast_to(aloc[last_of(c):last_of(c) + 1], (CHUNK, LANES)) for c in range(N_CHUNKS)], axis=0)
    wloc = jnp.exp(gr - chunk_max)
    expand = exp_ref[...]
    aloc_x = _dot_exact_rhs(aloc, expand)
    b_x = _dot_exact_rhs(b_i, expand)
    wloc_x = _dot_exact_rhs(wloc, expand)
    gr_t = gr.T

    lane = lax.broadcasted_iota(jnp.int32, (CHUNK, LANES), 1)
    trow = lax.broadcasted_iota(jnp.int32, (CHUNK, LANES), 0)
    low = lane < MLSTM_DH
    tri_ok = (lane % MLSTM_DH <= trow) if fwd else (lane % MLSTM_DH >= trow)
    lane2 = lax.broadcasted_iota(jnp.int32, (LANES, LANES), 1)
    row2 = lax.broadcasted_iota(jnp.int32, (LANES, LANES), 0)
    same_head = (lane2 < MLSTM_DH) == (row2 < MLSTM_DH)
    bd_ones = jnp.where(same_head, 1.0, 0.0).astype(BF16)
    lane_row = lax.broadcasted_iota(jnp.int32, (1, LANES), 1)
    dn_nt = (((1,), (1,)), ((), ()))

    def pair_blockdiag(x):
        zero = jnp.zeros_like(x)
        return jnp.concatenate([jnp.where(low, x, zero), jnp.where(low, zero, x)], axis=0)

    for p in range(MLSTM_HEADS // 2):
        lp = slice(p * LANES, (p + 1) * LANES)
        ji0 = direction * 2 * MLSTM_HEADS + 2 * p
        bdc, bdn = c_scr[p], n_scr[p]
        m_row = m_scr[p][0:1]
        kt2 = {}
        for c in order:
            ts = slice(c * CHUNK, (c + 1) * CHUNK)
            c2 = c // 2
            ts2 = slice(c2 * 2 * CHUNK, (c2 + 1) * 2 * CHUNK)
            if c2 not in kt2:
                kt2[c2] = (k_ref[ts2, lp].astype(F32) * MLSTM_DH ** -0.5).T.astype(BF16)
            ra = gr_t[ji0:ji0 + 1, ts2]
            rb = gr_t[ji0 + 1:ji0 + 2, ts2]
            if c % 2 == 0:
                g_row = jnp.where(lane_row < MLSTM_DH, ra, pltpu.roll(rb, MLSTM_DH, 1))
            else:
                g_row = jnp.where(lane_row < MLSTM_DH, pltpu.roll(ra, MLSTM_DH, 1), rb)
            al = aloc_x[ts, lp]
            d_loc = jnp.exp(jnp.where(tri_ok, g_row - al, neg_inf))
            qp = q_ref[ts, lp]
            kp = k_ref[ts, lp] * MLSTM_DH ** -0.5
            vp = v_ref[ts, lp]
            s = lax.dot_general(qp, pair_blockdiag(kp), dn_nt, preferred_element_type=F32) * d_loc
            s_hi = s.astype(BF16)
            r1 = jnp.dot(s_hi, jnp.concatenate([pair_blockdiag(vp), bd_ones], axis=1),
                         preferred_element_type=F32)
            den_lo = jnp.dot((s - s_hi.astype(F32)).astype(BF16), bd_ones, preferred_element_type=F32)
            r2 = jnp.dot(qp, jnp.concatenate([bdc, bdn], axis=1).astype(BF16),
                         preferred_element_type=F32)
            a = jnp.maximum(m_row, al)
            corr = jnp.exp(al - a)
            w_inter = jnp.exp(m_row - a)
            num = corr * r1[:, :LANES] + w_inter * r2[:, :LANES]
            den = corr * (r1[:, LANES:] + den_lo) + w_inter * r2[:, LANES:]
            o_ref[ts, lp] = num / jnp.maximum(jnp.abs(den), jnp.exp(-(b_x[ts, lp] + a)))
            last = last_of(c)
            a_last = jnp.maximum(m_row, aloc_x[last:last + 1, lp])
            decay = jnp.exp(m_row - a_last)
            scale = jnp.exp(aloc_x[last:last + 1, lp] - a_last)
            wx2 = wloc_x[ts2, lp]
            in_chunk = (lax.broadcasted_iota(jnp.int32, (2 * CHUNK, LANES), 0) // CHUNK) == (c % 2)
            wv = jnp.where(in_chunk, wx2 * v_ref[ts2, lp].astype(F32), 0.0)
            ww = jnp.where(in_chunk, wx2, 0.0)
            upd = jnp.dot(kt2[c2], jnp.concatenate([wv, ww], axis=1).astype(BF16),
                          preferred_element_type=F32)
            bdc = decay * bdc + scale * jnp.where(same_head, upd[:, :LANES], 0.0)
            bdn = decay * bdn + scale * jnp.where(same_head, upd[:, LANES:], 0.0)
            m_row = b_x[last:last + 1, lp] + a_last
        c_scr[p] = bdc
        n_scr[p] = bdn
        m_scr[p] = jnp.broadcast_to(m_row, (SUBLANES, LANES))


def _mlstm_call(main, gates, gate_bias, tri, expand, direction):
    t = main.shape[0]
    nb = t // TB
    blk = lambda i: _scan_block(i, nb, direction)
    wq = MLSTM_WIDTH
    n_pairs = MLSTM_HEADS // 2
    return pl.pallas_call(
        functools.partial(_mlstm_kernel, direction=direction),
        grid=(nb,),
        in_specs=[pl.BlockSpec((TB, wq), lambda i: (blk(i), C_MQ // wq)),
                  pl.BlockSpec((TB, wq), lambda i: (blk(i), C_MK // wq)),
                  pl.BlockSpec((TB, wq), lambda i: (blk(i), C_MV // wq)),
                  pl.BlockSpec((TB, N_GATE), lambda i: (blk(i), 0)),
                  pl.BlockSpec((1, N_GATE), lambda i: (0, 0)),
                  pl.BlockSpec((TB, TB), lambda i: (0, 0)),
                  pl.BlockSpec((N_GATE, wq), lambda i: (0, 0))],
        out_specs=pl.BlockSpec((TB, wq), lambda i: (blk(i), 0)),
        out_shape=jax.ShapeDtypeStruct((t, wq), F32),
        scratch_shapes=[pltpu.VMEM((n_pairs, LANES, LANES), F32),
                        pltpu.VMEM((n_pairs, LANES, LANES), F32),
                        pltpu.VMEM((n_pairs, SUBLANES, LANES), F32)],
        compiler_params=_cparams(1),
        name=f"mlstm_dir{direction}",
    )(main, main, main, gates, gate_bias, tri, expand)


def _gla_kernel(q_ref, k_ref, v_ref, g_ref, gup_ref, gb_ref, tri_ref, o_ref, s_scr, *, direction):
    @pl.when(pl.program_id(0) == 0)
    def _():
        s_scr[...] = jnp.zeros_like(s_scr)

    z = jnp.dot(g_ref[...], gup_ref[...], preferred_element_type=F32, precision=HIGHEST) + gb_ref[...]
    lg = _log_sigmoid(z) * (1.0 / GLA_TAU)
    b = _dot_exact_lhs(tri_ref[...], lg)
    q = q_ref[...].astype(F32) * GLA_DK ** -0.5
    k = k_ref[...].astype(F32)
    v = v_ref[...]
    qt = q * jnp.exp(b)
    kt_t = (k * jnp.exp(-b)).T
    b_t = b.T
    kdec_parts = []
    for c in range(N_CHUNKS):
        last = c * CHUNK + (CHUNK - 1 if direction == 0 else 0)
        ts = slice(c * CHUNK, (c + 1) * CHUNK)
        kdec_parts.append(k[ts] * jnp.exp(b[last:last + 1] - b[ts]))
    kdec_t = jnp.concatenate(kdec_parts, axis=0).T
    mask, _ = _tri_masks(direction)
    order = range(N_CHUNKS) if direction == 0 else range(N_CHUNKS - 1, -1, -1)

    for h in range(GLA_HEADS):
        ks = slice(h * GLA_DK, (h + 1) * GLA_DK)
        vs = slice(h * GLA_DV, (h + 1) * GLA_DV)
        s_h = s_scr[h]
        for c in order:
            ts = slice(c * CHUNK, (c + 1) * CHUNK)
            last = c * CHUNK + (CHUNK - 1 if direction == 0 else 0)
            qth = qt[ts, ks].astype(BF16)
            vh = v[ts, vs]
            a = jnp.dot(qth, kt_t[ks, ts].astype(BF16), preferred_element_type=F32)
            a = jnp.where(mask, a, 0.0)
            o = (jnp.dot(a.astype(BF16), vh, preferred_element_type=F32)
                 + jnp.dot(qth, s_h.astype(BF16), preferred_element_type=F32))
            o_ref[ts, vs] = o
            dec_col = jnp.exp(b_t[ks, last:last + 1])
            s_h = dec_col * s_h + jnp.dot(kdec_t[ks, ts].astype(BF16), vh, preferred_element_type=F32)
        s_scr[h] = s_h


def _gla_call(main, gates, gup_pad, gb, tri, direction):
    t = main.shape[0]
    nb = t // TB
    blk = lambda i: _scan_block(i, nb, direction)
    return pl.pallas_call(
        functools.partial(_gla_kernel, direction=direction),
        grid=(nb,),
        in_specs=[pl.BlockSpec((TB, GLA_KEY_WIDTH), lambda i: (blk(i), C_GQ // GLA_KEY_WIDTH)),
                  pl.BlockSpec((TB, GLA_KEY_WIDTH), lambda i: (blk(i), C_GK // GLA_KEY_WIDTH)),
                  pl.BlockSpec((TB, GLA_WIDTH), lambda i: (blk(i), C_GV // GLA_WIDTH)),
                  pl.BlockSpec((TB, N_GATE), lambda i: (blk(i), 0)),
                  pl.BlockSpec((N_GATE, GLA_KEY_WIDTH), lambda i: (0, 0)),
                  pl.BlockSpec((1, GLA_KEY_WIDTH), lambda i: (0, 0)),
                  pl.BlockSpec((TB, TB), lambda i: (0, 0))],
        out_specs=pl.BlockSpec((TB, GLA_WIDTH), lambda i: (blk(i), 0)),
        out_shape=jax.ShapeDtypeStruct((t, GLA_WIDTH), F32),
        scratch_shapes=[pltpu.VMEM((GLA_HEADS, GLA_DK, GLA_DV), F32)],
        compiler_params=_cparams(1),
        name=f"gla_dir{direction}",
    )(main, main, main, gates, gup_pad, gb, tri)


def _attn_kernel(sink_ref, q_ref, kp_ref, kc_ref, kn_ref, kx_ref, vp_ref, vc_ref, vn_ref, vx_ref, o_ref, *, seq):
    i = pl.program_id(0)
    half = TB // 2
    n_loc = 2 * TB
    keys = jnp.concatenate([kp_ref[...], kc_ref[...], kn_ref[...], kx_ref[...]], axis=0)
    vals = jnp.concatenate([vp_ref[...], vc_ref[...], vn_ref[...], vx_ref[...]], axis=0)
    n_keys = keys.shape[0]
    lane = lax.broadcasted_iota(jnp.int32, (n_keys, LANES), 1)
    zero = jnp.zeros_like(keys)
    keys_g = [jnp.where(lane < HEAD_DIM, keys, zero), jnp.where(lane >= HEAD_DIM, keys, zero)]
    r = lax.broadcasted_iota(jnp.int32, (TB, n_keys), 0)
    c = lax.broadcasted_iota(jnp.int32, (TB, n_keys), 1)
    rel = c - half - r
    kpos = (i - 1) * TB + c - half
    local_ok = (jnp.abs(rel) <= WINDOW) & (kpos >= 0) & (kpos < seq) & (i > 0)
    valid = local_ok | (c >= n_loc)
    out_lane = lax.broadcasted_iota(jnp.int32, (TB, LANES), 1)
    neg_inf = jnp.float32(-jnp.inf)
    dn = (((1,), (1,)), ((), ()))
    for m in range(ATTN_Q_WIDTH // LANES):
        qm = q_ref[:, m * LANES:(m + 1) * LANES]
        res = []
        for gidx in range(ATTN_KV_HEADS):
            sink = sink_ref[gidx * (ATTN_HEADS // ATTN_KV_HEADS) + m]
            s = lax.dot_general(qm, keys_g[gidx], dn, preferred_element_type=F32)
            s = jnp.where(valid, s, neg_inf)
            mx = jnp.maximum(jnp.max(s, axis=1, keepdims=True), sink)
            p = jnp.exp(s - mx)
            denom = jnp.sum(p, axis=1, keepdims=True) + jnp.exp(sink - mx)
            pv = jnp.dot(p.astype(BF16), vals, preferred_element_type=F32)
            res.append(pv / denom)
        o_ref[:, m * LANES:(m + 1) * LANES] = jnp.where(out_lane < HEAD_DIM, res[0], res[1]).astype(BF16)


def _attn_call(main, sink, seq):
    t = main.shape[0]
    nb = t // TB
    half = TB // 2
    nhb = t // half
    kcol, vcol = C_AK // LANES, C_AV // LANES
    prev_i = lambda i: jnp.maximum(2 * i - 1, 0)
    next_i = lambda i: jnp.minimum(2 * i + 2, nhb - 1)
    grid_spec = pltpu.PrefetchScalarGridSpec(
        num_scalar_prefetch=1,
        grid=(nb,),
        in_specs=[pl.BlockSpec((TB, ATTN_Q_WIDTH), lambda i, s: (i, C_AQ // ATTN_Q_WIDTH)),
                  pl.BlockSpec((half, LANES), lambda i, s: (prev_i(i), kcol)),
                  pl.BlockSpec((TB, LANES), lambda i, s: (i, kcol)),
                  pl.BlockSpec((half, LANES), lambda i, s: (next_i(i), kcol)),
                  pl.BlockSpec((TB, LANES), lambda i, s: (0, kcol)),
                  pl.BlockSpec((half, LANES), lambda i, s: (prev_i(i), vcol)),
                  pl.BlockSpec((TB, LANES), lambda i, s: (i, vcol)),
                  pl.BlockSpec((half, LANES), lambda i, s: (next_i(i), vcol)),
                  pl.BlockSpec((TB, LANES), lambda i, s: (0, vcol))],
        out_specs=pl.BlockSpec((TB, ATTN_Q_WIDTH), lambda i, s: (i, 0)),
    )
    return pl.pallas_call(
        functools.partial(_attn_kernel, seq=seq),
        grid_spec=grid_spec,
        out_shape=jax.ShapeDtypeStruct((t, ATTN_Q_WIDTH), BF16),
        compiler_params=_cparams(1),
        name="window_attn",
    )(sink, main, main, main, main, main, main, main, main, main)


def _head_norm(x, avg_bf16, w):
    mu = _dot_exact_rhs(x, avg_bf16)
    cen = x - mu
    var = _dot_exact_rhs(cen * cen, avg_bf16)
    return cen * lax.rsqrt(var + LN_EPS) * w


def _outproj_kernel(mf_ref, mb_ref, mo_ref, at_ref, gf_ref, gbk_ref, go_ref, mnw_ref, gnw_ref, avg_ref,
                    w_ref, x_ref, mod_ref, lnw_ref, lnb_ref, x1_ref, h2_ref):
    avg = avg_ref[...]
    hm = _head_norm(mf_ref[...] + mb_ref[...], avg, mnw_ref[...]) * _sigmoid(mo_ref[...].astype(F32))
    gate = go_ref[...].astype(F32)
    hg = _head_norm(gf_ref[...] + gbk_ref[...], avg, gnw_ref[...]) * (gate * _sigmoid(gate))
    a0, a1, a2 = MLSTM_WIDTH, MLSTM_WIDTH + ATTN_Q_WIDTH, D_MODEL
    mix = (jnp.dot(hm.astype(BF16), w_ref[:a0], preferred_element_type=F32)
           + jnp.dot(at_ref[...], w_ref[a0:a1], preferred_element_type=F32)
           + jnp.dot(hg.astype(BF16), w_ref[a1:a2], preferred_element_type=F32))
    m = mod_ref[...]
    x1 = _layer_norm(DEEPNORM_ALPHA * x_ref[...] + m[2:3] * mix, lnw_ref[...], lnb_ref[...])
    x1_ref[...] = x1
    h2_ref[...] = (x1 * (1.0 + m[4:5]) + m[3:4]).astype(h2_ref.dtype)


def _outproj_call(mf, mb, main, attn, gf, gbk, mnw, gnw, avg, w_out, layer, x, mod, lnw, lnb, h2_dtype):
    t, d = x.shape
    nb = t // TB
    row = lambda i: (i, 0)
    const = lambda i: (0, 0)
    return pl.pallas_call(
        _outproj_kernel,
        grid=(nb,),
        in_specs=[pl.BlockSpec((TB, MLSTM_WIDTH), row),
                  pl.BlockSpec((TB, MLSTM_WIDTH), row),
                  pl.BlockSpec((TB, MLSTM_WIDTH), lambda i: (i, C_MO // MLSTM_WIDTH)),
                  pl.BlockSpec((TB, ATTN_Q_WIDTH), row),
                  pl.BlockSpec((TB, GLA_WIDTH), row),
                  pl.BlockSpec((TB, GLA_WIDTH), row),
                  pl.BlockSpec((TB, GLA_WIDTH), lambda i: (i, C_GO // GLA_WIDTH)),
                  pl.BlockSpec((1, MLSTM_WIDTH), const),
                  pl.BlockSpec((1, GLA_WIDTH), const),
                  pl.BlockSpec((MLSTM_WIDTH, MLSTM_WIDTH), const),
                  pl.BlockSpec((None, d, d), lambda i: (layer, 0, 0)),
                  pl.BlockSpec((TB, d), row),
                  pl.BlockSpec((None, SUBLANES, d), lambda i: (_who(i), 0, 0)),
                  pl.BlockSpec((1, d), const),
                  pl.BlockSpec((1, d), const)],
        out_specs=[pl.BlockSpec((TB, d), row), pl.BlockSpec((TB, d), row)],
        out_shape=[jax.ShapeDtypeStruct((t, d), F32), jax.ShapeDtypeStruct((t, d), h2_dtype)],
        compiler_params=_cparams(1),
        name="out_proj",
    )(mf, mb, main, attn, gf, gbk, main, mnw, gnw, avg, w_out, x, mod, lnw, lnb)


def _select_mod(mod_ref, k, i, tm):
    rows = i * tm + lax.broadcasted_iota(jnp.int32, (tm, 1), 0)
    return jnp.where(rows < CTX_LEN, mod_ref[0, k:k + 1, :], mod_ref[1, k:k + 1, :])


def _ffn_kernel(h_ref, wg_ref, wu_ref, wd_ref, x_ref, mod_ref, lnw_ref, lnb_ref, o_ref, acc_ref):
    f = pl.program_id(1)
    h = h_ref[...]
    g = jnp.dot(h, wg_ref[...], preferred_element_type=F32)
    u = jnp.dot(h, wu_ref[...], preferred_element_type=F32)
    part = jnp.dot((g * _sigmoid(g) * u).astype(BF16), wd_ref[...], preferred_element_type=F32)

    @pl.when(f == 0)
    def _():
        acc_ref[...] = part

    @pl.when(f > 0)
    def _():
        acc_ref[...] += part

    @pl.when(f == pl.num_programs(1) - 1)
    def _():
        m5 = _select_mod(mod_ref, 5, pl.program_id(0), h_ref.shape[0])
        o_ref[...] = _layer_norm(DEEPNORM_ALPHA * x_ref[...] + m5 * acc_ref[...], lnw_ref[...], lnb_ref[...])


def _ffn_call(h2, w_gu, w_down, layer, x1, mod, lnw, lnb):
    t, d = x1.shape
    nf = D_FF // FFN_TF
    return pl.pallas_call(
        _ffn_kernel,
        grid=(t // FFN_TM, nf),
        in_specs=[pl.BlockSpec((FFN_TM, d), lambda i, f: (i, 0)),
                  pl.BlockSpec((None, d, FFN_TF), lambda i, f: (layer, 0, f)),
                  pl.BlockSpec((None, d, FFN_TF), lambda i, f: (layer, 0, nf + f)),
                  pl.BlockSpec((None, FFN_TF, d), lambda i, f: (layer, f, 0)),
                  pl.BlockSpec((FFN_TM, d), lambda i, f: (i, 0)),
                  pl.BlockSpec((2, SUBLANES, d), lambda i, f: (0, 0, 0)),
                  pl.BlockSpec((1, d), lambda i, f: (0, 0)),
                  pl.BlockSpec((1, d), lambda i, f: (0, 0))],
        out_specs=pl.BlockSpec((FFN_TM, d), lambda i, f: (i, 0)),
        out_shape=jax.ShapeDtypeStruct((t, d), F32),
        scratch_shapes=[pltpu.VMEM((FFN_TM, d), F32)],
        compiler_params=_cparams(2),
        name="dense_ffn",
    )(h2, w_gu, w_gu, w_down, x1, mod, lnw, lnb)


def _router_kernel(h_ref, wr_ref, br_ref, su_ref, route_ref, cnt_ref, base_scr):
    @pl.when(pl.program_id(0) == 0)
    def _():
        base_scr[...] = jnp.zeros_like(base_scr)

    logits = jnp.dot(h_ref[...], wr_ref[...], preferred_element_type=F32, precision=HIGHEST)
    lt = logits.T[:N_EXPERTS] + br_ref[...]
    idx = lax.broadcasted_iota(jnp.int32, lt.shape, 0)
    neg_inf = jnp.float32(-jnp.inf)
    m1 = jnp.max(lt, axis=0, keepdims=True)
    e1 = jnp.min(jnp.where(lt == m1, idx, N_EXPERTS), axis=0, keepdims=True)
    lt2 = jnp.where(idx == e1, neg_inf, lt)
    m2 = jnp.max(lt2, axis=0, keepdims=True)
    e2 = jnp.min(jnp.where(lt2 == m2, idx, N_EXPERTS), axis=0, keepdims=True)
    t2 = jnp.exp(m2 - m1)
    w1 = 1.0 / (1.0 + t2)
    w2 = t2 / (1.0 + t2)
    su = su_ref[...]
    base = base_scr[...][:, 0:1]
    oh1 = (idx == e1).astype(F32)
    cum1 = jnp.dot(oh1.astype(BF16), su, preferred_element_type=F32)
    rank1 = jnp.sum(oh1 * (base + cum1), axis=0, keepdims=True)
    base = base + jnp.sum(oh1, axis=1, keepdims=True)
    oh2 = (idx == e2).astype(F32)
    cum2 = jnp.dot(oh2.astype(BF16), su, preferred_element_type=F32)
    rank2 = jnp.sum(oh2 * (base + cum2), axis=0, keepdims=True)
    base = base + jnp.sum(oh2, axis=1, keepdims=True)
    base_scr[...] = jnp.broadcast_to(base, base_scr.shape)
    cnt_ref[...] = jnp.broadcast_to(base, cnt_ref.shape)
    zero = jnp.zeros_like(w1)
    route_ref[...] = jnp.concatenate(
        [e1.astype(F32), e2.astype(F32), w1, w2, rank1, rank2, zero, zero], axis=0)


def _router_call(h2, wr_pad, br, su):
    t, d = h2.shape
    nb = t // TB
    return pl.pallas_call(
        _router_kernel,
        grid=(nb,),
        in_specs=[pl.BlockSpec((TB, d), lambda i: (i, 0)),
                  pl.BlockSpec((d, LANES), lambda i: (0, 0)),
                  pl.BlockSpec((N_EXPERTS, 1), lambda i: (0, 0)),
                  pl.BlockSpec((TB, TB), lambda i: (0, 0))],
        out_specs=[pl.BlockSpec((SUBLANES, TB), lambda i: (0, i)),
                   pl.BlockSpec((N_EXPERTS, LANES), lambda i: (0, 0))],
        out_shape=[jax.ShapeDtypeStruct((SUBLANES, t), F32),
                   jax.ShapeDtypeStruct((N_EXPERTS, LANES), F32)],
        scratch_shapes=[pltpu.VMEM((N_EXPERTS, LANES), F32)],
        compiler_params=_cparams(1),
        name="moe_router",
    )(h2, wr_pad, br, su)


def _dispatch_kernel(s1_ref, s2_ref, h_ref, xs_in_ref, xs_ref, sem):
    del xs_in_ref
    i = pl.program_id(0)

    def copy(r, slot_ref):
        return pltpu.make_async_copy(h_ref.at[pl.ds(r, 1)], xs_ref.at[pl.ds(slot_ref[i * TB + r], 1)], sem)

    def start(r, carry):
        copy(r, s1_ref).start()
        copy(r, s2_ref).start()
        return carry

    def wait(r, carry):
        copy(r, s1_ref).wait()
        copy(r, s2_ref).wait()
        return carry

    lax.fori_loop(0, TB, start, 0)
    lax.fori_loop(0, TB, wait, 0)


def _dispatch_call(slot1, slot2, h2, xs_init):
    t, d = h2.shape
    grid_spec = pltpu.PrefetchScalarGridSpec(
        num_scalar_prefetch=2,
        grid=(t // TB,),
        in_specs=[pl.BlockSpec((TB, d), lambda i, a, b: (i, 0)),
                  pl.BlockSpec(memory_space=pl.ANY)],
        out_specs=pl.BlockSpec(memory_space=pl.ANY),
        scratch_shapes=[pltpu.SemaphoreType.DMA(())],
    )
    return pl.pallas_call(
        _dispatch_kernel,
        grid_spec=grid_spec,
        out_shape=jax.ShapeDtypeStruct(xs_init.shape, xs_init.dtype),
        input_output_aliases={3: 0},
        compiler_params=_cparams(1),
        name="moe_dispatch",
    )(slot1, slot2, h2, xs_init)


def _expert_kernel(be_ref, used_ref, x_ref, wg_ref, wu_ref, wd_ref, y_ref, xb_scr, acc_scr):
    b, f = pl.program_id(0), pl.program_id(1)
    live = b < used_ref[0]

    @pl.when(live & (f == 0))
    def _():
        xb_scr[...] = x_ref[...].astype(BF16)

    @pl.when(live)
    def _():
        x = xb_scr[...]
        g = jnp.dot(x, wg_ref[...], preferred_element_type=F32)
        u = jnp.dot(x, wu_ref[...], preferred_element_type=F32)
        part = jnp.dot((g * _sigmoid(g) * u).astype(BF16), wd_ref[...], preferred_element_type=F32)

        @pl.when(f == 0)
        def _():
            acc_scr[...] = part

        @pl.when(f > 0)
        def _():
            acc_scr[...] += part

    @pl.when(f == pl.num_programs(1) - 1)
    def _():
        @pl.when(live)
        def _():
            y_ref[...] = acc_scr[...]

        @pl.when(jnp.logical_not(live))
        def _():
            y_ref[...] = jnp.zeros_like(y_ref)


def _expert_call(block_e, used, xs, w_gu, w_down, layer):
    ns, d = xs.shape
    nblk = ns // MOE_G
    nf = D_EXPERT // MOE_TF

    def f_eff(b, f, used_ref):
        return jnp.where(b < used_ref[0], f, nf - 1)

    grid_spec = pltpu.PrefetchScalarGridSpec(
        num_scalar_prefetch=2,
        grid=(nblk, nf),
        in_specs=[pl.BlockSpec((MOE_G, d), lambda b, f, be, us: (b, 0)),
                  pl.BlockSpec((None, None, d, MOE_TF), lambda b, f, be, us: (layer, be[b], 0, f_eff(b, f, us))),
                  pl.BlockSpec((None, None, d, MOE_TF),
                               lambda b, f, be, us: (layer, be[b], 0, nf + f_eff(b, f, us))),
                  pl.BlockSpec((None, None, MOE_TF, d), lambda b, f, be, us: (layer, be[b], f_eff(b, f, us), 0))],
        out_specs=pl.BlockSpec((MOE_G, d), lambda b, f, be, us: (b, 0)),
        scratch_shapes=[pltpu.VMEM((MOE_G, d), BF16), pltpu.VMEM((MOE_G, d), F32)],
    )
    return pl.pallas_call(
        _expert_kernel,
        grid_spec=grid_spec,
        out_shape=jax.ShapeDtypeStruct((ns, d), F32),
        compiler_params=_cparams(2),
        name="moe_experts",
    )(block_e, used, xs, w_gu, w_gu, w_down)


def _combine_kernel(s1_ref, s2_ref, yb_ref, route_ref, x_ref, mod_ref, lnw_ref, lnb_ref, o_ref, buf1, buf2, sem):
    i = pl.program_id(0)

    def copy(r, slot_ref, buf):
        return pltpu.make_async_copy(yb_ref.at[pl.ds(slot_ref[i * TB + r], 1)], buf.at[pl.ds(r, 1)], sem)

    def start(r, carry):
        copy(r, s1_ref, buf1).start()
        copy(r, s2_ref, buf2).start()
        return carry

    def wait(r, carry):
        copy(r, s1_ref, buf1).wait()
        copy(r, s2_ref, buf2).wait()
        return carry

    lax.fori_loop(0, TB, start, 0)
    lax.fori_loop(0, TB, wait, 0)
    rt = route_ref[...].T
    y = rt[:, 2:3] * buf1[...] + rt[:, 3:4] * buf2[...]
    m = mod_ref[...]
    o_ref[...] = _layer_norm(DEEPNORM_ALPHA * x_ref[...] + m[5:6] * y, lnw_ref[...], lnb_ref[...])


def _combine_call(slot1, slot2, yb, route, x1, mod, lnw, lnb, latent_only):
    t, d = x1.shape
    if latent_only:
        out_rows, out_map = t - CTX_LEN, lambda i, a, b: (jnp.maximum(i - 1, 0), 0)
    else:
        out_rows, out_map = t, lambda i, a, b: (i, 0)
    grid_spec = pltpu.PrefetchScalarGridSpec(
        num_scalar_prefetch=2,
        grid=(t // TB,),
        in_specs=[pl.BlockSpec(memory_space=pl.ANY),
                  pl.BlockSpec((SUBLANES, TB), lambda i, a, b: (0, i)),
                  pl.BlockSpec((TB, d), lambda i, a, b: (i, 0)),
                  pl.BlockSpec((None, SUBLANES, d), lambda i, a, b: (_who(i), 0, 0)),
                  pl.BlockSpec((1, d), lambda i, a, b: (0, 0)),
                  pl.BlockSpec((1, d), lambda i, a, b: (0, 0))],
        out_specs=pl.BlockSpec((TB, d), out_map),
        scratch_shapes=[pltpu.VMEM((TB, d), F32), pltpu.VMEM((TB, d), F32), pltpu.SemaphoreType.DMA(())],
    )
    return pl.pallas_call(
        _combine_kernel,
        grid_spec=grid_spec,
        out_shape=jax.ShapeDtypeStruct((out_rows, d), F32),
        compiler_params=_cparams(1),
        name="moe_combine",
    )(slot1, slot2, yb, route, x1, mod, lnw, lnb)


def _moe_layer(h2, x1, mod, lnw, lnb, w_router, b_router, w_gu, w_down, layer, su, latent_only):
    t, d = h2.shape
    wr_pad = jnp.pad(w_router, ((0, 0), (0, LANES - N_EXPERTS)))
    route, cnt = _router_call(h2, wr_pad, b_router.reshape(N_EXPERTS, 1), su)
    counts = cnt[:, 0].astype(jnp.int32)
    padded = (counts + MOE_G - 1) // MOE_G * MOE_G
    pend = jnp.cumsum(padded)
    pstart = pend - padded
    e1, e2 = route[0].astype(jnp.int32), route[1].astype(jnp.int32)
    slot1 = pstart[e1] + route[4].astype(jnp.int32)
    slot2 = pstart[e2] + route[5].astype(jnp.int32)
    nblk = -(-(2 * t) // MOE_G) + N_EXPERTS
    blk_start = jnp.arange(nblk, dtype=jnp.int32) * MOE_G
    block_e = jnp.minimum(jnp.sum((pend[None, :] <= blk_start[:, None]).astype(jnp.int32), axis=1), N_EXPERTS - 1)
    used = (pend[-1:] // MOE_G).astype(jnp.int32)
    xs = _dispatch_call(slot1, slot2, h2, jnp.zeros((nblk * MOE_G, d), F32))
    yb = _expert_call(block_e, used, xs, w_gu, w_down, layer)
    return _combine_call(slot1, slot2, yb, route, x1, mod, lnw, lnb, latent_only)


def _attn_head_perm():
    cols = []
    for m in range(ATTN_HEADS // ATTN_KV_HEADS):
        for hq in (m, m + ATTN_HEADS // ATTN_KV_HEADS):
            cols.extend(range(hq * HEAD_DIM, (hq + 1) * HEAD_DIM))
    return np.asarray(cols, np.int32)


def _in_col_perm():
    n_m = 4 * MLSTM_WIDTH
    mg = 2 * 2 * MLSTM_HEADS
    a0 = n_m + mg
    perm = list(range(n_m))
    perm += [a0 + int(j) for j in _attn_head_perm()]
    perm += list(range(a0 + ATTN_Q_WIDTH, a0 + ATTN_Q_WIDTH + 2 * ATTN_KV_WIDTH))
    g0 = a0 + ATTN_Q_WIDTH + 2 * ATTN_KV_WIDTH
    perm += list(range(g0, g0 + 2 * GLA_KEY_WIDTH + 2 * GLA_WIDTH))
    perm += list(range(n_m, n_m + mg))
    perm += list(range(g0 + 2 * GLA_KEY_WIDTH + 2 * GLA_WIDTH, g0 + 2 * GLA_KEY_WIDTH + 2 * GLA_WIDTH + 2 * GLA_RANK))
    return np.asarray(perm, np.int32)


def _take_runs(w, perm, axis):
    runs, start = [], 0
    for j in range(1, len(perm) + 1):
        if j == len(perm) or perm[j] != perm[j - 1] + 1:
            runs.append(lax.slice_in_dim(w, int(perm[start]), int(perm[j - 1]) + 1, axis=axis))
            start = j
    return jnp.concatenate(runs, axis=axis)


def _rope_tables(seq):
    inv = ROPE_BASE ** (-jnp.arange(ROPE_PAIRS, dtype=F32) / ROPE_PAIRS)
    rows = seq // GRID_W
    ang_r = jnp.arange(rows).astype(F32)[:, None] * inv
    ang_c = jnp.arange(GRID_W).astype(F32)[:, None] * inv
    rep = lambda a: jnp.repeat(a, GRID_W, axis=0)
    til = lambda a: jnp.tile(a, (rows, 1))
    cos_r, sin_r, cos_c, sin_c = rep(jnp.cos(ang_r)), rep(jnp.sin(ang_r)), til(jnp.cos(ang_c)), til(jnp.sin(ang_c))
    cos64 = jnp.concatenate([cos_r, cos_r, cos_c, cos_c], -1)
    sin64 = jnp.concatenate([-sin_r, sin_r, -sin_c, sin_c], -1)
    cos_l = jnp.concatenate([cos64, cos64], -1)
    sin_l = jnp.concatenate([sin64, sin64], -1)
    cos_t = jnp.concatenate([jnp.ones((CTX_LEN, LANES), F32), cos_l], 0)
    sin_t = jnp.concatenate([jnp.zeros((CTX_LEN, LANES), F32), sin_l], 0)
    return cos_t, sin_t


def _block_tri(direction):
    r = np.arange(TB)[:, None]
    c = np.arange(TB)[None, :]
    same = (r // CHUNK) == (c // CHUNK)
    tri = (c <= r) if direction == 0 else (c >= r)
    return jnp.asarray(same & tri, BF16)


def kernel(x, c, ctx, c_ctx, w_ada, b_ada, w_in, mlstm_gate_b, mlstm_norm_w, attn_sink, gla_gate_up, gla_gate_b,
           gla_norm_w, w_out, ln_w, ln_b, ffn_w_gu, ffn_w_down, router_w, router_b, moe_w_gu, moe_w_down):
    seq, d = x.shape[1], x.shape[2]
    depth = w_in.shape[0]
    xt = jnp.concatenate([ctx[0], x[0]], axis=0)

    cvec = jnp.zeros((SUBLANES, d), F32).at[0].set(c_ctx).at[1].set(c[0])
    mods = _mod_call(cvec, w_ada, b_ada)[:, :2].reshape(depth, 2, 6, d)
    mods = jnp.pad(mods, ((0, 0), (0, 0), (0, SUBLANES - 6), (0, 0)))

    perm = _in_col_perm()
    w_in_p = jnp.concatenate([_take_runs(w_in, perm, 2).astype(BF16),
                              jnp.zeros((depth, d, N_PROJ - perm.shape[0]), BF16)], axis=2)
    out_perm = np.concatenate([np.arange(MLSTM_WIDTH), MLSTM_WIDTH + _attn_head_perm(),
                               np.arange(MLSTM_WIDTH + ATTN_Q_WIDTH, d)]).astype(np.int32)
    w_out_p = _take_runs(w_out, out_perm, 1).astype(BF16)
    gate_bias = jnp.pad(mlstm_gate_b.reshape(depth, 1, -1), ((0, 0), (0, 0), (0, N_GATE - 4 * MLSTM_HEADS)))
    gup_pad = jnp.zeros((depth, 2, N_GATE, GLA_KEY_WIDTH), F32)
    for dr in range(2):
        lo = GLA_GATE_OFF + dr * GLA_RANK
        gup_pad = gup_pad.at[:, dr, lo:lo + GLA_RANK, :].set(gla_gate_up[:, dr])
    cos_t, sin_t = _rope_tables(seq)
    tri = [_block_tri(0), _block_tri(1)]
    su = jnp.asarray(np.arange(TB)[:, None] < np.arange(TB)[None, :], BF16)
    hh = np.arange(MLSTM_WIDTH) // MLSTM_DH
    avg = jnp.asarray((hh[:, None] == hh[None, :]) / MLSTM_DH, BF16)
    ffn_gu, ffn_dn = ffn_w_gu.astype(BF16), ffn_w_down.astype(BF16)
    moe_gu, moe_dn = moe_w_gu.astype(BF16), moe_w_down.astype(BF16)

    expand = []
    for dr in range(2):
        e = np.zeros((N_GATE, MLSTM_WIDTH), np.float32)
        for h in range(MLSTM_HEADS):
            e[dr * 2 * MLSTM_HEADS + h, h * MLSTM_DH:(h + 1) * MLSTM_DH] = 1.0
        expand.append(jnp.asarray(e, BF16))

    for l in range(depth):
        is_moe = l % 2 == 1
        last = l == depth - 1
        main, gates = _inproj_call(xt, mods[l], w_in_p, l, cos_t, sin_t)
        mf = _mlstm_call(main, gates, gate_bias[l], tri[0], expand[0], 0)
        mb = _mlstm_call(main, gates, gate_bias[l], tri[1], expand[1], 1)
        gf = _gla_call(main, gates, gup_pad[l, 0], gla_gate_b[l, 0:1], tri[0], 0)
        gbk = _gla_call(main, gates, gup_pad[l, 1], gla_gate_b[l, 1:2], tri[1], 1)
        attn = _attn_call(main, attn_sink[l], seq)
        x1, h2 = _outproj_call(mf, mb, main, attn, gf, gbk, mlstm_norm_w[l:l + 1], gla_norm_w[l:l + 1], avg,
                               w_out_p, l, xt, mods[l], ln_w[l, 0:1], ln_b[l, 0:1], F32 if is_moe else BF16)
        if is_moe:
            xt = _moe_layer(h2, x1, mods[l], ln_w[l, 1:2], ln_b[l, 1:2], router_w[l // 2], router_b[l // 2],
                            moe_gu, moe_dn, l // 2, su, latent_only=last)
        else:
            xt = _ffn_call(h2, ffn_gu, ffn_dn, l // 2, x1, mods[l], ln_w[l, 1:2], ln_b[l, 1:2])
    return (xt if depth % 2 == 0 else xt[CTX_LEN:])[None]
```

```python
import functools

import jax
import jax.numpy as jnp
import numpy as np
from jax import lax
from jax.experimental import pallas as pl
from jax.experimental.pallas import tpu as pltpu

F32 = jnp.float32
BF16 = jnp.bfloat16
HIGHEST = lax.Precision.HIGHEST

D_MODEL = 1024
SEQ = 16384
DEPTH = 4
GRID_W = 64
CTX_LEN = 256
MLSTM_HEADS = 4
MLSTM_DH = 64
MLSTM_WIDTH = 256
HEAD_DIM = 64
ATTN_HEADS = 8
ATTN_KV_HEADS = 2
ATTN_Q_WIDTH = 512
ATTN_KV_WIDTH = 128
WINDOW = 128
ROPE_BASE = 10000.0
ROPE_PAIRS = 16
GLA_HEADS = 4
GLA_DK = 32
GLA_DV = 64
GLA_WIDTH = 256
GLA_KEY_WIDTH = 128
GLA_RANK = 16
GLA_TAU = 16.0
CHUNK = 64
D_FF = 2816
N_EXPERTS = 8
D_EXPERT = 3584
DEEPNORM_ALPHA = (2 * DEPTH) ** 0.25
LN_EPS = 1e-5

LANES = 128
SUBLANES = 8
VMEM_LIMIT = 56 * 1024 * 1024

TB = 256
N_CHUNKS = TB // CHUNK
FFN_TM = 640
MOE_G = 512
MOE_TF = 1792

C_MQ, C_MK, C_MV, C_MO = 0, 256, 512, 768
C_AQ, C_AK, C_AV = 1024, 1536, 1664
C_GQ, C_GK, C_GV, C_GO = 1792, 1920, 2048, 2304
N_MAIN = 2560
N_GATE = 128
N_PROJ = N_MAIN + N_GATE
GLA_GATE_OFF = 16


def _cparams(n_axes=1):
    return pltpu.CompilerParams(dimension_semantics=("arbitrary",) * n_axes,
                                vmem_limit_bytes=VMEM_LIMIT)


def _sigmoid(x):
    return 1.0 / (1.0 + jnp.exp(-x))


def _log_sigmoid(x):
    return jnp.minimum(x, 0.0) - jnp.log(1.0 + jnp.exp(-jnp.abs(x)))


def _split3(x):
    hi = x.astype(BF16)
    r1 = x - hi.astype(F32)
    mid = r1.astype(BF16)
    lo = (r1 - mid.astype(F32)).astype(BF16)
    return hi, mid, lo


def _dot_exact_rhs(x, m_bf16):
    hi, mid, lo = _split3(x)
    d = lambda a: jnp.dot(a, m_bf16, preferred_element_type=F32)
    return d(hi) + d(mid) + d(lo)


def _dot_exact_lhs(m_bf16, x):
    hi, mid, lo = _split3(x)
    d = lambda a: jnp.dot(m_bf16, a, preferred_element_type=F32)
    return d(hi) + d(mid) + d(lo)


def _layer_norm(v, w, b):
    mu = jnp.mean(v, axis=-1, keepdims=True)
    cen = v - mu
    var = jnp.mean(cen * cen, axis=-1, keepdims=True)
    return cen * lax.rsqrt(var + LN_EPS) * w + b


def _mod_kernel(c_ref, w_ref, b_ref, o_ref):
    c = c_ref[...]
    sc = c * _sigmoid(c)
    o_ref[...] = jnp.dot(sc, w_ref[...], preferred_element_type=F32, precision=HIGHEST) + b_ref[...]


def _mod_call(cvec, w_ada, b_ada):
    depth, d, n = w_ada.shape
    tn = 1536
    return pl.pallas_call(
        _mod_kernel,
        grid=(depth, n // tn),
        in_specs=[pl.BlockSpec((SUBLANES, d), lambda l, j: (0, 0)),
                  pl.BlockSpec((None, d, tn), lambda l, j: (l, 0, j)),
                  pl.BlockSpec((None, 1, tn), lambda l, j: (l, 0, j))],
        out_specs=pl.BlockSpec((None, SUBLANES, tn), lambda l, j: (l, 0, j)),
        out_shape=jax.ShapeDtypeStruct((depth, SUBLANES, n), F32),
        compiler_params=_cparams(2),
        name="ada_mod",
    )(cvec, w_ada, b_ada.reshape(depth, 1, n))


def _who(i):
    return jnp.minimum(i, 1)


def _rope(x, cos, sin_signed, first_half):
    swapped = jnp.where(first_half, pltpu.roll(x, LANES - ROPE_PAIRS, 1), pltpu.roll(x, ROPE_PAIRS, 1))
    return x * cos + swapped * sin_signed


def _inproj_kernel(x_ref, mod_ref, w_ref, cos_ref, sin_ref, main_ref, gate_ref):
    m = mod_ref[...]
    h = x_ref[...] * (1.0 + m[1:2]) + m[0:1]
    p = jnp.dot(h.astype(BF16), w_ref[...], preferred_element_type=F32)
    cos, sin = cos_ref[...], sin_ref[...]
    lane = lax.broadcasted_iota(jnp.int32, (TB, LANES), 1)
    first_half = (lane % (2 * ROPE_PAIRS)) < ROPE_PAIRS
    main_ref[:, :C_AQ] = p[:, :C_AQ].astype(BF16)
    for j in range(ATTN_Q_WIDTH // LANES):
        lo = C_AQ + j * LANES
        main_ref[:, lo:lo + LANES] = (_rope(p[:, lo:lo + LANES], cos, sin, first_half) * HEAD_DIM ** -0.5).astype(BF16)
    main_ref[:, C_AK:C_AV] = _rope(p[:, C_AK:C_AV], cos, sin, first_half).astype(BF16)
    main_ref[:, C_AV:] = p[:, C_AV:N_MAIN].astype(BF16)
    gate_ref[...] = p[:, N_MAIN:]


def _inproj_call(x, mod, w, layer, cos_t, sin_t):
    t, d = x.shape
    nb = t // TB
    return pl.pallas_call(
        _inproj_kernel,
        grid=(nb,),
        in_specs=[pl.BlockSpec((TB, d), lambda i: (i, 0)),
                  pl.BlockSpec((None, SUBLANES, d), lambda i: (_who(i), 0, 0)),
                  pl.BlockSpec((None, d, N_PROJ), lambda i: (layer, 0, 0), pipeline_mode=pl.Buffered(1)),
                  pl.BlockSpec((TB, LANES), lambda i: (i, 0)),
                  pl.BlockSpec((TB, LANES), lambda i: (i, 0))],
        out_specs=[pl.BlockSpec((TB, N_MAIN), lambda i: (i, 0)),
                   pl.BlockSpec((TB, N_GATE), lambda i: (i, 0))],
        out_shape=[jax.ShapeDtypeStruct((t, N_MAIN), BF16),
                   jax.ShapeDtypeStruct((t, N_GATE), F32)],
        compiler_params=_cparams(1),
        name="in_proj",
    )(x, mod, w, cos_t, sin_t)


def _scan_block(i, nb, direction):
    if direction == 0:
        return i
    return jnp.where(i == 0, 0, nb - i)


def _tri_masks(direction):
    r = lax.broadcasted_iota(jnp.int32, (CHUNK, CHUNK), 0)
    c = lax.broadcasted_iota(jnp.int32, (CHUNK, CHUNK), 1)
    if direction == 0:
        return c <= r, r <= c
    return c >= r, r >= c


def _mlstm_kernel(q_ref, k_ref, v_ref, g_ref, gb_ref, tri_ref, exp_ref, o_ref, c_scr, n_scr, m_scr, *, direction):
    @pl.when(pl.program_id(0) == 0)
    def _():
        c_scr[...] = jnp.zeros_like(c_scr)
        n_scr[...] = jnp.zeros_like(n_scr)
        m_scr[...] = jnp.zeros_like(m_scr)

    fwd = direction == 0
    neg_inf = jnp.float32(-jnp.inf)
    order = range(N_CHUNKS) if fwd else range(N_CHUNKS - 1, -1, -1)
    last_of = lambda c: c * CHUNK + (CHUNK - 1 if fwd else 0)

    g = g_ref[...] + gb_ref[...]
    b_f = _dot_exact_lhs(tri_ref[...], _log_sigmoid(g))
    b_i = pltpu.roll(b_f, LANES - MLSTM_HEADS, 1)
    gr = g - b_i
    tok = lax.broadcasted_iota(jnp.int32, (TB, LANES), 0) % CHUNK
    aloc = gr
    for sh in (1, 2, 4, 8, 16, 32):
        if fwd:
            shifted, ok = pltpu.roll(aloc, sh, 0), tok >= sh
        else:
            shifted, ok = pltpu.roll(aloc, TB - sh, 0), tok < CHUNK - sh
        aloc = jnp.maximum(aloc, jnp.where(ok, shifted, neg_inf))
    chunk_max = jnp.concatenate(
        [jnp.broadcast_to(aloc[last_of(c):last_of(c) + 1], (CHUNK, LANES)) for c in range(N_CHUNKS)], axis=0)
    wloc = jnp.exp(gr - chunk_max)
    expand = exp_ref[...]
    aloc_x = _dot_exact_rhs(aloc, expand)
    b_x = _dot_exact_rhs(b_i, expand)
    wloc_x = _dot_exact_rhs(wloc, expand)
    gr_t = gr.T

    lane = lax.broadcasted_iota(jnp.int32, (CHUNK, LANES), 1)
    trow = lax.broadcasted_iota(jnp.int32, (CHUNK, LANES), 0)
    low = lane < MLSTM_DH
    tri_ok = (lane % MLSTM_DH <= trow) if fwd else (lane % MLSTM_DH >= trow)
    lane2 = lax.broadcasted_iota(jnp.int32, (LANES, LANES), 1)
    row2 = lax.broadcasted_iota(jnp.int32, (LANES, LANES), 0)
    same_head = (lane2 < MLSTM_DH) == (row2 < MLSTM_DH)
    bd_ones = jnp.where(same_head, 1.0, 0.0).astype(BF16)
    lane_row = lax.broadcasted_iota(jnp.int32, (1, LANES), 1)
    dn_nt = (((1,), (1,)), ((), ()))

    def pair_blockdiag(x):
        zero = jnp.zeros_like(x)
        return jnp.concatenate([jnp.where(low, x, zero), jnp.where(low, zero, x)], axis=0)

    for p in range(MLSTM_HEADS // 2):
        lp = slice(p * LANES, (p + 1) * LANES)
        ji0 = direction * 2 * MLSTM_HEADS + 2 * p
        bdc, bdn = c_scr[p], n_scr[p]
        m_row = m_scr[p][0:1]
        kt2 = {}
        for c in order:
            ts = slice(c * CHUNK, (c + 1) * CHUNK)
            c2 = c // 2
            ts2 = slice(c2 * 2 * CHUNK, (c2 + 1) * 2 * CHUNK)
            if c2 not in kt2:
                kt2[c2] = (k_ref[ts2, lp].astype(F32) * MLSTM_DH ** -0.5).T.astype(BF16)
            ra = gr_t[ji0:ji0 + 1, ts2]
            rb = gr_t[ji0 + 1:ji0 + 2, ts2]
            if c % 2 == 0:
                g_row = jnp.where(lane_row < MLSTM_DH, ra, pltpu.roll(rb, MLSTM_DH, 1))
            else:
                g_row = jnp.where(lane_row < MLSTM_DH, pltpu.roll(ra, MLSTM_DH, 1), rb)
            al = aloc_x[ts, lp]
            d_loc = jnp.exp(jnp.where(tri_ok, g_row - al, neg_inf))
            qp = q_ref[ts, lp]
            kp = k_ref[ts, lp] * MLSTM_DH ** -0.5
            vp = v_ref[ts, lp]
            s = lax.dot_general(qp, pair_blockdiag(kp), dn_nt, preferred_element_type=F32) * d_loc
            s_hi = s.astype(BF16)
            r1 = jnp.dot(s_hi, jnp.concatenate([pair_blockdiag(vp), bd_ones], axis=1),
                         preferred_element_type=F32)
            den_lo = jnp.dot((s - s_hi.astype(F32)).astype(BF16), bd_ones, preferred_element_type=F32)
            r2 = jnp.dot(qp, jnp.concatenate([bdc, bdn], axis=1).astype(BF16),
                         preferred_element_type=F32)
            a = jnp.maximum(m_row, al)
            corr = jnp.exp(al - a)
            w_inter = jnp.exp(m_row - a)
            num = corr * r1[:, :LANES] + w_inter * r2[:, :LANES]
            den = corr * (r1[:, LANES:] + den_lo) + w_inter * r2[:, LANES:]
            o_ref[ts, lp] = num / jnp.maximum(jnp.abs(den), jnp.exp(-(b_x[ts, lp] + a)))
            last = last_of(c)
            a_last = jnp.maximum(m_row, aloc_x[last:last + 1, lp])
            decay = jnp.exp(m_row - a_last)
            scale = jnp.exp(aloc_x[last:last + 1, lp] - a_last)
            wx2 = wloc_x[ts2, lp]
            in_chunk = (lax.broadcasted_iota(jnp.int32, (2 * CHUNK, LANES), 0) // CHUNK) == (c % 2)
            wv = jnp.where(in_chunk, wx2 * v_ref[ts2, lp].astype(F32), 0.0)
            ww = jnp.where(in_chunk, wx2, 0.0)
            upd = jnp.dot(kt2[c2], jnp.concatenate([wv, ww], axis=1).astype(BF16),
                          preferred_element_type=F32)
            bdc = decay * bdc + scale * jnp.where(same_head, upd[:, :LANES], 0.0)
            bdn = decay * bdn + scale * jnp.where(same_head, upd[:, LANES:], 0.0)
            m_row = b_x[last:last + 1, lp] + a_last
        c_scr[p] = bdc
        n_scr[p] = bdn
        m_scr[p] = jnp.broadcast_to(m_row, (SUBLANES, LANES))


def _mlstm_call(main, gates, gate_bias, tri, expand, direction):
    t = main.shape[0]
    nb = t // TB
    blk = lambda i: _scan_block(i, nb, direction)
    wq = MLSTM_WIDTH
    n_pairs = MLSTM_HEADS // 2
    return pl.pallas_call(
        functools.partial(_mlstm_kernel, direction=direction),
        grid=(nb,),
        in_specs=[pl.BlockSpec((TB, wq), lambda i: (blk(i), C_MQ // wq)),
                  pl.BlockSpec((TB, wq), lambda i: (blk(i), C_MK // wq)),
                  pl.BlockSpec((TB, wq), lambda i: (blk(i), C_MV // wq)),
                  pl.BlockSpec((TB, N_GATE), lambda i: (blk(i), 0)),
                  pl.BlockSpec((1, N_GATE), lambda i: (0, 0)),
                  pl.BlockSpec((TB, TB), lambda i: (0, 0)),
                  pl.BlockSpec((N_GATE, wq), lambda i: (0, 0))],
        out_specs=pl.BlockSpec((TB, wq), lambda i: (blk(i), 0)),
        out_shape=jax.ShapeDtypeStruct((t, wq), F32),
        scratch_shapes=[pltpu.VMEM((n_pairs, LANES, LANES), F32),
                        pltpu.VMEM((n_pairs, LANES, LANES), F32),
                        pltpu.VMEM((n_pairs, SUBLANES, LANES), F32)],
        compiler_params=_cparams(1),
        name=f"mlstm_dir{direction}",
    )(main, main, main, gates, gate_bias, tri, expand)


def _gla_kernel(q_ref, k_ref, v_ref, g_ref, gup_ref, gb_ref, tri_ref, o_ref, s_scr, *, direction):
    @pl.when(pl.program_id(0) == 0)
    def _():
        s_scr[...] = jnp.zeros_like(s_scr)

    z = jnp.dot(g_ref[...], gup_ref[...], preferred_element_type=F32, precision=HIGHEST) + gb_ref[...]
    lg = _log_sigmoid(z) * (1.0 / GLA_TAU)
    b = _dot_exact_lhs(tri_ref[...], lg)
    q = q_ref[...].astype(F32) * GLA_DK ** -0.5
    k = k_ref[...].astype(F32)
    v = v_ref[...]
    qt = q * jnp.exp(b)
    kt_t = (k * jnp.exp(-b)).T
    b_t = b.T
    kdec_parts = []
    for c in range(N_CHUNKS):
        last = c * CHUNK + (CHUNK - 1 if direction == 0 else 0)
        ts = slice(c * CHUNK, (c + 1) * CHUNK)
        kdec_parts.append(k[ts] * jnp.exp(b[last:last + 1] - b[ts]))
    kdec_t = jnp.concatenate(kdec_parts, axis=0).T
    mask, _ = _tri_masks(direction)
    order = range(N_CHUNKS) if direction == 0 else range(N_CHUNKS - 1, -1, -1)

    for h in range(GLA_HEADS):
        ks = slice(h * GLA_DK, (h + 1) * GLA_DK)
        vs = slice(h * GLA_DV, (h + 1) * GLA_DV)
        s_h = s_scr[h]
        for c in order:
            ts = slice(c * CHUNK, (c + 1) * CHUNK)
            last = c * CHUNK + (CHUNK - 1 if direction == 0 else 0)
            qth = qt[ts, ks].astype(BF16)
            vh = v[ts, vs]
            a = jnp.dot(qth, kt_t[ks, ts].astype(BF16), preferred_element_type=F32)
            a = jnp.where(mask, a, 0.0)
            o = (jnp.dot(a.astype(BF16), vh, preferred_element_type=F32)
                 + jnp.dot(qth, s_h.astype(BF16), preferred_element_type=F32))
            o_ref[ts, vs] = o
            dec_col = jnp.exp(b_t[ks, last:last + 1])
            s_h = dec_col * s_h + jnp.dot(kdec_t[ks, ts].astype(BF16), vh, preferred_element_type=F32)
        s_scr[h] = s_h


def _gla_call(main, gates, gup_pad, gb, tri, direction):
    t = main.shape[0]
    nb = t // TB
    blk = lambda i: _scan_block(i, nb, direction)
    return pl.pallas_call(
        functools.partial(_gla_kernel, direction=direction),
        grid=(nb,),
        in_specs=[pl.BlockSpec((TB, GLA_KEY_WIDTH), lambda i: (blk(i), C_GQ // GLA_KEY_WIDTH)),
                  pl.BlockSpec((TB, GLA_KEY_WIDTH), lambda i: (blk(i), C_GK // GLA_KEY_WIDTH)),
                  pl.BlockSpec((TB, GLA_WIDTH), lambda i: (blk(i), C_GV // GLA_WIDTH)),
                  pl.BlockSpec((TB, N_GATE), lambda i: (blk(i), 0)),
                  pl.BlockSpec((N_GATE, GLA_KEY_WIDTH), lambda i: (0, 0)),
                  pl.BlockSpec((1, GLA_KEY_WIDTH), lambda i: (0, 0)),
                  pl.BlockSpec((TB, TB), lambda i: (0, 0))],
        out_specs=pl.BlockSpec((TB, GLA_WIDTH), lambda i: (blk(i), 0)),
        out_shape=jax.ShapeDtypeStruct((t, GLA_WIDTH), F32),
        scratch_shapes=[pltpu.VMEM((GLA_HEADS, GLA_DK, GLA_DV), F32)],
        compiler_params=_cparams(1),
        name=f"gla_dir{direction}",
    )(main, main, main, gates, gup_pad, gb, tri)


def _attn_kernel(sink_ref, q_ref, kp_ref, kc_ref, kn_ref, kx_ref, vp_ref, vc_ref, vn_ref, vx_ref, o_ref, *, seq):
    i = pl.program_id(0)
    half = TB // 2
    n_loc = 2 * TB
    keys = jnp.concatenate([kp_ref[...], kc_ref[...], kn_ref[...], kx_ref[...]], axis=0)
    vals = jnp.concatenate([vp_ref[...], vc_ref[...], vn_ref[...], vx_ref[...]], axis=0)
    n_keys = keys.shape[0]
    lane = lax.broadcasted_iota(jnp.int32, (n_keys, LANES), 1)
    zero = jnp.zeros_like(keys)
    keys_g = [jnp.where(lane < HEAD_DIM, keys, zero), jnp.where(lane >= HEAD_DIM, keys, zero)]
    r = lax.broadcasted_iota(jnp.int32, (TB, n_keys), 0)
    c = lax.broadcasted_iota(jnp.int32, (TB, n_keys), 1)
    rel = c - half - r
    kpos = (i - 1) * TB + c - half
    local_ok = (jnp.abs(rel) <= WINDOW) & (kpos >= 0) & (kpos < seq) & (i > 0)
    valid = local_ok | (c >= n_loc)
    out_lane = lax.broadcasted_iota(jnp.int32, (TB, LANES), 1)
    neg_inf = jnp.float32(-jnp.inf)
    dn = (((1,), (1,)), ((), ()))
    for m in range(ATTN_Q_WIDTH // LANES):
        qm = q_ref[:, m * LANES:(m + 1) * LANES]
        res = []
        for gidx in range(ATTN_KV_HEADS):
            sink = sink_ref[gidx * (ATTN_HEADS // ATTN_KV_HEADS) + m]
            s = lax.dot_general(qm, keys_g[gidx], dn, preferred_element_type=F32)
            s = jnp.where(valid, s, neg_inf)
            mx = jnp.maximum(jnp.max(s, axis=1, keepdims=True), sink)
            p = jnp.exp(s - mx)
            denom = jnp.sum(p, axis=1, keepdims=True) + jnp.exp(sink - mx)
            pv = jnp.dot(p.astype(BF16), vals, preferred_element_type=F32)
            res.append(pv / denom)
        o_ref[:, m * LANES:(m + 1) * LANES] = jnp.where(out_lane < HEAD_DIM, res[0], res[1]).astype(BF16)


def _attn_call(main, sink, seq):
    t = main.shape[0]
    nb = t // TB
    half = TB // 2
    nhb = t // half
    kcol, vcol = C_AK // LANES, C_AV // LANES
    prev_i = lambda i: jnp.maximum(2 * i - 1, 0)
    next_i = lambda i: jnp.minimum(2 * i + 2, nhb - 1)
    grid_spec = pltpu.PrefetchScalarGridSpec(
        num_scalar_prefetch=1,
        grid=(nb,),
        in_specs=[pl.BlockSpec((TB, ATTN_Q_WIDTH), lambda i, s: (i, C_AQ // ATTN_Q_WIDTH)),
                  pl.BlockSpec((half, LANES), lambda i, s: (prev_i(i), kcol)),
                  pl.BlockSpec((TB, LANES), lambda i, s: (i, kcol)),
                  pl.BlockSpec((half, LANES), lambda i, s: (next_i(i), kcol)),
                  pl.BlockSpec((TB, LANES), lambda i, s: (0, kcol)),
                  pl.BlockSpec((half, LANES), lambda i, s: (prev_i(i), vcol)),
                  pl.BlockSpec((TB, LANES), lambda i, s: (i, vcol)),
                  pl.BlockSpec((half, LANES), lambda i, s: (next_i(i), vcol)),
                  pl.BlockSpec((TB, LANES), lambda i, s: (0, vcol))],
        out_specs=pl.BlockSpec((TB, ATTN_Q_WIDTH), lambda i, s: (i, 0)),
    )
    return pl.pallas_call(
        functools.partial(_attn_kernel, seq=seq),
        grid_spec=grid_spec,
        out_shape=jax.ShapeDtypeStruct((t, ATTN_Q_WIDTH), BF16),
        compiler_params=_cparams(1),
        name="window_attn",
    )(sink, main, main, main, main, main, main, main, main, main)


def _head_norm(x, avg_bf16, w):
    mu = _dot_exact_rhs(x, avg_bf16)
    cen = x - mu
    var = _dot_exact_rhs(cen * cen, avg_bf16)
    return cen * lax.rsqrt(var + LN_EPS) * w


def _outproj_kernel(mf_ref, mb_ref, mo_ref, at_ref, gf_ref, gbk_ref, go_ref, mnw_ref, gnw_ref, avg_ref,
                    w_ref, x_ref, mod_ref, lnw_ref, lnb_ref, x1_ref, h2_ref):
    avg = avg_ref[...]
    hm = _head_norm(mf_ref[...] + mb_ref[...], avg, mnw_ref[...]) * _sigmoid(mo_ref[...].astype(F32))
    gate = go_ref[...].astype(F32)
    hg = _head_norm(gf_ref[...] + gbk_ref[...], avg, gnw_ref[...]) * (gate * _sigmoid(gate))
    a0, a1, a2 = MLSTM_WIDTH, MLSTM_WIDTH + ATTN_Q_WIDTH, D_MODEL
    mix = (jnp.dot(hm.astype(BF16), w_ref[:a0], preferred_element_type=F32)
           + jnp.dot(at_ref[...], w_ref[a0:a1], preferred_element_type=F32)
           + jnp.dot(hg.astype(BF16), w_ref[a1:a2], preferred_element_type=F32))
    m = mod_ref[...]
    x1 = _layer_norm(DEEPNORM_ALPHA * x_ref[...] + m[2:3] * mix, lnw_ref[...], lnb_ref[...])
    x1_ref[...] = x1
    h2_ref[...] = (x1 * (1.0 + m[4:5]) + m[3:4]).astype(h2_ref.dtype)


def _outproj_call(mf, mb, main, attn, gf, gbk, mnw, gnw, avg, w_out, layer, x, mod, lnw, lnb, h2_dtype):
    t, d = x.shape
    nb = t // TB
    row = lambda i: (i, 0)
    const = lambda i: (0, 0)
    return pl.pallas_call(
        _outproj_kernel,
        grid=(nb,),
        in_specs=[pl.BlockSpec((TB, MLSTM_WIDTH), row),
                  pl.BlockSpec((TB, MLSTM_WIDTH), row),
                  pl.BlockSpec((TB, MLSTM_WIDTH), lambda i: (i, C_MO // MLSTM_WIDTH)),
                  pl.BlockSpec((TB, ATTN_Q_WIDTH), row),
                  pl.BlockSpec((TB, GLA_WIDTH), row),
                  pl.BlockSpec((TB, GLA_WIDTH), row),
                  pl.BlockSpec((TB, GLA_WIDTH), lambda i: (i, C_GO // GLA_WIDTH)),
                  pl.BlockSpec((1, MLSTM_WIDTH), const),
                  pl.BlockSpec((1, GLA_WIDTH), const),
                  pl.BlockSpec((MLSTM_WIDTH, MLSTM_WIDTH), const),
                  pl.BlockSpec((None, d, d), lambda i: (layer, 0, 0), pipeline_mode=pl.Buffered(1)),
                  pl.BlockSpec((TB, d), row),
                  pl.BlockSpec((None, SUBLANES, d), lambda i: (_who(i), 0, 0)),
                  pl.BlockSpec((1, d), const),
                  pl.BlockSpec((1, d), const)],
        out_specs=[pl.BlockSpec((TB, d), row), pl.BlockSpec((TB, d), row)],
        out_shape=[jax.ShapeDtypeStruct((t, d), F32), jax.ShapeDtypeStruct((t, d), h2_dtype)],
        compiler_params=_cparams(1),
        name="out_proj",
    )(mf, mb, main, attn, gf, gbk, main, mnw, gnw, avg, w_out, x, mod, lnw, lnb)


def _select_mod(mod_ref, k, i, tm):
    rows = i * tm + lax.broadcasted_iota(jnp.int32, (tm, 1), 0)
    return jnp.where(rows < CTX_LEN, mod_ref[0, k:k + 1, :], mod_ref[1, k:k + 1, :])


def _ffn_kernel(h_ref, wgu_ref, wd_ref, x_ref, mod_ref, lnw_ref, lnb_ref, o_ref):
    h = h_ref[...]
    g = jnp.dot(h, wgu_ref[:, :D_FF], preferred_element_type=F32)
    u = jnp.dot(h, wgu_ref[:, D_FF:], preferred_element_type=F32)
    y = jnp.dot((g * _sigmoid(g) * u).astype(BF16), wd_ref[...], preferred_element_type=F32)
    m5 = _select_mod(mod_ref, 5, pl.program_id(0), h_ref.shape[0])
    o_ref[...] = _layer_norm(DEEPNORM_ALPHA * x_ref[...] + m5 * y, lnw_ref[...], lnb_ref[...])


def _ffn_call(h2, w_gu, w_down, layer, x1, mod, lnw, lnb):
    t, d = x1.shape
    resident = pl.Buffered(1)
    return pl.pallas_call(
        _ffn_kernel,
        grid=(t // FFN_TM,),
        in_specs=[pl.BlockSpec((FFN_TM, d), lambda i: (i, 0)),
                  pl.BlockSpec((None, d, 2 * D_FF), lambda i: (layer, 0, 0), pipeline_mode=resident),
                  pl.BlockSpec((None, D_FF, d), lambda i: (layer, 0, 0), pipeline_mode=resident),
                  pl.BlockSpec((FFN_TM, d), lambda i: (i, 0)),
                  pl.BlockSpec((2, SUBLANES, d), lambda i: (0, 0, 0)),
                  pl.BlockSpec((1, d), lambda i: (0, 0)),
                  pl.BlockSpec((1, d), lambda i: (0, 0))],
        out_specs=pl.BlockSpec((FFN_TM, d), lambda i: (i, 0)),
        out_shape=jax.ShapeDtypeStruct((t, d), F32),
        compiler_params=_cparams(1),
        name="dense_ffn",
    )(h2, w_gu, w_down, x1, mod, lnw, lnb)


def _router_kernel(h_ref, wr_ref, br_ref, su_ref, route_ref, cnt_ref, base_scr):
    @pl.when(pl.program_id(0) == 0)
    def _():
        base_scr[...] = jnp.zeros_like(base_scr)

    lt = lax.dot_general(wr_ref[...], h_ref[...], (((1,), (1,)), ((), ())),
                         preferred_element_type=F32, precision=HIGHEST) + br_ref[...]
    idx = lax.broadcasted_iota(jnp.int32, lt.shape, 0)
    neg_inf = jnp.float32(-jnp.inf)
    m1 = jnp.max(lt, axis=0, keepdims=True)
    e1 = jnp.min(jnp.where(lt == m1, idx, N_EXPERTS), axis=0, keepdims=True)
    lt2 = jnp.where(idx == e1, neg_inf, lt)
    m2 = jnp.max(lt2, axis=0, keepdims=True)
    e2 = jnp.min(jnp.where(lt2 == m2, idx, N_EXPERTS), axis=0, keepdims=True)
    t2 = jnp.exp(m2 - m1)
    w1 = 1.0 / (1.0 + t2)
    w2 = t2 / (1.0 + t2)
    su = su_ref[...]
    base = base_scr[...][:, 0:1]
    oh1 = (idx == e1).astype(F32)
    cum1 = jnp.dot(oh1.astype(BF16), su, preferred_element_type=F32)
    rank1 = jnp.sum(oh1 * (base + cum1), axis=0, keepdims=True)
    base = base + jnp.sum(oh1, axis=1, keepdims=True)
    oh2 = (idx == e2).astype(F32)
    cum2 = jnp.dot(oh2.astype(BF16), su, preferred_element_type=F32)
    rank2 = jnp.sum(oh2 * (base + cum2), axis=0, keepdims=True)
    base = base + jnp.sum(oh2, axis=1, keepdims=True)
    base_scr[...] = jnp.broadcast_to(base, base_scr.shape)
    cnt_ref[...] = jnp.broadcast_to(base, cnt_ref.shape)
    zero = jnp.zeros_like(w1)
    route_ref[...] = jnp.concatenate(
        [e1.astype(F32), e2.astype(F32), w1, w2, rank1, rank2, zero, zero], axis=0)


def _router_call(h2, wr_t, br, su):
    t, d = h2.shape
    nb = t // TB
    return pl.pallas_call(
        _router_kernel,
        grid=(nb,),
        in_specs=[pl.BlockSpec((TB, d), lambda i: (i, 0)),
                  pl.BlockSpec((N_EXPERTS, d), lambda i: (0, 0)),
                  pl.BlockSpec((N_EXPERTS, 1), lambda i: (0, 0)),
                  pl.BlockSpec((TB, TB), lambda i: (0, 0))],
        out_specs=[pl.BlockSpec((SUBLANES, TB), lambda i: (0, i)),
                   pl.BlockSpec((N_EXPERTS, LANES), lambda i: (0, 0))],
        out_shape=[jax.ShapeDtypeStruct((SUBLANES, t), F32),
                   jax.ShapeDtypeStruct((N_EXPERTS, LANES), F32)],
        scratch_shapes=[pltpu.VMEM((N_EXPERTS, LANES), F32)],
        compiler_params=_cparams(1),
        name="moe_router",
    )(h2, wr_t, br, su)


def _dispatch_kernel(s1_ref, s2_ref, h_ref, xs_in_ref, xs_ref, sem):
    del xs_in_ref
    i = pl.program_id(0)

    def copy(r, slot_ref):
        return pltpu.make_async_copy(h_ref.at[pl.ds(r, 1)], xs_ref.at[pl.ds(slot_ref[i * TB + r], 1)], sem)

    def start(r, carry):
        copy(r, s1_ref).start()
        copy(r, s2_ref).start()
        return carry

    def wait(r, carry):
        copy(r, s1_ref).wait()
        copy(r, s2_ref).wait()
        return carry

    lax.fori_loop(0, TB, start, 0)
    lax.fori_loop(0, TB, wait, 0)


def _dispatch_call(slot1, slot2, h2, xs_init):
    t, d = h2.shape
    grid_spec = pltpu.PrefetchScalarGridSpec(
        num_scalar_prefetch=2,
        grid=(t // TB,),
        in_specs=[pl.BlockSpec((TB, d), lambda i, a, b: (i, 0)),
                  pl.BlockSpec(memory_space=pl.ANY)],
        out_specs=pl.BlockSpec(memory_space=pl.ANY),
        scratch_shapes=[pltpu.SemaphoreType.DMA(())],
    )
    return pl.pallas_call(
        _dispatch_kernel,
        grid_spec=grid_spec,
        out_shape=jax.ShapeDtypeStruct(xs_init.shape, xs_init.dtype),
        input_output_aliases={3: 0},
        compiler_params=_cparams(1),
        name="moe_dispatch",
    )(slot1, slot2, h2, xs_init)


def _expert_kernel(be_ref, used_ref, x_ref, wg_ref, wu_ref, wd_ref, y_ref, xb_scr, acc_scr):
    b, f = pl.program_id(0), pl.program_id(1)
    live = b < used_ref[0]

    @pl.when(live & (f == 0))
    def _():
        xb_scr[...] = x_ref[...].astype(BF16)

    @pl.when(live)
    def _():
        x = xb_scr[...]
        g = jnp.dot(x, wg_ref[...], preferred_element_type=F32)
        u = jnp.dot(x, wu_ref[...], preferred_element_type=F32)
        part = jnp.dot((g * _sigmoid(g) * u).astype(BF16), wd_ref[...], preferred_element_type=F32)

        @pl.when(f == 0)
        def _():
            acc_scr[...] = part

        @pl.when(f > 0)
        def _():
            acc_scr[...] += part

    @pl.when(f == pl.num_programs(1) - 1)
    def _():
        @pl.when(live)
        def _():
            y_ref[...] = acc_scr[...]

        @pl.when(jnp.logical_not(live))
        def _():
            y_ref[...] = jnp.zeros_like(y_ref)


def _expert_call(block_e, used, xs, w_gu, w_down, layer):
    ns, d = xs.shape
    nblk = ns // MOE_G
    nf = D_EXPERT // MOE_TF

    def f_eff(b, f, used_ref):
        return jnp.where(b < used_ref[0], f, nf - 1)

    grid_spec = pltpu.PrefetchScalarGridSpec(
        num_scalar_prefetch=2,
        grid=(nblk, nf),
        in_specs=[pl.BlockSpec((MOE_G, d), lambda b, f, be, us: (b, 0)),
                  pl.BlockSpec((None, None, d, MOE_TF), lambda b, f, be, us: (layer, be[b], 0, f_eff(b, f, us))),
                  pl.BlockSpec((None, None, d, MOE_TF),
                               lambda b, f, be, us: (layer, be[b], 0, nf + f_eff(b, f, us))),
                  pl.BlockSpec((None, None, MOE_TF, d), lambda b, f, be, us: (layer, be[b], f_eff(b, f, us), 0))],
        out_specs=pl.BlockSpec((MOE_G, d), lambda b, f, be, us: (b, 0)),
        scratch_shapes=[pltpu.VMEM((MOE_G, d), BF16), pltpu.VMEM((MOE_G, d), F32)],
    )
    return pl.pallas_call(
        _expert_kernel,
        grid_spec=grid_spec,
        out_shape=jax.ShapeDtypeStruct((ns, d), F32),
        compiler_params=_cparams(2),
        name="moe_experts",
    )(block_e, used, xs, w_gu, w_gu, w_down)


def _combine_kernel(s1_ref, s2_ref, yb_ref, route_ref, x_ref, mod_ref, lnw_ref, lnb_ref, o_ref, buf1, buf2, sem):
    i = pl.program_id(0)

    def copy(r, slot_ref, buf):
        return pltpu.make_async_copy(yb_ref.at[pl.ds(slot_ref[i * TB + r], 1)], buf.at[pl.ds(r, 1)], sem)

    def start(r, carry):
        copy(r, s1_ref, buf1).start()
        copy(r, s2_ref, buf2).start()
        return carry

    def wait(r, carry):
        copy(r, s1_ref, buf1).wait()
        copy(r, s2_ref, buf2).wait()
        return carry

    lax.fori_loop(0, TB, start, 0)
    lax.fori_loop(0, TB, wait, 0)
    rt = route_ref[...].T
    y = rt[:, 2:3] * buf1[...] + rt[:, 3:4] * buf2[...]
    m = mod_ref[...]
    o_ref[...] = _layer_norm(DEEPNORM_ALPHA * x_ref[...] + m[5:6] * y, lnw_ref[...], lnb_ref[...])


def _combine_call(slot1, slot2, yb, route, x1, mod, lnw, lnb, latent_only):
    t, d = x1.shape
    if latent_only:
        out_rows, out_map = t - CTX_LEN, lambda i, a, b: (jnp.maximum(i - 1, 0), 0)
    else:
        out_rows, out_map = t, lambda i, a, b: (i, 0)
    grid_spec = pltpu.PrefetchScalarGridSpec(
        num_scalar_prefetch=2,
        grid=(t // TB,),
        in_specs=[pl.BlockSpec(memory_space=pl.ANY),
                  pl.BlockSpec((SUBLANES, TB), lambda i, a, b: (0, i)),
                  pl.BlockSpec((TB, d), lambda i, a, b: (i, 0)),
                  pl.BlockSpec((None, SUBLANES, d), lambda i, a, b: (_who(i), 0, 0)),
                  pl.BlockSpec((1, d), lambda i, a, b: (0, 0)),
                  pl.BlockSpec((1, d), lambda i, a, b: (0, 0))],
        out_specs=pl.BlockSpec((TB, d), out_map),
        scratch_shapes=[pltpu.VMEM((TB, d), F32), pltpu.VMEM((TB, d), F32), pltpu.SemaphoreType.DMA(())],
    )
    return pl.pallas_call(
        _combine_kernel,
        grid_spec=grid_spec,
        out_shape=jax.ShapeDtypeStruct((out_rows, d), F32),
        compiler_params=_cparams(1),
        name="moe_combine",
    )(slot1, slot2, yb, route, x1, mod, lnw, lnb)


def _moe_layer(h2, x1, mod, lnw, lnb, w_router, b_router, w_gu, w_down, layer, su, latent_only):
    t, d = h2.shape
    route, cnt = _router_call(h2, w_router.T, b_router.reshape(N_EXPERTS, 1), su)
    counts = cnt[:, 0].astype(jnp.int32)
    padded = (counts + MOE_G - 1) // MOE_G * MOE_G
    pend = jnp.cumsum(padded)
    pstart = pend - padded
    e1, e2 = route[0].astype(jnp.int32), route[1].astype(jnp.int32)
    slot1 = pstart[e1] + route[4].astype(jnp.int32)
    slot2 = pstart[e2] + route[5].astype(jnp.int32)
    nblk = -(-(2 * t) // MOE_G) + N_EXPERTS
    blk_start = jnp.arange(nblk, dtype=jnp.int32) * MOE_G
    block_e = jnp.minimum(jnp.sum((pend[None, :] <= blk_start[:, None]).astype(jnp.int32), axis=1), N_EXPERTS - 1)
    used = (pend[-1:] // MOE_G).astype(jnp.int32)
    xs = _dispatch_call(slot1, slot2, h2, jnp.zeros((nblk * MOE_G, d), F32))
    yb = _expert_call(block_e, used, xs, w_gu, w_down, layer)
    return _combine_call(slot1, slot2, yb, route, x1, mod, lnw, lnb, latent_only)


def _attn_head_perm():
    cols = []
    for m in range(ATTN_HEADS // ATTN_KV_HEADS):
        for hq in (m, m + ATTN_HEADS // ATTN_KV_HEADS):
            cols.extend(range(hq * HEAD_DIM, (hq + 1) * HEAD_DIM))
    return np.asarray(cols, np.int32)


def _in_col_perm():
    n_m = 4 * MLSTM_WIDTH
    mg = 2 * 2 * MLSTM_HEADS
    a0 = n_m + mg
    perm = list(range(n_m))
    perm += [a0 + int(j) for j in _attn_head_perm()]
    perm += list(range(a0 + ATTN_Q_WIDTH, a0 + ATTN_Q_WIDTH + 2 * ATTN_KV_WIDTH))
    g0 = a0 + ATTN_Q_WIDTH + 2 * ATTN_KV_WIDTH
    perm += list(range(g0, g0 + 2 * GLA_KEY_WIDTH + 2 * GLA_WIDTH))
    perm += list(range(n_m, n_m + mg))
    perm += list(range(g0 + 2 * GLA_KEY_WIDTH + 2 * GLA_WIDTH, g0 + 2 * GLA_KEY_WIDTH + 2 * GLA_WIDTH + 2 * GLA_RANK))
    return np.asarray(perm, np.int32)


def _take_runs(w, perm, axis):
    runs, start = [], 0
    for j in range(1, len(perm) + 1):
        if j == len(perm) or perm[j] != perm[j - 1] + 1:
            runs.append(lax.slice_in_dim(w, int(perm[start]), int(perm[j - 1]) + 1, axis=axis))
            start = j
    return jnp.concatenate(runs, axis=axis)


def _rope_tables(seq):
    inv = ROPE_BASE ** (-jnp.arange(ROPE_PAIRS, dtype=F32) / ROPE_PAIRS)
    rows = seq // GRID_W
    ang_r = jnp.arange(rows).astype(F32)[:, None] * inv
    ang_c = jnp.arange(GRID_W).astype(F32)[:, None] * inv
    rep = lambda a: jnp.repeat(a, GRID_W, axis=0)
    til = lambda a: jnp.tile(a, (rows, 1))
    cos_r, sin_r, cos_c, sin_c = rep(jnp.cos(ang_r)), rep(jnp.sin(ang_r)), til(jnp.cos(ang_c)), til(jnp.sin(ang_c))
    cos64 = jnp.concatenate([cos_r, cos_r, cos_c, cos_c], -1)
    sin64 = jnp.concatenate([-sin_r, sin_r, -sin_c, sin_c], -1)
    cos_l = jnp.concatenate([cos64, cos64], -1)
    sin_l = jnp.concatenate([sin64, sin64], -1)
    cos_t = jnp.concatenate([jnp.ones((CTX_LEN, LANES), F32), cos_l], 0)
    sin_t = jnp.concatenate([jnp.zeros((CTX_LEN, LANES), F32), sin_l], 0)
    return cos_t, sin_t


def _block_tri(direction):
    r = np.arange(TB)[:, None]
    c = np.arange(TB)[None, :]
    same = (r // CHUNK) == (c // CHUNK)
    tri = (c <= r) if direction == 0 else (c >= r)
    return jnp.asarray(same & tri, BF16)


def kernel(x, c, ctx, c_ctx, w_ada, b_ada, w_in, mlstm_gate_b, mlstm_norm_w, attn_sink, gla_gate_up, gla_gate_b,
           gla_norm_w, w_out, ln_w, ln_b, ffn_w_gu, ffn_w_down, router_w, router_b, moe_w_gu, moe_w_down):
    seq, d = x.shape[1], x.shape[2]
    depth = w_in.shape[0]
    xt = jnp.concatenate([ctx[0], x[0]], axis=0)

    cvec = jnp.zeros((SUBLANES, d), F32).at[0].set(c_ctx).at[1].set(c[0])
    mods = _mod_call(cvec, w_ada, b_ada)[:, :2].reshape(depth, 2, 6, d)
    mods = jnp.pad(mods, ((0, 0), (0, 0), (0, SUBLANES - 6), (0, 0)))

    perm = _in_col_perm()
    w_in_p = jnp.concatenate([_take_runs(w_in, perm, 2).astype(BF16),
                              jnp.zeros((depth, d, N_PROJ - perm.shape[0]), BF16)], axis=2)
    out_perm = np.concatenate([np.arange(MLSTM_WIDTH), MLSTM_WIDTH + _attn_head_perm(),
                               np.arange(MLSTM_WIDTH + ATTN_Q_WIDTH, d)]).astype(np.int32)
    w_out_p = _take_runs(w_out, out_perm, 1).astype(BF16)
    gate_bias = jnp.pad(mlstm_gate_b.reshape(depth, 1, -1), ((0, 0), (0, 0), (0, N_GATE - 4 * MLSTM_HEADS)))
    gup_pad = jnp.zeros((depth, 2, N_GATE, GLA_KEY_WIDTH), F32)
    for dr in range(2):
        lo = GLA_GATE_OFF + dr * GLA_RANK
        gup_pad = gup_pad.at[:, dr, lo:lo + GLA_RANK, :].set(gla_gate_up[:, dr])
    cos_t, sin_t = _rope_tables(seq)
    tri = [_block_tri(0), _block_tri(1)]
    su = jnp.asarray(np.arange(TB)[:, None] < np.arange(TB)[None, :], BF16)
    hh = np.arange(MLSTM_WIDTH) // MLSTM_DH
    avg = jnp.asarray((hh[:, None] == hh[None, :]) / MLSTM_DH, BF16)
    ffn_gu, ffn_dn = ffn_w_gu.astype(BF16), ffn_w_down.astype(BF16)
    moe_gu, moe_dn = moe_w_gu.astype(BF16), moe_w_down.astype(BF16)

    expand = []
    for dr in range(2):
        e = np.zeros((N_GATE, MLSTM_WIDTH), np.float32)
        for h in range(MLSTM_HEADS):
            e[dr * 2 * MLSTM_HEADS + h, h * MLSTM_DH:(h + 1) * MLSTM_DH] = 1.0
        expand.append(jnp.asarray(e, BF16))

    for l in range(depth):
        is_moe = l % 2 == 1
        last = l == depth - 1
        main, gates = _inproj_call(xt, mods[l], w_in_p, l, cos_t, sin_t)
        mf = _mlstm_call(main, gates, gate_bias[l], tri[0], expand[0], 0)
        mb = _mlstm_call(main, gates, gate_bias[l], tri[1], expand[1], 1)
        gf = _gla_call(main, gates, gup_pad[l, 0], gla_gate_b[l, 0:1], tri[0], 0)
        gbk = _gla_call(main, gates, gup_pad[l, 1], gla_gate_b[l, 1:2], tri[1], 1)
        attn = _attn_call(main, attn_sink[l], seq)
        x1, h2 = _outproj_call(mf, mb, main, attn, gf, gbk, mlstm_norm_w[l:l + 1], gla_norm_w[l:l + 1], avg,
                               w_out_p, l, xt, mods[l], ln_w[l, 0:1], ln_b[l, 0:1], F32 if is_moe else BF16)
        if is_moe:
            xt = _moe_layer(h2, x1, mods[l], ln_w[l, 1:2], ln_b[l, 1:2], router_w[l // 2], router_b[l // 2],
                            moe_gu, moe_dn, l // 2, su, latent_only=last)
        else:
            xt = _ffn_call(h2, ffn_gu, ffn_dn, l // 2, x1, mods[l], ln_w[l, 1:2], ln_b[l, 1:2])
    return (xt if depth % 2 == 0 else xt[CTX_LEN:])[None]
```

```python
import functools

import jax
import jax.numpy as jnp
import numpy as np
from jax import lax
from jax.experimental import pallas as pl
from jax.experimental.pallas import tpu as pltpu

F32 = jnp.float32
BF16 = jnp.bfloat16
HIGHEST = lax.Precision.HIGHEST

D_MODEL = 1024
SEQ = 16384
DEPTH = 4
GRID_W = 64
CTX_LEN = 256
MLSTM_HEADS = 4
MLSTM_DH = 64
MLSTM_WIDTH = 256
HEAD_DIM = 64
ATTN_HEADS = 8
ATTN_KV_HEADS = 2
ATTN_Q_WIDTH = 512
ATTN_KV_WIDTH = 128
WINDOW = 128
ROPE_BASE = 10000.0
ROPE_PAIRS = 16
GLA_HEADS = 4
GLA_DK = 32
GLA_DV = 64
GLA_WIDTH = 256
GLA_KEY_WIDTH = 128
GLA_RANK = 16
GLA_TAU = 16.0
CHUNK = 64
D_FF = 2816
N_EXPERTS = 8
D_EXPERT = 3584
DEEPNORM_ALPHA = (2 * DEPTH) ** 0.25
LN_EPS = 1e-5

LANES = 128
SUBLANES = 8
VMEM_LIMIT = 56 * 1024 * 1024

TB = 256
N_CHUNKS = TB // CHUNK
FFN_TM = 640
MOE_G = 512
MOE_TF = 1792

C_MQ, C_MK, C_MV, C_MO = 0, 256, 512, 768
C_AQ, C_AK, C_AV = 1024, 1536, 1664
C_GQ, C_GK, C_GV, C_GO = 1792, 1920, 2048, 2304
N_MAIN = 2560
N_GATE = 128
N_PROJ = N_MAIN + N_GATE
GLA_GATE_OFF = 16


def _cparams(n_axes=1):
    return pltpu.CompilerParams(dimension_semantics=("arbitrary",) * n_axes,
                                vmem_limit_bytes=VMEM_LIMIT)


def _sigmoid(x):
    return 1.0 / (1.0 + jnp.exp(-x))


def _log_sigmoid(x):
    return jnp.minimum(x, 0.0) - jnp.log(1.0 + jnp.exp(-jnp.abs(x)))


def _split3(x):
    hi = x.astype(BF16)
    r1 = x - hi.astype(F32)
    mid = r1.astype(BF16)
    lo = (r1 - mid.astype(F32)).astype(BF16)
    return hi, mid, lo


def _dot_exact_rhs(x, m_bf16):
    hi, mid, lo = _split3(x)
    d = lambda a: jnp.dot(a, m_bf16, preferred_element_type=F32)
    return d(hi) + d(mid) + d(lo)


def _dot_exact_lhs(m_bf16, x):
    hi, mid, lo = _split3(x)
    d = lambda a: jnp.dot(m_bf16, a, preferred_element_type=F32)
    return d(hi) + d(mid) + d(lo)


def _dot_bf16x3(a, b):
    a_hi, b_hi = a.astype(BF16), b.astype(BF16)
    a_lo = (a - a_hi.astype(F32)).astype(BF16)
    b_lo = (b - b_hi.astype(F32)).astype(BF16)
    d = lambda x, y: jnp.dot(x, y, preferred_element_type=F32)
    return d(a_hi, b_hi) + d(a_hi, b_lo) + d(a_lo, b_hi)


def _layer_norm(v, w, b):
    mu = jnp.mean(v, axis=-1, keepdims=True)
    cen = v - mu
    var = jnp.mean(cen * cen, axis=-1, keepdims=True)
    return cen * lax.rsqrt(var + LN_EPS) * w + b


def _mod_kernel(c_ref, w_ref, b_ref, o_ref):
    c = c_ref[...]
    sc = c * _sigmoid(c)
    o_ref[...] = jnp.dot(sc, w_ref[...], preferred_element_type=F32, precision=HIGHEST) + b_ref[...]


def _mod_call(cvec, w_ada, b_ada):
    depth, d, n = w_ada.shape
    tn = 1536
    return pl.pallas_call(
        _mod_kernel,
        grid=(depth, n // tn),
        in_specs=[pl.BlockSpec((SUBLANES, d), lambda l, j: (0, 0)),
                  pl.BlockSpec((None, d, tn), lambda l, j: (l, 0, j)),
                  pl.BlockSpec((None, 1, tn), lambda l, j: (l, 0, j))],
        out_specs=pl.BlockSpec((None, SUBLANES, tn), lambda l, j: (l, 0, j)),
        out_shape=jax.ShapeDtypeStruct((depth, SUBLANES, n), F32),
        compiler_params=_cparams(2),
        name="ada_mod",
    )(cvec, w_ada, b_ada.reshape(depth, 1, n))


def _who(i):
    return jnp.minimum(i, 1)


def _rope(x, cos, sin_signed, first_half):
    swapped = jnp.where(first_half, pltpu.roll(x, LANES - ROPE_PAIRS, 1), pltpu.roll(x, ROPE_PAIRS, 1))
    return x * cos + swapped * sin_signed


def _inproj_kernel(x_ref, mod_ref, w_ref, cos_ref, sin_ref, main_ref, gate_ref):
    m = mod_ref[...]
    h = x_ref[...] * (1.0 + m[1:2]) + m[0:1]
    p = jnp.dot(h.astype(BF16), w_ref[...], preferred_element_type=F32)
    cos, sin = cos_ref[...], sin_ref[...]
    lane = lax.broadcasted_iota(jnp.int32, (TB, LANES), 1)
    first_half = (lane % (2 * ROPE_PAIRS)) < ROPE_PAIRS
    main_ref[:, :C_AQ] = p[:, :C_AQ].astype(BF16)
    for j in range(ATTN_Q_WIDTH // LANES):
        lo = C_AQ + j * LANES
        main_ref[:, lo:lo + LANES] = (_rope(p[:, lo:lo + LANES], cos, sin, first_half) * HEAD_DIM ** -0.5).astype(BF16)
    main_ref[:, C_AK:C_AV] = _rope(p[:, C_AK:C_AV], cos, sin, first_half).astype(BF16)
    main_ref[:, C_AV:] = p[:, C_AV:N_MAIN].astype(BF16)
    gate_ref[...] = p[:, N_MAIN:]


def _inproj_call(x, mod, w, layer, cos_t, sin_t):
    t, d = x.shape
    nb = t // TB
    return pl.pallas_call(
        _inproj_kernel,
        grid=(nb,),
        in_specs=[pl.BlockSpec((TB, d), lambda i: (i, 0)),
                  pl.BlockSpec((None, SUBLANES, d), lambda i: (_who(i), 0, 0)),
                  pl.BlockSpec((None, d, N_PROJ), lambda i: (layer, 0, 0), pipeline_mode=pl.Buffered(1)),
                  pl.BlockSpec((TB, LANES), lambda i: (i, 0)),
                  pl.BlockSpec((TB, LANES), lambda i: (i, 0))],
        out_specs=[pl.BlockSpec((TB, N_MAIN), lambda i: (i, 0)),
                   pl.BlockSpec((TB, N_GATE), lambda i: (i, 0))],
        out_shape=[jax.ShapeDtypeStruct((t, N_MAIN), BF16),
                   jax.ShapeDtypeStruct((t, N_GATE), F32)],
        compiler_params=_cparams(1),
        name="in_proj",
    )(x, mod, w, cos_t, sin_t)


def _scan_block(i, nb, direction):
    if direction == 0:
        return i
    return jnp.where(i == 0, 0, nb - i)


def _mlstm_kernel(q_ref, k_ref, v_ref, g_ref, gb_ref, tri_ref, exp_ref, o_ref, c_scr, n_scr, m_scr, *, direction):
    fwd = direction == 0
    neg_inf = jnp.float32(-jnp.inf)
    order = range(N_CHUNKS) if fwd else range(N_CHUNKS - 1, -1, -1)
    last_of = lambda c: c * CHUNK + (CHUNK - 1 if fwd else 0)

    g = g_ref[...] + gb_ref[...]
    b_f = _dot_exact_lhs(tri_ref[...], _log_sigmoid(g))
    b_i = pltpu.roll(b_f, LANES - MLSTM_HEADS, 1)
    gr = g - b_i
    tok = lax.broadcasted_iota(jnp.int32, (TB, LANES), 0) % CHUNK
    aloc = gr
    for sh in (1, 2, 4, 8, 16, 32):
        if fwd:
            shifted, ok = pltpu.roll(aloc, sh, 0), tok >= sh
        else:
            shifted, ok = pltpu.roll(aloc, TB - sh, 0), tok < CHUNK - sh
        aloc = jnp.maximum(aloc, jnp.where(ok, shifted, neg_inf))
    chunk_max = jnp.concatenate(
        [jnp.broadcast_to(aloc[last_of(c):last_of(c) + 1], (CHUNK, LANES)) for c in range(N_CHUNKS)], axis=0)
    wloc = jnp.exp(gr - chunk_max)
    expand = exp_ref[...]
    aloc_x = _dot_exact_rhs(aloc, expand)
    b_x = _dot_exact_rhs(b_i, expand)
    wloc_x = _dot_exact_rhs(wloc, expand)
    gr_t = gr.T

    lane = lax.broadcasted_iota(jnp.int32, (CHUNK, LANES), 1)
    trow = lax.broadcasted_iota(jnp.int32, (CHUNK, LANES), 0)
    low = lane < MLSTM_DH
    tri_ok = (lane % MLSTM_DH <= trow) if fwd else (lane % MLSTM_DH >= trow)
    lane2 = lax.broadcasted_iota(jnp.int32, (LANES, LANES), 1)
    row2 = lax.broadcasted_iota(jnp.int32, (LANES, LANES), 0)
    same_head = (lane2 < MLSTM_DH) == (row2 < MLSTM_DH)
    bd_ones = jnp.where(same_head, 1.0, 0.0).astype(BF16)
    lane_row = lax.broadcasted_iota(jnp.int32, (1, LANES), 1)
    dn_nt = (((1,), (1,)), ((), ()))

    def pair_blockdiag(x):
        zero = jnp.zeros_like(x)
        return jnp.concatenate([jnp.where(low, x, zero), jnp.where(low, zero, x)], axis=0)

    for p in range(MLSTM_HEADS // 2):
        lp = slice(p * LANES, (p + 1) * LANES)
        ji0 = direction * 2 * MLSTM_HEADS + 2 * p
        bdc, bdn = c_scr[p], n_scr[p]
        m_row = m_scr[p][0:1]
        kt2 = {}
        for c in order:
            ts = slice(c * CHUNK, (c + 1) * CHUNK)
            c2 = c // 2
            ts2 = slice(c2 * 2 * CHUNK, (c2 + 1) * 2 * CHUNK)
            if c2 not in kt2:
                kt2[c2] = (k_ref[ts2, lp].astype(F32) * MLSTM_DH ** -0.5).T.astype(BF16)
            ra = gr_t[ji0:ji0 + 1, ts2]
            rb = gr_t[ji0 + 1:ji0 + 2, ts2]
            if c % 2 == 0:
                g_row = jnp.where(lane_row < MLSTM_DH, ra, pltpu.roll(rb, MLSTM_DH, 1))
            else:
                g_row = jnp.where(lane_row < MLSTM_DH, pltpu.roll(ra, MLSTM_DH, 1), rb)
            al = aloc_x[ts, lp]
            d_loc = jnp.exp(jnp.where(tri_ok, g_row - al, neg_inf))
            qp = q_ref[ts, lp]
            kp = k_ref[ts, lp] * MLSTM_DH ** -0.5
            vp = v_ref[ts, lp]
            s = lax.dot_general(qp, pair_blockdiag(kp), dn_nt, preferred_element_type=F32) * d_loc
            s_hi = s.astype(BF16)
            r1 = jnp.dot(s_hi, jnp.concatenate([pair_blockdiag(vp), bd_ones], axis=1),
                         preferred_element_type=F32)
            den_lo = jnp.dot((s - s_hi.astype(F32)).astype(BF16), bd_ones, preferred_element_type=F32)
            r2 = jnp.dot(qp, jnp.concatenate([bdc, bdn], axis=1).astype(BF16),
                         preferred_element_type=F32)
            a = jnp.maximum(m_row, al)
            corr = jnp.exp(al - a)
            w_inter = jnp.exp(m_row - a)
            num = corr * r1[:, :LANES] + w_inter * r2[:, :LANES]
            den = corr * (r1[:, LANES:] + den_lo) + w_inter * r2[:, LANES:]
            o_ref[ts, lp] = num / jnp.maximum(jnp.abs(den), jnp.exp(-(b_x[ts, lp] + a)))
            last = last_of(c)
            a_last = jnp.maximum(m_row, aloc_x[last:last + 1, lp])
            decay = jnp.exp(m_row - a_last)
            scale = jnp.exp(aloc_x[last:last + 1, lp] - a_last)
            wx2 = wloc_x[ts2, lp]
            in_chunk = (lax.broadcasted_iota(jnp.int32, (2 * CHUNK, LANES), 0) // CHUNK) == (c % 2)
            wv = jnp.where(in_chunk, wx2 * v_ref[ts2, lp].astype(F32), 0.0)
            ww = jnp.where(in_chunk, wx2, 0.0)
            upd = jnp.dot(kt2[c2], jnp.concatenate([wv, ww], axis=1).astype(BF16),
                          preferred_element_type=F32)
            bdc = decay * bdc + scale * jnp.where(same_head, upd[:, :LANES], 0.0)
            bdn = decay * bdn + scale * jnp.where(same_head, upd[:, LANES:], 0.0)
            m_row = b_x[last:last + 1, lp] + a_last
        c_scr[p] = bdc
        n_scr[p] = bdn
        m_scr[p] = jnp.broadcast_to(m_row, (SUBLANES, LANES))


def _gla_kernel(q_ref, k_ref, v_ref, g_ref, gup_ref, gb_ref, tri_ref, o_ref, s_scr, *, direction):
    fwd = direction == 0
    order = range(N_CHUNKS) if fwd else range(N_CHUNKS - 1, -1, -1)
    z = _dot_bf16x3(g_ref[...], gup_ref[...]) + gb_ref[...]
    lg = _log_sigmoid(z) * (1.0 / GLA_TAU)
    b = _dot_exact_lhs(tri_ref[...], lg)
    k = k_ref[...].astype(F32)
    qt = (q_ref[...].astype(F32) * GLA_DK ** -0.5 * jnp.exp(b)).astype(BF16)
    kt = (k * jnp.exp(-b)).astype(BF16)
    v = v_ref[...]
    v_t = v.astype(F32).T.astype(BF16)

    khead = lax.broadcasted_iota(jnp.int32, (CHUNK, GLA_KEY_WIDTH), 1) // GLA_DK
    vlane = lax.broadcasted_iota(jnp.int32, (CHUNK, GLA_WIDTH), 1)
    vhead = vlane // GLA_DV
    trow = lax.broadcasted_iota(jnp.int32, (CHUNK, GLA_WIDTH), 0)
    tri_ok = (vlane % CHUNK <= trow) if fwd else (vlane % CHUNK >= trow)
    st_head_r = lax.broadcasted_iota(jnp.int32, (GLA_WIDTH, GLA_KEY_WIDTH), 0) // GLA_DV
    st_head_c = lax.broadcasted_iota(jnp.int32, (GLA_WIDTH, GLA_KEY_WIDTH), 1) // GLA_DK
    same_head = st_head_r == st_head_c
    pair_row = lax.broadcasted_iota(jnp.int32, (2 * CHUNK, GLA_KEY_WIDTH), 0) // CHUNK
    dn_nt = (((1,), (1,)), ((), ()))

    def stack_heads(x, head_of_lane):
        zero = jnp.zeros_like(x)
        return jnp.concatenate([jnp.where(head_of_lane == h, x, zero) for h in range(GLA_HEADS)], axis=0)

    s_t = s_scr[...]
    for c in order:
        ts = slice(c * CHUNK, (c + 1) * CHUNK)
        c2 = c // 2
        ts2 = slice(c2 * 2 * CHUNK, (c2 + 1) * 2 * CHUNK)
        last = c * CHUNK + (CHUNK - 1 if fwd else 0)
        a = lax.dot_general(qt[ts], stack_heads(kt[ts], khead), dn_nt, preferred_element_type=F32)
        a = jnp.where(tri_ok, a, 0.0)
        o_ref[ts, :] = (jnp.dot(a.astype(BF16), stack_heads(v[ts], vhead), preferred_element_type=F32)
                        + lax.dot_general(qt[ts], s_t.astype(BF16), dn_nt, preferred_element_type=F32))
        kdec = jnp.where(pair_row == c % 2, k[ts2] * jnp.exp(b[last:last + 1] - b[ts2]), 0.0)
        upd = jnp.dot(v_t[:, ts2], kdec.astype(BF16), preferred_element_type=F32)
        s_t = jnp.exp(b[last:last + 1]) * s_t + jnp.where(same_head, upd, 0.0)
    s_scr[...] = s_t


def _scans_kernel(mq_f, mk_f, mv_f, gq_f, gk_f, gv_f, g_f, mq_b, mk_b, mv_b, gq_b, gk_b, gv_b, g_b,
                  gbias, tri_f, tri_b, exp_f, exp_b, gup_f, gup_b, glb_f, glb_b,
                  om_f, om_b, og_f, og_b, c_f, n_f, m_f, c_b, n_b, m_b, s_f, s_b):
    @pl.when(pl.program_id(0) == 0)
    def _():
        for scr in (c_f, n_f, m_f, c_b, n_b, m_b, s_f, s_b):
            scr[...] = jnp.zeros_like(scr)

    _mlstm_kernel(mq_f, mk_f, mv_f, g_f, gbias, tri_f, exp_f, om_f, c_f, n_f, m_f, direction=0)
    _mlstm_kernel(mq_b, mk_b, mv_b, g_b, gbias, tri_b, exp_b, om_b, c_b, n_b, m_b, direction=1)
    _gla_kernel(gq_f, gk_f, gv_f, g_f, gup_f, glb_f, tri_f, og_f, s_f, direction=0)
    _gla_kernel(gq_b, gk_b, gv_b, g_b, gup_b, glb_b, tri_b, og_b, s_b, direction=1)


def _scans_call(main, gates, gate_bias, tri, expand, gup_pad, gla_b):
    t = main.shape[0]
    nb = t // TB
    const = lambda i: (0, 0)

    def streams(direction):
        blk = lambda i: _scan_block(i, nb, direction)
        col = lambda c, w: (lambda i: (blk(i), c // w))
        return [pl.BlockSpec((TB, MLSTM_WIDTH), col(C_MQ, MLSTM_WIDTH)),
                pl.BlockSpec((TB, MLSTM_WIDTH), col(C_MK, MLSTM_WIDTH)),
                pl.BlockSpec((TB, MLSTM_WIDTH), col(C_MV, MLSTM_WIDTH)),
                pl.BlockSpec((TB, GLA_KEY_WIDTH), col(C_GQ, GLA_KEY_WIDTH)),
                pl.BlockSpec((TB, GLA_KEY_WIDTH), col(C_GK, GLA_KEY_WIDTH)),
                pl.BlockSpec((TB, GLA_WIDTH), col(C_GV, GLA_WIDTH)),
                pl.BlockSpec((TB, N_GATE), lambda i: (blk(i), 0))]

    consts = [pl.BlockSpec((1, N_GATE), const),
              pl.BlockSpec((TB, TB), const), pl.BlockSpec((TB, TB), const),
              pl.BlockSpec((N_GATE, MLSTM_WIDTH), const), pl.BlockSpec((N_GATE, MLSTM_WIDTH), const),
              pl.BlockSpec((N_GATE, GLA_KEY_WIDTH), const), pl.BlockSpec((N_GATE, GLA_KEY_WIDTH), const),
              pl.BlockSpec((1, GLA_KEY_WIDTH), const), pl.BlockSpec((1, GLA_KEY_WIDTH), const)]
    out_spec = lambda direction, w: pl.BlockSpec((TB, w), lambda i: (_scan_block(i, nb, direction), 0))
    n_pairs = MLSTM_HEADS // 2
    mlstm_state = [pltpu.VMEM((n_pairs, LANES, LANES), F32), pltpu.VMEM((n_pairs, LANES, LANES), F32),
                   pltpu.VMEM((n_pairs, SUBLANES, LANES), F32)]
    gla_state = [pltpu.VMEM((GLA_WIDTH, GLA_KEY_WIDTH), F32)]
    return pl.pallas_call(
        _scans_kernel,
        grid=(nb,),
        in_specs=streams(0) + streams(1) + consts,
        out_specs=[out_spec(0, MLSTM_WIDTH), out_spec(1, MLSTM_WIDTH), out_spec(0, GLA_WIDTH), out_spec(1, GLA_WIDTH)],
        out_shape=[jax.ShapeDtypeStruct((t, MLSTM_WIDTH), F32)] * 2 + [jax.ShapeDtypeStruct((t, GLA_WIDTH), F32)] * 2,
        scratch_shapes=mlstm_state + mlstm_state + gla_state + gla_state,
        compiler_params=_cparams(1),
        name="scans",
    )(*([main] * 6 + [gates]) * 2, gate_bias, tri[0], tri[1], expand[0], expand[1],
      gup_pad[0], gup_pad[1], gla_b[0:1], gla_b[1:2])


def _attn_kernel(sink_ref, q_ref, kp_ref, kc_ref, kn_ref, kx_ref, vp_ref, vc_ref, vn_ref, vx_ref, o_ref, *, seq):
    i = pl.program_id(0)
    half = TB // 2
    n_loc = 2 * TB
    keys = jnp.concatenate([kp_ref[...], kc_ref[...], kn_ref[...], kx_ref[...]], axis=0)
    vals = jnp.concatenate([vp_ref[...], vc_ref[...], vn_ref[...], vx_ref[...]], axis=0)
    n_keys = keys.shape[0]
    lane = lax.broadcasted_iota(jnp.int32, (n_keys, LANES), 1)
    zero = jnp.zeros_like(keys)
    keys_g = [jnp.where(lane < HEAD_DIM, keys, zero), jnp.where(lane >= HEAD_DIM, keys, zero)]
    r = lax.broadcasted_iota(jnp.int32, (TB, n_keys), 0)
    c = lax.broadcasted_iota(jnp.int32, (TB, n_keys), 1)
    rel = c - half - r
    kpos = (i - 1) * TB + c - half
    local_ok = (jnp.abs(rel) <= WINDOW) & (kpos >= 0) & (kpos < seq) & (i > 0)
    valid = local_ok | (c >= n_loc)
    out_lane = lax.broadcasted_iota(jnp.int32, (TB, LANES), 1)
    neg_inf = jnp.float32(-jnp.inf)
    dn = (((1,), (1,)), ((), ()))
    for m in range(ATTN_Q_WIDTH // LANES):
        qm = q_ref[:, m * LANES:(m + 1) * LANES]
        res = []
        for gidx in range(ATTN_KV_HEADS):
            sink = sink_ref[gidx * (ATTN_HEADS // ATTN_KV_HEADS) + m]
            s = lax.dot_general(qm, keys_g[gidx], dn, preferred_element_type=F32)
            s = jnp.where(valid, s, neg_inf)
            mx = jnp.maximum(jnp.max(s, axis=1, keepdims=True), sink)
            p = jnp.exp(s - mx)
            denom = jnp.sum(p, axis=1, keepdims=True) + jnp.exp(sink - mx)
            pv = jnp.dot(p.astype(BF16), vals, preferred_element_type=F32)
            res.append(pv / denom)
        o_ref[:, m * LANES:(m + 1) * LANES] = jnp.where(out_lane < HEAD_DIM, res[0], res[1]).astype(BF16)


def _attn_call(main, sink, seq):
    t = main.shape[0]
    nb = t // TB
    half = TB // 2
    nhb = t // half
    kcol, vcol = C_AK // LANES, C_AV // LANES
    prev_i = lambda i: jnp.maximum(2 * i - 1, 0)
    next_i = lambda i: jnp.minimum(2 * i + 2, nhb - 1)
    grid_spec = pltpu.PrefetchScalarGridSpec(
        num_scalar_prefetch=1,
        grid=(nb,),
        in_specs=[pl.BlockSpec((TB, ATTN_Q_WIDTH), lambda i, s: (i, C_AQ // ATTN_Q_WIDTH)),
                  pl.BlockSpec((half, LANES), lambda i, s: (prev_i(i), kcol)),
                  pl.BlockSpec((TB, LANES), lambda i, s: (i, kcol)),
                  pl.BlockSpec((half, LANES), lambda i, s: (next_i(i), kcol)),
                  pl.BlockSpec((TB, LANES), lambda i, s: (0, kcol)),
                  pl.BlockSpec((half, LANES), lambda i, s: (prev_i(i), vcol)),
                  pl.BlockSpec((TB, LANES), lambda i, s: (i, vcol)),
                  pl.BlockSpec((half, LANES), lambda i, s: (next_i(i), vcol)),
                  pl.BlockSpec((TB, LANES), lambda i, s: (0, vcol))],
        out_specs=pl.BlockSpec((TB, ATTN_Q_WIDTH), lambda i, s: (i, 0)),
    )
    return pl.pallas_call(
        functools.partial(_attn_kernel, seq=seq),
        grid_spec=grid_spec,
        out_shape=jax.ShapeDtypeStruct((t, ATTN_Q_WIDTH), BF16),
        compiler_params=_cparams(1),
        name="window_attn",
    )(sink, main, main, main, main, main, main, main, main, main)


def _head_norm(x, avg_bf16, w):
    mu = _dot_exact_rhs(x, avg_bf16)
    cen = x - mu
    var = _dot_exact_rhs(cen * cen, avg_bf16)
    return cen * lax.rsqrt(var + LN_EPS) * w


def _outproj_kernel(mf_ref, mb_ref, mo_ref, at_ref, gf_ref, gbk_ref, go_ref, mnw_ref, gnw_ref, avg_ref,
                    w_ref, x_ref, mod_ref, lnw_ref, lnb_ref, x1_ref, h2_ref):
    avg = avg_ref[...]
    hm = _head_norm(mf_ref[...] + mb_ref[...], avg, mnw_ref[...]) * _sigmoid(mo_ref[...].astype(F32))
    gate = go_ref[...].astype(F32)
    hg = _head_norm(gf_ref[...] + gbk_ref[...], avg, gnw_ref[...]) * (gate * _sigmoid(gate))
    a0, a1, a2 = MLSTM_WIDTH, MLSTM_WIDTH + ATTN_Q_WIDTH, D_MODEL
    mix = (jnp.dot(hm.astype(BF16), w_ref[:a0], preferred_element_type=F32)
           + jnp.dot(at_ref[...], w_ref[a0:a1], preferred_element_type=F32)
           + jnp.dot(hg.astype(BF16), w_ref[a1:a2], preferred_element_type=F32))
    m = mod_ref[...]
    x1 = _layer_norm(DEEPNORM_ALPHA * x_ref[...] + m[2:3] * mix, lnw_ref[...], lnb_ref[...])
    x1_ref[...] = x1
    h2_ref[...] = (x1 * (1.0 + m[4:5]) + m[3:4]).astype(h2_ref.dtype)


def _outproj_call(mf, mb, main, attn, gf, gbk, mnw, gnw, avg, w_out, layer, x, mod, lnw, lnb, h2_dtype):
    t, d = x.shape
    nb = t // TB
    row = lambda i: (i, 0)
    const = lambda i: (0, 0)
    return pl.pallas_call(
        _outproj_kernel,
        grid=(nb,),
        in_specs=[pl.BlockSpec((TB, MLSTM_WIDTH), row),
                  pl.BlockSpec((TB, MLSTM_WIDTH), row),
                  pl.BlockSpec((TB, MLSTM_WIDTH), lambda i: (i, C_MO // MLSTM_WIDTH)),
                  pl.BlockSpec((TB, ATTN_Q_WIDTH), row),
                  pl.BlockSpec((TB, GLA_WIDTH), row),
                  pl.BlockSpec((TB, GLA_WIDTH), row),
                  pl.BlockSpec((TB, GLA_WIDTH), lambda i: (i, C_GO // GLA_WIDTH)),
                  pl.BlockSpec((1, MLSTM_WIDTH), const),
                  pl.BlockSpec((1, GLA_WIDTH), const),
                  pl.BlockSpec((MLSTM_WIDTH, MLSTM_WIDTH), const),
                  pl.BlockSpec((None, d, d), lambda i: (layer, 0, 0), pipeline_mode=pl.Buffered(1)),
                  pl.BlockSpec((TB, d), row),
                  pl.BlockSpec((None, SUBLANES, d), lambda i: (_who(i), 0, 0)),
                  pl.BlockSpec((1, d), const),
                  pl.BlockSpec((1, d), const)],
        out_specs=[pl.BlockSpec((TB, d), row), pl.BlockSpec((TB, d), row)],
        out_shape=[jax.ShapeDtypeStruct((t, d), F32), jax.ShapeDtypeStruct((t, d), h2_dtype)],
        compiler_params=_cparams(1),
        name="out_proj",
    )(mf, mb, main, attn, gf, gbk, main, mnw, gnw, avg, w_out, x, mod, lnw, lnb)


def _select_mod(mod_ref, k, i, tm):
    rows = i * tm + lax.broadcasted_iota(jnp.int32, (tm, 1), 0)
    return jnp.where(rows < CTX_LEN, mod_ref[0, k:k + 1, :], mod_ref[1, k:k + 1, :])


def _ffn_kernel(h_ref, wgu_ref, wd_ref, x_ref, mod_ref, lnw_ref, lnb_ref, o_ref):
    h = h_ref[...]
    g = jnp.dot(h, wgu_ref[:, :D_FF], preferred_element_type=F32)
    u = jnp.dot(h, wgu_ref[:, D_FF:], preferred_element_type=F32)
    y = jnp.dot((g * _sigmoid(g) * u).astype(BF16), wd_ref[...], preferred_element_type=F32)
    m5 = _select_mod(mod_ref, 5, pl.program_id(0), h_ref.shape[0])
    o_ref[...] = _layer_norm(DEEPNORM_ALPHA * x_ref[...] + m5 * y, lnw_ref[...], lnb_ref[...])


def _ffn_call(h2, w_gu, w_down, layer, x1, mod, lnw, lnb):
    t, d = x1.shape
    resident = pl.Buffered(1)
    return pl.pallas_call(
        _ffn_kernel,
        grid=(t // FFN_TM,),
        in_specs=[pl.BlockSpec((FFN_TM, d), lambda i: (i, 0)),
                  pl.BlockSpec((None, d, 2 * D_FF), lambda i: (layer, 0, 0), pipeline_mode=resident),
                  pl.BlockSpec((None, D_FF, d), lambda i: (layer, 0, 0), pipeline_mode=resident),
                  pl.BlockSpec((FFN_TM, d), lambda i: (i, 0)),
                  pl.BlockSpec((2, SUBLANES, d), lambda i: (0, 0, 0)),
                  pl.BlockSpec((1, d), lambda i: (0, 0)),
                  pl.BlockSpec((1, d), lambda i: (0, 0))],
        out_specs=pl.BlockSpec((FFN_TM, d), lambda i: (i, 0)),
        out_shape=jax.ShapeDtypeStruct((t, d), F32),
        compiler_params=_cparams(1),
        name="dense_ffn",
    )(h2, w_gu, w_down, x1, mod, lnw, lnb)


def _router_kernel(h_ref, wr_ref, br_ref, su_ref, route_ref, cnt_ref, base_scr):
    @pl.when(pl.program_id(0) == 0)
    def _():
        base_scr[...] = jnp.zeros_like(base_scr)

    lt = lax.dot_general(wr_ref[...], h_ref[...], (((1,), (1,)), ((), ())),
                         preferred_element_type=F32, precision=HIGHEST) + br_ref[...]
    idx = lax.broadcasted_iota(jnp.int32, lt.shape, 0)
    neg_inf = jnp.float32(-jnp.inf)
    m1 = jnp.max(lt, axis=0, keepdims=True)
    e1 = jnp.min(jnp.where(lt == m1, idx, N_EXPERTS), axis=0, keepdims=True)
    lt2 = jnp.where(idx == e1, neg_inf, lt)
    m2 = jnp.max(lt2, axis=0, keepdims=True)
    e2 = jnp.min(jnp.where(lt2 == m2, idx, N_EXPERTS), axis=0, keepdims=True)
    t2 = jnp.exp(m2 - m1)
    w1 = 1.0 / (1.0 + t2)
    w2 = t2 / (1.0 + t2)
    su = su_ref[...]
    base = base_scr[...][:, 0:1]
    oh1 = (idx == e1).astype(F32)
    cum1 = jnp.dot(oh1.astype(BF16), su, preferred_element_type=F32)
    rank1 = jnp.sum(oh1 * (base + cum1), axis=0, keepdims=True)
    base = base + jnp.sum(oh1, axis=1, keepdims=True)
    oh2 = (idx == e2).astype(F32)
    cum2 = jnp.dot(oh2.astype(BF16), su, preferred_element_type=F32)
    rank2 = jnp.sum(oh2 * (base + cum2), axis=0, keepdims=True)
    base = base + jnp.sum(oh2, axis=1, keepdims=True)
    base_scr[...] = jnp.broadcast_to(base, base_scr.shape)
    cnt_ref[...] = jnp.broadcast_to(base, cnt_ref.shape)
    zero = jnp.zeros_like(w1)
    route_ref[...] = jnp.concatenate(
        [e1.astype(F32), e2.astype(F32), w1, w2, rank1, rank2, zero, zero], axis=0)


def _router_call(h2, wr_t, br, su):
    t, d = h2.shape
    nb = t // TB
    return pl.pallas_call(
        _router_kernel,
        grid=(nb,),
        in_specs=[pl.BlockSpec((TB, d), lambda i: (i, 0)),
                  pl.BlockSpec((N_EXPERTS, d), lambda i: (0, 0)),
                  pl.BlockSpec((N_EXPERTS, 1), lambda i: (0, 0)),
                  pl.BlockSpec((TB, TB), lambda i: (0, 0))],
        out_specs=[pl.BlockSpec((SUBLANES, TB), lambda i: (0, i)),
                   pl.BlockSpec((N_EXPERTS, LANES), lambda i: (0, 0))],
        out_shape=[jax.ShapeDtypeStruct((SUBLANES, t), F32),
                   jax.ShapeDtypeStruct((N_EXPERTS, LANES), F32)],
        scratch_shapes=[pltpu.VMEM((N_EXPERTS, LANES), F32)],
        compiler_params=_cparams(1),
        name="moe_router",
    )(h2, wr_t, br, su)


ROW_TILE = D_MODEL // LANES
DMA_UNROLL = 8


def _store_row_tiles(ref, val):
    n = val.shape[0]
    for j in range(ROW_TILE):
        ref[pl.ds(j, n, stride=ROW_TILE), :] = val[:, j * LANES:(j + 1) * LANES]


def _load_row_tiles(ref, n, dtype=F32):
    return jnp.concatenate([ref[pl.ds(j, n, stride=ROW_TILE), :].astype(dtype) for j in range(ROW_TILE)], axis=1)


def _row(ref, idx):
    return ref.at[pl.ds(pl.multiple_of(idx * ROW_TILE, ROW_TILE), ROW_TILE)]


def _dispatch_kernel(s1_ref, s2_ref, h_ref, xs_in_ref, xs_ref, hr_scr, sem):
    del xs_in_ref
    i = pl.program_id(0)
    _store_row_tiles(hr_scr, h_ref[...])

    def copy(r, slot_ref):
        return pltpu.make_async_copy(_row(hr_scr, r), _row(xs_ref, slot_ref[i * TB + r]), sem)

    def start(r, carry):
        copy(r, s1_ref).start()
        copy(r, s2_ref).start()
        return carry

    def wait(r, carry):
        copy(r, s1_ref).wait()
        copy(r, s2_ref).wait()
        return carry

    lax.fori_loop(0, TB, start, 0, unroll=DMA_UNROLL)
    lax.fori_loop(0, TB, wait, 0, unroll=DMA_UNROLL)


def _dispatch_call(slot1, slot2, h2, xs_init):
    t, d = h2.shape
    grid_spec = pltpu.PrefetchScalarGridSpec(
        num_scalar_prefetch=2,
        grid=(t // TB,),
        in_specs=[pl.BlockSpec((TB, d), lambda i, a, b: (i, 0)),
                  pl.BlockSpec(memory_space=pl.ANY)],
        out_specs=pl.BlockSpec(memory_space=pl.ANY),
        scratch_shapes=[pltpu.VMEM((TB * ROW_TILE, LANES), F32), pltpu.SemaphoreType.DMA(())],
    )
    return pl.pallas_call(
        _dispatch_kernel,
        grid_spec=grid_spec,
        out_shape=jax.ShapeDtypeStruct(xs_init.shape, xs_init.dtype),
        input_output_aliases={3: 0},
        compiler_params=_cparams(1),
        name="moe_dispatch",
    )(slot1, slot2, h2, xs_init)


def _expert_kernel(be_ref, used_ref, x_ref, wg_ref, wu_ref, wd_ref, y_ref, xb_scr, acc_scr):
    b, f = pl.program_id(0), pl.program_id(1)
    live = b < used_ref[0]

    @pl.when(live & (f == 0))
    def _():
        for j in range(ROW_TILE):
            xb_scr[:, j * LANES:(j + 1) * LANES] = x_ref[pl.ds(j, MOE_G, stride=ROW_TILE), :].astype(BF16)

    @pl.when(live)
    def _():
        x = xb_scr[...]
        g = jnp.dot(x, wg_ref[...], preferred_element_type=F32)
        u = jnp.dot(x, wu_ref[...], preferred_element_type=F32)
        part = jnp.dot((g * _sigmoid(g) * u).astype(BF16), wd_ref[...], preferred_element_type=F32)

        @pl.when(f == 0)
        def _():
            acc_scr[...] = part

        @pl.when(f > 0)
        def _():
            acc_scr[...] += part

    @pl.when(f == pl.num_programs(1) - 1)
    def _():
        @pl.when(live)
        def _():
            _store_row_tiles(y_ref, acc_scr[...])

        @pl.when(jnp.logical_not(live))
        def _():
            y_ref[...] = jnp.zeros_like(y_ref)


def _expert_call(block_e, used, xs, w_gu, w_down, layer):
    d = D_MODEL
    ns = xs.shape[0] // ROW_TILE
    nblk = ns // MOE_G
    nf = D_EXPERT // MOE_TF
    rows = MOE_G * ROW_TILE

    def f_eff(b, f, used_ref):
        return jnp.where(b < used_ref[0], f, nf - 1)

    grid_spec = pltpu.PrefetchScalarGridSpec(
        num_scalar_prefetch=2,
        grid=(nblk, nf),
        in_specs=[pl.BlockSpec((rows, LANES), lambda b, f, be, us: (b, 0)),
                  pl.BlockSpec((None, None, d, MOE_TF), lambda b, f, be, us: (layer, be[b], 0, f_eff(b, f, us))),
                  pl.BlockSpec((None, None, d, MOE_TF),
                               lambda b, f, be, us: (layer, be[b], 0, nf + f_eff(b, f, us))),
                  pl.BlockSpec((None, None, MOE_TF, d), lambda b, f, be, us: (layer, be[b], f_eff(b, f, us), 0))],
        out_specs=pl.BlockSpec((rows, LANES), lambda b, f, be, us: (b, 0)),
        scratch_shapes=[pltpu.VMEM((MOE_G, d), BF16), pltpu.VMEM((MOE_G, d), F32)],
    )
    return pl.pallas_call(
        _expert_kernel,
        grid_spec=grid_spec,
        out_shape=jax.ShapeDtypeStruct(xs.shape, F32),
        compiler_params=_cparams(2),
        name="moe_experts",
    )(block_e, used, xs, w_gu, w_gu, w_down)


def _combine_kernel(s1_ref, s2_ref, yb_ref, route_ref, x_ref, mod_ref, lnw_ref, lnb_ref, o_ref, buf1, buf2, sem):
    i = pl.program_id(0)

    def copy(r, slot_ref, buf):
        return pltpu.make_async_copy(_row(yb_ref, slot_ref[i * TB + r]), _row(buf, r), sem)

    def start(r, carry):
        copy(r, s1_ref, buf1).start()
        copy(r, s2_ref, buf2).start()
        return carry

    def wait(r, carry):
        copy(r, s1_ref, buf1).wait()
        copy(r, s2_ref, buf2).wait()
        return carry

    lax.fori_loop(0, TB, start, 0, unroll=DMA_UNROLL)
    lax.fori_loop(0, TB, wait, 0, unroll=DMA_UNROLL)
    rt = route_ref[...].T
    y = rt[:, 2:3] * _load_row_tiles(buf1, TB) + rt[:, 3:4] * _load_row_tiles(buf2, TB)
    m = mod_ref[...]
    o_ref[...] = _layer_norm(DEEPNORM_ALPHA * x_ref[...] + m[5:6] * y, lnw_ref[...], lnb_ref[...])


def _combine_call(slot1, slot2, yb, route, x1, mod, lnw, lnb, latent_only):
    t, d = x1.shape
    if latent_only:
        out_rows, out_map = t - CTX_LEN, lambda i, a, b: (jnp.maximum(i - 1, 0), 0)
    else:
        out_rows, out_map = t, lambda i, a, b: (i, 0)
    grid_spec = pltpu.PrefetchScalarGridSpec(
        num_scalar_prefetch=2,
        grid=(t // TB,),
        in_specs=[pl.BlockSpec(memory_space=pl.ANY),
                  pl.BlockSpec((SUBLANES, TB), lambda i, a, b: (0, i)),
                  pl.BlockSpec((TB, d), lambda i, a, b: (i, 0)),
                  pl.BlockSpec((None, SUBLANES, d), lambda i, a, b: (_who(i), 0, 0)),
                  pl.BlockSpec((1, d), lambda i, a, b: (0, 0)),
                  pl.BlockSpec((1, d), lambda i, a, b: (0, 0))],
        out_specs=pl.BlockSpec((TB, d), out_map),
        scratch_shapes=[pltpu.VMEM((TB * ROW_TILE, LANES), F32), pltpu.VMEM((TB * ROW_TILE, LANES), F32),
                        pltpu.SemaphoreType.DMA(())],
    )
    return pl.pallas_call(
        _combine_kernel,
        grid_spec=grid_spec,
        out_shape=jax.ShapeDtypeStruct((out_rows, d), F32),
        compiler_params=_cparams(1),
        name="moe_combine",
    )(slot1, slot2, yb, route, x1, mod, lnw, lnb)


def _moe_layer(h2, x1, mod, lnw, lnb, w_router, b_router, w_gu, w_down, layer, su, latent_only):
    t, d = h2.shape
    route, cnt = _router_call(h2, w_router.T, b_router.reshape(N_EXPERTS, 1), su)
    counts = cnt[:, 0].astype(jnp.int32)
    padded = (counts + MOE_G - 1) // MOE_G * MOE_G
    pend = jnp.cumsum(padded)
    pstart = pend - padded
    e1, e2 = route[0].astype(jnp.int32), route[1].astype(jnp.int32)
    slot1 = pstart[e1] + route[4].astype(jnp.int32)
    slot2 = pstart[e2] + route[5].astype(jnp.int32)
    nblk = -(-(2 * t) // MOE_G) + N_EXPERTS
    blk_start = jnp.arange(nblk, dtype=jnp.int32) * MOE_G
    block_e = jnp.minimum(jnp.sum((pend[None, :] <= blk_start[:, None]).astype(jnp.int32), axis=1), N_EXPERTS - 1)
    used = (pend[-1:] // MOE_G).astype(jnp.int32)
    xs = _dispatch_call(slot1, slot2, h2, jnp.zeros((nblk * MOE_G * ROW_TILE, LANES), F32))
    yb = _expert_call(block_e, used, xs, w_gu, w_down, layer)
    return _combine_call(slot1, slot2, yb, route, x1, mod, lnw, lnb, latent_only)


def _attn_head_perm():
    cols = []
    for m in range(ATTN_HEADS // ATTN_KV_HEADS):
        for hq in (m, m + ATTN_HEADS // ATTN_KV_HEADS):
            cols.extend(range(hq * HEAD_DIM, (hq + 1) * HEAD_DIM))
    return np.asarray(cols, np.int32)


def _in_col_perm():
    n_m = 4 * MLSTM_WIDTH
    mg = 2 * 2 * MLSTM_HEADS
    a0 = n_m + mg
    perm = list(range(n_m))
    perm += [a0 + int(j) for j in _attn_head_perm()]
    perm += list(range(a0 + ATTN_Q_WIDTH, a0 + ATTN_Q_WIDTH + 2 * ATTN_KV_WIDTH))
    g0 = a0 + ATTN_Q_WIDTH + 2 * ATTN_KV_WIDTH
    perm += list(range(g0, g0 + 2 * GLA_KEY_WIDTH + 2 * GLA_WIDTH))
    perm += list(range(n_m, n_m + mg))
    perm += list(range(g0 + 2 * GLA_KEY_WIDTH + 2 * GLA_WIDTH, g0 + 2 * GLA_KEY_WIDTH + 2 * GLA_WIDTH + 2 * GLA_RANK))
    return np.asarray(perm, np.int32)


def _take_runs(w, perm, axis):
    runs, start = [], 0
    for j in range(1, len(perm) + 1):
        if j == len(perm) or perm[j] != perm[j - 1] + 1:
            runs.append(lax.slice_in_dim(w, int(perm[start]), int(perm[j - 1]) + 1, axis=axis))
            start = j
    return jnp.concatenate(runs, axis=axis)


def _rope_tables(seq):
    inv = ROPE_BASE ** (-jnp.arange(ROPE_PAIRS, dtype=F32) / ROPE_PAIRS)
    rows = seq // GRID_W
    ang_r = jnp.arange(rows).astype(F32)[:, None] * inv
    ang_c = jnp.arange(GRID_W).astype(F32)[:, None] * inv
    rep = lambda a: jnp.repeat(a, GRID_W, axis=0)
    til = lambda a: jnp.tile(a, (rows, 1))
    cos_r, sin_r, cos_c, sin_c = rep(jnp.cos(ang_r)), rep(jnp.sin(ang_r)), til(jnp.cos(ang_c)), til(jnp.sin(ang_c))
    cos64 = jnp.concatenate([cos_r, cos_r, cos_c, cos_c], -1)
    sin64 = jnp.concatenate([-sin_r, sin_r, -sin_c, sin_c], -1)
    cos_l = jnp.concatenate([cos64, cos64], -1)
    sin_l = jnp.concatenate([sin64, sin64], -1)
    cos_t = jnp.concatenate([jnp.ones((CTX_LEN, LANES), F32), cos_l], 0)
    sin_t = jnp.concatenate([jnp.zeros((CTX_LEN, LANES), F32), sin_l], 0)
    return cos_t, sin_t


def _block_tri(direction):
    r = np.arange(TB)[:, None]
    c = np.arange(TB)[None, :]
    same = (r // CHUNK) == (c // CHUNK)
    tri = (c <= r) if direction == 0 else (c >= r)
    return jnp.asarray(same & tri, BF16)


def kernel(x, c, ctx, c_ctx, w_ada, b_ada, w_in, mlstm_gate_b, mlstm_norm_w, attn_sink, gla_gate_up, gla_gate_b,
           gla_norm_w, w_out, ln_w, ln_b, ffn_w_gu, ffn_w_down, router_w, router_b, moe_w_gu, moe_w_down):
    seq, d = x.shape[1], x.shape[2]
    depth = w_in.shape[0]
    xt = jnp.concatenate([ctx[0], x[0]], axis=0)

    cvec = jnp.zeros((SUBLANES, d), F32).at[0].set(c_ctx).at[1].set(c[0])
    mods = _mod_call(cvec, w_ada, b_ada)[:, :2].reshape(depth, 2, 6, d)
    mods = jnp.pad(mods, ((0, 0), (0, 0), (0, SUBLANES - 6), (0, 0)))

    perm = _in_col_perm()
    w_in_p = jnp.concatenate([_take_runs(w_in, perm, 2).astype(BF16),
                              jnp.zeros((depth, d, N_PROJ - perm.shape[0]), BF16)], axis=2)
    out_perm = np.concatenate([np.arange(MLSTM_WIDTH), MLSTM_WIDTH + _attn_head_perm(),
                               np.arange(MLSTM_WIDTH + ATTN_Q_WIDTH, d)]).astype(np.int32)
    w_out_p = _take_runs(w_out, out_perm, 1).astype(BF16)
    gate_bias = jnp.pad(mlstm_gate_b.reshape(depth, 1, -1), ((0, 0), (0, 0), (0, N_GATE - 4 * MLSTM_HEADS)))
    gup_pad = jnp.zeros((depth, 2, N_GATE, GLA_KEY_WIDTH), F32)
    for dr in range(2):
        lo = GLA_GATE_OFF + dr * GLA_RANK
        gup_pad = gup_pad.at[:, dr, lo:lo + GLA_RANK, :].set(gla_gate_up[:, dr])
    cos_t, sin_t = _rope_tables(seq)
    tri = [_block_tri(0), _block_tri(1)]
    su = jnp.asarray(np.arange(TB)[:, None] < np.arange(TB)[None, :], BF16)
    hh = np.arange(MLSTM_WIDTH) // MLSTM_DH
    avg = jnp.asarray((hh[:, None] == hh[None, :]) / MLSTM_DH, BF16)
    ffn_gu, ffn_dn = ffn_w_gu.astype(BF16), ffn_w_down.astype(BF16)
    moe_gu, moe_dn = moe_w_gu.astype(BF16), moe_w_down.astype(BF16)

    expand = []
    for dr in range(2):
        e = np.zeros((N_GATE, MLSTM_WIDTH), np.float32)
        for h in range(MLSTM_HEADS):
            e[dr * 2 * MLSTM_HEADS + h, h * MLSTM_DH:(h + 1) * MLSTM_DH] = 1.0
        expand.append(jnp.asarray(e, BF16))

    for l in range(depth):
        is_moe = l % 2 == 1
        last = l == depth - 1
        main, gates = _inproj_call(xt, mods[l], w_in_p, l, cos_t, sin_t)
        mf, mb, gf, gbk = _scans_call(main, gates, gate_bias[l], tri, expand, gup_pad[l], gla_gate_b[l])
        attn = _attn_call(main, attn_sink[l], seq)
        x1, h2 = _outproj_call(mf, mb, main, attn, gf, gbk, mlstm_norm_w[l:l + 1], gla_norm_w[l:l + 1], avg,
                               w_out_p, l, xt, mods[l], ln_w[l, 0:1], ln_b[l, 0:1], F32 if is_moe else BF16)
        if is_moe:
            xt = _moe_layer(h2, x1, mods[l], ln_w[l, 1:2], ln_b[l, 1:2], router_w[l // 2], router_b[l // 2],
                            moe_gu, moe_dn, l // 2, su, latent_only=last)
        else:
            xt = _ffn_call(h2, ffn_gu, ffn_dn, l // 2, x1, mods[l], ln_w[l, 1:2], ln_b[l, 1:2])
    return (xt if depth % 2 == 0 else xt[CTX_LEN:])[None]
```

```python
import functools

import jax
import jax.numpy as jnp
import numpy as np
from jax import lax
from jax.experimental import pallas as pl
from jax.experimental.pallas import tpu as pltpu

F32 = jnp.float32
BF16 = jnp.bfloat16
HIGHEST = lax.Precision.HIGHEST

D_MODEL = 1024
SEQ = 16384
DEPTH = 4
GRID_W = 64
CTX_LEN = 256
MLSTM_HEADS = 4
MLSTM_DH = 64
MLSTM_WIDTH = 256
HEAD_DIM = 64
ATTN_HEADS = 8
ATTN_KV_HEADS = 2
ATTN_Q_WIDTH = 512
ATTN_KV_WIDTH = 128
WINDOW = 128
ROPE_BASE = 10000.0
ROPE_PAIRS = 16
GLA_HEADS = 4
GLA_DK = 32
GLA_DV = 64
GLA_WIDTH = 256
GLA_KEY_WIDTH = 128
GLA_RANK = 16
GLA_TAU = 16.0
CHUNK = 64
D_FF = 2816
N_EXPERTS = 8
D_EXPERT = 3584
DEEPNORM_ALPHA = (2 * DEPTH) ** 0.25
LN_EPS = 1e-5

LANES = 128
SUBLANES = 8
VMEM_LIMIT = 56 * 1024 * 1024

TB = 256
N_CHUNKS = TB // CHUNK
FFN_TM = 640
MOE_G = 512
MOE_TF = 1792

C_MQ, C_MK, C_MV, C_MO = 0, 256, 512, 768
C_AQ, C_AK, C_AV = 1024, 1536, 1664
C_GQ, C_GK, C_GV, C_GO = 1792, 1920, 2048, 2304
N_MAIN = 2560
N_GATE = 128
N_PROJ = N_MAIN + N_GATE
GLA_GATE_OFF = 16


def _cparams(n_axes=1):
    return pltpu.CompilerParams(dimension_semantics=("arbitrary",) * n_axes,
                                vmem_limit_bytes=VMEM_LIMIT)


def _sigmoid(x):
    return 1.0 / (1.0 + jnp.exp(-x))


def _log_sigmoid(x):
    return jnp.minimum(x, 0.0) - jnp.log(1.0 + jnp.exp(-jnp.abs(x)))


def _split3(x):
    hi = x.astype(BF16)
    r1 = x - hi.astype(F32)
    mid = r1.astype(BF16)
    lo = (r1 - mid.astype(F32)).astype(BF16)
    return hi, mid, lo


def _dot_exact_rhs(x, m_bf16):
    hi, mid, lo = _split3(x)
    d = lambda a: jnp.dot(a, m_bf16, preferred_element_type=F32)
    return d(hi) + d(mid) + d(lo)


def _dot_exact_lhs(m_bf16, x):
    hi, mid, lo = _split3(x)
    d = lambda a: jnp.dot(m_bf16, a, preferred_element_type=F32)
    return d(hi) + d(mid) + d(lo)


def _dot_bf16x3(a, b):
    a_hi, b_hi = a.astype(BF16), b.astype(BF16)
    a_lo = (a - a_hi.astype(F32)).astype(BF16)
    b_lo = (b - b_hi.astype(F32)).astype(BF16)
    d = lambda x, y: jnp.dot(x, y, preferred_element_type=F32)
    return d(a_hi, b_hi) + d(a_hi, b_lo) + d(a_lo, b_hi)


def _layer_norm(v, w, b):
    mu = jnp.mean(v, axis=-1, keepdims=True)
    cen = v - mu
    var = jnp.mean(cen * cen, axis=-1, keepdims=True)
    return cen * lax.rsqrt(var + LN_EPS) * w + b


def _mod_kernel(c_ref, w_ref, b_ref, o_ref):
    c = c_ref[...]
    sc = c * _sigmoid(c)
    o_ref[...] = jnp.dot(sc, w_ref[...], preferred_element_type=F32, precision=HIGHEST) + b_ref[...]


def _mod_call(cvec, w_ada, b_ada):
    depth, d, n = w_ada.shape
    tn = 1536
    return pl.pallas_call(
        _mod_kernel,
        grid=(depth, n // tn),
        in_specs=[pl.BlockSpec((SUBLANES, d), lambda l, j: (0, 0)),
                  pl.BlockSpec((None, d, tn), lambda l, j: (l, 0, j)),
                  pl.BlockSpec((None, 1, tn), lambda l, j: (l, 0, j))],
        out_specs=pl.BlockSpec((None, SUBLANES, tn), lambda l, j: (l, 0, j)),
        out_shape=jax.ShapeDtypeStruct((depth, SUBLANES, n), F32),
        compiler_params=_cparams(2),
        name="ada_mod",
    )(cvec, w_ada, b_ada.reshape(depth, 1, n))


def _who(i):
    return jnp.minimum(i, 1)


def _rope(x, cos, sin_signed, first_half):
    swapped = jnp.where(first_half, pltpu.roll(x, LANES - ROPE_PAIRS, 1), pltpu.roll(x, ROPE_PAIRS, 1))
    return x * cos + swapped * sin_signed


def _inproj_kernel(x_ref, mod_ref, w_ref, cos_ref, sin_ref, main_ref, gate_ref):
    m = mod_ref[...]
    h = x_ref[...] * (1.0 + m[1:2]) + m[0:1]
    p = jnp.dot(h.astype(BF16), w_ref[...], preferred_element_type=F32)
    cos, sin = cos_ref[...], sin_ref[...]
    lane = lax.broadcasted_iota(jnp.int32, (TB, LANES), 1)
    first_half = (lane % (2 * ROPE_PAIRS)) < ROPE_PAIRS
    main_ref[:, :C_AQ] = p[:, :C_AQ].astype(BF16)
    for j in range(ATTN_Q_WIDTH // LANES):
        lo = C_AQ + j * LANES
        main_ref[:, lo:lo + LANES] = (_rope(p[:, lo:lo + LANES], cos, sin, first_half) * HEAD_DIM ** -0.5).astype(BF16)
    main_ref[:, C_AK:C_AV] = _rope(p[:, C_AK:C_AV], cos, sin, first_half).astype(BF16)
    main_ref[:, C_AV:] = p[:, C_AV:N_MAIN].astype(BF16)
    gate_ref[...] = p[:, N_MAIN:]


def _inproj_call(x, mod, w, layer, cos_t, sin_t):
    t, d = x.shape
    nb = t // TB
    return pl.pallas_call(
        _inproj_kernel,
        grid=(nb,),
        in_specs=[pl.BlockSpec((TB, d), lambda i: (i, 0)),
                  pl.BlockSpec((None, SUBLANES, d), lambda i: (_who(i), 0, 0)),
                  pl.BlockSpec((None, d, N_PROJ), lambda i: (layer, 0, 0), pipeline_mode=pl.Buffered(1)),
                  pl.BlockSpec((TB, LANES), lambda i: (i, 0)),
                  pl.BlockSpec((TB, LANES), lambda i: (i, 0))],
        out_specs=[pl.BlockSpec((TB, N_MAIN), lambda i: (i, 0)),
                   pl.BlockSpec((TB, N_GATE), lambda i: (i, 0))],
        out_shape=[jax.ShapeDtypeStruct((t, N_MAIN), BF16),
                   jax.ShapeDtypeStruct((t, N_GATE), F32)],
        compiler_params=_cparams(1),
        name="in_proj",
    )(x, mod, w, cos_t, sin_t)


def _scan_block(i, nb, direction):
    if direction == 0:
        return i
    return jnp.where(i == 0, 0, nb - i)


def _scans_kernel(mq_f, mk_f, mv_f, gq_f, gk_f, gv_f, g_f, mq_b, mk_b, mv_b, gq_b, gk_b, gv_b, g_b,
                  gbias, tri_f, tri_b, exp_f, exp_b, gup_f, gup_b, glb_f, glb_b,
                  om_f, om_b, og_f, og_b, c_f, n_f, m_f, c_b, n_b, m_b, s_f, s_b):
    @pl.when(pl.program_id(0) == 0)
    def _():
        for scr in (c_f, n_f, m_f, c_b, n_b, m_b, s_f, s_b):
            scr[...] = jnp.zeros_like(scr)

    _run_interleaved([
        _mlstm_phases(mq_f, mk_f, mv_f, g_f, gbias, tri_f, exp_f, om_f, c_f, n_f, m_f, 0),
        _mlstm_phases(mq_b, mk_b, mv_b, g_b, gbias, tri_b, exp_b, om_b, c_b, n_b, m_b, 1),
        _gla_phases(gq_f, gk_f, gv_f, g_f, gup_f, glb_f, tri_f, og_f, s_f, 0),
        _gla_phases(gq_b, gk_b, gv_b, g_b, gup_b, glb_b, tri_b, og_b, s_b, 1),
    ])


def _run_interleaved(streams):
    for k in range(max(len(s) for s in streams)):
        for s in streams:
            if k < len(s):
                s[k]()


def _mlstm_phases(q_ref, k_ref, v_ref, g_ref, gb_ref, tri_ref, exp_ref, o_ref, c_scr, n_scr, m_scr, direction):
    fwd = direction == 0
    neg_inf = jnp.float32(-jnp.inf)
    order = list(range(N_CHUNKS)) if fwd else list(range(N_CHUNKS - 1, -1, -1))
    pairs = list(range(MLSTM_HEADS // 2))
    inst = [(p, c) for p in pairs for c in order]
    last_of = lambda c: c * CHUNK + (CHUNK - 1 if fwd else 0)
    ts_of = lambda c: slice(c * CHUNK, (c + 1) * CHUNK)
    ts2_of = lambda c: slice((c // 2) * 2 * CHUNK, (c // 2 + 1) * 2 * CHUNK)
    lp_of = lambda p: slice(p * LANES, (p + 1) * LANES)
    dn_nt = (((1,), (1,)), ((), ()))
    st = {}

    lane = lax.broadcasted_iota(jnp.int32, (CHUNK, LANES), 1)
    trow = lax.broadcasted_iota(jnp.int32, (CHUNK, LANES), 0)
    low = lane < MLSTM_DH
    tri_ok = (lane % MLSTM_DH <= trow) if fwd else (lane % MLSTM_DH >= trow)
    lane2 = lax.broadcasted_iota(jnp.int32, (LANES, LANES), 1)
    row2 = lax.broadcasted_iota(jnp.int32, (LANES, LANES), 0)
    same_head = (lane2 < MLSTM_DH) == (row2 < MLSTM_DH)
    bd_ones = jnp.where(same_head, 1.0, 0.0).astype(BF16)
    lane_row = lax.broadcasted_iota(jnp.int32, (1, LANES), 1)
    pair_half = lax.broadcasted_iota(jnp.int32, (2 * CHUNK, LANES), 0) // CHUNK

    def pair_blockdiag(x):
        zero = jnp.zeros_like(x)
        return jnp.concatenate([jnp.where(low, x, zero), jnp.where(low, zero, x)], axis=0)

    def ph_cumsum():
        st['g'] = g_ref[...] + gb_ref[...]
        st['b_f'] = _dot_exact_lhs(tri_ref[...], _log_sigmoid(st['g']))

    def ph_gates():
        b_i = pltpu.roll(st['b_f'], LANES - MLSTM_HEADS, 1)
        gr = st['g'] - b_i
        tok = lax.broadcasted_iota(jnp.int32, (TB, LANES), 0) % CHUNK
        aloc = gr
        for sh in (1, 2, 4, 8, 16, 32):
            if fwd:
                shifted, ok = pltpu.roll(aloc, sh, 0), tok >= sh
            else:
                shifted, ok = pltpu.roll(aloc, TB - sh, 0), tok < CHUNK - sh
            aloc = jnp.maximum(aloc, jnp.where(ok, shifted, neg_inf))
        chunk_max = jnp.concatenate(
            [jnp.broadcast_to(aloc[last_of(c):last_of(c) + 1], (CHUNK, LANES)) for c in range(N_CHUNKS)], axis=0)
        st['b_i'], st['aloc'] = b_i, aloc
        st['wloc'] = jnp.exp(gr - chunk_max)
        st['gr_t'] = gr.T

    def ph_expand():
        expand = exp_ref[...]
        st['aloc_x'] = _dot_exact_rhs(st['aloc'], expand)
        st['b_x'] = _dot_exact_rhs(st['b_i'], expand)
        st['wloc_x'] = _dot_exact_rhs(st['wloc'], expand)

    def ph_scores():
        for p, c in inst:
            kp = k_ref[ts_of(c), lp_of(p)] * MLSTM_DH ** -0.5
            st['s', p, c] = lax.dot_general(q_ref[ts_of(c), lp_of(p)], pair_blockdiag(kp), dn_nt,
                                            preferred_element_type=F32)
        for p in pairs:
            for c2 in range(N_CHUNKS // 2):
                st['kt2', p, c2] = (k_ref[ts2_of(2 * c2), lp_of(p)].astype(F32) * MLSTM_DH ** -0.5).T.astype(BF16)

    def ph_weights():
        for p, c in inst:
            ji0 = direction * 2 * MLSTM_HEADS + 2 * p
            ra = st['gr_t'][ji0:ji0 + 1, ts2_of(c)]
            rb = st['gr_t'][ji0 + 1:ji0 + 2, ts2_of(c)]
            if c % 2 == 0:
                g_row = jnp.where(lane_row < MLSTM_DH, ra, pltpu.roll(rb, MLSTM_DH, 1))
            else:
                g_row = jnp.where(lane_row < MLSTM_DH, pltpu.roll(ra, MLSTM_DH, 1), rb)
            al = st['aloc_x'][ts_of(c), lp_of(p)]
            s = st['s', p, c] * jnp.exp(jnp.where(tri_ok, g_row - al, neg_inf))
            s_hi = s.astype(BF16)
            st['s_hi', p, c] = s_hi
            st['s_lo', p, c] = (s - s_hi.astype(F32)).astype(BF16)
            wx2 = st['wloc_x'][ts2_of(c), lp_of(p)]
            in_chunk = pair_half == (c % 2)
            wv = jnp.where(in_chunk, wx2 * v_ref[ts2_of(c), lp_of(p)].astype(F32), 0.0)
            ww = jnp.where(in_chunk, wx2, 0.0)
            st['wvw', p, c] = jnp.concatenate([wv, ww], axis=1).astype(BF16)

    def ph_intra():
        for p, c in inst:
            vp = v_ref[ts_of(c), lp_of(p)]
            st['r1', p, c] = jnp.dot(st['s_hi', p, c], jnp.concatenate([pair_blockdiag(vp), bd_ones], axis=1),
                                     preferred_element_type=F32)
            st['den_lo', p, c] = jnp.dot(st['s_lo', p, c], bd_ones, preferred_element_type=F32)
            st['upd', p, c] = jnp.dot(st['kt2', p, c // 2], st['wvw', p, c], preferred_element_type=F32)

    def ph_state():
        for p in pairs:
            bdc, bdn = c_scr[p], n_scr[p]
            m_row = m_scr[p][0:1]
            for c in order:
                st['cn', p, c] = jnp.concatenate([bdc, bdn], axis=1).astype(BF16)
                st['m', p, c] = m_row
                last = last_of(c)
                aloc_last = st['aloc_x'][last:last + 1, lp_of(p)]
                a_last = jnp.maximum(m_row, aloc_last)
                decay = jnp.exp(m_row - a_last)
                scale = jnp.exp(aloc_last - a_last)
                upd = st['upd', p, c]
                bdc = decay * bdc + scale * jnp.where(same_head, upd[:, :LANES], 0.0)
                bdn = decay * bdn + scale * jnp.where(same_head, upd[:, LANES:], 0.0)
                m_row = st['b_x'][last:last + 1, lp_of(p)] + a_last
            c_scr[p] = bdc
            n_scr[p] = bdn
            m_scr[p] = jnp.broadcast_to(m_row, (SUBLANES, LANES))

    def ph_inter():
        for p, c in inst:
            st['r2', p, c] = jnp.dot(q_ref[ts_of(c), lp_of(p)], st['cn', p, c],
                                     preferred_element_type=F32)

    def ph_out():
        for p, c in inst:
            al = st['aloc_x'][ts_of(c), lp_of(p)]
            m_row, r1, r2 = st['m', p, c], st['r1', p, c], st['r2', p, c]
            a = jnp.maximum(m_row, al)
            corr = jnp.exp(al - a)
            w_inter = jnp.exp(m_row - a)
            num = corr * r1[:, :LANES] + w_inter * r2[:, :LANES]
            den = corr * (r1[:, LANES:] + st['den_lo', p, c]) + w_inter * r2[:, LANES:]
            o_ref[ts_of(c), lp_of(p)] = num / jnp.maximum(jnp.abs(den),
                                                          jnp.exp(-(st['b_x'][ts_of(c), lp_of(p)] + a)))

    return [ph_cumsum, ph_gates, ph_expand, ph_scores, ph_weights, ph_intra, ph_state, ph_inter, ph_out]


def _gla_phases(q_ref, k_ref, v_ref, g_ref, gup_ref, gb_ref, tri_ref, o_ref, s_scr, direction):
    fwd = direction == 0
    order = list(range(N_CHUNKS)) if fwd else list(range(N_CHUNKS - 1, -1, -1))
    ts_of = lambda c: slice(c * CHUNK, (c + 1) * CHUNK)
    ts2_of = lambda c: slice((c // 2) * 2 * CHUNK, (c // 2 + 1) * 2 * CHUNK)
    last_of = lambda c: c * CHUNK + (CHUNK - 1 if fwd else 0)
    dn_nt = (((1,), (1,)), ((), ()))
    st = {}

    khead = lax.broadcasted_iota(jnp.int32, (CHUNK, GLA_KEY_WIDTH), 1) // GLA_DK
    vlane = lax.broadcasted_iota(jnp.int32, (CHUNK, GLA_WIDTH), 1)
    vhead = vlane // GLA_DV
    trow = lax.broadcasted_iota(jnp.int32, (CHUNK, GLA_WIDTH), 0)
    tri_ok = (vlane % CHUNK <= trow) if fwd else (vlane % CHUNK >= trow)
    st_head_r = lax.broadcasted_iota(jnp.int32, (GLA_WIDTH, GLA_KEY_WIDTH), 0) // GLA_DV
    st_head_c = lax.broadcasted_iota(jnp.int32, (GLA_WIDTH, GLA_KEY_WIDTH), 1) // GLA_DK
    same_head = st_head_r == st_head_c
    pair_row = lax.broadcasted_iota(jnp.int32, (2 * CHUNK, GLA_KEY_WIDTH), 0) // CHUNK

    def stack_heads(x, head_of_lane):
        zero = jnp.zeros_like(x)
        return jnp.concatenate([jnp.where(head_of_lane == h, x, zero) for h in range(GLA_HEADS)], axis=0)

    def ph_gate_proj():
        st['z'] = _dot_bf16x3(g_ref[...], gup_ref[...]) + gb_ref[...]

    def ph_log_decay():
        st['lg'] = _log_sigmoid(st['z']) * (1.0 / GLA_TAU)

    def ph_cumsum():
        st['b'] = _dot_exact_lhs(tri_ref[...], st['lg'])

    def ph_prep():
        b = st['b']
        st['k'] = k_ref[...].astype(F32)
        st['qt'] = (q_ref[...].astype(F32) * GLA_DK ** -0.5 * jnp.exp(b)).astype(BF16)
        st['kt'] = (st['k'] * jnp.exp(-b)).astype(BF16)
        st['v_t'] = v_ref[...].astype(F32).T.astype(BF16)

    def ph_scores():
        for c in order:
            st['a', c] = lax.dot_general(st['qt'][ts_of(c)], stack_heads(st['kt'][ts_of(c)], khead), dn_nt,
                                         preferred_element_type=F32)

    def ph_mask():
        b, k = st['b'], st['k']
        for c in order:
            st['a', c] = jnp.where(tri_ok, st['a', c], 0.0).astype(BF16)
            last = last_of(c)
            st['kdec', c] = jnp.where(pair_row == c % 2, k[ts2_of(c)] * jnp.exp(b[last:last + 1] - b[ts2_of(c)]),
                                      0.0).astype(BF16)

    def ph_intra():
        for c in order:
            st['o', c] = jnp.dot(st['a', c], stack_heads(v_ref[ts_of(c), :], vhead), preferred_element_type=F32)
            st['upd', c] = jnp.dot(st['v_t'][:, ts2_of(c)], st['kdec', c], preferred_element_type=F32)

    def ph_state():
        s_t = s_scr[...]
        for c in order:
            st['s', c] = s_t.astype(BF16)
            last = last_of(c)
            s_t = jnp.exp(st['b'][last:last + 1]) * s_t + jnp.where(same_head, st['upd', c], 0.0)
        s_scr[...] = s_t

    def ph_inter():
        for c in order:
            st['oi', c] = lax.dot_general(st['qt'][ts_of(c)], st['s', c], dn_nt, preferred_element_type=F32)

    def ph_out():
        for c in order:
            o_ref[ts_of(c), :] = st['o', c] + st['oi', c]

    return [ph_gate_proj, ph_log_decay, ph_cumsum, ph_prep, ph_scores, ph_mask, ph_intra, ph_state, ph_inter, ph_out]


def _scans_call(main, gates, gate_bias, tri, expand, gup_pad, gla_b):
    t = main.shape[0]
    nb = t // TB
    const = lambda i: (0, 0)

    def streams(direction):
        blk = lambda i: _scan_block(i, nb, direction)
        col = lambda c, w: (lambda i: (blk(i), c // w))
        return [pl.BlockSpec((TB, MLSTM_WIDTH), col(C_MQ, MLSTM_WIDTH)),
                pl.BlockSpec((TB, MLSTM_WIDTH), col(C_MK, MLSTM_WIDTH)),
                pl.BlockSpec((TB, MLSTM_WIDTH), col(C_MV, MLSTM_WIDTH)),
                pl.BlockSpec((TB, GLA_KEY_WIDTH), col(C_GQ, GLA_KEY_WIDTH)),
                pl.BlockSpec((TB, GLA_KEY_WIDTH), col(C_GK, GLA_KEY_WIDTH)),
                pl.BlockSpec((TB, GLA_WIDTH), col(C_GV, GLA_WIDTH)),
                pl.BlockSpec((TB, N_GATE), lambda i: (blk(i), 0))]

    consts = [pl.BlockSpec((1, N_GATE), const),
              pl.BlockSpec((TB, TB), const), pl.BlockSpec((TB, TB), const),
              pl.BlockSpec((N_GATE, MLSTM_WIDTH), const), pl.BlockSpec((N_GATE, MLSTM_WIDTH), const),
              pl.BlockSpec((N_GATE, GLA_KEY_WIDTH), const), pl.BlockSpec((N_GATE, GLA_KEY_WIDTH), const),
              pl.BlockSpec((1, GLA_KEY_WIDTH), const), pl.BlockSpec((1, GLA_KEY_WIDTH), const)]
    out_spec = lambda direction, w: pl.BlockSpec((TB, w), lambda i: (_scan_block(i, nb, direction), 0))
    n_pairs = MLSTM_HEADS // 2
    mlstm_state = [pltpu.VMEM((n_pairs, LANES, LANES), F32), pltpu.VMEM((n_pairs, LANES, LANES), F32),
                   pltpu.VMEM((n_pairs, SUBLANES, LANES), F32)]
    gla_state = [pltpu.VMEM((GLA_WIDTH, GLA_KEY_WIDTH), F32)]
    return pl.pallas_call(
        _scans_kernel,
        grid=(nb,),
        in_specs=streams(0) + streams(1) + consts,
        out_specs=[out_spec(0, MLSTM_WIDTH), out_spec(1, MLSTM_WIDTH), out_spec(0, GLA_WIDTH), out_spec(1, GLA_WIDTH)],
        out_shape=[jax.ShapeDtypeStruct((t, MLSTM_WIDTH), F32)] * 2 + [jax.ShapeDtypeStruct((t, GLA_WIDTH), F32)] * 2,
        scratch_shapes=mlstm_state + mlstm_state + gla_state + gla_state,
        compiler_params=_cparams(1),
        name="scans",
    )(*([main] * 6 + [gates]) * 2, gate_bias, tri[0], tri[1], expand[0], expand[1],
      gup_pad[0], gup_pad[1], gla_b[0:1], gla_b[1:2])


def _attn_kernel(sink_ref, q_ref, kp_ref, kc_ref, kn_ref, kx_ref, vp_ref, vc_ref, vn_ref, vx_ref, o_ref, *, seq):
    i = pl.program_id(0)
    half = TB // 2
    n_loc = 2 * TB
    keys = jnp.concatenate([kp_ref[...], kc_ref[...], kn_ref[...], kx_ref[...]], axis=0)
    vals = jnp.concatenate([vp_ref[...], vc_ref[...], vn_ref[...], vx_ref[...]], axis=0)
    n_keys = keys.shape[0]
    lane = lax.broadcasted_iota(jnp.int32, (n_keys, LANES), 1)
    zero = jnp.zeros_like(keys)
    keys_g = [jnp.where(lane < HEAD_DIM, keys, zero), jnp.where(lane >= HEAD_DIM, keys, zero)]
    r = lax.broadcasted_iota(jnp.int32, (TB, n_keys), 0)
    c = lax.broadcasted_iota(jnp.int32, (TB, n_keys), 1)
    rel = c - half - r
    kpos = (i - 1) * TB + c - half
    local_ok = (jnp.abs(rel) <= WINDOW) & (kpos >= 0) & (kpos < seq) & (i > 0)
    valid = local_ok | (c >= n_loc)
    out_lane = lax.broadcasted_iota(jnp.int32, (TB, LANES), 1)
    neg_inf = jnp.float32(-jnp.inf)
    dn = (((1,), (1,)), ((), ()))
    heads = [(m, gidx) for m in range(ATTN_Q_WIDTH // LANES) for gidx in range(ATTN_KV_HEADS)]
    scores = [lax.dot_general(q_ref[:, m * LANES:(m + 1) * LANES], keys_g[gidx], dn, preferred_element_type=F32)
              for m, gidx in heads]
    probs, denoms = [], []
    for (m, gidx), s in zip(heads, scores):
        sink = sink_ref[gidx * (ATTN_HEADS // ATTN_KV_HEADS) + m]
        s = jnp.where(valid, s, neg_inf)
        mx = jnp.maximum(jnp.max(s, axis=1, keepdims=True), sink)
        p = jnp.exp(s - mx)
        denoms.append(jnp.sum(p, axis=1, keepdims=True) + jnp.exp(sink - mx))
        probs.append(p.astype(BF16))
    pvs = [jnp.dot(p, vals, preferred_element_type=F32) for p in probs]
    for m in range(ATTN_Q_WIDTH // LANES):
        r0 = pvs[2 * m] / denoms[2 * m]
        r1 = pvs[2 * m + 1] / denoms[2 * m + 1]
        o_ref[:, m * LANES:(m + 1) * LANES] = jnp.where(out_lane < HEAD_DIM, r0, r1).astype(BF16)


def _attn_call(main, sink, seq):
    t = main.shape[0]
    nb = t // TB
    half = TB // 2
    nhb = t // half
    kcol, vcol = C_AK // LANES, C_AV // LANES
    prev_i = lambda i: jnp.maximum(2 * i - 1, 0)
    next_i = lambda i: jnp.minimum(2 * i + 2, nhb - 1)
    grid_spec = pltpu.PrefetchScalarGridSpec(
        num_scalar_prefetch=1,
        grid=(nb,),
        in_specs=[pl.BlockSpec((TB, ATTN_Q_WIDTH), lambda i, s: (i, C_AQ // ATTN_Q_WIDTH)),
                  pl.BlockSpec((half, LANES), lambda i, s: (prev_i(i), kcol)),
                  pl.BlockSpec((TB, LANES), lambda i, s: (i, kcol)),
                  pl.BlockSpec((half, LANES), lambda i, s: (next_i(i), kcol)),
                  pl.BlockSpec((TB, LANES), lambda i, s: (0, kcol)),
                  pl.BlockSpec((half, LANES), lambda i, s: (prev_i(i), vcol)),
                  pl.BlockSpec((TB, LANES), lambda i, s: (i, vcol)),
                  pl.BlockSpec((half, LANES), lambda i, s: (next_i(i), vcol)),
                  pl.BlockSpec((TB, LANES), lambda i, s: (0, vcol))],
        out_specs=pl.BlockSpec((TB, ATTN_Q_WIDTH), lambda i, s: (i, 0)),
    )
    return pl.pallas_call(
        functools.partial(_attn_kernel, seq=seq),
        grid_spec=grid_spec,
        out_shape=jax.ShapeDtypeStruct((t, ATTN_Q_WIDTH), BF16),
        compiler_params=_cparams(1),
        name="window_attn",
    )(sink, main, main, main, main, main, main, main, main, main)


def _head_norm(x, avg_bf16, w):
    mu = _dot_exact_rhs(x, avg_bf16)
    cen = x - mu
    var = _dot_exact_rhs(cen * cen, avg_bf16)
    return cen * lax.rsqrt(var + LN_EPS) * w


def _outproj_kernel(mf_ref, mb_ref, mo_ref, at_ref, gf_ref, gbk_ref, go_ref, mnw_ref, gnw_ref, avg_ref,
                    w_ref, x_ref, mod_ref, lnw_ref, lnb_ref, x1_ref, h2_ref):
    avg = avg_ref[...]
    hm = _head_norm(mf_ref[...] + mb_ref[...], avg, mnw_ref[...]) * _sigmoid(mo_ref[...].astype(F32))
    gate = go_ref[...].astype(F32)
    hg = _head_norm(gf_ref[...] + gbk_ref[...], avg, gnw_ref[...]) * (gate * _sigmoid(gate))
    a0, a1, a2 = MLSTM_WIDTH, MLSTM_WIDTH + ATTN_Q_WIDTH, D_MODEL
    mix = (jnp.dot(hm.astype(BF16), w_ref[:a0], preferred_element_type=F32)
           + jnp.dot(at_ref[...], w_ref[a0:a1], preferred_element_type=F32)
           + jnp.dot(hg.astype(BF16), w_ref[a1:a2], preferred_element_type=F32))
    m = mod_ref[...]
    x1 = _layer_norm(DEEPNORM_ALPHA * x_ref[...] + m[2:3] * mix, lnw_ref[...], lnb_ref[...])
    x1_ref[...] = x1
    h2_ref[...] = (x1 * (1.0 + m[4:5]) + m[3:4]).astype(h2_ref.dtype)


def _outproj_call(mf, mb, main, attn, gf, gbk, mnw, gnw, avg, w_out, layer, x, mod, lnw, lnb, h2_dtype):
    t, d = x.shape
    nb = t // TB
    row = lambda i: (i, 0)
    const = lambda i: (0, 0)
    return pl.pallas_call(
        _outproj_kernel,
        grid=(nb,),
        in_specs=[pl.BlockSpec((TB, MLSTM_WIDTH), row),
                  pl.BlockSpec((TB, MLSTM_WIDTH), row),
                  pl.BlockSpec((TB, MLSTM_WIDTH), lambda i: (i, C_MO // MLSTM_WIDTH)),
                  pl.BlockSpec((TB, ATTN_Q_WIDTH), row),
                  pl.BlockSpec((TB, GLA_WIDTH), row),
                  pl.BlockSpec((TB, GLA_WIDTH), row),
                  pl.BlockSpec((TB, GLA_WIDTH), lambda i: (i, C_GO // GLA_WIDTH)),
                  pl.BlockSpec((1, MLSTM_WIDTH), const),
                  pl.BlockSpec((1, GLA_WIDTH), const),
                  pl.BlockSpec((MLSTM_WIDTH, MLSTM_WIDTH), const),
                  pl.BlockSpec((None, d, d), lambda i: (layer, 0, 0), pipeline_mode=pl.Buffered(1)),
                  pl.BlockSpec((TB, d), row),
                  pl.BlockSpec((None, SUBLANES, d), lambda i: (_who(i), 0, 0)),
                  pl.BlockSpec((1, d), const),
                  pl.BlockSpec((1, d), const)],
        out_specs=[pl.BlockSpec((TB, d), row), pl.BlockSpec((TB, d), row)],
        out_shape=[jax.ShapeDtypeStruct((t, d), F32), jax.ShapeDtypeStruct((t, d), h2_dtype)],
        compiler_params=_cparams(1),
        name="out_proj",
    )(mf, mb, main, attn, gf, gbk, main, mnw, gnw, avg, w_out, x, mod, lnw, lnb)


def _select_mod(mod_ref, k, i, tm):
    rows = i * tm + lax.broadcasted_iota(jnp.int32, (tm, 1), 0)
    return jnp.where(rows < CTX_LEN, mod_ref[0, k:k + 1, :], mod_ref[1, k:k + 1, :])


def _ffn_kernel(h_ref, wgu_ref, wd_ref, x_ref, mod_ref, lnw_ref, lnb_ref, o_ref):
    h = h_ref[...]
    g = jnp.dot(h, wgu_ref[:, :D_FF], preferred_element_type=F32)
    u = jnp.dot(h, wgu_ref[:, D_FF:], preferred_element_type=F32)
    y = jnp.dot((g * _sigmoid(g) * u).astype(BF16), wd_ref[...], preferred_element_type=F32)
    m5 = _select_mod(mod_ref, 5, pl.program_id(0), h_ref.shape[0])
    o_ref[...] = _layer_norm(DEEPNORM_ALPHA * x_ref[...] + m5 * y, lnw_ref[...], lnb_ref[...])


def _ffn_call(h2, w_gu, w_down, layer, x1, mod, lnw, lnb):
    t, d = x1.shape
    resident = pl.Buffered(1)
    return pl.pallas_call(
        _ffn_kernel,
        grid=(t // FFN_TM,),
        in_specs=[pl.BlockSpec((FFN_TM, d), lambda i: (i, 0)),
                  pl.BlockSpec((None, d, 2 * D_FF), lambda i: (layer, 0, 0), pipeline_mode=resident),
                  pl.BlockSpec((None, D_FF, d), lambda i: (layer, 0, 0), pipeline_mode=resident),
                  pl.BlockSpec((FFN_TM, d), lambda i: (i, 0)),
                  pl.BlockSpec((2, SUBLANES, d), lambda i: (0, 0, 0)),
                  pl.BlockSpec((1, d), lambda i: (0, 0)),
                  pl.BlockSpec((1, d), lambda i: (0, 0))],
        out_specs=pl.BlockSpec((FFN_TM, d), lambda i: (i, 0)),
        out_shape=jax.ShapeDtypeStruct((t, d), F32),
        compiler_params=_cparams(1),
        name="dense_ffn",
    )(h2, w_gu, w_down, x1, mod, lnw, lnb)


def _router_kernel(h_ref, wr_ref, br_ref, su_ref, route_ref, cnt_ref, base_scr):
    @pl.when(pl.program_id(0) == 0)
    def _():
        base_scr[...] = jnp.zeros_like(base_scr)

    lt = lax.dot_general(wr_ref[...], h_ref[...], (((1,), (1,)), ((), ())),
                         preferred_element_type=F32, precision=HIGHEST) + br_ref[...]
    idx = lax.broadcasted_iota(jnp.int32, lt.shape, 0)
    neg_inf = jnp.float32(-jnp.inf)
    m1 = jnp.max(lt, axis=0, keepdims=True)
    e1 = jnp.min(jnp.where(lt == m1, idx, N_EXPERTS), axis=0, keepdims=True)
    lt2 = jnp.where(idx == e1, neg_inf, lt)
    m2 = jnp.max(lt2, axis=0, keepdims=True)
    e2 = jnp.min(jnp.where(lt2 == m2, idx, N_EXPERTS), axis=0, keepdims=True)
    t2 = jnp.exp(m2 - m1)
    w1 = 1.0 / (1.0 + t2)
    w2 = t2 / (1.0 + t2)
    su = su_ref[...]
    base = base_scr[...][:, 0:1]
    oh1 = (idx == e1).astype(F32)
    cum1 = jnp.dot(oh1.astype(BF16), su, preferred_element_type=F32)
    rank1 = jnp.sum(oh1 * (base + cum1), axis=0, keepdims=True)
    base = base + jnp.sum(oh1, axis=1, keepdims=True)
    oh2 = (idx == e2).astype(F32)
    cum2 = jnp.dot(oh2.astype(BF16), su, preferred_element_type=F32)
    rank2 = jnp.sum(oh2 * (base + cum2), axis=0, keepdims=True)
    base = base + jnp.sum(oh2, axis=1, keepdims=True)
    base_scr[...] = jnp.broadcast_to(base, base_scr.shape)
    cnt_ref[...] = jnp.broadcast_to(base, cnt_ref.shape)
    zero = jnp.zeros_like(w1)
    route_ref[...] = jnp.concatenate(
        [e1.astype(F32), e2.astype(F32), w1, w2, rank1, rank2, zero, zero], axis=0)


def _router_call(h2, wr_t, br, su):
    t, d = h2.shape
    nb = t // TB
    return pl.pallas_call(
        _router_kernel,
        grid=(nb,),
        in_specs=[pl.BlockSpec((TB, d), lambda i: (i, 0)),
                  pl.BlockSpec((N_EXPERTS, d), lambda i: (0, 0)),
                  pl.BlockSpec((N_EXPERTS, 1), lambda i: (0, 0)),
                  pl.BlockSpec((TB, TB), lambda i: (0, 0))],
        out_specs=[pl.BlockSpec((SUBLANES, TB), lambda i: (0, i)),
                   pl.BlockSpec((N_EXPERTS, LANES), lambda i: (0, 0))],
        out_shape=[jax.ShapeDtypeStruct((SUBLANES, t), F32),
                   jax.ShapeDtypeStruct((N_EXPERTS, LANES), F32)],
        scratch_shapes=[pltpu.VMEM((N_EXPERTS, LANES), F32)],
        compiler_params=_cparams(1),
        name="moe_router",
    )(h2, wr_t, br, su)


ROW_TILE = D_MODEL // LANES
DMA_UNROLL = 8


def _store_row_tiles(ref, val):
    n = val.shape[0]
    for j in range(ROW_TILE):
        ref[pl.ds(j, n, stride=ROW_TILE), :] = val[:, j * LANES:(j + 1) * LANES]


def _load_row_tiles(ref, n, dtype=F32):
    return jnp.concatenate([ref[pl.ds(j, n, stride=ROW_TILE), :].astype(dtype) for j in range(ROW_TILE)], axis=1)


def _row(ref, idx):
    return ref.at[pl.ds(pl.multiple_of(idx * ROW_TILE, ROW_TILE), ROW_TILE)]


def _dispatch_kernel(s1_ref, s2_ref, h_ref, xs_in_ref, xs_ref, hr_scr, sem):
    del xs_in_ref
    i = pl.program_id(0)
    n = pl.num_programs(0)
    cur = i % 2

    def copy(blk, r, slot_ref):
        buf = blk % 2
        return pltpu.make_async_copy(_row(hr_scr.at[buf], r), _row(xs_ref, slot_ref[blk * TB + r]), sem.at[buf])

    def start(r, carry):
        copy(i, r, s1_ref).start()
        copy(i, r, s2_ref).start()
        return carry

    def wait_block(blk):
        def wait(r, carry):
            copy(blk, r, s1_ref).wait()
            copy(blk, r, s2_ref).wait()
            return carry
        lax.fori_loop(0, TB, wait, 0, unroll=DMA_UNROLL)

    _store_row_tiles(hr_scr.at[cur], h_ref[...])
    lax.fori_loop(0, TB, start, 0, unroll=DMA_UNROLL)

    @pl.when(i > 0)
    def _():
        wait_block(i - 1)

    @pl.when(i == n - 1)
    def _():
        wait_block(i)


def _dispatch_call(slot1, slot2, h2, xs_init):
    t, d = h2.shape
    grid_spec = pltpu.PrefetchScalarGridSpec(
        num_scalar_prefetch=2,
        grid=(t // TB,),
        in_specs=[pl.BlockSpec((TB, d), lambda i, a, b: (i, 0)),
                  pl.BlockSpec(memory_space=pl.ANY)],
        out_specs=pl.BlockSpec(memory_space=pl.ANY),
        scratch_shapes=[pltpu.VMEM((2, TB * ROW_TILE, LANES), F32), pltpu.SemaphoreType.DMA((2,))],
    )
    return pl.pallas_call(
        _dispatch_kernel,
        grid_spec=grid_spec,
        out_shape=jax.ShapeDtypeStruct(xs_init.shape, xs_init.dtype),
        input_output_aliases={3: 0},
        compiler_params=_cparams(1),
        name="moe_dispatch",
    )(slot1, slot2, h2, xs_init)


def _expert_kernel(be_ref, used_ref, x_ref, wg_ref, wu_ref, wd_ref, y_ref, xb_scr, acc_scr):
    b, f = pl.program_id(0), pl.program_id(1)
    live = b < used_ref[0]

    @pl.when(live & (f == 0))
    def _():
        for j in range(ROW_TILE):
            xb_scr[:, j * LANES:(j + 1) * LANES] = x_ref[pl.ds(j, MOE_G, stride=ROW_TILE), :].astype(BF16)

    @pl.when(live)
    def _():
        x = xb_scr[...]
        g = jnp.dot(x, wg_ref[...], preferred_element_type=F32)
        u = jnp.dot(x, wu_ref[...], preferred_element_type=F32)
        part = jnp.dot((g * _sigmoid(g) * u).astype(BF16), wd_ref[...], preferred_element_type=F32)

        @pl.when(f == 0)
        def _():
            acc_scr[...] = part

        @pl.when(f > 0)
        def _():
            acc_scr[...] += part

    @pl.when(f == pl.num_programs(1) - 1)
    def _():
        @pl.when(live)
        def _():
            _store_row_tiles(y_ref, acc_scr[...])

        @pl.when(jnp.logical_not(live))
        def _():
            y_ref[...] = jnp.zeros_like(y_ref)


def _expert_call(block_e, used, xs, w_gu, w_down, layer):
    d = D_MODEL
    ns = xs.shape[0] // ROW_TILE
    nblk = ns // MOE_G
    nf = D_EXPERT // MOE_TF
    rows = MOE_G * ROW_TILE

    def f_eff(b, f, used_ref):
        return jnp.where(b < used_ref[0], f, nf - 1)

    grid_spec = pltpu.PrefetchScalarGridSpec(
        num_scalar_prefetch=2,
        grid=(nblk, nf),
        in_specs=[pl.BlockSpec((rows, LANES), lambda b, f, be, us: (b, 0)),
                  pl.BlockSpec((None, None, d, MOE_TF), lambda b, f, be, us: (layer, be[b], 0, f_eff(b, f, us))),
                  pl.BlockSpec((None, None, d, MOE_TF),
                               lambda b, f, be, us: (layer, be[b], 0, nf + f_eff(b, f, us))),
                  pl.BlockSpec((None, None, MOE_TF, d), lambda b, f, be, us: (layer, be[b], f_eff(b, f, us), 0))],
        out_specs=pl.BlockSpec((rows, LANES), lambda b, f, be, us: (b, 0)),
        scratch_shapes=[pltpu.VMEM((MOE_G, d), BF16), pltpu.VMEM((MOE_G, d), F32)],
    )
    return pl.pallas_call(
        _expert_kernel,
        grid_spec=grid_spec,
        out_shape=jax.ShapeDtypeStruct(xs.shape, F32),
        compiler_params=_cparams(2),
        name="moe_experts",
    )(block_e, used, xs, w_gu, w_gu, w_down)


def _combine_kernel(s1_ref, s2_ref, yb_ref, route_ref, x_ref, mod_ref, lnw_ref, lnb_ref, o_ref, buf1, buf2, sem):
    i = pl.program_id(0)
    n = pl.num_programs(0)
    cur = i % 2

    def copy(blk, r, slot_ref, buf):
        b = blk % 2
        return pltpu.make_async_copy(_row(yb_ref, slot_ref[blk * TB + r]), _row(buf.at[b], r), sem.at[b])

    def start_block(blk):
        def start(r, carry):
            copy(blk, r, s1_ref, buf1).start()
            copy(blk, r, s2_ref, buf2).start()
            return carry
        lax.fori_loop(0, TB, start, 0, unroll=DMA_UNROLL)

    def wait(r, carry):
        copy(i, r, s1_ref, buf1).wait()
        copy(i, r, s2_ref, buf2).wait()
        return carry

    @pl.when(i == 0)
    def _():
        start_block(i)

    @pl.when(i + 1 < n)
    def _():
        start_block(i + 1)

    lax.fori_loop(0, TB, wait, 0, unroll=DMA_UNROLL)
    rt = route_ref[...].T
    y = rt[:, 2:3] * _load_row_tiles(buf1.at[cur], TB) + rt[:, 3:4] * _load_row_tiles(buf2.at[cur], TB)
    m = mod_ref[...]
    o_ref[...] = _layer_norm(DEEPNORM_ALPHA * x_ref[...] + m[5:6] * y, lnw_ref[...], lnb_ref[...])


def _combine_call(slot1, slot2, yb, route, x1, mod, lnw, lnb, latent_only):
    t, d = x1.shape
    if latent_only:
        out_rows, out_map = t - CTX_LEN, lambda i, a, b: (jnp.maximum(i - 1, 0), 0)
    else:
        out_rows, out_map = t, lambda i, a, b: (i, 0)
    grid_spec = pltpu.PrefetchScalarGridSpec(
        num_scalar_prefetch=2,
        grid=(t // TB,),
        in_specs=[pl.BlockSpec(memory_space=pl.ANY),
                  pl.BlockSpec((SUBLANES, TB), lambda i, a, b: (0, i)),
                  pl.BlockSpec((TB, d), lambda i, a, b: (i, 0)),
                  pl.BlockSpec((None, SUBLANES, d), lambda i, a, b: (_who(i), 0, 0)),
                  pl.BlockSpec((1, d), lambda i, a, b: (0, 0)),
                  pl.BlockSpec((1, d), lambda i, a, b: (0, 0))],
        out_specs=pl.BlockSpec((TB, d), out_map),
        scratch_shapes=[pltpu.VMEM((2, TB * ROW_TILE, LANES), F32), pltpu.VMEM((2, TB * ROW_TILE, LANES), F32),
                        pltpu.SemaphoreType.DMA((2,))],
    )
    return pl.pallas_call(
        _combine_kernel,
        grid_spec=grid_spec,
        out_shape=jax.ShapeDtypeStruct((out_rows, d), F32),
        compiler_params=_cparams(1),
        name="moe_combine",
    )(slot1, slot2, yb, route, x1, mod, lnw, lnb)


def _moe_layer(h2, x1, mod, lnw, lnb, w_router, b_router, w_gu, w_down, layer, su, latent_only):
    t, d = h2.shape
    route, cnt = _router_call(h2, w_router.T, b_router.reshape(N_EXPERTS, 1), su)
    counts = cnt[:, 0].astype(jnp.int32)
    padded = (counts + MOE_G - 1) // MOE_G * MOE_G
    pend = jnp.cumsum(padded)
    pstart = pend - padded
    e1, e2 = route[0].astype(jnp.int32), route[1].astype(jnp.int32)
    slot1 = pstart[e1] + route[4].astype(jnp.int32)
    slot2 = pstart[e2] + route[5].astype(jnp.int32)
    nblk = -(-(2 * t) // MOE_G) + N_EXPERTS
    blk_start = jnp.arange(nblk, dtype=jnp.int32) * MOE_G
    block_e = jnp.minimum(jnp.sum((pend[None, :] <= blk_start[:, None]).astype(jnp.int32), axis=1), N_EXPERTS - 1)
    used = (pend[-1:] // MOE_G).astype(jnp.int32)
    xs = _dispatch_call(slot1, slot2, h2, jnp.zeros((nblk * MOE_G * ROW_TILE, LANES), F32))
    yb = _expert_call(block_e, used, xs, w_gu, w_down, layer)
    return _combine_call(slot1, slot2, yb, route, x1, mod, lnw, lnb, latent_only)


def _attn_head_perm():
    cols = []
    for m in range(ATTN_HEADS // ATTN_KV_HEADS):
        for hq in (m, m + ATTN_HEADS // ATTN_KV_HEADS):
            cols.extend(range(hq * HEAD_DIM, (hq + 1) * HEAD_DIM))
    return np.asarray(cols, np.int32)


def _in_col_perm():
    n_m = 4 * MLSTM_WIDTH
    mg = 2 * 2 * MLSTM_HEADS
    a0 = n_m + mg
    perm = list(range(n_m))
    perm += [a0 + int(j) for j in _attn_head_perm()]
    perm += list(range(a0 + ATTN_Q_WIDTH, a0 + ATTN_Q_WIDTH + 2 * ATTN_KV_WIDTH))
    g0 = a0 + ATTN_Q_WIDTH + 2 * ATTN_KV_WIDTH
    perm += list(range(g0, g0 + 2 * GLA_KEY_WIDTH + 2 * GLA_WIDTH))
    perm += list(range(n_m, n_m + mg))
    perm += list(range(g0 + 2 * GLA_KEY_WIDTH + 2 * GLA_WIDTH, g0 + 2 * GLA_KEY_WIDTH + 2 * GLA_WIDTH + 2 * GLA_RANK))
    return np.asarray(perm, np.int32)


def _take_runs(w, perm, axis):
    runs, start = [], 0
    for j in range(1, len(perm) + 1):
        if j == len(perm) or perm[j] != perm[j - 1] + 1:
            runs.append(lax.slice_in_dim(w, int(perm[start]), int(perm[j - 1]) + 1, axis=axis))
            start = j
    return jnp.concatenate(runs, axis=axis)


def _rope_tables(seq):
    inv = ROPE_BASE ** (-jnp.arange(ROPE_PAIRS, dtype=F32) / ROPE_PAIRS)
    rows = seq // GRID_W
    ang_r = jnp.arange(rows).astype(F32)[:, None] * inv
    ang_c = jnp.arange(GRID_W).astype(F32)[:, None] * inv
    rep = lambda a: jnp.repeat(a, GRID_W, axis=0)
    til = lambda a: jnp.tile(a, (rows, 1))
    cos_r, sin_r, cos_c, sin_c = rep(jnp.cos(ang_r)), rep(jnp.sin(ang_r)), til(jnp.cos(ang_c)), til(jnp.sin(ang_c))
    cos64 = jnp.concatenate([cos_r, cos_r, cos_c, cos_c], -1)
    sin64 = jnp.concatenate([-sin_r, sin_r, -sin_c, sin_c], -1)
    cos_l = jnp.concatenate([cos64, cos64], -1)
    sin_l = jnp.concatenate([sin64, sin64], -1)
    cos_t = jnp.concatenate([jnp.ones((CTX_LEN, LANES), F32), cos_l], 0)
    sin_t = jnp.concatenate([jnp.zeros((CTX_LEN, LANES), F32), sin_l], 0)
    return cos_t, sin_t


def _block_tri(direction):
    r = np.arange(TB)[:, None]
    c = np.arange(TB)[None, :]
    same = (r // CHUNK) == (c // CHUNK)
    tri = (c <= r) if direction == 0 else (c >= r)
    return jnp.asarray(same & tri, BF16)


def kernel(x, c, ctx, c_ctx, w_ada, b_ada, w_in, mlstm_gate_b, mlstm_norm_w, attn_sink, gla_gate_up, gla_gate_b,
           gla_norm_w, w_out, ln_w, ln_b, ffn_w_gu, ffn_w_down, router_w, router_b, moe_w_gu, moe_w_down):
    seq, d = x.shape[1], x.shape[2]
    depth = w_in.shape[0]
    xt = jnp.concatenate([ctx[0], x[0]], axis=0)

    cvec = jnp.zeros((SUBLANES, d), F32).at[0].set(c_ctx).at[1].set(c[0])
    mods = _mod_call(cvec, w_ada, b_ada)[:, :2].reshape(depth, 2, 6, d)
    mods = jnp.pad(mods, ((0, 0), (0, 0), (0, SUBLANES - 6), (0, 0)))

    perm = _in_col_perm()
    w_in_p = jnp.concatenate([_take_runs(w_in, perm, 2).astype(BF16),
                              jnp.zeros((depth, d, N_PROJ - perm.shape[0]), BF16)], axis=2)
    out_perm = np.concatenate([np.arange(MLSTM_WIDTH), MLSTM_WIDTH + _attn_head_perm(),
                               np.arange(MLSTM_WIDTH + ATTN_Q_WIDTH, d)]).astype(np.int32)
    w_out_p = _take_runs(w_out, out_perm, 1).astype(BF16)
    gate_bias = jnp.pad(mlstm_gate_b.reshape(depth, 1, -1), ((0, 0), (0, 0), (0, N_GATE - 4 * MLSTM_HEADS)))
    gup_pad = jnp.zeros((depth, 2, N_GATE, GLA_KEY_WIDTH), F32)
    for dr in range(2):
        lo = GLA_GATE_OFF + dr * GLA_RANK
        gup_pad = gup_pad.at[:, dr, lo:lo + GLA_RANK, :].set(gla_gate_up[:, dr])
    cos_t, sin_t = _rope_tables(seq)
    tri = [_block_tri(0), _block_tri(1)]
    su = jnp.asarray(np.arange(TB)[:, None] < np.arange(TB)[None, :], BF16)
    hh = np.arange(MLSTM_WIDTH) // MLSTM_DH
    avg = jnp.asarray((hh[:, None] == hh[None, :]) / MLSTM_DH, BF16)
    ffn_gu, ffn_dn = ffn_w_gu.astype(BF16), ffn_w_down.astype(BF16)
    moe_gu, moe_dn = moe_w_gu.astype(BF16), moe_w_down.astype(BF16)

    expand = []
    for dr in range(2):
        e = np.zeros((N_GATE, MLSTM_WIDTH), np.float32)
        for h in range(MLSTM_HEADS):
            e[dr * 2 * MLSTM_HEADS + h, h * MLSTM_DH:(h + 1) * MLSTM_DH] = 1.0
        expand.append(jnp.asarray(e, BF16))

    for l in range(depth):
        is_moe = l % 2 == 1
        last = l == depth - 1
        main, gates = _inproj_call(xt, mods[l], w_in_p, l, cos_t, sin_t)
        mf, mb, gf, gbk = _scans_call(main, gates, gate_bias[l], tri, expand, gup_pad[l], gla_gate_b[l])
        attn = _attn_call(main, attn_sink[l], seq)
        x1, h2 = _outproj_call(mf, mb, main, attn, gf, gbk, mlstm_norm_w[l:l + 1], gla_norm_w[l:l + 1], avg,
                               w_out_p, l, xt, mods[l], ln_w[l, 0:1], ln_b[l, 0:1], F32 if is_moe else BF16)
        if is_moe:
            xt = _moe_layer(h2, x1, mods[l], ln_w[l, 1:2], ln_b[l, 1:2], router_w[l // 2], router_b[l // 2],
                            moe_gu, moe_dn, l // 2, su, latent_only=last)
        else:
            xt = _ffn_call(h2, ffn_gu, ffn_dn, l // 2, x1, mods[l], ln_w[l, 1:2], ln_b[l, 1:2])
    return (xt if depth % 2 == 0 else xt[CTX_LEN:])[None]
```

```python
import functools

import jax
import jax.numpy as jnp
import numpy as np
from jax import lax
from jax.experimental import pallas as pl
from jax.experimental.pallas import tpu as pltpu

F32 = jnp.float32
BF16 = jnp.bfloat16
HIGHEST = lax.Precision.HIGHEST

D_MODEL = 1024
SEQ = 16384
DEPTH = 4
GRID_W = 64
CTX_LEN = 256
MLSTM_HEADS = 4
MLSTM_DH = 64
MLSTM_WIDTH = 256
HEAD_DIM = 64
ATTN_HEADS = 8
ATTN_KV_HEADS = 2
ATTN_Q_WIDTH = 512
ATTN_KV_WIDTH = 128
WINDOW = 128
ROPE_BASE = 10000.0
ROPE_PAIRS = 16
GLA_HEADS = 4
GLA_DK = 32
GLA_DV = 64
GLA_WIDTH = 256
GLA_KEY_WIDTH = 128
GLA_RANK = 16
GLA_TAU = 16.0
CHUNK = 64
D_FF = 2816
N_EXPERTS = 8
D_EXPERT = 3584
DEEPNORM_ALPHA = (2 * DEPTH) ** 0.25
LN_EPS = 1e-5

LANES = 128
SUBLANES = 8
VMEM_LIMIT = 56 * 1024 * 1024

TB = 256
N_CHUNKS = TB // CHUNK
PROJ_TM = 640
FFN_TM = 640
MOE_G = 512
MOE_TF = 1792

C_MQ, C_MK, C_MV, C_MO = 0, 256, 512, 768
C_AQ, C_AK, C_AV = 1024, 1536, 1664
C_GQ, C_GK, C_GV, C_GO = 1792, 1920, 2048, 2304
N_MAIN = 2560
N_GATE = 128
N_PROJ = N_MAIN + N_GATE
GLA_GATE_OFF = 16


def _cparams(n_axes=1):
    return pltpu.CompilerParams(dimension_semantics=("arbitrary",) * n_axes,
                                vmem_limit_bytes=VMEM_LIMIT)


def _sigmoid(x):
    return 1.0 / (1.0 + jnp.exp(-x))


def _log_sigmoid(x):
    return jnp.minimum(x, 0.0) - jnp.log(1.0 + jnp.exp(-jnp.abs(x)))


def _split3(x):
    hi = x.astype(BF16)
    r1 = x - hi.astype(F32)
    mid = r1.astype(BF16)
    lo = (r1 - mid.astype(F32)).astype(BF16)
    return hi, mid, lo


def _dot_exact_rhs(x, m_bf16):
    hi, mid, lo = _split3(x)
    d = lambda a: jnp.dot(a, m_bf16, preferred_element_type=F32)
    return d(hi) + d(mid) + d(lo)


def _dot_exact_lhs(m_bf16, x):
    hi, mid, lo = _split3(x)
    d = lambda a: jnp.dot(m_bf16, a, preferred_element_type=F32)
    return d(hi) + d(mid) + d(lo)


def _dot_bf16x3(a, b):
    a_hi, b_hi = a.astype(BF16), b.astype(BF16)
    a_lo = (a - a_hi.astype(F32)).astype(BF16)
    b_lo = (b - b_hi.astype(F32)).astype(BF16)
    d = lambda x, y: jnp.dot(x, y, preferred_element_type=F32)
    return d(a_hi, b_hi) + d(a_hi, b_lo) + d(a_lo, b_hi)


def _layer_norm(v, w, b):
    mu = jnp.mean(v, axis=-1, keepdims=True)
    cen = v - mu
    var = jnp.mean(cen * cen, axis=-1, keepdims=True)
    return cen * lax.rsqrt(var + LN_EPS) * w + b


def _mod_kernel(c_ref, w_ref, b_ref, o_ref):
    c = c_ref[...]
    sc = c * _sigmoid(c)
    o_ref[...] = jnp.dot(sc, w_ref[...], preferred_element_type=F32, precision=HIGHEST) + b_ref[...]


def _mod_call(cvec, w_ada, b_ada):
    depth, d, n = w_ada.shape
    tn = 1536
    return pl.pallas_call(
        _mod_kernel,
        grid=(depth, n // tn),
        in_specs=[pl.BlockSpec((SUBLANES, d), lambda l, j: (0, 0)),
                  pl.BlockSpec((None, d, tn), lambda l, j: (l, 0, j)),
                  pl.BlockSpec((None, 1, tn), lambda l, j: (l, 0, j))],
        out_specs=pl.BlockSpec((None, SUBLANES, tn), lambda l, j: (l, 0, j)),
        out_shape=jax.ShapeDtypeStruct((depth, SUBLANES, n), F32),
        compiler_params=_cparams(2),
        name="ada_mod",
    )(cvec, w_ada, b_ada.reshape(depth, 1, n))


def _who(i):
    return jnp.minimum(i, 1)


def _rope(x, cos, sin_signed, first_half):
    swapped = jnp.where(first_half, pltpu.roll(x, LANES - ROPE_PAIRS, 1), pltpu.roll(x, ROPE_PAIRS, 1))
    return x * cos + swapped * sin_signed


def _inproj_kernel(x_ref, mod_ref, w_ref, cos_ref, sin_ref, main_ref, gate_ref, wbf_scr):
    i = pl.program_id(0)
    tm = x_ref.shape[0]

    @pl.when(i == 0)
    def _():
        for dst, src, n in _column_runs(_in_col_perm()):
            wbf_scr[:, dst:dst + n] = w_ref[:, src:src + n].astype(BF16)
        wbf_scr[:, w_ref.shape[1]:] = jnp.zeros((w_ref.shape[0], N_PROJ - w_ref.shape[1]), BF16)

    h = x_ref[...] * (1.0 + _select_mod(mod_ref, 1, i, tm)) + _select_mod(mod_ref, 0, i, tm)
    p = jnp.dot(h.astype(BF16), wbf_scr[...], preferred_element_type=F32)
    cos, sin = cos_ref[...], sin_ref[...]
    lane = lax.broadcasted_iota(jnp.int32, (tm, LANES), 1)
    first_half = (lane % (2 * ROPE_PAIRS)) < ROPE_PAIRS
    main_ref[:, :C_AQ] = p[:, :C_AQ].astype(BF16)
    for j in range(ATTN_Q_WIDTH // LANES):
        lo = C_AQ + j * LANES
        main_ref[:, lo:lo + LANES] = (_rope(p[:, lo:lo + LANES], cos, sin, first_half) * HEAD_DIM ** -0.5).astype(BF16)
    main_ref[:, C_AK:C_AV] = _rope(p[:, C_AK:C_AV], cos, sin, first_half).astype(BF16)
    main_ref[:, C_AV:] = p[:, C_AV:N_MAIN].astype(BF16)
    gate_ref[...] = p[:, N_MAIN:]


def _inproj_call(x, mod, w_in, layer, cos_t, sin_t):
    t, d = x.shape
    n_in = w_in.shape[2]
    tm = PROJ_TM
    return pl.pallas_call(
        _inproj_kernel,
        grid=(t // tm,),
        in_specs=[pl.BlockSpec((tm, d), lambda i: (i, 0)),
                  pl.BlockSpec((2, SUBLANES, d), lambda i: (0, 0, 0)),
                  pl.BlockSpec((None, d, n_in), lambda i: (layer, 0, 0), pipeline_mode=pl.Buffered(1)),
                  pl.BlockSpec((tm, LANES), lambda i: (i, 0)),
                  pl.BlockSpec((tm, LANES), lambda i: (i, 0))],
        out_specs=[pl.BlockSpec((tm, N_MAIN), lambda i: (i, 0)),
                   pl.BlockSpec((tm, N_GATE), lambda i: (i, 0))],
        out_shape=[jax.ShapeDtypeStruct((t, N_MAIN), BF16),
                   jax.ShapeDtypeStruct((t, N_GATE), F32)],
        scratch_shapes=[pltpu.VMEM((d, N_PROJ), BF16)],
        compiler_params=_cparams(1),
        name="in_proj",
    )(x, mod, w_in, cos_t, sin_t)


def _scan_block(i, nb, direction):
    if direction == 0:
        return i
    return jnp.where(i == 0, 0, nb - i)


def _scans_kernel(mq_f, mk_f, mv_f, gq_f, gk_f, gv_f, g_f, mq_b, mk_b, mv_b, gq_b, gk_b, gv_b, g_b,
                  gbias, tri_f, tri_b, exp_f, exp_b, gup_f, gup_b, glb_f, glb_b,
                  om_f, om_b, og_f, og_b, c_f, n_f, m_f, c_b, n_b, m_b, s_f, s_b):
    @pl.when(pl.program_id(0) == 0)
    def _():
        for scr in (c_f, n_f, m_f, c_b, n_b, m_b, s_f, s_b):
            scr[...] = jnp.zeros_like(scr)

    _run_interleaved([
        _mlstm_phases(mq_f, mk_f, mv_f, g_f, gbias, tri_f, exp_f, om_f, c_f, n_f, m_f, 0),
        _mlstm_phases(mq_b, mk_b, mv_b, g_b, gbias, tri_b, exp_b, om_b, c_b, n_b, m_b, 1),
        _gla_phases(gq_f, gk_f, gv_f, g_f, gup_f, glb_f, tri_f, og_f, s_f, 0),
        _gla_phases(gq_b, gk_b, gv_b, g_b, gup_b, glb_b, tri_b, og_b, s_b, 1),
    ])


def _run_interleaved(streams):
    for k in range(max(len(s) for s in streams)):
        for s in streams:
            if k < len(s):
                s[k]()


def _mlstm_phases(q_ref, k_ref, v_ref, g_ref, gb_ref, tri_ref, exp_ref, o_ref, c_scr, n_scr, m_scr, direction):
    fwd = direction == 0
    neg_inf = jnp.float32(-jnp.inf)
    order = list(range(N_CHUNKS)) if fwd else list(range(N_CHUNKS - 1, -1, -1))
    pairs = list(range(MLSTM_HEADS // 2))
    inst = [(p, c) for p in pairs for c in order]
    last_of = lambda c: c * CHUNK + (CHUNK - 1 if fwd else 0)
    ts_of = lambda c: slice(c * CHUNK, (c + 1) * CHUNK)
    ts2_of = lambda c: slice((c // 2) * 2 * CHUNK, (c // 2 + 1) * 2 * CHUNK)
    lp_of = lambda p: slice(p * LANES, (p + 1) * LANES)
    dn_nt = (((1,), (1,)), ((), ()))
    st = {}

    lane = lax.broadcasted_iota(jnp.int32, (CHUNK, LANES), 1)
    trow = lax.broadcasted_iota(jnp.int32, (CHUNK, LANES), 0)
    low = lane < MLSTM_DH
    tri_ok = (lane % MLSTM_DH <= trow) if fwd else (lane % MLSTM_DH >= trow)
    lane2 = lax.broadcasted_iota(jnp.int32, (LANES, LANES), 1)
    row2 = lax.broadcasted_iota(jnp.int32, (LANES, LANES), 0)
    same_head = (lane2 < MLSTM_DH) == (row2 < MLSTM_DH)
    bd_ones = jnp.where(same_head, 1.0, 0.0).astype(BF16)
    lane_row = lax.broadcasted_iota(jnp.int32, (1, LANES), 1)
    pair_half = lax.broadcasted_iota(jnp.int32, (2 * CHUNK, LANES), 0) // CHUNK

    def pair_blockdiag(x):
        zero = jnp.zeros_like(x)
        return jnp.concatenate([jnp.where(low, x, zero), jnp.where(low, zero, x)], axis=0)

    def ph_cumsum():
        st['g'] = g_ref[...] + gb_ref[...]
        st['b_f'] = _dot_exact_lhs(tri_ref[...], _log_sigmoid(st['g']))

    def ph_gates():
        b_i = pltpu.roll(st['b_f'], LANES - MLSTM_HEADS, 1)
        gr = st['g'] - b_i
        tok = lax.broadcasted_iota(jnp.int32, (TB, LANES), 0) % CHUNK
        aloc = gr
        for sh in (1, 2, 4, 8, 16, 32):
            if fwd:
                shifted, ok = pltpu.roll(aloc, sh, 0), tok >= sh
            else:
                shifted, ok = pltpu.roll(aloc, TB - sh, 0), tok < CHUNK - sh
            aloc = jnp.maximum(aloc, jnp.where(ok, shifted, neg_inf))
        chunk_max = jnp.concatenate(
            [jnp.broadcast_to(aloc[last_of(c):last_of(c) + 1], (CHUNK, LANES)) for c in range(N_CHUNKS)], axis=0)
        st['b_i'], st['aloc'] = b_i, aloc
        st['wloc'] = jnp.exp(gr - chunk_max)
        st['gr_t'] = gr.T

    def ph_expand():
        expand = exp_ref[...]
        st['aloc_x'] = _dot_exact_rhs(st['aloc'], expand)
        st['b_x'] = _dot_exact_rhs(st['b_i'], expand)
        st['wloc_x'] = _dot_exact_rhs(st['wloc'], expand)

    def ph_scores():
        for p, c in inst:
            kp = k_ref[ts_of(c), lp_of(p)] * MLSTM_DH ** -0.5
            st['s', p, c] = lax.dot_general(q_ref[ts_of(c), lp_of(p)], pair_blockdiag(kp), dn_nt,
                                            preferred_element_type=F32)
        for p in pairs:
            for c2 in range(N_CHUNKS // 2):
                st['kt2', p, c2] = (k_ref[ts2_of(2 * c2), lp_of(p)].astype(F32) * MLSTM_DH ** -0.5).T.astype(BF16)

    def ph_weights():
        for p, c in inst:
            ji0 = direction * 2 * MLSTM_HEADS + 2 * p
            ra = st['gr_t'][ji0:ji0 + 1, ts2_of(c)]
            rb = st['gr_t'][ji0 + 1:ji0 + 2, ts2_of(c)]
            if c % 2 == 0:
                g_row = jnp.where(lane_row < MLSTM_DH, ra, pltpu.roll(rb, MLSTM_DH, 1))
            else:
                g_row = jnp.where(lane_row < MLSTM_DH, pltpu.roll(ra, MLSTM_DH, 1), rb)
            al = st['aloc_x'][ts_of(c), lp_of(p)]
            s = st['s', p, c] * jnp.exp(jnp.where(tri_ok, g_row - al, neg_inf))
            s_hi = s.astype(BF16)
            st['s_hi', p, c] = s_hi
            st['s_lo', p, c] = (s - s_hi.astype(F32)).astype(BF16)
            wx2 = st['wloc_x'][ts2_of(c), lp_of(p)]
            in_chunk = pair_half == (c % 2)
            wv = jnp.where(in_chunk, wx2 * v_ref[ts2_of(c), lp_of(p)].astype(F32), 0.0)
            ww = jnp.where(in_chunk, wx2, 0.0)
            st['wvw', p, c] = jnp.concatenate([wv, ww], axis=1).astype(BF16)

    def ph_intra():
        for p, c in inst:
            vp = v_ref[ts_of(c), lp_of(p)]
            st['r1', p, c] = jnp.dot(st['s_hi', p, c], jnp.concatenate([pair_blockdiag(vp), bd_ones], axis=1),
                                     preferred_element_type=F32)
            st['den_lo', p, c] = jnp.dot(st['s_lo', p, c], bd_ones, preferred_element_type=F32)
            st['upd', p, c] = jnp.dot(st['kt2', p, c // 2], st['wvw', p, c], preferred_element_type=F32)

    def ph_state():
        for p in pairs:
            bdc, bdn = c_scr[p], n_scr[p]
            m_row = m_scr[p][0:1]
            for c in order:
                st['cn', p, c] = jnp.concatenate([bdc, bdn], axis=1).astype(BF16)
                st['m', p, c] = m_row
                last = last_of(c)
                aloc_last = st['aloc_x'][last:last + 1, lp_of(p)]
                a_last = jnp.maximum(m_row, aloc_last)
                decay = jnp.exp(m_row - a_last)
                scale = jnp.exp(aloc_last - a_last)
                upd = st['upd', p, c]
                bdc = decay * bdc + scale * jnp.where(same_head, upd[:, :LANES], 0.0)
                bdn = decay * bdn + scale * jnp.where(same_head, upd[:, LANES:], 0.0)
                m_row = st['b_x'][last:last + 1, lp_of(p)] + a_last
            c_scr[p] = bdc
            n_scr[p] = bdn
            m_scr[p] = jnp.broadcast_to(m_row, (SUBLANES, LANES))

    def ph_inter():
        for p, c in inst:
            st['r2', p, c] = jnp.dot(q_ref[ts_of(c), lp_of(p)], st['cn', p, c],
                                     preferred_element_type=F32)

    def ph_out():
        for p, c in inst:
            al = st['aloc_x'][ts_of(c), lp_of(p)]
            m_row, r1, r2 = st['m', p, c], st['r1', p, c], st['r2', p, c]
            a = jnp.maximum(m_row, al)
            corr = jnp.exp(al - a)
            w_inter = jnp.exp(m_row - a)
            num = corr * r1[:, :LANES] + w_inter * r2[:, :LANES]
            den = corr * (r1[:, LANES:] + st['den_lo', p, c]) + w_inter * r2[:, LANES:]
            o_ref[ts_of(c), lp_of(p)] = num / jnp.maximum(jnp.abs(den),
                                                          jnp.exp(-(st['b_x'][ts_of(c), lp_of(p)] + a)))

    return [ph_cumsum, ph_gates, ph_expand, ph_scores, ph_weights, ph_intra, ph_state, ph_inter, ph_out]


def _gla_phases(q_ref, k_ref, v_ref, g_ref, gup_ref, gb_ref, tri_ref, o_ref, s_scr, direction):
    fwd = direction == 0
    order = list(range(N_CHUNKS)) if fwd else list(range(N_CHUNKS - 1, -1, -1))
    ts_of = lambda c: slice(c * CHUNK, (c + 1) * CHUNK)
    ts2_of = lambda c: slice((c // 2) * 2 * CHUNK, (c // 2 + 1) * 2 * CHUNK)
    last_of = lambda c: c * CHUNK + (CHUNK - 1 if fwd else 0)
    dn_nt = (((1,), (1,)), ((), ()))
    st = {}

    khead = lax.broadcasted_iota(jnp.int32, (CHUNK, GLA_KEY_WIDTH), 1) // GLA_DK
    vlane = lax.broadcasted_iota(jnp.int32, (CHUNK, GLA_WIDTH), 1)
    vhead = vlane // GLA_DV
    trow = lax.broadcasted_iota(jnp.int32, (CHUNK, GLA_WIDTH), 0)
    tri_ok = (vlane % CHUNK <= trow) if fwd else (vlane % CHUNK >= trow)
    st_head_r = lax.broadcasted_iota(jnp.int32, (GLA_WIDTH, GLA_KEY_WIDTH), 0) // GLA_DV
    st_head_c = lax.broadcasted_iota(jnp.int32, (GLA_WIDTH, GLA_KEY_WIDTH), 1) // GLA_DK
    same_head = st_head_r == st_head_c
    pair_row = lax.broadcasted_iota(jnp.int32, (2 * CHUNK, GLA_KEY_WIDTH), 0) // CHUNK

    def stack_heads(x, head_of_lane):
        zero = jnp.zeros_like(x)
        return jnp.concatenate([jnp.where(head_of_lane == h, x, zero) for h in range(GLA_HEADS)], axis=0)

    def ph_gate_proj():
        st['z'] = _dot_bf16x3(g_ref[...], gup_ref[...]) + gb_ref[...]

    def ph_log_decay():
        st['lg'] = _log_sigmoid(st['z']) * (1.0 / GLA_TAU)

    def ph_cumsum():
        st['b'] = _dot_exact_lhs(tri_ref[...], st['lg'])

    def ph_prep():
        b = st['b']
        st['k'] = k_ref[...].astype(F32)
        st['qt'] = (q_ref[...].astype(F32) * GLA_DK ** -0.5 * jnp.exp(b)).astype(BF16)
        st['kt'] = (st['k'] * jnp.exp(-b)).astype(BF16)
        st['v_t'] = v_ref[...].astype(F32).T.astype(BF16)

    def ph_scores():
        for c in order:
            st['a', c] = lax.dot_general(st['qt'][ts_of(c)], stack_heads(st['kt'][ts_of(c)], khead), dn_nt,
                                         preferred_element_type=F32)

    def ph_mask():
        b, k = st['b'], st['k']
        for c in order:
            st['a', c] = jnp.where(tri_ok, st['a', c], 0.0).astype(BF16)
            last = last_of(c)
            st['kdec', c] = jnp.where(pair_row == c % 2, k[ts2_of(c)] * jnp.exp(b[last:last + 1] - b[ts2_of(c)]),
                                      0.0).astype(BF16)

    def ph_intra():
        for c in order:
            st['o', c] = jnp.dot(st['a', c], stack_heads(v_ref[ts_of(c), :], vhead), preferred_element_type=F32)
            st['upd', c] = jnp.dot(st['v_t'][:, ts2_of(c)], st['kdec', c], preferred_element_type=F32)

    def ph_state():
        s_t = s_scr[...]
        for c in order:
            st['s', c] = s_t.astype(BF16)
            last = last_of(c)
            s_t = jnp.exp(st['b'][last:last + 1]) * s_t + jnp.where(same_head, st['upd', c], 0.0)
        s_scr[...] = s_t

    def ph_inter():
        for c in order:
            st['oi', c] = lax.dot_general(st['qt'][ts_of(c)], st['s', c], dn_nt, preferred_element_type=F32)

    def ph_out():
        for c in order:
            o_ref[ts_of(c), :] = st['o', c] + st['oi', c]

    return [ph_gate_proj, ph_log_decay, ph_cumsum, ph_prep, ph_scores, ph_mask, ph_intra, ph_state, ph_inter, ph_out]


def _scans_call(main, gates, gate_bias, tri, expand, gup_pad, gla_b):
    t = main.shape[0]
    nb = t // TB
    const = lambda i: (0, 0)

    def streams(direction):
        blk = lambda i: _scan_block(i, nb, direction)
        col = lambda c, w: (lambda i: (blk(i), c // w))
        return [pl.BlockSpec((TB, MLSTM_WIDTH), col(C_MQ, MLSTM_WIDTH)),
                pl.BlockSpec((TB, MLSTM_WIDTH), col(C_MK, MLSTM_WIDTH)),
                pl.BlockSpec((TB, MLSTM_WIDTH), col(C_MV, MLSTM_WIDTH)),
                pl.BlockSpec((TB, GLA_KEY_WIDTH), col(C_GQ, GLA_KEY_WIDTH)),
                pl.BlockSpec((TB, GLA_KEY_WIDTH), col(C_GK, GLA_KEY_WIDTH)),
                pl.BlockSpec((TB, GLA_WIDTH), col(C_GV, GLA_WIDTH)),
                pl.BlockSpec((TB, N_GATE), lambda i: (blk(i), 0))]

    consts = [pl.BlockSpec((1, N_GATE), const),
              pl.BlockSpec((TB, TB), const), pl.BlockSpec((TB, TB), const),
              pl.BlockSpec((N_GATE, MLSTM_WIDTH), const), pl.BlockSpec((N_GATE, MLSTM_WIDTH), const),
              pl.BlockSpec((N_GATE, GLA_KEY_WIDTH), const), pl.BlockSpec((N_GATE, GLA_KEY_WIDTH), const),
              pl.BlockSpec((1, GLA_KEY_WIDTH), const), pl.BlockSpec((1, GLA_KEY_WIDTH), const)]
    out_spec = lambda direction, w: pl.BlockSpec((TB, w), lambda i: (_scan_block(i, nb, direction), 0))
    n_pairs = MLSTM_HEADS // 2
    mlstm_state = [pltpu.VMEM((n_pairs, LANES, LANES), F32), pltpu.VMEM((n_pairs, LANES, LANES), F32),
                   pltpu.VMEM((n_pairs, SUBLANES, LANES), F32)]
    gla_state = [pltpu.VMEM((GLA_WIDTH, GLA_KEY_WIDTH), F32)]
    return pl.pallas_call(
        _scans_kernel,
        grid=(nb,),
        in_specs=streams(0) + streams(1) + consts,
        out_specs=[out_spec(0, MLSTM_WIDTH), out_spec(1, MLSTM_WIDTH), out_spec(0, GLA_WIDTH), out_spec(1, GLA_WIDTH)],
        out_shape=[jax.ShapeDtypeStruct((t, MLSTM_WIDTH), F32)] * 2 + [jax.ShapeDtypeStruct((t, GLA_WIDTH), F32)] * 2,
        scratch_shapes=mlstm_state + mlstm_state + gla_state + gla_state,
        compiler_params=_cparams(1),
        name="scans",
    )(*([main] * 6 + [gates]) * 2, gate_bias, tri[0], tri[1], expand[0], expand[1],
      gup_pad[0], gup_pad[1], gla_b[0:1], gla_b[1:2])


def _attn_kernel(sink_ref, q_ref, kp_ref, kc_ref, kn_ref, kx_ref, vp_ref, vc_ref, vn_ref, vx_ref, o_ref, *, seq):
    i = pl.program_id(0)
    half = TB // 2
    n_loc = 2 * TB
    keys = jnp.concatenate([kp_ref[...], kc_ref[...], kn_ref[...], kx_ref[...]], axis=0)
    vals = jnp.concatenate([vp_ref[...], vc_ref[...], vn_ref[...], vx_ref[...]], axis=0)
    n_keys = keys.shape[0]
    lane = lax.broadcasted_iota(jnp.int32, (n_keys, LANES), 1)
    zero = jnp.zeros_like(keys)
    keys_g = [jnp.where(lane < HEAD_DIM, keys, zero), jnp.where(lane >= HEAD_DIM, keys, zero)]
    r = lax.broadcasted_iota(jnp.int32, (TB, n_keys), 0)
    c = lax.broadcasted_iota(jnp.int32, (TB, n_keys), 1)
    rel = c - half - r
    kpos = (i - 1) * TB + c - half
    local_ok = (jnp.abs(rel) <= WINDOW) & (kpos >= 0) & (kpos < seq) & (i > 0)
    valid = local_ok | (c >= n_loc)
    out_lane = lax.broadcasted_iota(jnp.int32, (TB, LANES), 1)
    neg_inf = jnp.float32(-jnp.inf)
    dn = (((1,), (1,)), ((), ()))
    heads = [(m, gidx) for m in range(ATTN_Q_WIDTH // LANES) for gidx in range(ATTN_KV_HEADS)]
    scores = [lax.dot_general(q_ref[:, m * LANES:(m + 1) * LANES], keys_g[gidx], dn, preferred_element_type=F32)
              for m, gidx in heads]
    probs, denoms = [], []
    for (m, gidx), s in zip(heads, scores):
        sink = sink_ref[gidx * (ATTN_HEADS // ATTN_KV_HEADS) + m]
        s = jnp.where(valid, s, neg_inf)
        mx = jnp.maximum(jnp.max(s, axis=1, keepdims=True), sink)
        p = jnp.exp(s - mx)
        denoms.append(jnp.sum(p, axis=1, keepdims=True) + jnp.exp(sink - mx))
        probs.append(p.astype(BF16))
    pvs = [jnp.dot(p, vals, preferred_element_type=F32) for p in probs]
    for m in range(ATTN_Q_WIDTH // LANES):
        r0 = pvs[2 * m] / denoms[2 * m]
        r1 = pvs[2 * m + 1] / denoms[2 * m + 1]
        o_ref[:, m * LANES:(m + 1) * LANES] = jnp.where(out_lane < HEAD_DIM, r0, r1).astype(BF16)


def _attn_call(main, sink, seq):
    t = main.shape[0]
    nb = t // TB
    half = TB // 2
    nhb = t // half
    kcol, vcol = C_AK // LANES, C_AV // LANES
    prev_i = lambda i: jnp.maximum(2 * i - 1, 0)
    next_i = lambda i: jnp.minimum(2 * i + 2, nhb - 1)
    grid_spec = pltpu.PrefetchScalarGridSpec(
        num_scalar_prefetch=1,
        grid=(nb,),
        in_specs=[pl.BlockSpec((TB, ATTN_Q_WIDTH), lambda i, s: (i, C_AQ // ATTN_Q_WIDTH)),
                  pl.BlockSpec((half, LANES), lambda i, s: (prev_i(i), kcol)),
                  pl.BlockSpec((TB, LANES), lambda i, s: (i, kcol)),
                  pl.BlockSpec((half, LANES), lambda i, s: (next_i(i), kcol)),
                  pl.BlockSpec((TB, LANES), lambda i, s: (0, kcol)),
                  pl.BlockSpec((half, LANES), lambda i, s: (prev_i(i), vcol)),
                  pl.BlockSpec((TB, LANES), lambda i, s: (i, vcol)),
                  pl.BlockSpec((half, LANES), lambda i, s: (next_i(i), vcol)),
                  pl.BlockSpec((TB, LANES), lambda i, s: (0, vcol))],
        out_specs=pl.BlockSpec((TB, ATTN_Q_WIDTH), lambda i, s: (i, 0)),
    )
    return pl.pallas_call(
        functools.partial(_attn_kernel, seq=seq),
        grid_spec=grid_spec,
        out_shape=jax.ShapeDtypeStruct((t, ATTN_Q_WIDTH), BF16),
        compiler_params=_cparams(1),
        name="window_attn",
    )(sink, main, main, main, main, main, main, main, main, main)


def _head_norm(x, avg_bf16, w):
    mu = _dot_exact_rhs(x, avg_bf16)
    cen = x - mu
    var = _dot_exact_rhs(cen * cen, avg_bf16)
    return cen * lax.rsqrt(var + LN_EPS) * w


def _outproj_kernel(mf_ref, mb_ref, mo_ref, at_ref, gf_ref, gbk_ref, go_ref, mnw_ref, gnw_ref, avg_ref,
                    w_ref, x_ref, mod_ref, lnw_ref, lnb_ref, x1_ref, h2_ref, wbf_scr):
    i = pl.program_id(0)
    tm = x_ref.shape[0]

    @pl.when(i == 0)
    def _():
        for dst, src, n in _column_runs(_out_row_perm()):
            wbf_scr[dst:dst + n, :] = w_ref[src:src + n, :].astype(BF16)

    avg = avg_ref[...]
    hm = _head_norm(mf_ref[...] + mb_ref[...], avg, mnw_ref[...]) * _sigmoid(mo_ref[...].astype(F32))
    gate = go_ref[...].astype(F32)
    hg = _head_norm(gf_ref[...] + gbk_ref[...], avg, gnw_ref[...]) * (gate * _sigmoid(gate))
    a0, a1, a2 = MLSTM_WIDTH, MLSTM_WIDTH + ATTN_Q_WIDTH, D_MODEL
    mix = (jnp.dot(hm.astype(BF16), wbf_scr[:a0], preferred_element_type=F32)
           + jnp.dot(at_ref[...], wbf_scr[a0:a1], preferred_element_type=F32)
           + jnp.dot(hg.astype(BF16), wbf_scr[a1:a2], preferred_element_type=F32))
    x1 = _layer_norm(DEEPNORM_ALPHA * x_ref[...] + _select_mod(mod_ref, 2, i, tm) * mix, lnw_ref[...], lnb_ref[...])
    x1_ref[...] = x1
    h2_ref[...] = (x1 * (1.0 + _select_mod(mod_ref, 4, i, tm)) + _select_mod(mod_ref, 3, i, tm)).astype(h2_ref.dtype)


def _outproj_call(mf, mb, main, attn, gf, gbk, mnw, gnw, avg, w_out, layer, x, mod, lnw, lnb, h2_dtype):
    t, d = x.shape
    tm = PROJ_TM
    row = lambda i: (i, 0)
    const = lambda i: (0, 0)
    return pl.pallas_call(
        _outproj_kernel,
        grid=(t // tm,),
        in_specs=[pl.BlockSpec((tm, MLSTM_WIDTH), row),
                  pl.BlockSpec((tm, MLSTM_WIDTH), row),
                  pl.BlockSpec((tm, MLSTM_WIDTH), lambda i: (i, C_MO // MLSTM_WIDTH)),
                  pl.BlockSpec((tm, ATTN_Q_WIDTH), row),
                  pl.BlockSpec((tm, GLA_WIDTH), row),
                  pl.BlockSpec((tm, GLA_WIDTH), row),
                  pl.BlockSpec((tm, GLA_WIDTH), lambda i: (i, C_GO // GLA_WIDTH)),
                  pl.BlockSpec((1, MLSTM_WIDTH), const),
                  pl.BlockSpec((1, GLA_WIDTH), const),
                  pl.BlockSpec((MLSTM_WIDTH, MLSTM_WIDTH), const),
                  pl.BlockSpec((None, d, d), lambda i: (layer, 0, 0), pipeline_mode=pl.Buffered(1)),
                  pl.BlockSpec((tm, d), row),
                  pl.BlockSpec((2, SUBLANES, d), lambda i: (0, 0, 0)),
                  pl.BlockSpec((1, d), const),
                  pl.BlockSpec((1, d), const)],
        out_specs=[pl.BlockSpec((tm, d), row), pl.BlockSpec((tm, d), row)],
        out_shape=[jax.ShapeDtypeStruct((t, d), F32), jax.ShapeDtypeStruct((t, d), h2_dtype)],
        scratch_shapes=[pltpu.VMEM((d, d), BF16)],
        compiler_params=_cparams(1),
        name="out_proj",
    )(mf, mb, main, attn, gf, gbk, main, mnw, gnw, avg, w_out, x, mod, lnw, lnb)


def _select_mod(mod_ref, k, i, tm):
    rows = i * tm + lax.broadcasted_iota(jnp.int32, (tm, 1), 0)
    return jnp.where(rows < CTX_LEN, mod_ref[0, k:k + 1, :], mod_ref[1, k:k + 1, :])


def _ffn_kernel(h_ref, wgu_ref, wd_ref, x_ref, mod_ref, lnw_ref, lnb_ref, o_ref):
    h = h_ref[...]
    g = jnp.dot(h, wgu_ref[:, :D_FF], preferred_element_type=F32)
    u = jnp.dot(h, wgu_ref[:, D_FF:], preferred_element_type=F32)
    y = jnp.dot((g * _sigmoid(g) * u).astype(BF16), wd_ref[...], preferred_element_type=F32)
    m5 = _select_mod(mod_ref, 5, pl.program_id(0), h_ref.shape[0])
    o_ref[...] = _layer_norm(DEEPNORM_ALPHA * x_ref[...] + m5 * y, lnw_ref[...], lnb_ref[...])


def _ffn_call(h2, w_gu, w_down, layer, x1, mod, lnw, lnb):
    t, d = x1.shape
    resident = pl.Buffered(1)
    return pl.pallas_call(
        _ffn_kernel,
        grid=(t // FFN_TM,),
        in_specs=[pl.BlockSpec((FFN_TM, d), lambda i: (i, 0)),
                  pl.BlockSpec((None, d, 2 * D_FF), lambda i: (layer, 0, 0), pipeline_mode=resident),
                  pl.BlockSpec((None, D_FF, d), lambda i: (layer, 0, 0), pipeline_mode=resident),
                  pl.BlockSpec((FFN_TM, d), lambda i: (i, 0)),
                  pl.BlockSpec((2, SUBLANES, d), lambda i: (0, 0, 0)),
                  pl.BlockSpec((1, d), lambda i: (0, 0)),
                  pl.BlockSpec((1, d), lambda i: (0, 0))],
        out_specs=pl.BlockSpec((FFN_TM, d), lambda i: (i, 0)),
        out_shape=jax.ShapeDtypeStruct((t, d), F32),
        compiler_params=_cparams(1),
        name="dense_ffn",
    )(h2, w_gu, w_down, x1, mod, lnw, lnb)


def _router_kernel(h_ref, wr_ref, br_ref, su_ref, route_ref, cnt_ref, base_scr):
    @pl.when(pl.program_id(0) == 0)
    def _():
        base_scr[...] = jnp.zeros_like(base_scr)

    lt = lax.dot_general(wr_ref[...], h_ref[...], (((1,), (1,)), ((), ())),
                         preferred_element_type=F32, precision=HIGHEST) + br_ref[...]
    idx = lax.broadcasted_iota(jnp.int32, lt.shape, 0)
    neg_inf = jnp.float32(-jnp.inf)
    m1 = jnp.max(lt, axis=0, keepdims=True)
    e1 = jnp.min(jnp.where(lt == m1, idx, N_EXPERTS), axis=0, keepdims=True)
    lt2 = jnp.where(idx == e1, neg_inf, lt)
    m2 = jnp.max(lt2, axis=0, keepdims=True)
    e2 = jnp.min(jnp.where(lt2 == m2, idx, N_EXPERTS), axis=0, keepdims=True)
    t2 = jnp.exp(m2 - m1)
    w1 = 1.0 / (1.0 + t2)
    w2 = t2 / (1.0 + t2)
    su = su_ref[...]
    base = base_scr[...][:, 0:1]
    oh1 = (idx == e1).astype(F32)
    cum1 = jnp.dot(oh1.astype(BF16), su, preferred_element_type=F32)
    rank1 = jnp.sum(oh1 * (base + cum1), axis=0, keepdims=True)
    base = base + jnp.sum(oh1, axis=1, keepdims=True)
    oh2 = (idx == e2).astype(F32)
    cum2 = jnp.dot(oh2.astype(BF16), su, preferred_element_type=F32)
    rank2 = jnp.sum(oh2 * (base + cum2), axis=0, keepdims=True)
    base = base + jnp.sum(oh2, axis=1, keepdims=True)
    base_scr[...] = jnp.broadcast_to(base, base_scr.shape)
    cnt_ref[...] = jnp.broadcast_to(base, cnt_ref.shape)
    zero = jnp.zeros_like(w1)
    route_ref[...] = jnp.concatenate(
        [e1.astype(F32), e2.astype(F32), w1, w2, rank1, rank2, zero, zero], axis=0)


def _router_call(h2, wr_t, br, su):
    t, d = h2.shape
    nb = t // TB
    return pl.pallas_call(
        _router_kernel,
        grid=(nb,),
        in_specs=[pl.BlockSpec((TB, d), lambda i: (i, 0)),
                  pl.BlockSpec((N_EXPERTS, d), lambda i: (0, 0)),
                  pl.BlockSpec((N_EXPERTS, 1), lambda i: (0, 0)),
                  pl.BlockSpec((TB, TB), lambda i: (0, 0))],
        out_specs=[pl.BlockSpec((SUBLANES, TB), lambda i: (0, i)),
                   pl.BlockSpec((N_EXPERTS, LANES), lambda i: (0, 0))],
        out_shape=[jax.ShapeDtypeStruct((SUBLANES, t), F32),
                   jax.ShapeDtypeStruct((N_EXPERTS, LANES), F32)],
        scratch_shapes=[pltpu.VMEM((N_EXPERTS, LANES), F32)],
        compiler_params=_cparams(1),
        name="moe_router",
    )(h2, wr_t, br, su)


ROW_TILE = D_MODEL // LANES
DMA_UNROLL = 8


def _store_row_tiles(ref, val):
    n = val.shape[0]
    for j in range(ROW_TILE):
        ref[pl.ds(j, n, stride=ROW_TILE), :] = val[:, j * LANES:(j + 1) * LANES]


def _load_row_tiles(ref, n, dtype=F32):
    return jnp.concatenate([ref[pl.ds(j, n, stride=ROW_TILE), :].astype(dtype) for j in range(ROW_TILE)], axis=1)


def _row(ref, idx):
    return ref.at[pl.ds(pl.multiple_of(idx * ROW_TILE, ROW_TILE), ROW_TILE)]


def _dispatch_kernel(s1_ref, s2_ref, pad_lo_ref, pad_hi_ref, h_ref, xs_ref, hr_scr, zero_scr, sem, zsem):
    i = pl.program_id(0)

    @pl.when(i == 0)
    def _():
        zero_scr[...] = jnp.zeros_like(zero_scr)
        zero_row = lambda s: pltpu.make_async_copy(zero_scr, _row(xs_ref, s), zsem)
        for e in range(N_EXPERTS + 1):
            lax.fori_loop(pad_lo_ref[e], pad_hi_ref[e], lambda s, c: (zero_row(s).start(), c)[1], 0)
        for e in range(N_EXPERTS + 1):
            lax.fori_loop(pad_lo_ref[e], pad_hi_ref[e], lambda s, c: (zero_row(s).wait(), c)[1], 0)

    n = pl.num_programs(0)
    cur = i % 2

    def copy(blk, r, slot_ref):
        buf = blk % 2
        return pltpu.make_async_copy(_row(hr_scr.at[buf], r), _row(xs_ref, slot_ref[blk * TB + r]), sem.at[buf])

    def start(r, carry):
        copy(i, r, s1_ref).start()
        copy(i, r, s2_ref).start()
        return carry

    def wait_block(blk):
        def wait(r, carry):
            copy(blk, r, s1_ref).wait()
            copy(blk, r, s2_ref).wait()
            return carry
        lax.fori_loop(0, TB, wait, 0, unroll=DMA_UNROLL)

    _store_row_tiles(hr_scr.at[cur], h_ref[...])
    lax.fori_loop(0, TB, start, 0, unroll=DMA_UNROLL)

    @pl.when(i > 0)
    def _():
        wait_block(i - 1)

    @pl.when(i == n - 1)
    def _():
        wait_block(i)


def _dispatch_call(slot1, slot2, pad_lo, pad_hi, h2, n_slots):
    t, d = h2.shape
    grid_spec = pltpu.PrefetchScalarGridSpec(
        num_scalar_prefetch=4,
        grid=(t // TB,),
        in_specs=[pl.BlockSpec((TB, d), lambda i, *_: (i, 0))],
        out_specs=pl.BlockSpec(memory_space=pl.ANY),
        scratch_shapes=[pltpu.VMEM((2, TB * ROW_TILE, LANES), F32), pltpu.VMEM((ROW_TILE, LANES), F32),
                        pltpu.SemaphoreType.DMA((2,)), pltpu.SemaphoreType.DMA(())],
    )
    return pl.pallas_call(
        _dispatch_kernel,
        grid_spec=grid_spec,
        out_shape=jax.ShapeDtypeStruct((n_slots * ROW_TILE, LANES), F32),
        compiler_params=_cparams(1),
        name="moe_dispatch",
    )(slot1, slot2, pad_lo, pad_hi, h2)


def _expert_kernel(be_ref, used_ref, x_ref, wg_ref, wu_ref, wd_ref, y_ref, xb_scr, acc_scr):
    b, f = pl.program_id(0), pl.program_id(1)
    live = b < used_ref[0]

    @pl.when(live & (f == 0))
    def _():
        for j in range(ROW_TILE):
            xb_scr[:, j * LANES:(j + 1) * LANES] = x_ref[pl.ds(j, MOE_G, stride=ROW_TILE), :].astype(BF16)

    @pl.when(live)
    def _():
        x = xb_scr[...]
        g = jnp.dot(x, wg_ref[...], preferred_element_type=F32)
        u = jnp.dot(x, wu_ref[...], preferred_element_type=F32)
        part = jnp.dot((g * _sigmoid(g) * u).astype(BF16), wd_ref[...], preferred_element_type=F32)

        @pl.when(f == 0)
        def _():
            acc_scr[...] = part

        @pl.when(f > 0)
        def _():
            acc_scr[...] += part

    @pl.when(f == pl.num_programs(1) - 1)
    def _():
        @pl.when(live)
        def _():
            _store_row_tiles(y_ref, acc_scr[...])

        @pl.when(jnp.logical_not(live))
        def _():
            y_ref[...] = jnp.zeros_like(y_ref)


def _expert_call(block_e, used, xs, w_gu, w_down, layer):
    d = D_MODEL
    ns = xs.shape[0] // ROW_TILE
    nblk = ns // MOE_G
    nf = D_EXPERT // MOE_TF
    rows = MOE_G * ROW_TILE

    def f_eff(b, f, used_ref):
        return jnp.where(b < used_ref[0], f, nf - 1)

    grid_spec = pltpu.PrefetchScalarGridSpec(
        num_scalar_prefetch=2,
        grid=(nblk, nf),
        in_specs=[pl.BlockSpec((rows, LANES), lambda b, f, be, us: (b, 0)),
                  pl.BlockSpec((None, None, d, MOE_TF), lambda b, f, be, us: (layer, be[b], 0, f_eff(b, f, us))),
                  pl.BlockSpec((None, None, d, MOE_TF),
                               lambda b, f, be, us: (layer, be[b], 0, nf + f_eff(b, f, us))),
                  pl.BlockSpec((None, None, MOE_TF, d), lambda b, f, be, us: (layer, be[b], f_eff(b, f, us), 0))],
        out_specs=pl.BlockSpec((rows, LANES), lambda b, f, be, us: (b, 0)),
        scratch_shapes=[pltpu.VMEM((MOE_G, d), BF16), pltpu.VMEM((MOE_G, d), F32)],
    )
    return pl.pallas_call(
        _expert_kernel,
        grid_spec=grid_spec,
        out_shape=jax.ShapeDtypeStruct(xs.shape, F32),
        compiler_params=_cparams(2),
        name="moe_experts",
    )(block_e, used, xs, w_gu, w_gu, w_down)


def _combine_kernel(s1_ref, s2_ref, yb_ref, route_ref, x_ref, mod_ref, lnw_ref, lnb_ref, o_ref, buf1, buf2, sem):
    i = pl.program_id(0)
    n = pl.num_programs(0)
    cur = i % 2

    def copy(blk, r, slot_ref, buf):
        b = blk % 2
        return pltpu.make_async_copy(_row(yb_ref, slot_ref[blk * TB + r]), _row(buf.at[b], r), sem.at[b])

    def start_block(blk):
        def start(r, carry):
            copy(blk, r, s1_ref, buf1).start()
            copy(blk, r, s2_ref, buf2).start()
            return carry
        lax.fori_loop(0, TB, start, 0, unroll=DMA_UNROLL)

    def wait(r, carry):
        copy(i, r, s1_ref, buf1).wait()
        copy(i, r, s2_ref, buf2).wait()
        return carry

    @pl.when(i == 0)
    def _():
        start_block(i)

    @pl.when(i + 1 < n)
    def _():
        start_block(i + 1)

    lax.fori_loop(0, TB, wait, 0, unroll=DMA_UNROLL)
    rt = route_ref[...].T
    y = rt[:, 2:3] * _load_row_tiles(buf1.at[cur], TB) + rt[:, 3:4] * _load_row_tiles(buf2.at[cur], TB)
    m = mod_ref[...]
    o_ref[...] = _layer_norm(DEEPNORM_ALPHA * x_ref[...] + m[5:6] * y, lnw_ref[...], lnb_ref[...])


def _combine_call(slot1, slot2, yb, route, x1, mod, lnw, lnb, latent_only):
    t, d = x1.shape
    if latent_only:
        out_rows, out_map = t - CTX_LEN, lambda i, a, b: (jnp.maximum(i - 1, 0), 0)
    else:
        out_rows, out_map = t, lambda i, a, b: (i, 0)
    grid_spec = pltpu.PrefetchScalarGridSpec(
        num_scalar_prefetch=2,
        grid=(t // TB,),
        in_specs=[pl.BlockSpec(memory_space=pl.ANY),
                  pl.BlockSpec((SUBLANES, TB), lambda i, a, b: (0, i)),
                  pl.BlockSpec((TB, d), lambda i, a, b: (i, 0)),
                  pl.BlockSpec((None, SUBLANES, d), lambda i, a, b: (_who(i), 0, 0)),
                  pl.BlockSpec((1, d), lambda i, a, b: (0, 0)),
                  pl.BlockSpec((1, d), lambda i, a, b: (0, 0))],
        out_specs=pl.BlockSpec((TB, d), out_map),
        scratch_shapes=[pltpu.VMEM((2, TB * ROW_TILE, LANES), F32), pltpu.VMEM((2, TB * ROW_TILE, LANES), F32),
                        pltpu.SemaphoreType.DMA((2,))],
    )
    return pl.pallas_call(
        _combine_kernel,
        grid_spec=grid_spec,
        out_shape=jax.ShapeDtypeStruct((out_rows, d), F32),
        compiler_params=_cparams(1),
        name="moe_combine",
    )(slot1, slot2, yb, route, x1, mod, lnw, lnb)


def _moe_layer(h2, x1, mod, lnw, lnb, w_router, b_router, w_gu, w_down, layer, su, latent_only):
    t, d = h2.shape
    route, cnt = _router_call(h2, w_router.T, b_router.reshape(N_EXPERTS, 1), su)
    counts = cnt[:, 0].astype(jnp.int32)
    padded = (counts + MOE_G - 1) // MOE_G * MOE_G
    pend = jnp.cumsum(padded)
    pstart = pend - padded
    e1, e2 = route[0].astype(jnp.int32), route[1].astype(jnp.int32)
    slot1 = pstart[e1] + route[4].astype(jnp.int32)
    slot2 = pstart[e2] + route[5].astype(jnp.int32)
    nblk = -(-(2 * t) // MOE_G) + N_EXPERTS
    blk_start = jnp.arange(nblk, dtype=jnp.int32) * MOE_G
    block_e = jnp.minimum(jnp.sum((pend[None, :] <= blk_start[:, None]).astype(jnp.int32), axis=1), N_EXPERTS - 1)
    used = (pend[-1:] // MOE_G).astype(jnp.int32)
    n_slots = nblk * MOE_G
    pad_lo = jnp.concatenate([pstart + counts, pend[-1:]])
    pad_hi = jnp.concatenate([pend, jnp.full((1,), n_slots, jnp.int32)])
    xs = _dispatch_call(slot1, slot2, pad_lo, pad_hi, h2, n_slots)
    yb = _expert_call(block_e, used, xs, w_gu, w_down, layer)
    return _combine_call(slot1, slot2, yb, route, x1, mod, lnw, lnb, latent_only)


def _attn_head_perm():
    cols = []
    for m in range(ATTN_HEADS // ATTN_KV_HEADS):
        for hq in (m, m + ATTN_HEADS // ATTN_KV_HEADS):
            cols.extend(range(hq * HEAD_DIM, (hq + 1) * HEAD_DIM))
    return np.asarray(cols, np.int32)


def _in_col_perm():
    n_m = 4 * MLSTM_WIDTH
    mg = 2 * 2 * MLSTM_HEADS
    a0 = n_m + mg
    perm = list(range(n_m))
    perm += [a0 + int(j) for j in _attn_head_perm()]
    perm += list(range(a0 + ATTN_Q_WIDTH, a0 + ATTN_Q_WIDTH + 2 * ATTN_KV_WIDTH))
    g0 = a0 + ATTN_Q_WIDTH + 2 * ATTN_KV_WIDTH
    perm += list(range(g0, g0 + 2 * GLA_KEY_WIDTH + 2 * GLA_WIDTH))
    perm += list(range(n_m, n_m + mg))
    perm += list(range(g0 + 2 * GLA_KEY_WIDTH + 2 * GLA_WIDTH, g0 + 2 * GLA_KEY_WIDTH + 2 * GLA_WIDTH + 2 * GLA_RANK))
    return np.asarray(perm, np.int32)


def _out_row_perm():
    return np.concatenate([np.arange(MLSTM_WIDTH), MLSTM_WIDTH + _attn_head_perm(),
                           np.arange(MLSTM_WIDTH + ATTN_Q_WIDTH, D_MODEL)]).astype(np.int32)


def _column_runs(perm):
    runs, start = [], 0
    for j in range(1, len(perm) + 1):
        if j == len(perm) or perm[j] != perm[j - 1] + 1:
            runs.append((start, int(perm[start]), j - start))
            start = j
    return runs


def _rope_tables(seq):
    inv = ROPE_BASE ** (-jnp.arange(ROPE_PAIRS, dtype=F32) / ROPE_PAIRS)
    rows = seq // GRID_W
    ang_r = jnp.arange(rows).astype(F32)[:, None] * inv
    ang_c = jnp.arange(GRID_W).astype(F32)[:, None] * inv
    lane_pat = lambda a, b: jnp.tile(jnp.concatenate([a, b], -1), (1, LANES // (2 * ROPE_PAIRS)))
    cos_rows, sin_rows = lane_pat(jnp.cos(ang_r), jnp.cos(ang_r)), lane_pat(-jnp.sin(ang_r), jnp.sin(ang_r))
    cos_cols, sin_cols = lane_pat(jnp.cos(ang_c), jnp.cos(ang_c)), lane_pat(-jnp.sin(ang_c), jnp.sin(ang_c))
    row_part = (np.arange(LANES) % HEAD_DIM) < HEAD_DIM // 2
    per_token = lambda by_row, by_col: jnp.where(
        row_part, jnp.broadcast_to(by_row[:, None, :], (rows, GRID_W, LANES)),
        jnp.broadcast_to(by_col[None, :, :], (rows, GRID_W, LANES))).reshape(seq, LANES)
    cos_l, sin_l = per_token(cos_rows, cos_cols), per_token(sin_rows, sin_cols)
    cos_t = jnp.concatenate([jnp.ones((CTX_LEN, LANES), F32), cos_l], 0)
    sin_t = jnp.concatenate([jnp.zeros((CTX_LEN, LANES), F32), sin_l], 0)
    return cos_t, sin_t


def _block_tri(direction):
    r = np.arange(TB)[:, None]
    c = np.arange(TB)[None, :]
    same = (r // CHUNK) == (c // CHUNK)
    tri = (c <= r) if direction == 0 else (c >= r)
    return jnp.asarray(same & tri, BF16)


def kernel(x, c, ctx, c_ctx, w_ada, b_ada, w_in, mlstm_gate_b, mlstm_norm_w, attn_sink, gla_gate_up, gla_gate_b,
           gla_norm_w, w_out, ln_w, ln_b, ffn_w_gu, ffn_w_down, router_w, router_b, moe_w_gu, moe_w_down):
    seq, d = x.shape[1], x.shape[2]
    depth = w_in.shape[0]
    xt = jnp.concatenate([ctx[0], x[0]], axis=0)

    cvec = jnp.zeros((SUBLANES, d), F32).at[0].set(c_ctx).at[1].set(c[0])
    mods = _mod_call(cvec, w_ada, b_ada)[:, :2].reshape(depth, 2, 6, d)
    mods = jnp.pad(mods, ((0, 0), (0, 0), (0, SUBLANES - 6), (0, 0)))

    gate_bias = jnp.pad(mlstm_gate_b.reshape(depth, 1, -1), ((0, 0), (0, 0), (0, N_GATE - 4 * MLSTM_HEADS)))
    gup_pad = jnp.zeros((depth, 2, N_GATE, GLA_KEY_WIDTH), F32)
    for dr in range(2):
        lo = GLA_GATE_OFF + dr * GLA_RANK
        gup_pad = gup_pad.at[:, dr, lo:lo + GLA_RANK, :].set(gla_gate_up[:, dr])
    cos_t, sin_t = _rope_tables(seq)
    tri = [_block_tri(0), _block_tri(1)]
    su = jnp.asarray(np.arange(TB)[:, None] < np.arange(TB)[None, :], BF16)
    hh = np.arange(MLSTM_WIDTH) // MLSTM_DH
    avg = jnp.asarray((hh[:, None] == hh[None, :]) / MLSTM_DH, BF16)
    ffn_gu, ffn_dn = ffn_w_gu.astype(BF16), ffn_w_down.astype(BF16)
    moe_gu, moe_dn = moe_w_gu.astype(BF16), moe_w_down.astype(BF16)

    expand = []
    for dr in range(2):
        e = np.zeros((N_GATE, MLSTM_WIDTH), np.float32)
        for h in range(MLSTM_HEADS):
            e[dr * 2 * MLSTM_HEADS + h, h * MLSTM_DH:(h + 1) * MLSTM_DH] = 1.0
        expand.append(jnp.asarray(e, BF16))

    for l in range(depth):
        is_moe = l % 2 == 1
        last = l == depth - 1
        main, gates = _inproj_call(xt, mods[l], w_in, l, cos_t, sin_t)
        mf, mb, gf, gbk = _scans_call(main, gates, gate_bias[l], tri, expand, gup_pad[l], gla_gate_b[l])
        attn = _attn_call(main, attn_sink[l], seq)
        x1, h2 = _outproj_call(mf, mb, main, attn, gf, gbk, mlstm_norm_w[l:l + 1], gla_norm_w[l:l + 1], avg,
                               w_out, l, xt, mods[l], ln_w[l, 0:1], ln_b[l, 0:1], F32 if is_moe else BF16)
        if is_moe:
            xt = _moe_layer(h2, x1, mods[l], ln_w[l, 1:2], ln_b[l, 1:2], router_w[l // 2], router_b[l // 2],
                            moe_gu, moe_dn, l // 2, su, latent_only=last)
        else:
            xt = _ffn_call(h2, ffn_gu, ffn_dn, l // 2, x1, mods[l], ln_w[l, 1:2], ln_b[l, 1:2])
    return (xt if depth % 2 == 0 else xt[CTX_LEN:])[None]
```

```python
import functools

import jax
import jax.numpy as jnp
import numpy as np
from jax import lax
from jax.experimental import pallas as pl
from jax.experimental.pallas import tpu as pltpu

F32 = jnp.float32
BF16 = jnp.bfloat16
HIGHEST = lax.Precision.HIGHEST

D_MODEL = 1024
SEQ = 16384
DEPTH = 4
GRID_W = 64
CTX_LEN = 256
MLSTM_HEADS = 4
MLSTM_DH = 64
MLSTM_WIDTH = 256
HEAD_DIM = 64
ATTN_HEADS = 8
ATTN_KV_HEADS = 2
ATTN_Q_WIDTH = 512
ATTN_KV_WIDTH = 128
WINDOW = 128
ROPE_BASE = 10000.0
ROPE_PAIRS = 16
GLA_HEADS = 4
GLA_DK = 32
GLA_DV = 64
GLA_WIDTH = 256
GLA_KEY_WIDTH = 128
GLA_RANK = 16
GLA_TAU = 16.0
CHUNK = 64
D_FF = 2816
N_EXPERTS = 8
D_EXPERT = 3584
DEEPNORM_ALPHA = (2 * DEPTH) ** 0.25
LN_EPS = 1e-5

LANES = 128
SUBLANES = 8
VMEM_LIMIT = 56 * 1024 * 1024

TB = 256
N_CHUNKS = TB // CHUNK
PROJ_TM = 640
FFN_TM = 640
MOE_G = 512
MOE_TF = 1792

C_MQ, C_MK, C_MV, C_MO = 0, 256, 512, 768
C_AQ, C_AK, C_AV = 1024, 1536, 1664
C_GQ, C_GK, C_GV, C_GO = 1792, 1920, 2048, 2304
N_MAIN = 2560
N_GATE = 128
N_PROJ = N_MAIN + N_GATE
GLA_GATE_OFF = 16


def _cparams(n_axes=1):
    return pltpu.CompilerParams(dimension_semantics=("arbitrary",) * n_axes,
                                vmem_limit_bytes=VMEM_LIMIT)


def _sigmoid(x):
    return 1.0 / (1.0 + jnp.exp(-x))


def _log_sigmoid(x):
    return jnp.minimum(x, 0.0) - jnp.log(1.0 + jnp.exp(-jnp.abs(x)))


def _split3(x):
    hi = x.astype(BF16)
    r1 = x - hi.astype(F32)
    mid = r1.astype(BF16)
    lo = (r1 - mid.astype(F32)).astype(BF16)
    return hi, mid, lo


def _dot_exact_rhs(x, m_bf16):
    hi, mid, lo = _split3(x)
    d = lambda a: jnp.dot(a, m_bf16, preferred_element_type=F32)
    return d(hi) + d(mid) + d(lo)


def _dot_exact_lhs(m_bf16, x):
    hi, mid, lo = _split3(x)
    d = lambda a: jnp.dot(m_bf16, a, preferred_element_type=F32)
    return d(hi) + d(mid) + d(lo)


def _dot_bf16x3(a, b):
    a_hi, b_hi = a.astype(BF16), b.astype(BF16)
    a_lo = (a - a_hi.astype(F32)).astype(BF16)
    b_lo = (b - b_hi.astype(F32)).astype(BF16)
    d = lambda x, y: jnp.dot(x, y, preferred_element_type=F32)
    return d(a_hi, b_hi) + d(a_hi, b_lo) + d(a_lo, b_hi)


def _layer_norm(v, w, b):
    mu = jnp.mean(v, axis=-1, keepdims=True)
    cen = v - mu
    var = jnp.mean(cen * cen, axis=-1, keepdims=True)
    return cen * lax.rsqrt(var + LN_EPS) * w + b


def _mod_kernel(c_ref, w_ref, b_ref, o_ref):
    c = c_ref[...]
    sc = c * _sigmoid(c)
    o_ref[...] = jnp.dot(sc, w_ref[...], preferred_element_type=F32, precision=HIGHEST) + b_ref[...]


def _mod_call(cvec, w_ada, b_ada):
    depth, d, n = w_ada.shape
    tn = 1536
    return pl.pallas_call(
        _mod_kernel,
        grid=(depth, n // tn),
        in_specs=[pl.BlockSpec((SUBLANES, d), lambda l, j: (0, 0)),
                  pl.BlockSpec((None, d, tn), lambda l, j: (l, 0, j)),
                  pl.BlockSpec((None, 1, tn), lambda l, j: (l, 0, j))],
        out_specs=pl.BlockSpec((None, SUBLANES, tn), lambda l, j: (l, 0, j)),
        out_shape=jax.ShapeDtypeStruct((depth, SUBLANES, n), F32),
        compiler_params=_cparams(2),
        name="ada_mod",
    )(cvec, w_ada, b_ada.reshape(depth, 1, n))


def _who(i):
    return jnp.minimum(i, 1)


def _rope(x, cos, sin_signed, first_half):
    swapped = jnp.where(first_half, pltpu.roll(x, LANES - ROPE_PAIRS, 1), pltpu.roll(x, ROPE_PAIRS, 1))
    return x * cos + swapped * sin_signed


def _inproj_kernel(x_ref, mod_ref, w_ref, cos_ref, sin_ref, main_ref, gate_ref, wbf_scr):
    i = pl.program_id(0)
    tm = x_ref.shape[0]

    @pl.when(i == 0)
    def _():
        for dst, src, n in _column_runs(_in_col_perm()):
            wbf_scr[:, dst:dst + n] = w_ref[:, src:src + n].astype(BF16)
        wbf_scr[:, w_ref.shape[1]:] = jnp.zeros((w_ref.shape[0], N_PROJ - w_ref.shape[1]), BF16)

    h = x_ref[...] * (1.0 + _select_mod(mod_ref, 1, i, tm)) + _select_mod(mod_ref, 0, i, tm)
    p = jnp.dot(h.astype(BF16), wbf_scr[...], preferred_element_type=F32)
    cos, sin = cos_ref[...], sin_ref[...]
    lane = lax.broadcasted_iota(jnp.int32, (tm, LANES), 1)
    first_half = (lane % (2 * ROPE_PAIRS)) < ROPE_PAIRS
    main_ref[:, :C_AQ] = p[:, :C_AQ].astype(BF16)
    for j in range(ATTN_Q_WIDTH // LANES):
        lo = C_AQ + j * LANES
        main_ref[:, lo:lo + LANES] = (_rope(p[:, lo:lo + LANES], cos, sin, first_half) * HEAD_DIM ** -0.5).astype(BF16)
    main_ref[:, C_AK:C_AV] = _rope(p[:, C_AK:C_AV], cos, sin, first_half).astype(BF16)
    main_ref[:, C_AV:] = p[:, C_AV:N_MAIN].astype(BF16)
    gate_ref[...] = p[:, N_MAIN:]


def _inproj_call(x, mod, w_in, layer, cos_t, sin_t):
    t, d = x.shape
    n_in = w_in.shape[2]
    tm = PROJ_TM
    return pl.pallas_call(
        _inproj_kernel,
        grid=(t // tm,),
        in_specs=[pl.BlockSpec((tm, d), lambda i: (i, 0)),
                  pl.BlockSpec((2, SUBLANES, d), lambda i: (0, 0, 0)),
                  pl.BlockSpec((None, d, n_in), lambda i: (layer, 0, 0), pipeline_mode=pl.Buffered(1)),
                  pl.BlockSpec((tm, LANES), lambda i: (i, 0)),
                  pl.BlockSpec((tm, LANES), lambda i: (i, 0))],
        out_specs=[pl.BlockSpec((tm, N_MAIN), lambda i: (i, 0)),
                   pl.BlockSpec((tm, N_GATE), lambda i: (i, 0))],
        out_shape=[jax.ShapeDtypeStruct((t, N_MAIN), BF16),
                   jax.ShapeDtypeStruct((t, N_GATE), F32)],
        scratch_shapes=[pltpu.VMEM((d, N_PROJ), BF16)],
        compiler_params=_cparams(1),
        name="in_proj",
    )(x, mod, w_in, cos_t, sin_t)


def _scan_block(i, nb, direction):
    if direction == 0:
        return i
    return jnp.where(i == 0, 0, nb - i)


def _scans_kernel(mq_f, mk_f, mv_f, gq_f, gk_f, gv_f, g_f, mq_b, mk_b, mv_b, gq_b, gk_b, gv_b, g_b,
                  gbias, tri_f, tri_b, exp_f, exp_b, gup_f, gup_b, glb_f, glb_b,
                  om_f, om_b, og_f, og_b, c_f, n_f, m_f, c_b, n_b, m_b, s_f, s_b):
    @pl.when(pl.program_id(0) == 0)
    def _():
        for scr in (c_f, n_f, m_f, c_b, n_b, m_b, s_f, s_b):
            scr[...] = jnp.zeros_like(scr)

    _run_interleaved([
        _mlstm_phases(mq_f, mk_f, mv_f, g_f, gbias, tri_f, exp_f, om_f, c_f, n_f, m_f, 0),
        _mlstm_phases(mq_b, mk_b, mv_b, g_b, gbias, tri_b, exp_b, om_b, c_b, n_b, m_b, 1),
        _gla_phases(gq_f, gk_f, gv_f, g_f, gup_f, glb_f, tri_f, og_f, s_f, 0),
        _gla_phases(gq_b, gk_b, gv_b, g_b, gup_b, glb_b, tri_b, og_b, s_b, 1),
    ])


def _run_interleaved(streams):
    for k in range(max(len(s) for s in streams)):
        for s in streams:
            if k < len(s):
                s[k]()


def _mlstm_phases(q_ref, k_ref, v_ref, g_ref, gb_ref, tri_ref, exp_ref, o_ref, c_scr, n_scr, m_scr, direction):
    fwd = direction == 0
    neg_inf = jnp.float32(-jnp.inf)
    order = list(range(N_CHUNKS)) if fwd else list(range(N_CHUNKS - 1, -1, -1))
    pairs = list(range(MLSTM_HEADS // 2))
    inst = [(p, c) for p in pairs for c in order]
    last_of = lambda c: c * CHUNK + (CHUNK - 1 if fwd else 0)
    ts_of = lambda c: slice(c * CHUNK, (c + 1) * CHUNK)
    ts2_of = lambda c: slice((c // 2) * 2 * CHUNK, (c // 2 + 1) * 2 * CHUNK)
    lp_of = lambda p: slice(p * LANES, (p + 1) * LANES)
    dn_nt = (((1,), (1,)), ((), ()))
    st = {}

    lane = lax.broadcasted_iota(jnp.int32, (CHUNK, LANES), 1)
    trow = lax.broadcasted_iota(jnp.int32, (CHUNK, LANES), 0)
    low = lane < MLSTM_DH
    tri_ok = (lane % MLSTM_DH <= trow) if fwd else (lane % MLSTM_DH >= trow)
    lane2 = lax.broadcasted_iota(jnp.int32, (LANES, LANES), 1)
    row2 = lax.broadcasted_iota(jnp.int32, (LANES, LANES), 0)
    same_head = (lane2 < MLSTM_DH) == (row2 < MLSTM_DH)
    bd_ones = jnp.where(same_head, 1.0, 0.0).astype(BF16)
    lane_row = lax.broadcasted_iota(jnp.int32, (1, LANES), 1)
    pair_half = lax.broadcasted_iota(jnp.int32, (2 * CHUNK, LANES), 0) // CHUNK

    def pair_blockdiag(x):
        zero = jnp.zeros_like(x)
        return jnp.concatenate([jnp.where(low, x, zero), jnp.where(low, zero, x)], axis=0)

    def ph_cumsum():
        st['g'] = g_ref[...] + gb_ref[...]
        st['b_f'] = _dot_exact_lhs(tri_ref[...], _log_sigmoid(st['g']))

    def ph_gates():
        b_i = pltpu.roll(st['b_f'], LANES - MLSTM_HEADS, 1)
        gr = st['g'] - b_i
        tok = lax.broadcasted_iota(jnp.int32, (TB, LANES), 0) % CHUNK
        aloc = gr
        for sh in (1, 2, 4, 8, 16, 32):
            if fwd:
                shifted, ok = pltpu.roll(aloc, sh, 0), tok >= sh
            else:
                shifted, ok = pltpu.roll(aloc, TB - sh, 0), tok < CHUNK - sh
            aloc = jnp.maximum(aloc, jnp.where(ok, shifted, neg_inf))
        chunk_max = jnp.concatenate(
            [jnp.broadcast_to(aloc[last_of(c):last_of(c) + 1], (CHUNK, LANES)) for c in range(N_CHUNKS)], axis=0)
        st['b_i'], st['aloc'] = b_i, aloc
        st['wloc'] = jnp.exp(gr - chunk_max)
        st['gr_t'] = gr.T

    def ph_expand():
        expand = exp_ref[...]
        st['aloc_x'] = _dot_exact_rhs(st['aloc'], expand)
        st['b_x'] = _dot_exact_rhs(st['b_i'], expand)
        st['wloc_x'] = _dot_exact_rhs(st['wloc'], expand)

    def ph_scores():
        for p, c in inst:
            kp = k_ref[ts_of(c), lp_of(p)] * MLSTM_DH ** -0.5
            st['s', p, c] = lax.dot_general(q_ref[ts_of(c), lp_of(p)], pair_blockdiag(kp), dn_nt,
                                            preferred_element_type=F32)
        for p in pairs:
            for c2 in range(N_CHUNKS // 2):
                st['kt2', p, c2] = (k_ref[ts2_of(2 * c2), lp_of(p)].astype(F32) * MLSTM_DH ** -0.5).T.astype(BF16)

    def ph_weights():
        for p, c in inst:
            ji0 = direction * 2 * MLSTM_HEADS + 2 * p
            ra = st['gr_t'][ji0:ji0 + 1, ts2_of(c)]
            rb = st['gr_t'][ji0 + 1:ji0 + 2, ts2_of(c)]
            if c % 2 == 0:
                g_row = jnp.where(lane_row < MLSTM_DH, ra, pltpu.roll(rb, MLSTM_DH, 1))
            else:
                g_row = jnp.where(lane_row < MLSTM_DH, pltpu.roll(ra, MLSTM_DH, 1), rb)
            al = st['aloc_x'][ts_of(c), lp_of(p)]
            s = st['s', p, c] * jnp.exp(jnp.where(tri_ok, g_row - al, neg_inf))
            s_hi = s.astype(BF16)
            st['s_hi', p, c] = s_hi
            st['s_lo', p, c] = (s - s_hi.astype(F32)).astype(BF16)
            wx2 = st['wloc_x'][ts2_of(c), lp_of(p)]
            in_chunk = pair_half == (c % 2)
            wv = jnp.where(in_chunk, wx2 * v_ref[ts2_of(c), lp_of(p)].astype(F32), 0.0)
            ww = jnp.where(in_chunk, wx2, 0.0)
            st['wvw', p, c] = jnp.concatenate([wv, ww], axis=1).astype(BF16)

    def ph_intra():
        for p, c in inst:
            vp = v_ref[ts_of(c), lp_of(p)]
            st['r1', p, c] = jnp.dot(st['s_hi', p, c], jnp.concatenate([pair_blockdiag(vp), bd_ones], axis=1),
                                     preferred_element_type=F32)
            st['den_lo', p, c] = jnp.dot(st['s_lo', p, c], bd_ones, preferred_element_type=F32)
            st['upd', p, c] = jnp.dot(st['kt2', p, c // 2], st['wvw', p, c], preferred_element_type=F32)

    def ph_state():
        for p in pairs:
            bdc, bdn = c_scr[p], n_scr[p]
            m_row = m_scr[p][0:1]
            for c in order:
                st['cn', p, c] = jnp.concatenate([bdc, bdn], axis=1).astype(BF16)
                st['m', p, c] = m_row
                last = last_of(c)
                aloc_last = st['aloc_x'][last:last + 1, lp_of(p)]
                a_last = jnp.maximum(m_row, aloc_last)
                decay = jnp.exp(m_row - a_last)
                scale = jnp.exp(aloc_last - a_last)
                upd = st['upd', p, c]
                bdc = decay * bdc + scale * jnp.where(same_head, upd[:, :LANES], 0.0)
                bdn = decay * bdn + scale * jnp.where(same_head, upd[:, LANES:], 0.0)
                m_row = st['b_x'][last:last + 1, lp_of(p)] + a_last
            c_scr[p] = bdc
            n_scr[p] = bdn
            m_scr[p] = jnp.broadcast_to(m_row, (SUBLANES, LANES))

    def ph_inter():
        for p, c in inst:
            st['r2', p, c] = jnp.dot(q_ref[ts_of(c), lp_of(p)], st['cn', p, c],
                                     preferred_element_type=F32)

    def ph_out():
        for p, c in inst:
            al = st['aloc_x'][ts_of(c), lp_of(p)]
            m_row, r1, r2 = st['m', p, c], st['r1', p, c], st['r2', p, c]
            a = jnp.maximum(m_row, al)
            corr = jnp.exp(al - a)
            w_inter = jnp.exp(m_row - a)
            num = corr * r1[:, :LANES] + w_inter * r2[:, :LANES]
            den = corr * (r1[:, LANES:] + st['den_lo', p, c]) + w_inter * r2[:, LANES:]
            o_ref[ts_of(c), lp_of(p)] = num / jnp.maximum(jnp.abs(den),
                                                          jnp.exp(-(st['b_x'][ts_of(c), lp_of(p)] + a)))

    return [ph_cumsum, ph_gates, ph_expand, ph_scores, ph_weights, ph_intra, ph_state, ph_inter, ph_out]


def _gla_phases(q_ref, k_ref, v_ref, g_ref, gup_ref, gb_ref, tri_ref, o_ref, s_scr, direction):
    fwd = direction == 0
    order = list(range(N_CHUNKS)) if fwd else list(range(N_CHUNKS - 1, -1, -1))
    ts_of = lambda c: slice(c * CHUNK, (c + 1) * CHUNK)
    ts2_of = lambda c: slice((c // 2) * 2 * CHUNK, (c // 2 + 1) * 2 * CHUNK)
    last_of = lambda c: c * CHUNK + (CHUNK - 1 if fwd else 0)
    dn_nt = (((1,), (1,)), ((), ()))
    st = {}

    khead = lax.broadcasted_iota(jnp.int32, (CHUNK, GLA_KEY_WIDTH), 1) // GLA_DK
    vlane = lax.broadcasted_iota(jnp.int32, (CHUNK, GLA_WIDTH), 1)
    vhead = vlane // GLA_DV
    trow = lax.broadcasted_iota(jnp.int32, (CHUNK, GLA_WIDTH), 0)
    tri_ok = (vlane % CHUNK <= trow) if fwd else (vlane % CHUNK >= trow)
    st_head_r = lax.broadcasted_iota(jnp.int32, (GLA_WIDTH, GLA_KEY_WIDTH), 0) // GLA_DV
    st_head_c = lax.broadcasted_iota(jnp.int32, (GLA_WIDTH, GLA_KEY_WIDTH), 1) // GLA_DK
    same_head = st_head_r == st_head_c
    pair_row = lax.broadcasted_iota(jnp.int32, (2 * CHUNK, GLA_KEY_WIDTH), 0) // CHUNK

    def stack_heads(x, head_of_lane):
        zero = jnp.zeros_like(x)
        return jnp.concatenate([jnp.where(head_of_lane == h, x, zero) for h in range(GLA_HEADS)], axis=0)

    def ph_gate_proj():
        st['z'] = _dot_bf16x3(g_ref[...], gup_ref[...]) + gb_ref[...]

    def ph_log_decay():
        st['lg'] = _log_sigmoid(st['z']) * (1.0 / GLA_TAU)

    def ph_cumsum():
        st['b'] = _dot_exact_lhs(tri_ref[...], st['lg'])

    def ph_prep():
        b = st['b']
        st['k'] = k_ref[...].astype(F32)
        st['qt'] = (q_ref[...].astype(F32) * GLA_DK ** -0.5 * jnp.exp(b)).astype(BF16)
        st['kt'] = (st['k'] * jnp.exp(-b)).astype(BF16)
        st['v_t'] = v_ref[...].astype(F32).T.astype(BF16)

    def ph_scores():
        for c in order:
            st['a', c] = lax.dot_general(st['qt'][ts_of(c)], stack_heads(st['kt'][ts_of(c)], khead), dn_nt,
                                         preferred_element_type=F32)

    def ph_mask():
        b, k = st['b'], st['k']
        for c in order:
            st['a', c] = jnp.where(tri_ok, st['a', c], 0.0).astype(BF16)
            last = last_of(c)
            st['kdec', c] = jnp.where(pair_row == c % 2, k[ts2_of(c)] * jnp.exp(b[last:last + 1] - b[ts2_of(c)]),
                                      0.0).astype(BF16)

    def ph_intra():
        for c in order:
            st['o', c] = jnp.dot(st['a', c], stack_heads(v_ref[ts_of(c), :], vhead), preferred_element_type=F32)
            st['upd', c] = jnp.dot(st['v_t'][:, ts2_of(c)], st['kdec', c], preferred_element_type=F32)

    def ph_state():
        s_t = s_scr[...]
        for c in order:
            st['s', c] = s_t.astype(BF16)
            last = last_of(c)
            s_t = jnp.exp(st['b'][last:last + 1]) * s_t + jnp.where(same_head, st['upd', c], 0.0)
        s_scr[...] = s_t

    def ph_inter():
        for c in order:
            st['oi', c] = lax.dot_general(st['qt'][ts_of(c)], st['s', c], dn_nt, preferred_element_type=F32)

    def ph_out():
        for c in order:
            o_ref[ts_of(c), :] = st['o', c] + st['oi', c]

    return [ph_gate_proj, ph_log_decay, ph_cumsum, ph_prep, ph_scores, ph_mask, ph_intra, ph_state, ph_inter, ph_out]


def _scans_call(main, gates, gate_bias, tri, expand, gup_pad, gla_b):
    t = main.shape[0]
    nb = t // TB
    const = lambda i: (0, 0)

    def streams(direction):
        blk = lambda i: _scan_block(i, nb, direction)
        col = lambda c, w: (lambda i: (blk(i), c // w))
        return [pl.BlockSpec((TB, MLSTM_WIDTH), col(C_MQ, MLSTM_WIDTH)),
                pl.BlockSpec((TB, MLSTM_WIDTH), col(C_MK, MLSTM_WIDTH)),
                pl.BlockSpec((TB, MLSTM_WIDTH), col(C_MV, MLSTM_WIDTH)),
                pl.BlockSpec((TB, GLA_KEY_WIDTH), col(C_GQ, GLA_KEY_WIDTH)),
                pl.BlockSpec((TB, GLA_KEY_WIDTH), col(C_GK, GLA_KEY_WIDTH)),
                pl.BlockSpec((TB, GLA_WIDTH), col(C_GV, GLA_WIDTH)),
                pl.BlockSpec((TB, N_GATE), lambda i: (blk(i), 0))]

    consts = [pl.BlockSpec((1, N_GATE), const),
              pl.BlockSpec((TB, TB), const), pl.BlockSpec((TB, TB), const),
              pl.BlockSpec((N_GATE, MLSTM_WIDTH), const), pl.BlockSpec((N_GATE, MLSTM_WIDTH), const),
              pl.BlockSpec((N_GATE, GLA_KEY_WIDTH), const), pl.BlockSpec((N_GATE, GLA_KEY_WIDTH), const),
              pl.BlockSpec((1, GLA_KEY_WIDTH), const), pl.BlockSpec((1, GLA_KEY_WIDTH), const)]
    out_spec = lambda direction, w: pl.BlockSpec((TB, w), lambda i: (_scan_block(i, nb, direction), 0))
    n_pairs = MLSTM_HEADS // 2
    mlstm_state = [pltpu.VMEM((n_pairs, LANES, LANES), F32), pltpu.VMEM((n_pairs, LANES, LANES), F32),
                   pltpu.VMEM((n_pairs, SUBLANES, LANES), F32)]
    gla_state = [pltpu.VMEM((GLA_WIDTH, GLA_KEY_WIDTH), F32)]
    return pl.pallas_call(
        _scans_kernel,
        grid=(nb,),
        in_specs=streams(0) + streams(1) + consts,
        out_specs=[out_spec(0, MLSTM_WIDTH), out_spec(1, MLSTM_WIDTH), out_spec(0, GLA_WIDTH), out_spec(1, GLA_WIDTH)],
        out_shape=[jax.ShapeDtypeStruct((t, MLSTM_WIDTH), F32)] * 2 + [jax.ShapeDtypeStruct((t, GLA_WIDTH), F32)] * 2,
        scratch_shapes=mlstm_state + mlstm_state + gla_state + gla_state,
        compiler_params=_cparams(1),
        name="scans",
    )(*([main] * 6 + [gates]) * 2, gate_bias, tri[0], tri[1], expand[0], expand[1],
      gup_pad[0], gup_pad[1], gla_b[0:1], gla_b[1:2])


def _attn_kernel(sink_ref, q_ref, kp_ref, kc_ref, kn_ref, kx_ref, vp_ref, vc_ref, vn_ref, vx_ref, o_ref, *, seq):
    i = pl.program_id(0)
    half = TB // 2
    n_loc = 2 * TB
    keys = jnp.concatenate([kp_ref[...], kc_ref[...], kn_ref[...], kx_ref[...]], axis=0)
    vals = jnp.concatenate([vp_ref[...], vc_ref[...], vn_ref[...], vx_ref[...]], axis=0)
    n_keys = keys.shape[0]
    lane = lax.broadcasted_iota(jnp.int32, (n_keys, LANES), 1)
    zero = jnp.zeros_like(keys)
    keys_g = [jnp.where(lane < HEAD_DIM, keys, zero), jnp.where(lane >= HEAD_DIM, keys, zero)]
    r = lax.broadcasted_iota(jnp.int32, (TB, n_keys), 0)
    c = lax.broadcasted_iota(jnp.int32, (TB, n_keys), 1)
    rel = c - half - r
    kpos = (i - 1) * TB + c - half
    local_ok = (jnp.abs(rel) <= WINDOW) & (kpos >= 0) & (kpos < seq) & (i > 0)
    valid = local_ok | (c >= n_loc)
    out_lane = lax.broadcasted_iota(jnp.int32, (TB, LANES), 1)
    neg_inf = jnp.float32(-jnp.inf)
    dn = (((1,), (1,)), ((), ()))
    heads = [(m, gidx) for m in range(ATTN_Q_WIDTH // LANES) for gidx in range(ATTN_KV_HEADS)]
    scores = [lax.dot_general(q_ref[:, m * LANES:(m + 1) * LANES], keys_g[gidx], dn, preferred_element_type=F32)
              for m, gidx in heads]
    probs, denoms = [], []
    for (m, gidx), s in zip(heads, scores):
        sink = sink_ref[gidx * (ATTN_HEADS // ATTN_KV_HEADS) + m]
        s = jnp.where(valid, s, neg_inf)
        mx = jnp.maximum(jnp.max(s, axis=1, keepdims=True), sink)
        p = jnp.exp(s - mx)
        denoms.append(jnp.sum(p, axis=1, keepdims=True) + jnp.exp(sink - mx))
        probs.append(p.astype(BF16))
    pvs = [jnp.dot(p, vals, preferred_element_type=F32) for p in probs]
    for m in range(ATTN_Q_WIDTH // LANES):
        r0 = pvs[2 * m] / denoms[2 * m]
        r1 = pvs[2 * m + 1] / denoms[2 * m + 1]
        o_ref[:, m * LANES:(m + 1) * LANES] = jnp.where(out_lane < HEAD_DIM, r0, r1).astype(BF16)


def _attn_call(main, sink, seq):
    t = main.shape[0]
    nb = t // TB
    half = TB // 2
    nhb = t // half
    kcol, vcol = C_AK // LANES, C_AV // LANES
    prev_i = lambda i: jnp.maximum(2 * i - 1, 0)
    next_i = lambda i: jnp.minimum(2 * i + 2, nhb - 1)
    grid_spec = pltpu.PrefetchScalarGridSpec(
        num_scalar_prefetch=1,
        grid=(nb,),
        in_specs=[pl.BlockSpec((TB, ATTN_Q_WIDTH), lambda i, s: (i, C_AQ // ATTN_Q_WIDTH)),
                  pl.BlockSpec((half, LANES), lambda i, s: (prev_i(i), kcol)),
                  pl.BlockSpec((TB, LANES), lambda i, s: (i, kcol)),
                  pl.BlockSpec((half, LANES), lambda i, s: (next_i(i), kcol)),
                  pl.BlockSpec((TB, LANES), lambda i, s: (0, kcol)),
                  pl.BlockSpec((half, LANES), lambda i, s: (prev_i(i), vcol)),
                  pl.BlockSpec((TB, LANES), lambda i, s: (i, vcol)),
                  pl.BlockSpec((half, LANES), lambda i, s: (next_i(i), vcol)),
                  pl.BlockSpec((TB, LANES), lambda i, s: (0, vcol))],
        out_specs=pl.BlockSpec((TB, ATTN_Q_WIDTH), lambda i, s: (i, 0)),
    )
    return pl.pallas_call(
        functools.partial(_attn_kernel, seq=seq),
        grid_spec=grid_spec,
        out_shape=jax.ShapeDtypeStruct((t, ATTN_Q_WIDTH), BF16),
        compiler_params=_cparams(1),
        name="window_attn",
    )(sink, main, main, main, main, main, main, main, main, main)


def _group_mean(x, avg_bf16):
    hi = x.astype(BF16)
    lo = (x - hi.astype(F32)).astype(BF16)
    return (jnp.dot(hi, avg_bf16, preferred_element_type=F32) + jnp.dot(lo, avg_bf16, preferred_element_type=F32))


def _head_norm(x, avg_bf16, w):
    cen = x - _group_mean(x, avg_bf16)
    var = _group_mean(cen * cen, avg_bf16)
    return cen * lax.rsqrt(var + LN_EPS) * w


def _outproj_kernel(mf_ref, mb_ref, mo_ref, at_ref, gf_ref, gbk_ref, go_ref, mnw_ref, gnw_ref, avg_ref,
                    w_ref, x_ref, mod_ref, lnw_ref, lnb_ref, x1_ref, h2_ref, wbf_scr):
    i = pl.program_id(0)
    tm = x_ref.shape[0]

    @pl.when(i == 0)
    def _():
        for dst, src, n in _column_runs(_out_row_perm()):
            wbf_scr[dst:dst + n, :] = w_ref[src:src + n, :].astype(BF16)

    avg = avg_ref[...]
    hm = _head_norm(mf_ref[...] + mb_ref[...], avg, mnw_ref[...]) * _sigmoid(mo_ref[...].astype(F32))
    gate = go_ref[...].astype(F32)
    hg = _head_norm(gf_ref[...] + gbk_ref[...], avg, gnw_ref[...]) * (gate * _sigmoid(gate))
    a0, a1, a2 = MLSTM_WIDTH, MLSTM_WIDTH + ATTN_Q_WIDTH, D_MODEL
    mix = (jnp.dot(hm.astype(BF16), wbf_scr[:a0], preferred_element_type=F32)
           + jnp.dot(at_ref[...], wbf_scr[a0:a1], preferred_element_type=F32)
           + jnp.dot(hg.astype(BF16), wbf_scr[a1:a2], preferred_element_type=F32))
    x1 = _layer_norm(DEEPNORM_ALPHA * x_ref[...] + _select_mod(mod_ref, 2, i, tm) * mix, lnw_ref[...], lnb_ref[...])
    x1_ref[...] = x1
    h2_ref[...] = (x1 * (1.0 + _select_mod(mod_ref, 4, i, tm)) + _select_mod(mod_ref, 3, i, tm)).astype(h2_ref.dtype)


def _outproj_call(mf, mb, main, attn, gf, gbk, mnw, gnw, avg, w_out, layer, x, mod, lnw, lnb, h2_dtype):
    t, d = x.shape
    tm = PROJ_TM
    row = lambda i: (i, 0)
    const = lambda i: (0, 0)
    return pl.pallas_call(
        _outproj_kernel,
        grid=(t // tm,),
        in_specs=[pl.BlockSpec((tm, MLSTM_WIDTH), row),
                  pl.BlockSpec((tm, MLSTM_WIDTH), row),
                  pl.BlockSpec((tm, MLSTM_WIDTH), lambda i: (i, C_MO // MLSTM_WIDTH)),
                  pl.BlockSpec((tm, ATTN_Q_WIDTH), row),
                  pl.BlockSpec((tm, GLA_WIDTH), row),
                  pl.BlockSpec((tm, GLA_WIDTH), row),
                  pl.BlockSpec((tm, GLA_WIDTH), lambda i: (i, C_GO // GLA_WIDTH)),
                  pl.BlockSpec((1, MLSTM_WIDTH), const),
                  pl.BlockSpec((1, GLA_WIDTH), const),
                  pl.BlockSpec((MLSTM_WIDTH, MLSTM_WIDTH), const),
                  pl.BlockSpec((None, d, d), lambda i: (layer, 0, 0), pipeline_mode=pl.Buffered(1)),
                  pl.BlockSpec((tm, d), row),
                  pl.BlockSpec((2, SUBLANES, d), lambda i: (0, 0, 0)),
                  pl.BlockSpec((1, d), const),
                  pl.BlockSpec((1, d), const)],
        out_specs=[pl.BlockSpec((tm, d), row), pl.BlockSpec((tm, d), row)],
        out_shape=[jax.ShapeDtypeStruct((t, d), F32), jax.ShapeDtypeStruct((t, d), h2_dtype)],
        scratch_shapes=[pltpu.VMEM((d, d), BF16)],
        compiler_params=_cparams(1),
        name="out_proj",
    )(mf, mb, main, attn, gf, gbk, main, mnw, gnw, avg, w_out, x, mod, lnw, lnb)


def _select_mod(mod_ref, k, i, tm):
    rows = i * tm + lax.broadcasted_iota(jnp.int32, (tm, 1), 0)
    return jnp.where(rows < CTX_LEN, mod_ref[0, k:k + 1, :], mod_ref[1, k:k + 1, :])


def _ffn_kernel(h_ref, wgu_ref, wd_ref, x_ref, mod_ref, lnw_ref, lnb_ref, o_ref):
    h = h_ref[...]
    g = jnp.dot(h, wgu_ref[:, :D_FF], preferred_element_type=F32)
    u = jnp.dot(h, wgu_ref[:, D_FF:], preferred_element_type=F32)
    y = jnp.dot((g * _sigmoid(g) * u).astype(BF16), wd_ref[...], preferred_element_type=F32)
    m5 = _select_mod(mod_ref, 5, pl.program_id(0), h_ref.shape[0])
    o_ref[...] = _layer_norm(DEEPNORM_ALPHA * x_ref[...] + m5 * y, lnw_ref[...], lnb_ref[...])


def _ffn_call(h2, w_gu, w_down, layer, x1, mod, lnw, lnb):
    t, d = x1.shape
    resident = pl.Buffered(1)
    return pl.pallas_call(
        _ffn_kernel,
        grid=(t // FFN_TM,),
        in_specs=[pl.BlockSpec((FFN_TM, d), lambda i: (i, 0)),
                  pl.BlockSpec((None, d, 2 * D_FF), lambda i: (layer, 0, 0), pipeline_mode=resident),
                  pl.BlockSpec((None, D_FF, d), lambda i: (layer, 0, 0), pipeline_mode=resident),
                  pl.BlockSpec((FFN_TM, d), lambda i: (i, 0)),
                  pl.BlockSpec((2, SUBLANES, d), lambda i: (0, 0, 0)),
                  pl.BlockSpec((1, d), lambda i: (0, 0)),
                  pl.BlockSpec((1, d), lambda i: (0, 0))],
        out_specs=pl.BlockSpec((FFN_TM, d), lambda i: (i, 0)),
        out_shape=jax.ShapeDtypeStruct((t, d), F32),
        compiler_params=_cparams(1),
        name="dense_ffn",
    )(h2, w_gu, w_down, x1, mod, lnw, lnb)


def _router_kernel(h_ref, wr_ref, br_ref, su_ref, route_ref, cnt_ref, base_scr):
    @pl.when(pl.program_id(0) == 0)
    def _():
        base_scr[...] = jnp.zeros_like(base_scr)

    lt = lax.dot_general(wr_ref[...], h_ref[...], (((1,), (1,)), ((), ())),
                         preferred_element_type=F32, precision=HIGHEST) + br_ref[...]
    idx = lax.broadcasted_iota(jnp.int32, lt.shape, 0)
    neg_inf = jnp.float32(-jnp.inf)
    m1 = jnp.max(lt, axis=0, keepdims=True)
    e1 = jnp.min(jnp.where(lt == m1, idx, N_EXPERTS), axis=0, keepdims=True)
    lt2 = jnp.where(idx == e1, neg_inf, lt)
    m2 = jnp.max(lt2, axis=0, keepdims=True)
    e2 = jnp.min(jnp.where(lt2 == m2, idx, N_EXPERTS), axis=0, keepdims=True)
    t2 = jnp.exp(m2 - m1)
    w1 = 1.0 / (1.0 + t2)
    w2 = t2 / (1.0 + t2)
    su = su_ref[...]
    base = base_scr[...][:, 0:1]
    oh1 = (idx == e1).astype(F32)
    cum1 = jnp.dot(oh1.astype(BF16), su, preferred_element_type=F32)
    rank1 = jnp.sum(oh1 * (base + cum1), axis=0, keepdims=True)
    base = base + jnp.sum(oh1, axis=1, keepdims=True)
    oh2 = (idx == e2).astype(F32)
    cum2 = jnp.dot(oh2.astype(BF16), su, preferred_element_type=F32)
    rank2 = jnp.sum(oh2 * (base + cum2), axis=0, keepdims=True)
    base = base + jnp.sum(oh2, axis=1, keepdims=True)
    base_scr[...] = jnp.broadcast_to(base, base_scr.shape)
    cnt_ref[...] = jnp.broadcast_to(base, cnt_ref.shape)
    zero = jnp.zeros_like(w1)
    route_ref[...] = jnp.concatenate(
        [e1.astype(F32), e2.astype(F32), w1, w2, rank1, rank2, zero, zero], axis=0)


def _router_call(h2, wr_t, br, su):
    t, d = h2.shape
    nb = t // TB
    return pl.pallas_call(
        _router_kernel,
        grid=(nb,),
        in_specs=[pl.BlockSpec((TB, d), lambda i: (i, 0)),
                  pl.BlockSpec((N_EXPERTS, d), lambda i: (0, 0)),
                  pl.BlockSpec((N_EXPERTS, 1), lambda i: (0, 0)),
                  pl.BlockSpec((TB, TB), lambda i: (0, 0))],
        out_specs=[pl.BlockSpec((SUBLANES, TB), lambda i: (0, i)),
                   pl.BlockSpec((N_EXPERTS, LANES), lambda i: (0, 0))],
        out_shape=[jax.ShapeDtypeStruct((SUBLANES, t), F32),
                   jax.ShapeDtypeStruct((N_EXPERTS, LANES), F32)],
        scratch_shapes=[pltpu.VMEM((N_EXPERTS, LANES), F32)],
        compiler_params=_cparams(1),
        name="moe_router",
    )(h2, wr_t, br, su)


ROW_TILE = D_MODEL // LANES
DMA_UNROLL = 8
ZERO_CHUNK = 64


def _store_row_tiles(ref, val):
    n = val.shape[0]
    for j in range(ROW_TILE):
        ref[pl.ds(j, n, stride=ROW_TILE), :] = val[:, j * LANES:(j + 1) * LANES]


def _load_row_tiles(ref, n, dtype=F32):
    return jnp.concatenate([ref[pl.ds(j, n, stride=ROW_TILE), :].astype(dtype) for j in range(ROW_TILE)], axis=1)


def _row(ref, idx):
    return ref.at[pl.ds(pl.multiple_of(idx * ROW_TILE, ROW_TILE), ROW_TILE)]


def _dispatch_kernel(s1_ref, s2_ref, pad_lo_ref, pad_hi_ref, h_ref, xs_ref, hr_scr, zero_scr, sem, zsem):
    i = pl.program_id(0)

    @pl.when(i == 0)
    def _():
        zero_scr[...] = jnp.zeros_like(zero_scr)

        def zero_rows(s, n):
            dst = xs_ref.at[pl.ds(pl.multiple_of(s * ROW_TILE, ROW_TILE), n * ROW_TILE)]
            return pltpu.make_async_copy(zero_scr.at[pl.ds(0, n * ROW_TILE)], dst, zsem)

        def fill(e, act):
            lo, hi = pad_lo_ref[e], pad_hi_ref[e]
            n_big = (hi - lo) // ZERO_CHUNK
            lax.fori_loop(0, n_big, lambda k, c: (act(zero_rows(lo + k * ZERO_CHUNK, ZERO_CHUNK)), c)[1], 0)
            lax.fori_loop(lo + n_big * ZERO_CHUNK, hi, lambda s, c: (act(zero_rows(s, 1)), c)[1], 0)

        for e in range(N_EXPERTS + 1):
            fill(e, lambda cp: cp.start())
        for e in range(N_EXPERTS + 1):
            fill(e, lambda cp: cp.wait())

    n = pl.num_programs(0)
    cur = i % 2

    def copy(blk, r, slot_ref):
        buf = blk % 2
        return pltpu.make_async_copy(_row(hr_scr.at[buf], r), _row(xs_ref, slot_ref[blk * TB + r]), sem.at[buf])

    def start(r, carry):
        copy(i, r, s1_ref).start()
        copy(i, r, s2_ref).start()
        return carry

    def wait_block(blk):
        def wait(r, carry):
            copy(blk, r, s1_ref).wait()
            copy(blk, r, s2_ref).wait()
            return carry
        lax.fori_loop(0, TB, wait, 0, unroll=DMA_UNROLL)

    _store_row_tiles(hr_scr.at[cur], h_ref[...])
    lax.fori_loop(0, TB, start, 0, unroll=DMA_UNROLL)

    @pl.when(i > 0)
    def _():
        wait_block(i - 1)

    @pl.when(i == n - 1)
    def _():
        wait_block(i)


def _dispatch_call(slot1, slot2, pad_lo, pad_hi, h2, n_slots):
    t, d = h2.shape
    grid_spec = pltpu.PrefetchScalarGridSpec(
        num_scalar_prefetch=4,
        grid=(t // TB,),
        in_specs=[pl.BlockSpec((TB, d), lambda i, *_: (i, 0))],
        out_specs=pl.BlockSpec(memory_space=pl.ANY),
        scratch_shapes=[pltpu.VMEM((2, TB * ROW_TILE, LANES), F32), pltpu.VMEM((ZERO_CHUNK * ROW_TILE, LANES), F32),
                        pltpu.SemaphoreType.DMA((2,)), pltpu.SemaphoreType.DMA(())],
    )
    return pl.pallas_call(
        _dispatch_kernel,
        grid_spec=grid_spec,
        out_shape=jax.ShapeDtypeStruct((n_slots * ROW_TILE, LANES), F32),
        compiler_params=_cparams(1),
        name="moe_dispatch",
    )(slot1, slot2, pad_lo, pad_hi, h2)


def _expert_kernel(be_ref, used_ref, x_ref, wg_ref, wu_ref, wd_ref, y_ref, xb_scr, acc_scr, *, n_f):
    b, f = pl.program_id(0), pl.program_id(1)
    live = b < used_ref[0]
    last = n_f - 1

    def swiglu_part():
        x = xb_scr[...]
        g = jnp.dot(x, wg_ref[...], preferred_element_type=F32)
        u = jnp.dot(x, wu_ref[...], preferred_element_type=F32)
        return jnp.dot((g * _sigmoid(g) * u).astype(BF16), wd_ref[...], preferred_element_type=F32)

    @pl.when(live & (f == 0))
    def _():
        for j in range(ROW_TILE):
            xb_scr[:, j * LANES:(j + 1) * LANES] = x_ref[pl.ds(j, MOE_G, stride=ROW_TILE), :].astype(BF16)
        acc_scr[...] = swiglu_part()

    if n_f > 2:
        @pl.when(live & (f > 0) & (f < last))
        def _():
            acc_scr[...] += swiglu_part()

    @pl.when(live & (f == last))
    def _():
        _store_row_tiles(y_ref, acc_scr[...] + swiglu_part())

    @pl.when(jnp.logical_not(live) & (f == last))
    def _():
        y_ref[...] = jnp.zeros_like(y_ref)


def _expert_call(block_e, used, xs, w_gu, w_down, layer):
    d = D_MODEL
    ns = xs.shape[0] // ROW_TILE
    nblk = ns // MOE_G
    nf = D_EXPERT // MOE_TF
    rows = MOE_G * ROW_TILE

    def f_eff(b, f, used_ref):
        return jnp.where(b < used_ref[0], f, nf - 1)

    grid_spec = pltpu.PrefetchScalarGridSpec(
        num_scalar_prefetch=2,
        grid=(nblk, nf),
        in_specs=[pl.BlockSpec((rows, LANES), lambda b, f, be, us: (b, 0)),
                  pl.BlockSpec((None, None, d, MOE_TF), lambda b, f, be, us: (layer, be[b], 0, f_eff(b, f, us))),
                  pl.BlockSpec((None, None, d, MOE_TF),
                               lambda b, f, be, us: (layer, be[b], 0, nf + f_eff(b, f, us))),
                  pl.BlockSpec((None, None, MOE_TF, d), lambda b, f, be, us: (layer, be[b], f_eff(b, f, us), 0))],
        out_specs=pl.BlockSpec((rows, LANES), lambda b, f, be, us: (b, 0)),
        scratch_shapes=[pltpu.VMEM((MOE_G, d), BF16), pltpu.VMEM((MOE_G, d), F32)],
    )
    return pl.pallas_call(
        functools.partial(_expert_kernel, n_f=nf),
        grid_spec=grid_spec,
        out_shape=jax.ShapeDtypeStruct(xs.shape, F32),
        compiler_params=_cparams(2),
        name="moe_experts",
    )(block_e, used, xs, w_gu, w_gu, w_down)


def _combine_kernel(s1_ref, s2_ref, yb_ref, route_ref, x_ref, mod_ref, lnw_ref, lnb_ref, o_ref, buf1, buf2, sem):
    i = pl.program_id(0)
    n = pl.num_programs(0)
    cur = i % 2

    def copy(blk, r, slot_ref, buf):
        b = blk % 2
        return pltpu.make_async_copy(_row(yb_ref, slot_ref[blk * TB + r]), _row(buf.at[b], r), sem.at[b])

    def start_block(blk):
        def start(r, carry):
            copy(blk, r, s1_ref, buf1).start()
            copy(blk, r, s2_ref, buf2).start()
            return carry
        lax.fori_loop(0, TB, start, 0, unroll=DMA_UNROLL)

    def wait(r, carry):
        copy(i, r, s1_ref, buf1).wait()
        copy(i, r, s2_ref, buf2).wait()
        return carry

    @pl.when(i == 0)
    def _():
        start_block(i)

    @pl.when(i + 1 < n)
    def _():
        start_block(i + 1)

    lax.fori_loop(0, TB, wait, 0, unroll=DMA_UNROLL)
    rt = route_ref[...].T
    y = rt[:, 2:3] * _load_row_tiles(buf1.at[cur], TB) + rt[:, 3:4] * _load_row_tiles(buf2.at[cur], TB)
    m = mod_ref[...]
    o_ref[...] = _layer_norm(DEEPNORM_ALPHA * x_ref[...] + m[5:6] * y, lnw_ref[...], lnb_ref[...])


def _combine_call(slot1, slot2, yb, route, x1, mod, lnw, lnb, latent_only):
    t, d = x1.shape
    if latent_only:
        out_rows, out_map = t - CTX_LEN, lambda i, a, b: (jnp.maximum(i - 1, 0), 0)
    else:
        out_rows, out_map = t, lambda i, a, b: (i, 0)
    grid_spec = pltpu.PrefetchScalarGridSpec(
        num_scalar_prefetch=2,
        grid=(t // TB,),
        in_specs=[pl.BlockSpec(memory_space=pl.ANY),
                  pl.BlockSpec((SUBLANES, TB), lambda i, a, b: (0, i)),
                  pl.BlockSpec((TB, d), lambda i, a, b: (i, 0)),
                  pl.BlockSpec((None, SUBLANES, d), lambda i, a, b: (_who(i), 0, 0)),
                  pl.BlockSpec((1, d), lambda i, a, b: (0, 0)),
                  pl.BlockSpec((1, d), lambda i, a, b: (0, 0))],
        out_specs=pl.BlockSpec((TB, d), out_map),
        scratch_shapes=[pltpu.VMEM((2, TB * ROW_TILE, LANES), F32), pltpu.VMEM((2, TB * ROW_TILE, LANES), F32),
                        pltpu.SemaphoreType.DMA((2,))],
    )
    return pl.pallas_call(
        _combine_kernel,
        grid_spec=grid_spec,
        out_shape=jax.ShapeDtypeStruct((out_rows, d), F32),
        compiler_params=_cparams(1),
        name="moe_combine",
    )(slot1, slot2, yb, route, x1, mod, lnw, lnb)


def _moe_layer(h2, x1, mod, lnw, lnb, w_router, b_router, w_gu, w_down, layer, su, latent_only):
    t, d = h2.shape
    route, cnt = _router_call(h2, w_router.T, b_router.reshape(N_EXPERTS, 1), su)
    counts = cnt[:, 0].astype(jnp.int32)
    padded = (counts + MOE_G - 1) // MOE_G * MOE_G
    pend = jnp.cumsum(padded)
    pstart = pend - padded
    e1, e2 = route[0].astype(jnp.int32), route[1].astype(jnp.int32)
    slot1 = pstart[e1] + route[4].astype(jnp.int32)
    slot2 = pstart[e2] + route[5].astype(jnp.int32)
    nblk = -(-(2 * t) // MOE_G) + N_EXPERTS
    blk_start = jnp.arange(nblk, dtype=jnp.int32) * MOE_G
    block_e = jnp.minimum(jnp.sum((pend[None, :] <= blk_start[:, None]).astype(jnp.int32), axis=1), N_EXPERTS - 1)
    used = (pend[-1:] // MOE_G).astype(jnp.int32)
    n_slots = nblk * MOE_G
    pad_lo = jnp.concatenate([pstart + counts, pend[-1:]])
    pad_hi = jnp.concatenate([pend, jnp.full((1,), n_slots, jnp.int32)])
    xs = _dispatch_call(slot1, slot2, pad_lo, pad_hi, h2, n_slots)
    yb = _expert_call(block_e, used, xs, w_gu, w_down, layer)
    return _combine_call(slot1, slot2, yb, route, x1, mod, lnw, lnb, latent_only)


def _attn_head_perm():
    cols = []
    for m in range(ATTN_HEADS // ATTN_KV_HEADS):
        for hq in (m, m + ATTN_HEADS // ATTN_KV_HEADS):
            cols.extend(range(hq * HEAD_DIM, (hq + 1) * HEAD_DIM))
    return np.asarray(cols, np.int32)


def _in_col_perm():
    n_m = 4 * MLSTM_WIDTH
    mg = 2 * 2 * MLSTM_HEADS
    a0 = n_m + mg
    perm = list(range(n_m))
    perm += [a0 + int(j) for j in _attn_head_perm()]
    perm += list(range(a0 + ATTN_Q_WIDTH, a0 + ATTN_Q_WIDTH + 2 * ATTN_KV_WIDTH))
    g0 = a0 + ATTN_Q_WIDTH + 2 * ATTN_KV_WIDTH
    perm += list(range(g0, g0 + 2 * GLA_KEY_WIDTH + 2 * GLA_WIDTH))
    perm += list(range(n_m, n_m + mg))
    perm += list(range(g0 + 2 * GLA_KEY_WIDTH + 2 * GLA_WIDTH, g0 + 2 * GLA_KEY_WIDTH + 2 * GLA_WIDTH + 2 * GLA_RANK))
    return np.asarray(perm, np.int32)


def _out_row_perm():
    return np.concatenate([np.arange(MLSTM_WIDTH), MLSTM_WIDTH + _attn_head_perm(),
                           np.arange(MLSTM_WIDTH + ATTN_Q_WIDTH, D_MODEL)]).astype(np.int32)


def _column_runs(perm):
    runs, start = [], 0
    for j in range(1, len(perm) + 1):
        if j == len(perm) or perm[j] != perm[j - 1] + 1:
            runs.append((start, int(perm[start]), j - start))
            start = j
    return runs


def _rope_tables(seq):
    inv = ROPE_BASE ** (-jnp.arange(ROPE_PAIRS, dtype=F32) / ROPE_PAIRS)
    rows = seq // GRID_W
    ang_r = jnp.arange(rows).astype(F32)[:, None] * inv
    ang_c = jnp.arange(GRID_W).astype(F32)[:, None] * inv
    lane_pat = lambda a, b: jnp.tile(jnp.concatenate([a, b], -1), (1, LANES // (2 * ROPE_PAIRS)))
    cos_rows, sin_rows = lane_pat(jnp.cos(ang_r), jnp.cos(ang_r)), lane_pat(-jnp.sin(ang_r), jnp.sin(ang_r))
    cos_cols, sin_cols = lane_pat(jnp.cos(ang_c), jnp.cos(ang_c)), lane_pat(-jnp.sin(ang_c), jnp.sin(ang_c))
    row_part = (np.arange(LANES) % HEAD_DIM) < HEAD_DIM // 2
    per_token = lambda by_row, by_col: jnp.where(
        row_part, jnp.broadcast_to(by_row[:, None, :], (rows, GRID_W, LANES)),
        jnp.broadcast_to(by_col[None, :, :], (rows, GRID_W, LANES))).reshape(seq, LANES)
    cos_l, sin_l = per_token(cos_rows, cos_cols), per_token(sin_rows, sin_cols)
    cos_t = jnp.concatenate([jnp.ones((CTX_LEN, LANES), F32), cos_l], 0)
    sin_t = jnp.concatenate([jnp.zeros((CTX_LEN, LANES), F32), sin_l], 0)
    return cos_t, sin_t


def _block_tri(direction):
    r = np.arange(TB)[:, None]
    c = np.arange(TB)[None, :]
    same = (r // CHUNK) == (c // CHUNK)
    tri = (c <= r) if direction == 0 else (c >= r)
    return jnp.asarray(same & tri, BF16)


def kernel(x, c, ctx, c_ctx, w_ada, b_ada, w_in, mlstm_gate_b, mlstm_norm_w, attn_sink, gla_gate_up, gla_gate_b,
           gla_norm_w, w_out, ln_w, ln_b, ffn_w_gu, ffn_w_down, router_w, router_b, moe_w_gu, moe_w_down):
    seq, d = x.shape[1], x.shape[2]
    depth = w_in.shape[0]
    xt = jnp.concatenate([ctx[0], x[0]], axis=0)

    cvec = jnp.zeros((SUBLANES, d), F32).at[0].set(c_ctx).at[1].set(c[0])
    mods = _mod_call(cvec, w_ada, b_ada)[:, :2].reshape(depth, 2, 6, d)
    mods = jnp.pad(mods, ((0, 0), (0, 0), (0, SUBLANES - 6), (0, 0)))

    gate_bias = jnp.pad(mlstm_gate_b.reshape(depth, 1, -1), ((0, 0), (0, 0), (0, N_GATE - 4 * MLSTM_HEADS)))
    gup_pad = jnp.zeros((depth, 2, N_GATE, GLA_KEY_WIDTH), F32)
    for dr in range(2):
        lo = GLA_GATE_OFF + dr * GLA_RANK
        gup_pad = gup_pad.at[:, dr, lo:lo + GLA_RANK, :].set(gla_gate_up[:, dr])
    cos_t, sin_t = _rope_tables(seq)
    tri = [_block_tri(0), _block_tri(1)]
    su = jnp.asarray(np.arange(TB)[:, None] < np.arange(TB)[None, :], BF16)
    hh = np.arange(MLSTM_WIDTH) // MLSTM_DH
    avg = jnp.asarray((hh[:, None] == hh[None, :]) / MLSTM_DH, BF16)
    ffn_gu, ffn_dn = ffn_w_gu.astype(BF16), ffn_w_down.astype(BF16)
    moe_gu, moe_dn = moe_w_gu.astype(BF16), moe_w_down.astype(BF16)

    expand = []
    for dr in range(2):
        e = np.zeros((N_GATE, MLSTM_WIDTH), np.float32)
        for h in range(MLSTM_HEADS):
            e[dr * 2 * MLSTM_HEADS + h, h * MLSTM_DH:(h + 1) * MLSTM_DH] = 1.0
        expand.append(jnp.asarray(e, BF16))

    for l in range(depth):
        is_moe = l % 2 == 1
        last = l == depth - 1
        main, gates = _inproj_call(xt, mods[l], w_in, l, cos_t, sin_t)
        mf, mb, gf, gbk = _scans_call(main, gates, gate_bias[l], tri, expand, gup_pad[l], gla_gate_b[l])
        attn = _attn_call(main, attn_sink[l], seq)
        x1, h2 = _outproj_call(mf, mb, main, attn, gf, gbk, mlstm_norm_w[l:l + 1], gla_norm_w[l:l + 1], avg,
                               w_out, l, xt, mods[l], ln_w[l, 0:1], ln_b[l, 0:1], F32 if is_moe else BF16)
        if is_moe:
            xt = _moe_layer(h2, x1, mods[l], ln_w[l, 1:2], ln_b[l, 1:2], router_w[l // 2], router_b[l // 2],
                            moe_gu, moe_dn, l // 2, su, latent_only=last)
        else:
            xt = _ffn_call(h2, ffn_gu, ffn_dn, l // 2, x1, mods[l], ln_w[l, 1:2], ln_b[l, 1:2])
    return (xt if depth % 2 == 0 else xt[CTX_LEN:])[None]
```

```python
import functools

import jax
import jax.numpy as jnp
import numpy as np
from jax import lax
from jax.experimental import pallas as pl
from jax.experimental.pallas import tpu as pltpu

F32 = jnp.float32
BF16 = jnp.bfloat16
HIGHEST = lax.Precision.HIGHEST

D_MODEL = 1024
SEQ = 16384
DEPTH = 4
GRID_W = 64
CTX_LEN = 256
MLSTM_HEADS = 4
MLSTM_DH = 64
MLSTM_WIDTH = 256
HEAD_DIM = 64
ATTN_HEADS = 8
ATTN_KV_HEADS = 2
ATTN_Q_WIDTH = 512
ATTN_KV_WIDTH = 128
WINDOW = 128
ROPE_BASE = 10000.0
ROPE_PAIRS = 16
GLA_HEADS = 4
GLA_DK = 32
GLA_DV = 64
GLA_WIDTH = 256
GLA_KEY_WIDTH = 128
GLA_RANK = 16
GLA_TAU = 16.0
CHUNK = 64
D_FF = 2816
N_EXPERTS = 8
D_EXPERT = 3584
DEEPNORM_ALPHA = (2 * DEPTH) ** 0.25
LN_EPS = 1e-5

LANES = 128
SUBLANES = 8
VMEM_LIMIT = 56 * 1024 * 1024

TB = 256
N_CHUNKS = TB // CHUNK
PROJ_TM = 640
FFN_TM = 640
MOE_G = 512
MOE_TF = 1792

C_MQ, C_MK, C_MV, C_MO = 0, 256, 512, 768
C_AQ, C_AK, C_AV = 1024, 1536, 1664
C_GQ, C_GK, C_GV, C_GO = 1792, 1920, 2048, 2304
N_MAIN = 2560
N_GATE = 128
N_PROJ = N_MAIN + N_GATE
GLA_GATE_OFF = 16


def _cparams(n_axes=1):
    return pltpu.CompilerParams(dimension_semantics=("arbitrary",) * n_axes,
                                vmem_limit_bytes=VMEM_LIMIT)


def _sigmoid(x):
    return 1.0 / (1.0 + jnp.exp(-x))


def _log_sigmoid(x):
    return jnp.minimum(x, 0.0) - jnp.log(1.0 + jnp.exp(-jnp.abs(x)))


def _split3(x):
    hi = x.astype(BF16)
    r1 = x - hi.astype(F32)
    mid = r1.astype(BF16)
    lo = (r1 - mid.astype(F32)).astype(BF16)
    return hi, mid, lo


def _dot_exact_rhs(x, m_bf16):
    hi, mid, lo = _split3(x)
    d = lambda a: jnp.dot(a, m_bf16, preferred_element_type=F32)
    return d(hi) + d(mid) + d(lo)


def _dot_exact_lhs(m_bf16, x):
    hi, mid, lo = _split3(x)
    d = lambda a: jnp.dot(m_bf16, a, preferred_element_type=F32)
    return d(hi) + d(mid) + d(lo)


def _dot_bf16x3(a, b):
    a_hi, b_hi = a.astype(BF16), b.astype(BF16)
    a_lo = (a - a_hi.astype(F32)).astype(BF16)
    b_lo = (b - b_hi.astype(F32)).astype(BF16)
    d = lambda x, y: jnp.dot(x, y, preferred_element_type=F32)
    return d(a_hi, b_hi) + d(a_hi, b_lo) + d(a_lo, b_hi)


def _layer_norm(v, w, b):
    mu = jnp.mean(v, axis=-1, keepdims=True)
    cen = v - mu
    var = jnp.mean(cen * cen, axis=-1, keepdims=True)
    return cen * lax.rsqrt(var + LN_EPS) * w + b


def _mod_kernel(c_ref, w_ref, b_ref, o_ref):
    c = c_ref[...]
    sc = c * _sigmoid(c)
    o_ref[...] = jnp.dot(sc, w_ref[...], preferred_element_type=F32, precision=HIGHEST) + b_ref[...]


def _mod_call(cvec, w_ada, b_ada):
    depth, d, n = w_ada.shape
    tn = 1536
    return pl.pallas_call(
        _mod_kernel,
        grid=(depth, n // tn),
        in_specs=[pl.BlockSpec((SUBLANES, d), lambda l, j: (0, 0)),
                  pl.BlockSpec((None, d, tn), lambda l, j: (l, 0, j)),
                  pl.BlockSpec((None, 1, tn), lambda l, j: (l, 0, j))],
        out_specs=pl.BlockSpec((None, SUBLANES, tn), lambda l, j: (l, 0, j)),
        out_shape=jax.ShapeDtypeStruct((depth, SUBLANES, n), F32),
        compiler_params=_cparams(2),
        name="ada_mod",
    )(cvec, w_ada, b_ada.reshape(depth, 1, n))


def _who(i):
    return jnp.minimum(i, 1)


def _rope(x, cos, sin_signed, first_half):
    swapped = jnp.where(first_half, pltpu.roll(x, LANES - ROPE_PAIRS, 1), pltpu.roll(x, ROPE_PAIRS, 1))
    return x * cos + swapped * sin_signed


def _inproj_kernel(x_ref, mod_ref, w_ref, cos_ref, sin_ref, main_ref, gate_ref, wbf_scr):
    i = pl.program_id(0)
    tm = x_ref.shape[0]

    @pl.when(i == 0)
    def _():
        for dst, src, n in _column_runs(_in_col_perm()):
            wbf_scr[:, dst:dst + n] = w_ref[:, src:src + n].astype(BF16)
        wbf_scr[:, w_ref.shape[1]:] = jnp.zeros((w_ref.shape[0], N_PROJ - w_ref.shape[1]), BF16)

    h = x_ref[...] * (1.0 + _select_mod(mod_ref, 1, i, tm)) + _select_mod(mod_ref, 0, i, tm)
    p = jnp.dot(h.astype(BF16), wbf_scr[...], preferred_element_type=F32)
    cos, sin = cos_ref[...], sin_ref[...]
    lane = lax.broadcasted_iota(jnp.int32, (tm, LANES), 1)
    first_half = (lane % (2 * ROPE_PAIRS)) < ROPE_PAIRS
    main_ref[:, :C_AQ] = p[:, :C_AQ].astype(BF16)
    for j in range(ATTN_Q_WIDTH // LANES):
        lo = C_AQ + j * LANES
        main_ref[:, lo:lo + LANES] = (_rope(p[:, lo:lo + LANES], cos, sin, first_half) * HEAD_DIM ** -0.5).astype(BF16)
    main_ref[:, C_AK:C_AV] = _rope(p[:, C_AK:C_AV], cos, sin, first_half).astype(BF16)
    main_ref[:, C_AV:] = p[:, C_AV:N_MAIN].astype(BF16)
    gate_ref[...] = p[:, N_MAIN:]


def _inproj_call(x, mod, w_in, layer, cos_t, sin_t):
    t, d = x.shape
    n_in = w_in.shape[2]
    tm = PROJ_TM
    return pl.pallas_call(
        _inproj_kernel,
        grid=(t // tm,),
        in_specs=[pl.BlockSpec((tm, d), lambda i: (i, 0)),
                  pl.BlockSpec((2, SUBLANES, d), lambda i: (0, 0, 0)),
                  pl.BlockSpec((None, d, n_in), lambda i: (layer, 0, 0), pipeline_mode=pl.Buffered(1)),
                  pl.BlockSpec((tm, LANES), lambda i: (i, 0)),
                  pl.BlockSpec((tm, LANES), lambda i: (i, 0))],
        out_specs=[pl.BlockSpec((tm, N_MAIN), lambda i: (i, 0)),
                   pl.BlockSpec((tm, N_GATE), lambda i: (i, 0))],
        out_shape=[jax.ShapeDtypeStruct((t, N_MAIN), BF16),
                   jax.ShapeDtypeStruct((t, N_GATE), F32)],
        scratch_shapes=[pltpu.VMEM((d, N_PROJ), BF16)],
        compiler_params=_cparams(1),
        name="in_proj",
    )(x, mod, w_in, cos_t, sin_t)


def _scan_block(i, nb, direction):
    if direction == 0:
        return i
    return jnp.where(i == 0, 0, nb - i)


def _scans_kernel(*refs, cast_weights):
    (mq_f, mk_f, mv_f, gq_f, gk_f, gv_f, g_f, mq_b, mk_b, mv_b, gq_b, gk_b, gv_b, g_b,
     gbias, tri_f, tri_b, exp_f, exp_b, gup_f, gup_b, glb_f, glb_b) = refs[:23]
    refs = refs[23:]
    if cast_weights:
        w_src, refs = refs[0], refs[1:]
    om_f, om_b, og_f, og_b = refs[:4]
    refs = refs[4:]
    if cast_weights:
        w_dst, refs = refs[0], refs[1:]
    c_f, n_f, m_f, c_b, n_b, m_b, s_f, s_b = refs

    @pl.when(pl.program_id(0) == 0)
    def _():
        for scr in (c_f, n_f, m_f, c_b, n_b, m_b, s_f, s_b):
            scr[...] = jnp.zeros_like(scr)

    streams = [
        _mlstm_phases(mq_f, mk_f, mv_f, g_f, gbias, tri_f, exp_f, om_f, c_f, n_f, m_f, 0),
        _mlstm_phases(mq_b, mk_b, mv_b, g_b, gbias, tri_b, exp_b, om_b, c_b, n_b, m_b, 1),
        _gla_phases(gq_f, gk_f, gv_f, g_f, gup_f, glb_f, tri_f, og_f, s_f, 0),
        _gla_phases(gq_b, gk_b, gv_b, g_b, gup_b, glb_b, tri_b, og_b, s_b, 1),
    ]
    if cast_weights:
        rows = w_src.shape[0] // CAST_PIECES

        def cast_piece(k):
            def run():
                w_dst[k * rows:(k + 1) * rows, :] = w_src[k * rows:(k + 1) * rows, :].astype(BF16)
            return run
        streams.append([cast_piece(k) for k in range(CAST_PIECES)])
    _run_interleaved(streams)


def _run_interleaved(streams):
    for k in range(max(len(s) for s in streams)):
        for s in streams:
            if k < len(s):
                s[k]()


def _mlstm_phases(q_ref, k_ref, v_ref, g_ref, gb_ref, tri_ref, exp_ref, o_ref, c_scr, n_scr, m_scr, direction):
    fwd = direction == 0
    neg_inf = jnp.float32(-jnp.inf)
    order = list(range(N_CHUNKS)) if fwd else list(range(N_CHUNKS - 1, -1, -1))
    pairs = list(range(MLSTM_HEADS // 2))
    inst = [(p, c) for p in pairs for c in order]
    last_of = lambda c: c * CHUNK + (CHUNK - 1 if fwd else 0)
    ts_of = lambda c: slice(c * CHUNK, (c + 1) * CHUNK)
    ts2_of = lambda c: slice((c // 2) * 2 * CHUNK, (c // 2 + 1) * 2 * CHUNK)
    lp_of = lambda p: slice(p * LANES, (p + 1) * LANES)
    dn_nt = (((1,), (1,)), ((), ()))
    st = {}

    lane = lax.broadcasted_iota(jnp.int32, (CHUNK, LANES), 1)
    trow = lax.broadcasted_iota(jnp.int32, (CHUNK, LANES), 0)
    low = lane < MLSTM_DH
    tri_ok = (lane % MLSTM_DH <= trow) if fwd else (lane % MLSTM_DH >= trow)
    lane2 = lax.broadcasted_iota(jnp.int32, (LANES, LANES), 1)
    row2 = lax.broadcasted_iota(jnp.int32, (LANES, LANES), 0)
    same_head = (lane2 < MLSTM_DH) == (row2 < MLSTM_DH)
    bd_ones = jnp.where(same_head, 1.0, 0.0).astype(BF16)
    lane_row = lax.broadcasted_iota(jnp.int32, (1, LANES), 1)
    pair_half = lax.broadcasted_iota(jnp.int32, (2 * CHUNK, LANES), 0) // CHUNK

    def pair_blockdiag(x):
        zero = jnp.zeros_like(x)
        return jnp.concatenate([jnp.where(low, x, zero), jnp.where(low, zero, x)], axis=0)

    def ph_cumsum():
        st['g'] = g_ref[...] + gb_ref[...]
        st['b_f'] = _dot_exact_lhs(tri_ref[...], _log_sigmoid(st['g']))

    def ph_gates():
        b_i = pltpu.roll(st['b_f'], LANES - MLSTM_HEADS, 1)
        gr = st['g'] - b_i
        tok = lax.broadcasted_iota(jnp.int32, (TB, LANES), 0) % CHUNK
        aloc = gr
        for sh in (1, 2, 4, 8, 16, 32):
            if fwd:
                shifted, ok = pltpu.roll(aloc, sh, 0), tok >= sh
            else:
                shifted, ok = pltpu.roll(aloc, TB - sh, 0), tok < CHUNK - sh
            aloc = jnp.maximum(aloc, jnp.where(ok, shifted, neg_inf))
        chunk_max = jnp.concatenate(
            [jnp.broadcast_to(aloc[last_of(c):last_of(c) + 1], (CHUNK, LANES)) for c in range(N_CHUNKS)], axis=0)
        st['b_i'], st['aloc'] = b_i, aloc
        st['wloc'] = jnp.exp(gr - chunk_max)
        st['gr_t'] = gr.T

    def ph_expand():
        expand = exp_ref[...]
        st['aloc_x'] = _dot_exact_rhs(st['aloc'], expand)
        st['b_x'] = _dot_exact_rhs(st['b_i'], expand)
        st['wloc_x'] = _dot_exact_rhs(st['wloc'], expand)

    def ph_scores():
        for p, c in inst:
            kp = k_ref[ts_of(c), lp_of(p)] * MLSTM_DH ** -0.5
            st['s', p, c] = lax.dot_general(q_ref[ts_of(c), lp_of(p)], pair_blockdiag(kp), dn_nt,
                                            preferred_element_type=F32)
        for p in pairs:
            for c2 in range(N_CHUNKS // 2):
                st['kt2', p, c2] = (k_ref[ts2_of(2 * c2), lp_of(p)].astype(F32) * MLSTM_DH ** -0.5).T.astype(BF16)

    def ph_weights():
        for p, c in inst:
            ji0 = direction * 2 * MLSTM_HEADS + 2 * p
            ra = st['gr_t'][ji0:ji0 + 1, ts2_of(c)]
            rb = st['gr_t'][ji0 + 1:ji0 + 2, ts2_of(c)]
            if c % 2 == 0:
                g_row = jnp.where(lane_row < MLSTM_DH, ra, pltpu.roll(rb, MLSTM_DH, 1))
            else:
                g_row = jnp.where(lane_row < MLSTM_DH, pltpu.roll(ra, MLSTM_DH, 1), rb)
            al = st['aloc_x'][ts_of(c), lp_of(p)]
            s = st['s', p, c] * jnp.exp(jnp.where(tri_ok, g_row - al, neg_inf))
            s_hi = s.astype(BF16)
            st['s_hi', p, c] = s_hi
            st['s_lo', p, c] = (s - s_hi.astype(F32)).astype(BF16)
            wx2 = st['wloc_x'][ts2_of(c), lp_of(p)]
            in_chunk = pair_half == (c % 2)
            wv = jnp.where(in_chunk, wx2 * v_ref[ts2_of(c), lp_of(p)].astype(F32), 0.0)
            ww = jnp.where(in_chunk, wx2, 0.0)
            st['wvw', p, c] = jnp.concatenate([wv, ww], axis=1).astype(BF16)

    def ph_intra():
        for p, c in inst:
            vp = v_ref[ts_of(c), lp_of(p)]
            st['r1', p, c] = jnp.dot(st['s_hi', p, c], jnp.concatenate([pair_blockdiag(vp), bd_ones], axis=1),
                                     preferred_element_type=F32)
            st['den_lo', p, c] = jnp.dot(st['s_lo', p, c], bd_ones, preferred_element_type=F32)
            st['upd', p, c] = jnp.dot(st['kt2', p, c // 2], st['wvw', p, c], preferred_element_type=F32)

    def ph_state():
        for p in pairs:
            bdc, bdn = c_scr[p], n_scr[p]
            m_row = m_scr[p][0:1]
            for c in order:
                st['cn', p, c] = jnp.concatenate([bdc, bdn], axis=1).astype(BF16)
                st['m', p, c] = m_row
                last = last_of(c)
                aloc_last = st['aloc_x'][last:last + 1, lp_of(p)]
                a_last = jnp.maximum(m_row, aloc_last)
                decay = jnp.exp(m_row - a_last)
                scale = jnp.exp(aloc_last - a_last)
                upd = st['upd', p, c]
                bdc = decay * bdc + scale * jnp.where(same_head, upd[:, :LANES], 0.0)
                bdn = decay * bdn + scale * jnp.where(same_head, upd[:, LANES:], 0.0)
                m_row = st['b_x'][last:last + 1, lp_of(p)] + a_last
            c_scr[p] = bdc
            n_scr[p] = bdn
            m_scr[p] = jnp.broadcast_to(m_row, (SUBLANES, LANES))

    def ph_inter():
        for p, c in inst:
            st['r2', p, c] = jnp.dot(q_ref[ts_of(c), lp_of(p)], st['cn', p, c],
                                     preferred_element_type=F32)

    def ph_out():
        for p, c in inst:
            al = st['aloc_x'][ts_of(c), lp_of(p)]
            m_row, r1, r2 = st['m', p, c], st['r1', p, c], st['r2', p, c]
            a = jnp.maximum(m_row, al)
            corr = jnp.exp(al - a)
            w_inter = jnp.exp(m_row - a)
            num = corr * r1[:, :LANES] + w_inter * r2[:, :LANES]
            den = corr * (r1[:, LANES:] + st['den_lo', p, c]) + w_inter * r2[:, LANES:]
            o_ref[ts_of(c), lp_of(p)] = num / jnp.maximum(jnp.abs(den),
                                                          jnp.exp(-(st['b_x'][ts_of(c), lp_of(p)] + a)))

    return [ph_cumsum, ph_gates, ph_expand, ph_scores, ph_weights, ph_intra, ph_state, ph_inter, ph_out]


def _gla_phases(q_ref, k_ref, v_ref, g_ref, gup_ref, gb_ref, tri_ref, o_ref, s_scr, direction):
    fwd = direction == 0
    order = list(range(N_CHUNKS)) if fwd else list(range(N_CHUNKS - 1, -1, -1))
    ts_of = lambda c: slice(c * CHUNK, (c + 1) * CHUNK)
    ts2_of = lambda c: slice((c // 2) * 2 * CHUNK, (c // 2 + 1) * 2 * CHUNK)
    last_of = lambda c: c * CHUNK + (CHUNK - 1 if fwd else 0)
    dn_nt = (((1,), (1,)), ((), ()))
    st = {}

    khead = lax.broadcasted_iota(jnp.int32, (CHUNK, GLA_KEY_WIDTH), 1) // GLA_DK
    vlane = lax.broadcasted_iota(jnp.int32, (CHUNK, GLA_WIDTH), 1)
    vhead = vlane // GLA_DV
    trow = lax.broadcasted_iota(jnp.int32, (CHUNK, GLA_WIDTH), 0)
    tri_ok = (vlane % CHUNK <= trow) if fwd else (vlane % CHUNK >= trow)
    st_head_r = lax.broadcasted_iota(jnp.int32, (GLA_WIDTH, GLA_KEY_WIDTH), 0) // GLA_DV
    st_head_c = lax.broadcasted_iota(jnp.int32, (GLA_WIDTH, GLA_KEY_WIDTH), 1) // GLA_DK
    same_head = st_head_r == st_head_c
    pair_row = lax.broadcasted_iota(jnp.int32, (2 * CHUNK, GLA_KEY_WIDTH), 0) // CHUNK

    def stack_heads(x, head_of_lane):
        zero = jnp.zeros_like(x)
        return jnp.concatenate([jnp.where(head_of_lane == h, x, zero) for h in range(GLA_HEADS)], axis=0)

    def ph_gate_proj():
        st['z'] = _dot_bf16x3(g_ref[...], gup_ref[...]) + gb_ref[...]

    def ph_log_decay():
        st['lg'] = _log_sigmoid(st['z']) * (1.0 / GLA_TAU)

    def ph_cumsum():
        st['b'] = _dot_exact_lhs(tri_ref[...], st['lg'])

    def ph_prep():
        b = st['b']
        st['k'] = k_ref[...].astype(F32)
        st['qt'] = (q_ref[...].astype(F32) * GLA_DK ** -0.5 * jnp.exp(b)).astype(BF16)
        st['kt'] = (st['k'] * jnp.exp(-b)).astype(BF16)
        st['v_t'] = v_ref[...].astype(F32).T.astype(BF16)

    def ph_scores():
        for c in order:
            st['a', c] = lax.dot_general(st['qt'][ts_of(c)], stack_heads(st['kt'][ts_of(c)], khead), dn_nt,
                                         preferred_element_type=F32)

    def ph_mask():
        b, k = st['b'], st['k']
        for c in order:
            st['a', c] = jnp.where(tri_ok, st['a', c], 0.0).astype(BF16)
            last = last_of(c)
            st['kdec', c] = jnp.where(pair_row == c % 2, k[ts2_of(c)] * jnp.exp(b[last:last + 1] - b[ts2_of(c)]),
                                      0.0).astype(BF16)

    def ph_intra():
        for c in order:
            st['o', c] = jnp.dot(st['a', c], stack_heads(v_ref[ts_of(c), :], vhead), preferred_element_type=F32)
            st['upd', c] = jnp.dot(st['v_t'][:, ts2_of(c)], st['kdec', c], preferred_element_type=F32)

    def ph_state():
        s_t = s_scr[...]
        for c in order:
            st['s', c] = s_t.astype(BF16)
            last = last_of(c)
            s_t = jnp.exp(st['b'][last:last + 1]) * s_t + jnp.where(same_head, st['upd', c], 0.0)
        s_scr[...] = s_t

    def ph_inter():
        for c in order:
            st['oi', c] = lax.dot_general(st['qt'][ts_of(c)], st['s', c], dn_nt, preferred_element_type=F32)

    def ph_out():
        for c in order:
            o_ref[ts_of(c), :] = st['o', c] + st['oi', c]

    return [ph_gate_proj, ph_log_decay, ph_cumsum, ph_prep, ph_scores, ph_mask, ph_intra, ph_state, ph_inter, ph_out]


def _scans_call(main, gates, gate_bias, tri, expand, gup_pad, gla_b, w_cast=None):
    t = main.shape[0]
    nb = t // TB
    const = lambda i: (0, 0)
    extra_in, extra_out, extra_shape, extra_args = [], [], [], []
    if w_cast is not None:
        rows, cols = w_cast.shape
        blk_rows = rows // CAST_STEPS
        w_spec = pl.BlockSpec((blk_rows, cols), lambda i: (jnp.minimum(i, CAST_STEPS - 1), 0))
        extra_in, extra_out = [w_spec], [w_spec]
        extra_shape, extra_args = [jax.ShapeDtypeStruct((rows, cols), BF16)], [w_cast]

    def streams(direction):
        blk = lambda i: _scan_block(i, nb, direction)
        col = lambda c, w: (lambda i: (blk(i), c // w))
        return [pl.BlockSpec((TB, MLSTM_WIDTH), col(C_MQ, MLSTM_WIDTH)),
                pl.BlockSpec((TB, MLSTM_WIDTH), col(C_MK, MLSTM_WIDTH)),
                pl.BlockSpec((TB, MLSTM_WIDTH), col(C_MV, MLSTM_WIDTH)),
                pl.BlockSpec((TB, GLA_KEY_WIDTH), col(C_GQ, GLA_KEY_WIDTH)),
                pl.BlockSpec((TB, GLA_KEY_WIDTH), col(C_GK, GLA_KEY_WIDTH)),
                pl.BlockSpec((TB, GLA_WIDTH), col(C_GV, GLA_WIDTH)),
                pl.BlockSpec((TB, N_GATE), lambda i: (blk(i), 0))]

    consts = [pl.BlockSpec((1, N_GATE), const),
              pl.BlockSpec((TB, TB), const), pl.BlockSpec((TB, TB), const),
              pl.BlockSpec((N_GATE, MLSTM_WIDTH), const), pl.BlockSpec((N_GATE, MLSTM_WIDTH), const),
              pl.BlockSpec((N_GATE, GLA_KEY_WIDTH), const), pl.BlockSpec((N_GATE, GLA_KEY_WIDTH), const),
              pl.BlockSpec((1, GLA_KEY_WIDTH), const), pl.BlockSpec((1, GLA_KEY_WIDTH), const)]
    out_spec = lambda direction, w: pl.BlockSpec((TB, w), lambda i: (_scan_block(i, nb, direction), 0))
    n_pairs = MLSTM_HEADS // 2
    mlstm_state = [pltpu.VMEM((n_pairs, LANES, LANES), F32), pltpu.VMEM((n_pairs, LANES, LANES), F32),
                   pltpu.VMEM((n_pairs, SUBLANES, LANES), F32)]
    gla_state = [pltpu.VMEM((GLA_WIDTH, GLA_KEY_WIDTH), F32)]
    return pl.pallas_call(
        functools.partial(_scans_kernel, cast_weights=w_cast is not None),
        grid=(nb,),
        in_specs=streams(0) + streams(1) + consts + extra_in,
        out_specs=[out_spec(0, MLSTM_WIDTH), out_spec(1, MLSTM_WIDTH), out_spec(0, GLA_WIDTH),
                   out_spec(1, GLA_WIDTH)] + extra_out,
        out_shape=([jax.ShapeDtypeStruct((t, MLSTM_WIDTH), F32)] * 2 + [jax.ShapeDtypeStruct((t, GLA_WIDTH), F32)] * 2
                   + extra_shape),
        scratch_shapes=mlstm_state + mlstm_state + gla_state + gla_state,
        compiler_params=_cparams(1),
        name="scans",
    )(*([main] * 6 + [gates]) * 2, gate_bias, tri[0], tri[1], expand[0], expand[1],
      gup_pad[0], gup_pad[1], gla_b[0:1], gla_b[1:2], *extra_args)


def _attn_kernel(sink_ref, q_ref, kp_ref, kc_ref, kn_ref, kx_ref, vp_ref, vc_ref, vn_ref, vx_ref, o_ref, *, seq):
    i = pl.program_id(0)
    half = TB // 2
    n_loc = 2 * TB
    keys = jnp.concatenate([kp_ref[...], kc_ref[...], kn_ref[...], kx_ref[...]], axis=0)
    vals = jnp.concatenate([vp_ref[...], vc_ref[...], vn_ref[...], vx_ref[...]], axis=0)
    n_keys = keys.shape[0]
    lane = lax.broadcasted_iota(jnp.int32, (n_keys, LANES), 1)
    zero = jnp.zeros_like(keys)
    keys_g = [jnp.where(lane < HEAD_DIM, keys, zero), jnp.where(lane >= HEAD_DIM, keys, zero)]
    r = lax.broadcasted_iota(jnp.int32, (TB, n_keys), 0)
    c = lax.broadcasted_iota(jnp.int32, (TB, n_keys), 1)
    rel = c - half - r
    kpos = (i - 1) * TB + c - half
    local_ok = (jnp.abs(rel) <= WINDOW) & (kpos >= 0) & (kpos < seq) & (i > 0)
    valid = local_ok | (c >= n_loc)
    out_lane = lax.broadcasted_iota(jnp.int32, (TB, LANES), 1)
    neg_inf = jnp.float32(-jnp.inf)
    dn = (((1,), (1,)), ((), ()))
    heads = [(m, gidx) for m in range(ATTN_Q_WIDTH // LANES) for gidx in range(ATTN_KV_HEADS)]
    scores = [lax.dot_general(q_ref[:, m * LANES:(m + 1) * LANES], keys_g[gidx], dn, preferred_element_type=F32)
              for m, gidx in heads]
    probs, denoms = [], []
    for (m, gidx), s in zip(heads, scores):
        sink = sink_ref[gidx * (ATTN_HEADS // ATTN_KV_HEADS) + m]
        s = jnp.where(valid, s, neg_inf)
        mx = jnp.maximum(jnp.max(s, axis=1, keepdims=True), sink)
        p = jnp.exp(s - mx)
        denoms.append(jnp.sum(p, axis=1, keepdims=True) + jnp.exp(sink - mx))
        probs.append(p.astype(BF16))
    pvs = [jnp.dot(p, vals, preferred_element_type=F32) for p in probs]
    for m in range(ATTN_Q_WIDTH // LANES):
        r0 = pvs[2 * m] / denoms[2 * m]
        r1 = pvs[2 * m + 1] / denoms[2 * m + 1]
        o_ref[:, m * LANES:(m + 1) * LANES] = jnp.where(out_lane < HEAD_DIM, r0, r1).astype(BF16)


def _attn_call(main, sink, seq):
    t = main.shape[0]
    nb = t // TB
    half = TB // 2
    nhb = t // half
    kcol, vcol = C_AK // LANES, C_AV // LANES
    prev_i = lambda i: jnp.maximum(2 * i - 1, 0)
    next_i = lambda i: jnp.minimum(2 * i + 2, nhb - 1)
    grid_spec = pltpu.PrefetchScalarGridSpec(
        num_scalar_prefetch=1,
        grid=(nb,),
        in_specs=[pl.BlockSpec((TB, ATTN_Q_WIDTH), lambda i, s: (i, C_AQ // ATTN_Q_WIDTH)),
                  pl.BlockSpec((half, LANES), lambda i, s: (prev_i(i), kcol)),
                  pl.BlockSpec((TB, LANES), lambda i, s: (i, kcol)),
                  pl.BlockSpec((half, LANES), lambda i, s: (next_i(i), kcol)),
                  pl.BlockSpec((TB, LANES), lambda i, s: (0, kcol)),
                  pl.BlockSpec((half, LANES), lambda i, s: (prev_i(i), vcol)),
                  pl.BlockSpec((TB, LANES), lambda i, s: (i, vcol)),
                  pl.BlockSpec((half, LANES), lambda i, s: (next_i(i), vcol)),
                  pl.BlockSpec((TB, LANES), lambda i, s: (0, vcol))],
        out_specs=pl.BlockSpec((TB, ATTN_Q_WIDTH), lambda i, s: (i, 0)),
    )
    return pl.pallas_call(
        functools.partial(_attn_kernel, seq=seq),
        grid_spec=grid_spec,
        out_shape=jax.ShapeDtypeStruct((t, ATTN_Q_WIDTH), BF16),
        compiler_params=_cparams(1),
        name="window_attn",
    )(sink, main, main, main, main, main, main, main, main, main)


def _group_mean(x, avg_bf16):
    hi = x.astype(BF16)
    lo = (x - hi.astype(F32)).astype(BF16)
    return (jnp.dot(hi, avg_bf16, preferred_element_type=F32) + jnp.dot(lo, avg_bf16, preferred_element_type=F32))


def _head_norm(x, avg_bf16, w):
    cen = x - _group_mean(x, avg_bf16)
    var = _group_mean(cen * cen, avg_bf16)
    return cen * lax.rsqrt(var + LN_EPS) * w


def _outproj_kernel(mf_ref, mb_ref, mo_ref, at_ref, gf_ref, gbk_ref, go_ref, mnw_ref, gnw_ref, avg_ref,
                    w_ref, x_ref, mod_ref, lnw_ref, lnb_ref, x1_ref, h2_ref, wbf_scr):
    i = pl.program_id(0)
    tm = x_ref.shape[0]

    @pl.when(i == 0)
    def _():
        for dst, src, n in _column_runs(_out_row_perm()):
            wbf_scr[dst:dst + n, :] = w_ref[src:src + n, :].astype(BF16)

    avg = avg_ref[...]
    hm = _head_norm(mf_ref[...] + mb_ref[...], avg, mnw_ref[...]) * _sigmoid(mo_ref[...].astype(F32))
    gate = go_ref[...].astype(F32)
    hg = _head_norm(gf_ref[...] + gbk_ref[...], avg, gnw_ref[...]) * (gate * _sigmoid(gate))
    a0, a1, a2 = MLSTM_WIDTH, MLSTM_WIDTH + ATTN_Q_WIDTH, D_MODEL
    mix = (jnp.dot(hm.astype(BF16), wbf_scr[:a0], preferred_element_type=F32)
           + jnp.dot(at_ref[...], wbf_scr[a0:a1], preferred_element_type=F32)
           + jnp.dot(hg.astype(BF16), wbf_scr[a1:a2], preferred_element_type=F32))
    x1 = _layer_norm(DEEPNORM_ALPHA * x_ref[...] + _select_mod(mod_ref, 2, i, tm) * mix, lnw_ref[...], lnb_ref[...])
    x1_ref[...] = x1
    h2_ref[...] = (x1 * (1.0 + _select_mod(mod_ref, 4, i, tm)) + _select_mod(mod_ref, 3, i, tm)).astype(h2_ref.dtype)


def _outproj_call(mf, mb, main, attn, gf, gbk, mnw, gnw, avg, w_out, layer, x, mod, lnw, lnb, h2_dtype):
    t, d = x.shape
    tm = PROJ_TM
    row = lambda i: (i, 0)
    const = lambda i: (0, 0)
    return pl.pallas_call(
        _outproj_kernel,
        grid=(t // tm,),
        in_specs=[pl.BlockSpec((tm, MLSTM_WIDTH), row),
                  pl.BlockSpec((tm, MLSTM_WIDTH), row),
                  pl.BlockSpec((tm, MLSTM_WIDTH), lambda i: (i, C_MO // MLSTM_WIDTH)),
                  pl.BlockSpec((tm, ATTN_Q_WIDTH), row),
                  pl.BlockSpec((tm, GLA_WIDTH), row),
                  pl.BlockSpec((tm, GLA_WIDTH), row),
                  pl.BlockSpec((tm, GLA_WIDTH), lambda i: (i, C_GO // GLA_WIDTH)),
                  pl.BlockSpec((1, MLSTM_WIDTH), const),
                  pl.BlockSpec((1, GLA_WIDTH), const),
                  pl.BlockSpec((MLSTM_WIDTH, MLSTM_WIDTH), const),
                  pl.BlockSpec((None, d, d), lambda i: (layer, 0, 0), pipeline_mode=pl.Buffered(1)),
                  pl.BlockSpec((tm, d), row),
                  pl.BlockSpec((2, SUBLANES, d), lambda i: (0, 0, 0)),
                  pl.BlockSpec((1, d), const),
                  pl.BlockSpec((1, d), const)],
        out_specs=[pl.BlockSpec((tm, d), row), pl.BlockSpec((tm, d), row)],
        out_shape=[jax.ShapeDtypeStruct((t, d), F32), jax.ShapeDtypeStruct((t, d), h2_dtype)],
        scratch_shapes=[pltpu.VMEM((d, d), BF16)],
        compiler_params=_cparams(1),
        name="out_proj",
    )(mf, mb, main, attn, gf, gbk, main, mnw, gnw, avg, w_out, x, mod, lnw, lnb)


def _select_mod(mod_ref, k, i, tm):
    rows = i * tm + lax.broadcasted_iota(jnp.int32, (tm, 1), 0)
    return jnp.where(rows < CTX_LEN, mod_ref[0, k:k + 1, :], mod_ref[1, k:k + 1, :])


def _ffn_kernel(h_ref, wgu_ref, wd_ref, x_ref, mod_ref, lnw_ref, lnb_ref, o_ref):
    h = h_ref[...]
    g = jnp.dot(h, wgu_ref[:, :D_FF], preferred_element_type=F32)
    u = jnp.dot(h, wgu_ref[:, D_FF:], preferred_element_type=F32)
    y = jnp.dot((g * _sigmoid(g) * u).astype(BF16), wd_ref[...], preferred_element_type=F32)
    m5 = _select_mod(mod_ref, 5, pl.program_id(0), h_ref.shape[0])
    o_ref[...] = _layer_norm(DEEPNORM_ALPHA * x_ref[...] + m5 * y, lnw_ref[...], lnb_ref[...])


def _ffn_call(h2, w_gu, w_down, layer, x1, mod, lnw, lnb):
    t, d = x1.shape
    resident = pl.Buffered(1)
    return pl.pallas_call(
        _ffn_kernel,
        grid=(t // FFN_TM,),
        in_specs=[pl.BlockSpec((FFN_TM, d), lambda i: (i, 0)),
                  pl.BlockSpec((None, d, 2 * D_FF), lambda i: (layer, 0, 0), pipeline_mode=resident),
                  pl.BlockSpec((None, D_FF, d), lambda i: (layer, 0, 0), pipeline_mode=resident),
                  pl.BlockSpec((FFN_TM, d), lambda i: (i, 0)),
                  pl.BlockSpec((2, SUBLANES, d), lambda i: (0, 0, 0)),
                  pl.BlockSpec((1, d), lambda i: (0, 0)),
                  pl.BlockSpec((1, d), lambda i: (0, 0))],
        out_specs=pl.BlockSpec((FFN_TM, d), lambda i: (i, 0)),
        out_shape=jax.ShapeDtypeStruct((t, d), F32),
        compiler_params=_cparams(1),
        name="dense_ffn",
    )(h2, w_gu, w_down, x1, mod, lnw, lnb)


def _router_kernel(h_ref, wr_ref, br_ref, su_ref, route_ref, cnt_ref, base_scr):
    @pl.when(pl.program_id(0) == 0)
    def _():
        base_scr[...] = jnp.zeros_like(base_scr)

    lt = lax.dot_general(wr_ref[...], h_ref[...], (((1,), (1,)), ((), ())),
                         preferred_element_type=F32, precision=HIGHEST) + br_ref[...]
    idx = lax.broadcasted_iota(jnp.int32, lt.shape, 0)
    neg_inf = jnp.float32(-jnp.inf)
    m1 = jnp.max(lt, axis=0, keepdims=True)
    e1 = jnp.min(jnp.where(lt == m1, idx, N_EXPERTS), axis=0, keepdims=True)
    lt2 = jnp.where(idx == e1, neg_inf, lt)
    m2 = jnp.max(lt2, axis=0, keepdims=True)
    e2 = jnp.min(jnp.where(lt2 == m2, idx, N_EXPERTS), axis=0, keepdims=True)
    t2 = jnp.exp(m2 - m1)
    w1 = 1.0 / (1.0 + t2)
    w2 = t2 / (1.0 + t2)
    su = su_ref[...]
    base = base_scr[...][:, 0:1]
    oh1 = (idx == e1).astype(F32)
    cum1 = jnp.dot(oh1.astype(BF16), su, preferred_element_type=F32)
    rank1 = jnp.sum(oh1 * (base + cum1), axis=0, keepdims=True)
    base = base + jnp.sum(oh1, axis=1, keepdims=True)
    oh2 = (idx == e2).astype(F32)
    cum2 = jnp.dot(oh2.astype(BF16), su, preferred_element_type=F32)
    rank2 = jnp.sum(oh2 * (base + cum2), axis=0, keepdims=True)
    base = base + jnp.sum(oh2, axis=1, keepdims=True)
    base_scr[...] = jnp.broadcast_to(base, base_scr.shape)
    cnt_ref[...] = jnp.broadcast_to(base, cnt_ref.shape)
    zero = jnp.zeros_like(w1)
    route_ref[...] = jnp.concatenate(
        [e1.astype(F32), e2.astype(F32), w1, w2, rank1, rank2, zero, zero], axis=0)


def _router_call(h2, wr_t, br, su):
    t, d = h2.shape
    nb = t // TB
    return pl.pallas_call(
        _router_kernel,
        grid=(nb,),
        in_specs=[pl.BlockSpec((TB, d), lambda i: (i, 0)),
                  pl.BlockSpec((N_EXPERTS, d), lambda i: (0, 0)),
                  pl.BlockSpec((N_EXPERTS, 1), lambda i: (0, 0)),
                  pl.BlockSpec((TB, TB), lambda i: (0, 0))],
        out_specs=[pl.BlockSpec((SUBLANES, TB), lambda i: (0, i)),
                   pl.BlockSpec((N_EXPERTS, LANES), lambda i: (0, 0))],
        out_shape=[jax.ShapeDtypeStruct((SUBLANES, t), F32),
                   jax.ShapeDtypeStruct((N_EXPERTS, LANES), F32)],
        scratch_shapes=[pltpu.VMEM((N_EXPERTS, LANES), F32)],
        compiler_params=_cparams(1),
        name="moe_router",
    )(h2, wr_t, br, su)


ROW_TILE = D_MODEL // LANES
DMA_UNROLL = 8
ZERO_CHUNK = 64
CAST_STEPS = 64
CAST_PIECES = 4


def _store_row_tiles(ref, val):
    n = val.shape[0]
    for j in range(ROW_TILE):
        ref[pl.ds(j, n, stride=ROW_TILE), :] = val[:, j * LANES:(j + 1) * LANES]


def _load_row_tiles(ref, n, dtype=F32):
    return jnp.concatenate([ref[pl.ds(j, n, stride=ROW_TILE), :].astype(dtype) for j in range(ROW_TILE)], axis=1)


def _row(ref, idx):
    return ref.at[pl.ds(pl.multiple_of(idx * ROW_TILE, ROW_TILE), ROW_TILE)]


def _dispatch_kernel(s1_ref, s2_ref, pad_lo_ref, pad_hi_ref, h_ref, xs_ref, hr_scr, zero_scr, sem, zsem):
    i = pl.program_id(0)

    @pl.when(i == 0)
    def _():
        zero_scr[...] = jnp.zeros_like(zero_scr)

        def zero_rows(s, n):
            dst = xs_ref.at[pl.ds(pl.multiple_of(s * ROW_TILE, ROW_TILE), n * ROW_TILE)]
            return pltpu.make_async_copy(zero_scr.at[pl.ds(0, n * ROW_TILE)], dst, zsem)

        def fill(e, act):
            lo, hi = pad_lo_ref[e], pad_hi_ref[e]
            n_big = (hi - lo) // ZERO_CHUNK
            lax.fori_loop(0, n_big, lambda k, c: (act(zero_rows(lo + k * ZERO_CHUNK, ZERO_CHUNK)), c)[1], 0)
            lax.fori_loop(lo + n_big * ZERO_CHUNK, hi, lambda s, c: (act(zero_rows(s, 1)), c)[1], 0)

        for e in range(N_EXPERTS + 1):
            fill(e, lambda cp: cp.start())
        for e in range(N_EXPERTS + 1):
            fill(e, lambda cp: cp.wait())

    n = pl.num_programs(0)
    cur = i % 2

    def copy(blk, r, slot_ref):
        buf = blk % 2
        return pltpu.make_async_copy(_row(hr_scr.at[buf], r), _row(xs_ref, slot_ref[blk * TB + r]), sem.at[buf])

    def start(r, carry):
        copy(i, r, s1_ref).start()
        copy(i, r, s2_ref).start()
        return carry

    def wait_block(blk):
        def wait(r, carry):
            copy(blk, r, s1_ref).wait()
            copy(blk, r, s2_ref).wait()
            return carry
        lax.fori_loop(0, TB, wait, 0, unroll=DMA_UNROLL)

    _store_row_tiles(hr_scr.at[cur], h_ref[...])
    lax.fori_loop(0, TB, start, 0, unroll=DMA_UNROLL)

    @pl.when(i > 0)
    def _():
        wait_block(i - 1)

    @pl.when(i == n - 1)
    def _():
        wait_block(i)


def _dispatch_call(slot1, slot2, pad_lo, pad_hi, h2, n_slots):
    t, d = h2.shape
    grid_spec = pltpu.PrefetchScalarGridSpec(
        num_scalar_prefetch=4,
        grid=(t // TB,),
        in_specs=[pl.BlockSpec((TB, d), lambda i, *_: (i, 0))],
        out_specs=pl.BlockSpec(memory_space=pl.ANY),
        scratch_shapes=[pltpu.VMEM((2, TB * ROW_TILE, LANES), F32), pltpu.VMEM((ZERO_CHUNK * ROW_TILE, LANES), F32),
                        pltpu.SemaphoreType.DMA((2,)), pltpu.SemaphoreType.DMA(())],
    )
    return pl.pallas_call(
        _dispatch_kernel,
        grid_spec=grid_spec,
        out_shape=jax.ShapeDtypeStruct((n_slots * ROW_TILE, LANES), F32),
        compiler_params=_cparams(1),
        name="moe_dispatch",
    )(slot1, slot2, pad_lo, pad_hi, h2)


def _expert_kernel(be_ref, used_ref, x_ref, wg_ref, wu_ref, wd_ref, y_ref, xb_scr, acc_scr, *, n_f):
    b, f = pl.program_id(0), pl.program_id(1)
    live = b < used_ref[0]
    last = n_f - 1

    def swiglu_part():
        x = xb_scr[...]
        g = jnp.dot(x, wg_ref[...], preferred_element_type=F32)
        u = jnp.dot(x, wu_ref[...], preferred_element_type=F32)
        return jnp.dot((g * _sigmoid(g) * u).astype(BF16), wd_ref[...], preferred_element_type=F32)

    @pl.when(live & (f == 0))
    def _():
        for j in range(ROW_TILE):
            xb_scr[:, j * LANES:(j + 1) * LANES] = x_ref[pl.ds(j, MOE_G, stride=ROW_TILE), :].astype(BF16)
        acc_scr[...] = swiglu_part()

    if n_f > 2:
        @pl.when(live & (f > 0) & (f < last))
        def _():
            acc_scr[...] += swiglu_part()

    @pl.when(live & (f == last))
    def _():
        _store_row_tiles(y_ref, acc_scr[...] + swiglu_part())

    @pl.when(jnp.logical_not(live) & (f == last))
    def _():
        y_ref[...] = jnp.zeros_like(y_ref)


def _expert_call(block_e, used, xs, w_gu, w_down):
    d = D_MODEL
    ns = xs.shape[0] // ROW_TILE
    nblk = ns // MOE_G
    nf = D_EXPERT // MOE_TF
    rows = MOE_G * ROW_TILE

    def f_eff(b, f, used_ref):
        return jnp.where(b < used_ref[0], f, nf - 1)

    grid_spec = pltpu.PrefetchScalarGridSpec(
        num_scalar_prefetch=2,
        grid=(nblk, nf),
        in_specs=[pl.BlockSpec((rows, LANES), lambda b, f, be, us: (b, 0)),
                  pl.BlockSpec((None, d, MOE_TF), lambda b, f, be, us: (be[b], 0, f_eff(b, f, us))),
                  pl.BlockSpec((None, d, MOE_TF), lambda b, f, be, us: (be[b], 0, nf + f_eff(b, f, us))),
                  pl.BlockSpec((None, MOE_TF, d), lambda b, f, be, us: (be[b], f_eff(b, f, us), 0))],
        out_specs=pl.BlockSpec((rows, LANES), lambda b, f, be, us: (b, 0)),
        scratch_shapes=[pltpu.VMEM((MOE_G, d), BF16), pltpu.VMEM((MOE_G, d), F32)],
    )
    return pl.pallas_call(
        functools.partial(_expert_kernel, n_f=nf),
        grid_spec=grid_spec,
        out_shape=jax.ShapeDtypeStruct(xs.shape, F32),
        compiler_params=_cparams(2),
        name="moe_experts",
    )(block_e, used, xs, w_gu, w_gu, w_down)


def _combine_kernel(s1_ref, s2_ref, yb_ref, route_ref, x_ref, mod_ref, lnw_ref, lnb_ref, o_ref, buf1, buf2, sem):
    i = pl.program_id(0)
    n = pl.num_programs(0)
    cur = i % 2

    def copy(blk, r, slot_ref, buf):
        b = blk % 2
        return pltpu.make_async_copy(_row(yb_ref, slot_ref[blk * TB + r]), _row(buf.at[b], r), sem.at[b])

    def start_block(blk):
        def start(r, carry):
            copy(blk, r, s1_ref, buf1).start()
            copy(blk, r, s2_ref, buf2).start()
            return carry
        lax.fori_loop(0, TB, start, 0, unroll=DMA_UNROLL)

    def wait(r, carry):
        copy(i, r, s1_ref, buf1).wait()
        copy(i, r, s2_ref, buf2).wait()
        return carry

    @pl.when(i == 0)
    def _():
        start_block(i)

    @pl.when(i + 1 < n)
    def _():
        start_block(i + 1)

    lax.fori_loop(0, TB, wait, 0, unroll=DMA_UNROLL)
    rt = route_ref[...].T
    y = rt[:, 2:3] * _load_row_tiles(buf1.at[cur], TB) + rt[:, 3:4] * _load_row_tiles(buf2.at[cur], TB)
    m = mod_ref[...]
    o_ref[...] = _layer_norm(DEEPNORM_ALPHA * x_ref[...] + m[5:6] * y, lnw_ref[...], lnb_ref[...])


def _combine_call(slot1, slot2, yb, route, x1, mod, lnw, lnb, latent_only):
    t, d = x1.shape
    if latent_only:
        out_rows, out_map = t - CTX_LEN, lambda i, a, b: (jnp.maximum(i - 1, 0), 0)
    else:
        out_rows, out_map = t, lambda i, a, b: (i, 0)
    grid_spec = pltpu.PrefetchScalarGridSpec(
        num_scalar_prefetch=2,
        grid=(t // TB,),
        in_specs=[pl.BlockSpec(memory_space=pl.ANY),
                  pl.BlockSpec((SUBLANES, TB), lambda i, a, b: (0, i)),
                  pl.BlockSpec((TB, d), lambda i, a, b: (i, 0)),
                  pl.BlockSpec((None, SUBLANES, d), lambda i, a, b: (_who(i), 0, 0)),
                  pl.BlockSpec((1, d), lambda i, a, b: (0, 0)),
                  pl.BlockSpec((1, d), lambda i, a, b: (0, 0))],
        out_specs=pl.BlockSpec((TB, d), out_map),
        scratch_shapes=[pltpu.VMEM((2, TB * ROW_TILE, LANES), F32), pltpu.VMEM((2, TB * ROW_TILE, LANES), F32),
                        pltpu.SemaphoreType.DMA((2,))],
    )
    return pl.pallas_call(
        _combine_kernel,
        grid_spec=grid_spec,
        out_shape=jax.ShapeDtypeStruct((out_rows, d), F32),
        compiler_params=_cparams(1),
        name="moe_combine",
    )(slot1, slot2, yb, route, x1, mod, lnw, lnb)


def _moe_layer(h2, x1, mod, lnw, lnb, w_router, b_router, w_gu, w_down, su, latent_only):
    t, d = h2.shape
    route, cnt = _router_call(h2, w_router.T, b_router.reshape(N_EXPERTS, 1), su)
    counts = cnt[:, 0].astype(jnp.int32)
    padded = (counts + MOE_G - 1) // MOE_G * MOE_G
    pend = jnp.cumsum(padded)
    pstart = pend - padded
    e1, e2 = route[0].astype(jnp.int32), route[1].astype(jnp.int32)
    slot1 = pstart[e1] + route[4].astype(jnp.int32)
    slot2 = pstart[e2] + route[5].astype(jnp.int32)
    nblk = -(-(2 * t) // MOE_G) + N_EXPERTS
    blk_start = jnp.arange(nblk, dtype=jnp.int32) * MOE_G
    block_e = jnp.minimum(jnp.sum((pend[None, :] <= blk_start[:, None]).astype(jnp.int32), axis=1), N_EXPERTS - 1)
    used = (pend[-1:] // MOE_G).astype(jnp.int32)
    n_slots = nblk * MOE_G
    pad_lo = jnp.concatenate([pstart + counts, pend[-1:]])
    pad_hi = jnp.concatenate([pend, jnp.full((1,), n_slots, jnp.int32)])
    xs = _dispatch_call(slot1, slot2, pad_lo, pad_hi, h2, n_slots)
    yb = _expert_call(block_e, used, xs, w_gu, w_down)
    return _combine_call(slot1, slot2, yb, route, x1, mod, lnw, lnb, latent_only)


def _attn_head_perm():
    cols = []
    for m in range(ATTN_HEADS // ATTN_KV_HEADS):
        for hq in (m, m + ATTN_HEADS // ATTN_KV_HEADS):
            cols.extend(range(hq * HEAD_DIM, (hq + 1) * HEAD_DIM))
    return np.asarray(cols, np.int32)


def _in_col_perm():
    n_m = 4 * MLSTM_WIDTH
    mg = 2 * 2 * MLSTM_HEADS
    a0 = n_m + mg
    perm = list(range(n_m))
    perm += [a0 + int(j) for j in _attn_head_perm()]
    perm += list(range(a0 + ATTN_Q_WIDTH, a0 + ATTN_Q_WIDTH + 2 * ATTN_KV_WIDTH))
    g0 = a0 + ATTN_Q_WIDTH + 2 * ATTN_KV_WIDTH
    perm += list(range(g0, g0 + 2 * GLA_KEY_WIDTH + 2 * GLA_WIDTH))
    perm += list(range(n_m, n_m + mg))
    perm += list(range(g0 + 2 * GLA_KEY_WIDTH + 2 * GLA_WIDTH, g0 + 2 * GLA_KEY_WIDTH + 2 * GLA_WIDTH + 2 * GLA_RANK))
    return np.asarray(perm, np.int32)


def _out_row_perm():
    return np.concatenate([np.arange(MLSTM_WIDTH), MLSTM_WIDTH + _attn_head_perm(),
                           np.arange(MLSTM_WIDTH + ATTN_Q_WIDTH, D_MODEL)]).astype(np.int32)


def _column_runs(perm):
    runs, start = [], 0
    for j in range(1, len(perm) + 1):
        if j == len(perm) or perm[j] != perm[j - 1] + 1:
            runs.append((start, int(perm[start]), j - start))
            start = j
    return runs


def _rope_tables(seq):
    inv = ROPE_BASE ** (-jnp.arange(ROPE_PAIRS, dtype=F32) / ROPE_PAIRS)
    rows = seq // GRID_W
    ang_r = jnp.arange(rows).astype(F32)[:, None] * inv
    ang_c = jnp.arange(GRID_W).astype(F32)[:, None] * inv
    lane_pat = lambda a, b: jnp.tile(jnp.concatenate([a, b], -1), (1, LANES // (2 * ROPE_PAIRS)))
    cos_rows, sin_rows = lane_pat(jnp.cos(ang_r), jnp.cos(ang_r)), lane_pat(-jnp.sin(ang_r), jnp.sin(ang_r))
    cos_cols, sin_cols = lane_pat(jnp.cos(ang_c), jnp.cos(ang_c)), lane_pat(-jnp.sin(ang_c), jnp.sin(ang_c))
    row_part = (np.arange(LANES) % HEAD_DIM) < HEAD_DIM // 2
    per_token = lambda by_row, by_col: jnp.where(
        row_part, jnp.broadcast_to(by_row[:, None, :], (rows, GRID_W, LANES)),
        jnp.broadcast_to(by_col[None, :, :], (rows, GRID_W, LANES))).reshape(seq, LANES)
    cos_l, sin_l = per_token(cos_rows, cos_cols), per_token(sin_rows, sin_cols)
    cos_t = jnp.concatenate([jnp.ones((CTX_LEN, LANES), F32), cos_l], 0)
    sin_t = jnp.concatenate([jnp.zeros((CTX_LEN, LANES), F32), sin_l], 0)
    return cos_t, sin_t


def _block_tri(direction):
    r = np.arange(TB)[:, None]
    c = np.arange(TB)[None, :]
    same = (r // CHUNK) == (c // CHUNK)
    tri = (c <= r) if direction == 0 else (c >= r)
    return jnp.asarray(same & tri, BF16)


def kernel(x, c, ctx, c_ctx, w_ada, b_ada, w_in, mlstm_gate_b, mlstm_norm_w, attn_sink, gla_gate_up, gla_gate_b,
           gla_norm_w, w_out, ln_w, ln_b, ffn_w_gu, ffn_w_down, router_w, router_b, moe_w_gu, moe_w_down):
    seq, d = x.shape[1], x.shape[2]
    depth = w_in.shape[0]
    xt = jnp.concatenate([ctx[0], x[0]], axis=0)

    cvec = jnp.zeros((SUBLANES, d), F32).at[0].set(c_ctx).at[1].set(c[0])
    mods = _mod_call(cvec, w_ada, b_ada)[:, :2].reshape(depth, 2, 6, d)
    mods = jnp.pad(mods, ((0, 0), (0, 0), (0, SUBLANES - 6), (0, 0)))

    gate_bias = jnp.pad(mlstm_gate_b.reshape(depth, 1, -1), ((0, 0), (0, 0), (0, N_GATE - 4 * MLSTM_HEADS)))
    gup_pad = jnp.zeros((depth, 2, N_GATE, GLA_KEY_WIDTH), F32)
    for dr in range(2):
        lo = GLA_GATE_OFF + dr * GLA_RANK
        gup_pad = gup_pad.at[:, dr, lo:lo + GLA_RANK, :].set(gla_gate_up[:, dr])
    cos_t, sin_t = _rope_tables(seq)
    tri = [_block_tri(0), _block_tri(1)]
    su = jnp.asarray(np.arange(TB)[:, None] < np.arange(TB)[None, :], BF16)
    hh = np.arange(MLSTM_WIDTH) // MLSTM_DH
    avg = jnp.asarray((hh[:, None] == hh[None, :]) / MLSTM_DH, BF16)
    ffn_gu, ffn_dn = ffn_w_gu.astype(BF16), ffn_w_down.astype(BF16)
    n_exp, _, n_gu = moe_w_gu.shape[1:]
    cast_src = {2 * m: moe_w_gu[m].reshape(n_exp * d, n_gu) for m in range(moe_w_gu.shape[0])}
    cast_src.update({2 * m + 1: moe_w_down[m].reshape(n_exp * (n_gu // 2), d) for m in range(moe_w_down.shape[0])})
    cast_dst = {}

    expand = []
    for dr in range(2):
        e = np.zeros((N_GATE, MLSTM_WIDTH), np.float32)
        for h in range(MLSTM_HEADS):
            e[dr * 2 * MLSTM_HEADS + h, h * MLSTM_DH:(h + 1) * MLSTM_DH] = 1.0
        expand.append(jnp.asarray(e, BF16))

    for l in range(depth):
        is_moe = l % 2 == 1
        last = l == depth - 1
        main, gates = _inproj_call(xt, mods[l], w_in, l, cos_t, sin_t)
        mf, mb, gf, gbk, *cast = _scans_call(main, gates, gate_bias[l], tri, expand, gup_pad[l], gla_gate_b[l],
                                             cast_src.get(l))
        if cast:
            cast_dst[l] = cast[0]
        attn = _attn_call(main, attn_sink[l], seq)
        x1, h2 = _outproj_call(mf, mb, main, attn, gf, gbk, mlstm_norm_w[l:l + 1], gla_norm_w[l:l + 1], avg,
                               w_out, l, xt, mods[l], ln_w[l, 0:1], ln_b[l, 0:1], F32 if is_moe else BF16)
        if is_moe:
            moe_gu = cast_dst[l - 1].reshape(n_exp, d, n_gu)
            moe_dn = cast_dst[l].reshape(n_exp, n_gu // 2, d)
            xt = _moe_layer(h2, x1, mods[l], ln_w[l, 1:2], ln_b[l, 1:2], router_w[l // 2], router_b[l // 2],
                            moe_gu, moe_dn, su, latent_only=last)
        else:
            xt = _ffn_call(h2, ffn_gu, ffn_dn, l // 2, x1, mods[l], ln_w[l, 1:2], ln_b[l, 1:2])
    return (xt if depth % 2 == 0 else xt[CTX_LEN:])[None]
```

```python
import functools

import jax
import jax.numpy as jnp
import numpy as np
from jax import lax
from jax.experimental import pallas as pl
from jax.experimental.pallas import tpu as pltpu

F32 = jnp.float32
BF16 = jnp.bfloat16
HIGHEST = lax.Precision.HIGHEST

D_MODEL = 1024
SEQ = 16384
DEPTH = 4
GRID_W = 64
CTX_LEN = 256
MLSTM_HEADS = 4
MLSTM_DH = 64
MLSTM_WIDTH = 256
HEAD_DIM = 64
ATTN_HEADS = 8
ATTN_KV_HEADS = 2
ATTN_Q_WIDTH = 512
ATTN_KV_WIDTH = 128
WINDOW = 128
ROPE_BASE = 10000.0
ROPE_PAIRS = 16
GLA_HEADS = 4
GLA_DK = 32
GLA_DV = 64
GLA_WIDTH = 256
GLA_KEY_WIDTH = 128
GLA_RANK = 16
GLA_TAU = 16.0
CHUNK = 64
D_FF = 2816
N_EXPERTS = 8
D_EXPERT = 3584
DEEPNORM_ALPHA = (2 * DEPTH) ** 0.25
LN_EPS = 1e-5

LANES = 128
SUBLANES = 8
VMEM_LIMIT = 56 * 1024 * 1024

TB = 256
N_CHUNKS = TB // CHUNK
PROJ_TM = 640
FFN_TM = 640
MOE_G = 512
MOE_TF = 1792

C_MQ, C_MK, C_MV, C_MO = 0, 256, 512, 768
C_AQ, C_AK, C_AV = 1024, 1536, 1664
C_GQ, C_GK, C_GV, C_GO = 1792, 1920, 2048, 2304
N_MAIN = 2560
N_GATE = 128
N_PROJ = N_MAIN + N_GATE
GLA_GATE_OFF = 16


def _cparams(n_axes=1):
    return pltpu.CompilerParams(dimension_semantics=("arbitrary",) * n_axes,
                                vmem_limit_bytes=VMEM_LIMIT)


def _sigmoid(x):
    return 1.0 / (1.0 + jnp.exp(-x))


def _log_sigmoid(x):
    return jnp.minimum(x, 0.0) - jnp.log(1.0 + jnp.exp(-jnp.abs(x)))


def _split3(x):
    hi = x.astype(BF16)
    r1 = x - hi.astype(F32)
    mid = r1.astype(BF16)
    lo = (r1 - mid.astype(F32)).astype(BF16)
    return hi, mid, lo


def _dot_exact_rhs(x, m_bf16):
    hi, mid, lo = _split3(x)
    d = lambda a: jnp.dot(a, m_bf16, preferred_element_type=F32)
    return d(hi) + d(mid) + d(lo)


def _dot_exact_lhs(m_bf16, x):
    hi, mid, lo = _split3(x)
    d = lambda a: jnp.dot(m_bf16, a, preferred_element_type=F32)
    return d(hi) + d(mid) + d(lo)


def _dot_bf16x3(a, b):
    a_hi, b_hi = a.astype(BF16), b.astype(BF16)
    a_lo = (a - a_hi.astype(F32)).astype(BF16)
    b_lo = (b - b_hi.astype(F32)).astype(BF16)
    d = lambda x, y: jnp.dot(x, y, preferred_element_type=F32)
    return d(a_hi, b_hi) + d(a_hi, b_lo) + d(a_lo, b_hi)


def _layer_norm(v, w, b):
    mu = jnp.mean(v, axis=-1, keepdims=True)
    cen = v - mu
    var = jnp.mean(cen * cen, axis=-1, keepdims=True)
    return cen * lax.rsqrt(var + LN_EPS) * w + b


def _mod_kernel(c_ref, w_ref, b_ref, o_ref):
    c = c_ref[...]
    sc = c * _sigmoid(c)
    o_ref[...] = jnp.dot(sc, w_ref[...], preferred_element_type=F32, precision=HIGHEST) + b_ref[...]


def _mod_call(cvec, w_ada, b_ada):
    depth, d, n = w_ada.shape
    tn = 1536
    return pl.pallas_call(
        _mod_kernel,
        grid=(depth, n // tn),
        in_specs=[pl.BlockSpec((SUBLANES, d), lambda l, j: (0, 0)),
                  pl.BlockSpec((None, d, tn), lambda l, j: (l, 0, j)),
                  pl.BlockSpec((None, 1, tn), lambda l, j: (l, 0, j))],
        out_specs=pl.BlockSpec((None, SUBLANES, tn), lambda l, j: (l, 0, j)),
        out_shape=jax.ShapeDtypeStruct((depth, SUBLANES, n), F32),
        compiler_params=_cparams(2),
        name="ada_mod",
    )(cvec, w_ada, b_ada.reshape(depth, 1, n))


def _who(i):
    return jnp.minimum(i, 1)


def _rope(x, cos, sin_signed, first_half):
    swapped = jnp.where(first_half, pltpu.roll(x, LANES - ROPE_PAIRS, 1), pltpu.roll(x, ROPE_PAIRS, 1))
    return x * cos + swapped * sin_signed


def _inproj_kernel(x_ref, mod_ref, w_ref, cos_ref, sin_ref, main_ref, gate_ref, wbf_scr):
    i = pl.program_id(0)
    tm = x_ref.shape[0]

    @pl.when(i == 0)
    def _():
        for dst, src, n in _column_runs(_in_col_perm()):
            wbf_scr[:, dst:dst + n] = w_ref[:, src:src + n].astype(BF16)
        wbf_scr[:, w_ref.shape[1]:] = jnp.zeros((w_ref.shape[0], N_PROJ - w_ref.shape[1]), BF16)

    h = x_ref[...] * (1.0 + _select_mod(mod_ref, 1, i, tm)) + _select_mod(mod_ref, 0, i, tm)
    p = jnp.dot(h.astype(BF16), wbf_scr[...], preferred_element_type=F32)
    cos, sin = cos_ref[...], sin_ref[...]
    lane = lax.broadcasted_iota(jnp.int32, (tm, LANES), 1)
    first_half = (lane % (2 * ROPE_PAIRS)) < ROPE_PAIRS
    main_ref[:, :C_AQ] = p[:, :C_AQ].astype(BF16)
    for j in range(ATTN_Q_WIDTH // LANES):
        lo = C_AQ + j * LANES
        main_ref[:, lo:lo + LANES] = (_rope(p[:, lo:lo + LANES], cos, sin, first_half) * HEAD_DIM ** -0.5).astype(BF16)
    main_ref[:, C_AK:C_AV] = _rope(p[:, C_AK:C_AV], cos, sin, first_half).astype(BF16)
    main_ref[:, C_AV:] = p[:, C_AV:N_MAIN].astype(BF16)
    gate_ref[...] = p[:, N_MAIN:]


def _inproj_call(x, mod, w_in, layer, cos_t, sin_t):
    t, d = x.shape
    n_in = w_in.shape[2]
    tm = PROJ_TM
    return pl.pallas_call(
        _inproj_kernel,
        grid=(t // tm,),
        in_specs=[pl.BlockSpec((tm, d), lambda i: (i, 0)),
                  pl.BlockSpec((2, SUBLANES, d), lambda i: (0, 0, 0)),
                  pl.BlockSpec((None, d, n_in), lambda i: (layer, 0, 0), pipeline_mode=pl.Buffered(1)),
                  pl.BlockSpec((tm, LANES), lambda i: (i, 0)),
                  pl.BlockSpec((tm, LANES), lambda i: (i, 0))],
        out_specs=[pl.BlockSpec((tm, N_MAIN), lambda i: (i, 0)),
                   pl.BlockSpec((tm, N_GATE), lambda i: (i, 0))],
        out_shape=[jax.ShapeDtypeStruct((t, N_MAIN), BF16),
                   jax.ShapeDtypeStruct((t, N_GATE), F32)],
        scratch_shapes=[pltpu.VMEM((d, N_PROJ), BF16)],
        compiler_params=_cparams(1),
        name="in_proj",
    )(x, mod, w_in, cos_t, sin_t)


def _scan_block(i, nb, direction):
    if direction == 0:
        return i
    return jnp.where(i == 0, 0, nb - i)


def _scans_kernel(*refs, cast_weights):
    (mq_f, mk_f, mv_f, gq_f, gk_f, gv_f, g_f, mq_b, mk_b, mv_b, gq_b, gk_b, gv_b, g_b,
     gbias, tri_f, tri_b, exp_f, exp_b, gup_f, gup_b, glb_f, glb_b) = refs[:23]
    refs = refs[23:]
    if cast_weights:
        w_src, refs = refs[0], refs[1:]
    om_f, om_b, og_f, og_b = refs[:4]
    refs = refs[4:]
    if cast_weights:
        w_dst, refs = refs[0], refs[1:]
    c_f, n_f, m_f, c_b, n_b, m_b, s_f, s_b = refs

    @pl.when(pl.program_id(0) == 0)
    def _():
        for scr in (c_f, n_f, m_f, c_b, n_b, m_b, s_f, s_b):
            scr[...] = jnp.zeros_like(scr)

    streams = [
        _mlstm_phases(mq_f, mk_f, mv_f, g_f, gbias, tri_f, exp_f, om_f, c_f, n_f, m_f, 0),
        _mlstm_phases(mq_b, mk_b, mv_b, g_b, gbias, tri_b, exp_b, om_b, c_b, n_b, m_b, 1),
        _gla_phases(gq_f, gk_f, gv_f, g_f, gup_f, glb_f, tri_f, og_f, s_f, 0),
        _gla_phases(gq_b, gk_b, gv_b, g_b, gup_b, glb_b, tri_b, og_b, s_b, 1),
    ]
    if cast_weights:
        rows = w_src.shape[0] // CAST_PIECES

        def cast_piece(k):
            def run():
                w_dst[k * rows:(k + 1) * rows, :] = w_src[k * rows:(k + 1) * rows, :].astype(BF16)
            return run
        streams.append([cast_piece(k) for k in range(CAST_PIECES)])
    _run_interleaved(streams)


def _run_interleaved(streams):
    for k in range(max(len(s) for s in streams)):
        for s in streams:
            if k < len(s):
                s[k]()


def _mlstm_phases(q_ref, k_ref, v_ref, g_ref, gb_ref, tri_ref, exp_ref, o_ref, c_scr, n_scr, m_scr, direction):
    fwd = direction == 0
    neg_inf = jnp.float32(-jnp.inf)
    order = list(range(N_CHUNKS)) if fwd else list(range(N_CHUNKS - 1, -1, -1))
    pairs = list(range(MLSTM_HEADS // 2))
    inst = [(p, c) for p in pairs for c in order]
    last_of = lambda c: c * CHUNK + (CHUNK - 1 if fwd else 0)
    ts_of = lambda c: slice(c * CHUNK, (c + 1) * CHUNK)
    ts2_of = lambda c: slice((c // 2) * 2 * CHUNK, (c // 2 + 1) * 2 * CHUNK)
    lp_of = lambda p: slice(p * LANES, (p + 1) * LANES)
    dn_nt = (((1,), (1,)), ((), ()))
    st = {}

    lane = lax.broadcasted_iota(jnp.int32, (CHUNK, LANES), 1)
    trow = lax.broadcasted_iota(jnp.int32, (CHUNK, LANES), 0)
    low = lane < MLSTM_DH
    tri_ok = (lane % MLSTM_DH <= trow) if fwd else (lane % MLSTM_DH >= trow)
    lane2 = lax.broadcasted_iota(jnp.int32, (LANES, LANES), 1)
    row2 = lax.broadcasted_iota(jnp.int32, (LANES, LANES), 0)
    same_head = (lane2 < MLSTM_DH) == (row2 < MLSTM_DH)
    bd_ones = jnp.where(same_head, 1.0, 0.0).astype(BF16)
    lane_row = lax.broadcasted_iota(jnp.int32, (1, LANES), 1)
    pair_half = lax.broadcasted_iota(jnp.int32, (2 * CHUNK, LANES), 0) // CHUNK

    def pair_blockdiag(x):
        zero = jnp.zeros_like(x)
        return jnp.concatenate([jnp.where(low, x, zero), jnp.where(low, zero, x)], axis=0)

    def ph_cumsum():
        st['g'] = g_ref[...] + gb_ref[...]
        st['b_f'] = _dot_exact_lhs(tri_ref[...], _log_sigmoid(st['g']))

    def ph_gates():
        b_i = pltpu.roll(st['b_f'], LANES - MLSTM_HEADS, 1)
        gr = st['g'] - b_i
        tok = lax.broadcasted_iota(jnp.int32, (TB, LANES), 0) % CHUNK
        aloc = gr
        for sh in (1, 2, 4, 8, 16, 32):
            if fwd:
                shifted, ok = pltpu.roll(aloc, sh, 0), tok >= sh
            else:
                shifted, ok = pltpu.roll(aloc, TB - sh, 0), tok < CHUNK - sh
            aloc = jnp.maximum(aloc, jnp.where(ok, shifted, neg_inf))
        chunk_max = jnp.concatenate(
            [jnp.broadcast_to(aloc[last_of(c):last_of(c) + 1], (CHUNK, LANES)) for c in range(N_CHUNKS)], axis=0)
        st['b_i'], st['aloc'] = b_i, aloc
        st['wloc'] = jnp.exp(gr - chunk_max)
        st['gr_t'] = gr.T

    def ph_expand():
        expand = exp_ref[...]
        st['aloc_x'] = _dot_exact_rhs(st['aloc'], expand)
        st['b_x'] = _dot_exact_rhs(st['b_i'], expand)
        st['wloc_x'] = _dot_exact_rhs(st['wloc'], expand)

    def ph_scores():
        for p, c in inst:
            kp = k_ref[ts_of(c), lp_of(p)] * MLSTM_DH ** -0.5
            st['s', p, c] = lax.dot_general(q_ref[ts_of(c), lp_of(p)], pair_blockdiag(kp), dn_nt,
                                            preferred_element_type=F32)
        for p in pairs:
            for c2 in range(N_CHUNKS // 2):
                st['kt2', p, c2] = (k_ref[ts2_of(2 * c2), lp_of(p)].astype(F32) * MLSTM_DH ** -0.5).T.astype(BF16)

    def ph_weights():
        for p, c in inst:
            ji0 = direction * 2 * MLSTM_HEADS + 2 * p
            ra = st['gr_t'][ji0:ji0 + 1, ts2_of(c)]
            rb = st['gr_t'][ji0 + 1:ji0 + 2, ts2_of(c)]
            if c % 2 == 0:
                g_row = jnp.where(lane_row < MLSTM_DH, ra, pltpu.roll(rb, MLSTM_DH, 1))
            else:
                g_row = jnp.where(lane_row < MLSTM_DH, pltpu.roll(ra, MLSTM_DH, 1), rb)
            al = st['aloc_x'][ts_of(c), lp_of(p)]
            s = st['s', p, c] * jnp.exp(jnp.where(tri_ok, g_row - al, neg_inf))
            s_hi = s.astype(BF16)
            st['s_hi', p, c] = s_hi
            st['s_lo', p, c] = (s - s_hi.astype(F32)).astype(BF16)
            wx2 = st['wloc_x'][ts2_of(c), lp_of(p)]
            in_chunk = pair_half == (c % 2)
            wv = jnp.where(in_chunk, wx2 * v_ref[ts2_of(c), lp_of(p)].astype(F32), 0.0)
            ww = jnp.where(in_chunk, wx2, 0.0)
            st['wvw', p, c] = jnp.concatenate([wv, ww], axis=1).astype(BF16)

    def ph_intra():
        for p, c in inst:
            vp = v_ref[ts_of(c), lp_of(p)]
            st['r1', p, c] = jnp.dot(st['s_hi', p, c], jnp.concatenate([pair_blockdiag(vp), bd_ones], axis=1),
                                     preferred_element_type=F32)
            st['den_lo', p, c] = jnp.dot(st['s_lo', p, c], bd_ones, preferred_element_type=F32)
            st['upd', p, c] = jnp.dot(st['kt2', p, c // 2], st['wvw', p, c], preferred_element_type=F32)

    def ph_state():
        for p in pairs:
            bdc, bdn = c_scr[p], n_scr[p]
            m_row = m_scr[p][0:1]
            for c in order:
                st['cn', p, c] = jnp.concatenate([bdc, bdn], axis=1).astype(BF16)
                st['m', p, c] = m_row
                last = last_of(c)
                aloc_last = st['aloc_x'][last:last + 1, lp_of(p)]
                a_last = jnp.maximum(m_row, aloc_last)
                decay = jnp.exp(m_row - a_last)
                scale = jnp.exp(aloc_last - a_last)
                upd = st['upd', p, c]
                bdc = decay * bdc + scale * jnp.where(same_head, upd[:, :LANES], 0.0)
                bdn = decay * bdn + scale * jnp.where(same_head, upd[:, LANES:], 0.0)
                m_row = st['b_x'][last:last + 1, lp_of(p)] + a_last
            c_scr[p] = bdc
            n_scr[p] = bdn
            m_scr[p] = jnp.broadcast_to(m_row, (SUBLANES, LANES))

    def ph_inter():
        for p, c in inst:
            st['r2', p, c] = jnp.dot(q_ref[ts_of(c), lp_of(p)], st['cn', p, c],
                                     preferred_element_type=F32)

    def ph_out():
        for p, c in inst:
            al = st['aloc_x'][ts_of(c), lp_of(p)]
            m_row, r1, r2 = st['m', p, c], st['r1', p, c], st['r2', p, c]
            a = jnp.maximum(m_row, al)
            corr = jnp.exp(al - a)
            w_inter = jnp.exp(m_row - a)
            num = corr * r1[:, :LANES] + w_inter * r2[:, :LANES]
            den = corr * (r1[:, LANES:] + st['den_lo', p, c]) + w_inter * r2[:, LANES:]
            o_ref[ts_of(c), lp_of(p)] = num / jnp.maximum(jnp.abs(den),
                                                          jnp.exp(-(st['b_x'][ts_of(c), lp_of(p)] + a)))

    return [ph_cumsum, ph_gates, ph_expand, ph_scores, ph_weights, ph_intra, ph_state, ph_inter, ph_out]


def _gla_phases(q_ref, k_ref, v_ref, g_ref, gup_ref, gb_ref, tri_ref, o_ref, s_scr, direction):
    fwd = direction == 0
    order = list(range(N_CHUNKS)) if fwd else list(range(N_CHUNKS - 1, -1, -1))
    ts_of = lambda c: slice(c * CHUNK, (c + 1) * CHUNK)
    ts2_of = lambda c: slice((c // 2) * 2 * CHUNK, (c // 2 + 1) * 2 * CHUNK)
    last_of = lambda c: c * CHUNK + (CHUNK - 1 if fwd else 0)
    dn_nt = (((1,), (1,)), ((), ()))
    st = {}

    khead = lax.broadcasted_iota(jnp.int32, (CHUNK, GLA_KEY_WIDTH), 1) // GLA_DK
    vlane = lax.broadcasted_iota(jnp.int32, (CHUNK, GLA_WIDTH), 1)
    vhead = vlane // GLA_DV
    trow = lax.broadcasted_iota(jnp.int32, (CHUNK, GLA_WIDTH), 0)
    tri_ok = (vlane % CHUNK <= trow) if fwd else (vlane % CHUNK >= trow)
    st_head_r = lax.broadcasted_iota(jnp.int32, (GLA_WIDTH, GLA_KEY_WIDTH), 0) // GLA_DV
    st_head_c = lax.broadcasted_iota(jnp.int32, (GLA_WIDTH, GLA_KEY_WIDTH), 1) // GLA_DK
    same_head = st_head_r == st_head_c
    pair_row = lax.broadcasted_iota(jnp.int32, (2 * CHUNK, GLA_KEY_WIDTH), 0) // CHUNK

    def stack_heads(x, head_of_lane):
        zero = jnp.zeros_like(x)
        return jnp.concatenate([jnp.where(head_of_lane == h, x, zero) for h in range(GLA_HEADS)], axis=0)

    def ph_gate_proj():
        st['z'] = _dot_bf16x3(g_ref[...], gup_ref[...]) + gb_ref[...]

    def ph_log_decay():
        st['lg'] = _log_sigmoid(st['z']) * (1.0 / GLA_TAU)

    def ph_cumsum():
        st['b'] = _dot_exact_lhs(tri_ref[...], st['lg'])

    def ph_prep():
        b = st['b']
        st['k'] = k_ref[...].astype(F32)
        st['qt'] = (q_ref[...].astype(F32) * GLA_DK ** -0.5 * jnp.exp(b)).astype(BF16)
        st['kt'] = (st['k'] * jnp.exp(-b)).astype(BF16)
        st['v_t'] = v_ref[...].astype(F32).T.astype(BF16)

    def ph_scores():
        for c in order:
            st['a', c] = lax.dot_general(st['qt'][ts_of(c)], stack_heads(st['kt'][ts_of(c)], khead), dn_nt,
                                         preferred_element_type=F32)

    def ph_mask():
        b, k = st['b'], st['k']
        for c in order:
            st['a', c] = jnp.where(tri_ok, st['a', c], 0.0).astype(BF16)
            last = last_of(c)
            st['kdec', c] = jnp.where(pair_row == c % 2, k[ts2_of(c)] * jnp.exp(b[last:last + 1] - b[ts2_of(c)]),
                                      0.0).astype(BF16)

    def ph_intra():
        for c in order:
            st['o', c] = jnp.dot(st['a', c], stack_heads(v_ref[ts_of(c), :], vhead), preferred_element_type=F32)
            st['upd', c] = jnp.dot(st['v_t'][:, ts2_of(c)], st['kdec', c], preferred_element_type=F32)

    def ph_state():
        s_t = s_scr[...]
        for c in order:
            st['s', c] = s_t.astype(BF16)
            last = last_of(c)
            s_t = jnp.exp(st['b'][last:last + 1]) * s_t + jnp.where(same_head, st['upd', c], 0.0)
        s_scr[...] = s_t

    def ph_inter():
        for c in order:
            st['oi', c] = lax.dot_general(st['qt'][ts_of(c)], st['s', c], dn_nt, preferred_element_type=F32)

    def ph_out():
        for c in order:
            o_ref[ts_of(c), :] = st['o', c] + st['oi', c]

    return [ph_gate_proj, ph_log_decay, ph_cumsum, ph_prep, ph_scores, ph_mask, ph_intra, ph_state, ph_inter, ph_out]


def _scans_call(main, gates, gate_bias, tri, expand, gup_pad, gla_b, w_cast=None):
    t = main.shape[0]
    nb = t // TB
    const = lambda i: (0, 0)
    extra_in, extra_out, extra_shape, extra_args = [], [], [], []
    if w_cast is not None:
        w_all, part, n_parts = w_cast
        rows, cols = w_all.shape[0] // n_parts, w_all.shape[1]
        blk_rows = rows // CAST_STEPS
        step = lambda i: jnp.minimum(i, CAST_STEPS - 1)
        extra_in = [pl.BlockSpec((blk_rows, cols), lambda i: (part * CAST_STEPS + step(i), 0))]
        extra_out = [pl.BlockSpec((blk_rows, cols), lambda i: (step(i), 0))]
        extra_shape, extra_args = [jax.ShapeDtypeStruct((rows, cols), BF16)], [w_all]

    def streams(direction):
        blk = lambda i: _scan_block(i, nb, direction)
        col = lambda c, w: (lambda i: (blk(i), c // w))
        return [pl.BlockSpec((TB, MLSTM_WIDTH), col(C_MQ, MLSTM_WIDTH)),
                pl.BlockSpec((TB, MLSTM_WIDTH), col(C_MK, MLSTM_WIDTH)),
                pl.BlockSpec((TB, MLSTM_WIDTH), col(C_MV, MLSTM_WIDTH)),
                pl.BlockSpec((TB, GLA_KEY_WIDTH), col(C_GQ, GLA_KEY_WIDTH)),
                pl.BlockSpec((TB, GLA_KEY_WIDTH), col(C_GK, GLA_KEY_WIDTH)),
                pl.BlockSpec((TB, GLA_WIDTH), col(C_GV, GLA_WIDTH)),
                pl.BlockSpec((TB, N_GATE), lambda i: (blk(i), 0))]

    consts = [pl.BlockSpec((1, N_GATE), const),
              pl.BlockSpec((TB, TB), const), pl.BlockSpec((TB, TB), const),
              pl.BlockSpec((N_GATE, MLSTM_WIDTH), const), pl.BlockSpec((N_GATE, MLSTM_WIDTH), const),
              pl.BlockSpec((N_GATE, GLA_KEY_WIDTH), const), pl.BlockSpec((N_GATE, GLA_KEY_WIDTH), const),
              pl.BlockSpec((1, GLA_KEY_WIDTH), const), pl.BlockSpec((1, GLA_KEY_WIDTH), const)]
    out_spec = lambda direction, w: pl.BlockSpec((TB, w), lambda i: (_scan_block(i, nb, direction), 0))
    n_pairs = MLSTM_HEADS // 2
    mlstm_state = [pltpu.VMEM((n_pairs, LANES, LANES), F32), pltpu.VMEM((n_pairs, LANES, LANES), F32),
                   pltpu.VMEM((n_pairs, SUBLANES, LANES), F32)]
    gla_state = [pltpu.VMEM((GLA_WIDTH, GLA_KEY_WIDTH), F32)]
    return pl.pallas_call(
        functools.partial(_scans_kernel, cast_weights=w_cast is not None),
        grid=(nb,),
        in_specs=streams(0) + streams(1) + consts + extra_in,
        out_specs=[out_spec(0, MLSTM_WIDTH), out_spec(1, MLSTM_WIDTH), out_spec(0, GLA_WIDTH),
                   out_spec(1, GLA_WIDTH)] + extra_out,
        out_shape=([jax.ShapeDtypeStruct((t, MLSTM_WIDTH), F32)] * 2 + [jax.ShapeDtypeStruct((t, GLA_WIDTH), F32)] * 2
                   + extra_shape),
        scratch_shapes=mlstm_state + mlstm_state + gla_state + gla_state,
        compiler_params=_cparams(1),
        name="scans",
    )(*([main] * 6 + [gates]) * 2, gate_bias, tri[0], tri[1], expand[0], expand[1],
      gup_pad[0], gup_pad[1], gla_b[0:1], gla_b[1:2], *extra_args)


def _attn_kernel(sink_ref, q_ref, kp_ref, kc_ref, kn_ref, kx_ref, vp_ref, vc_ref, vn_ref, vx_ref, o_ref, *, seq):
    i = pl.program_id(0)
    half = TB // 2
    n_loc = 2 * TB
    keys = jnp.concatenate([kp_ref[...], kc_ref[...], kn_ref[...], kx_ref[...]], axis=0)
    vals = jnp.concatenate([vp_ref[...], vc_ref[...], vn_ref[...], vx_ref[...]], axis=0)
    n_keys = keys.shape[0]
    lane = lax.broadcasted_iota(jnp.int32, (n_keys, LANES), 1)
    zero = jnp.zeros_like(keys)
    keys_g = [jnp.where(lane < HEAD_DIM, keys, zero), jnp.where(lane >= HEAD_DIM, keys, zero)]
    r = lax.broadcasted_iota(jnp.int32, (TB, n_keys), 0)
    c = lax.broadcasted_iota(jnp.int32, (TB, n_keys), 1)
    rel = c - half - r
    kpos = (i - 1) * TB + c - half
    local_ok = (jnp.abs(rel) <= WINDOW) & (kpos >= 0) & (kpos < seq) & (i > 0)
    valid = local_ok | (c >= n_loc)
    out_lane = lax.broadcasted_iota(jnp.int32, (TB, LANES), 1)
    neg_inf = jnp.float32(-jnp.inf)
    dn = (((1,), (1,)), ((), ()))
    heads = [(m, gidx) for m in range(ATTN_Q_WIDTH // LANES) for gidx in range(ATTN_KV_HEADS)]
    scores = [lax.dot_general(q_ref[:, m * LANES:(m + 1) * LANES], keys_g[gidx], dn, preferred_element_type=F32)
              for m, gidx in heads]
    probs, denoms = [], []
    for (m, gidx), s in zip(heads, scores):
        sink = sink_ref[gidx * (ATTN_HEADS // ATTN_KV_HEADS) + m]
        s = jnp.where(valid, s, neg_inf)
        mx = jnp.maximum(jnp.max(s, axis=1, keepdims=True), sink)
        p = jnp.exp(s - mx)
        denoms.append(jnp.sum(p, axis=1, keepdims=True) + jnp.exp(sink - mx))
        probs.append(p.astype(BF16))
    pvs = [jnp.dot(p, vals, preferred_element_type=F32) for p in probs]
    for m in range(ATTN_Q_WIDTH // LANES):
        r0 = pvs[2 * m] / denoms[2 * m]
        r1 = pvs[2 * m + 1] / denoms[2 * m + 1]
        o_ref[:, m * LANES:(m + 1) * LANES] = jnp.where(out_lane < HEAD_DIM, r0, r1).astype(BF16)


def _attn_call(main, sink, seq):
    t = main.shape[0]
    nb = t // TB
    half = TB // 2
    nhb = t // half
    kcol, vcol = C_AK // LANES, C_AV // LANES
    prev_i = lambda i: jnp.maximum(2 * i - 1, 0)
    next_i = lambda i: jnp.minimum(2 * i + 2, nhb - 1)
    grid_spec = pltpu.PrefetchScalarGridSpec(
        num_scalar_prefetch=1,
        grid=(nb,),
        in_specs=[pl.BlockSpec((TB, ATTN_Q_WIDTH), lambda i, s: (i, C_AQ // ATTN_Q_WIDTH)),
                  pl.BlockSpec((half, LANES), lambda i, s: (prev_i(i), kcol)),
                  pl.BlockSpec((TB, LANES), lambda i, s: (i, kcol)),
                  pl.BlockSpec((half, LANES), lambda i, s: (next_i(i), kcol)),
                  pl.BlockSpec((TB, LANES), lambda i, s: (0, kcol)),
                  pl.BlockSpec((half, LANES), lambda i, s: (prev_i(i), vcol)),
                  pl.BlockSpec((TB, LANES), lambda i, s: (i, vcol)),
                  pl.BlockSpec((half, LANES), lambda i, s: (next_i(i), vcol)),
                  pl.BlockSpec((TB, LANES), lambda i, s: (0, vcol))],
        out_specs=pl.BlockSpec((TB, ATTN_Q_WIDTH), lambda i, s: (i, 0)),
    )
    return pl.pallas_call(
        functools.partial(_attn_kernel, seq=seq),
        grid_spec=grid_spec,
        out_shape=jax.ShapeDtypeStruct((t, ATTN_Q_WIDTH), BF16),
        compiler_params=_cparams(1),
        name="window_attn",
    )(sink, main, main, main, main, main, main, main, main, main)


def _group_mean(x, avg_bf16):
    hi = x.astype(BF16)
    lo = (x - hi.astype(F32)).astype(BF16)
    return (jnp.dot(hi, avg_bf16, preferred_element_type=F32) + jnp.dot(lo, avg_bf16, preferred_element_type=F32))


def _head_norm(x, avg_bf16, w):
    cen = x - _group_mean(x, avg_bf16)
    var = _group_mean(cen * cen, avg_bf16)
    return cen * lax.rsqrt(var + LN_EPS) * w


def _outproj_kernel(mf_ref, mb_ref, mo_ref, at_ref, gf_ref, gbk_ref, go_ref, mnw_ref, gnw_ref, avg_ref,
                    w_ref, x_ref, mod_ref, lnw_ref, lnb_ref, x1_ref, h2_ref, wbf_scr):
    i = pl.program_id(0)
    tm = x_ref.shape[0]

    @pl.when(i == 0)
    def _():
        for dst, src, n in _column_runs(_out_row_perm()):
            wbf_scr[dst:dst + n, :] = w_ref[src:src + n, :].astype(BF16)

    avg = avg_ref[...]
    hm = _head_norm(mf_ref[...] + mb_ref[...], avg, mnw_ref[...]) * _sigmoid(mo_ref[...].astype(F32))
    gate = go_ref[...].astype(F32)
    hg = _head_norm(gf_ref[...] + gbk_ref[...], avg, gnw_ref[...]) * (gate * _sigmoid(gate))
    a0, a1, a2 = MLSTM_WIDTH, MLSTM_WIDTH + ATTN_Q_WIDTH, D_MODEL
    mix = (jnp.dot(hm.astype(BF16), wbf_scr[:a0], preferred_element_type=F32)
           + jnp.dot(at_ref[...], wbf_scr[a0:a1], preferred_element_type=F32)
           + jnp.dot(hg.astype(BF16), wbf_scr[a1:a2], preferred_element_type=F32))
    x1 = _layer_norm(DEEPNORM_ALPHA * x_ref[...] + _select_mod(mod_ref, 2, i, tm) * mix, lnw_ref[...], lnb_ref[...])
    x1_ref[...] = x1
    h2_ref[...] = (x1 * (1.0 + _select_mod(mod_ref, 4, i, tm)) + _select_mod(mod_ref, 3, i, tm)).astype(h2_ref.dtype)


def _outproj_call(mf, mb, main, attn, gf, gbk, mnw, gnw, avg, w_out, layer, x, mod, lnw, lnb, h2_dtype):
    t, d = x.shape
    tm = PROJ_TM
    row = lambda i: (i, 0)
    const = lambda i: (0, 0)
    return pl.pallas_call(
        _outproj_kernel,
        grid=(t // tm,),
        in_specs=[pl.BlockSpec((tm, MLSTM_WIDTH), row),
                  pl.BlockSpec((tm, MLSTM_WIDTH), row),
                  pl.BlockSpec((tm, MLSTM_WIDTH), lambda i: (i, C_MO // MLSTM_WIDTH)),
                  pl.BlockSpec((tm, ATTN_Q_WIDTH), row),
                  pl.BlockSpec((tm, GLA_WIDTH), row),
                  pl.BlockSpec((tm, GLA_WIDTH), row),
                  pl.BlockSpec((tm, GLA_WIDTH), lambda i: (i, C_GO // GLA_WIDTH)),
                  pl.BlockSpec((1, MLSTM_WIDTH), const),
                  pl.BlockSpec((1, GLA_WIDTH), const),
                  pl.BlockSpec((MLSTM_WIDTH, MLSTM_WIDTH), const),
                  pl.BlockSpec((None, d, d), lambda i: (layer, 0, 0), pipeline_mode=pl.Buffered(1)),
                  pl.BlockSpec((tm, d), row),
                  pl.BlockSpec((2, SUBLANES, d), lambda i: (0, 0, 0)),
                  pl.BlockSpec((1, d), const),
                  pl.BlockSpec((1, d), const)],
        out_specs=[pl.BlockSpec((tm, d), row), pl.BlockSpec((tm, d), row)],
        out_shape=[jax.ShapeDtypeStruct((t, d), F32), jax.ShapeDtypeStruct((t, d), h2_dtype)],
        scratch_shapes=[pltpu.VMEM((d, d), BF16)],
        compiler_params=_cparams(1),
        name="out_proj",
    )(mf, mb, main, attn, gf, gbk, main, mnw, gnw, avg, w_out, x, mod, lnw, lnb)


def _select_mod(mod_ref, k, i, tm):
    rows = i * tm + lax.broadcasted_iota(jnp.int32, (tm, 1), 0)
    return jnp.where(rows < CTX_LEN, mod_ref[0, k:k + 1, :], mod_ref[1, k:k + 1, :])


def _ffn_kernel(h_ref, wgu_ref, wd_ref, x_ref, mod_ref, lnw_ref, lnb_ref, o_ref):
    h = h_ref[...]
    g = jnp.dot(h, wgu_ref[:, :D_FF], preferred_element_type=F32)
    u = jnp.dot(h, wgu_ref[:, D_FF:], preferred_element_type=F32)
    y = jnp.dot((g * _sigmoid(g) * u).astype(BF16), wd_ref[...], preferred_element_type=F32)
    m5 = _select_mod(mod_ref, 5, pl.program_id(0), h_ref.shape[0])
    o_ref[...] = _layer_norm(DEEPNORM_ALPHA * x_ref[...] + m5 * y, lnw_ref[...], lnb_ref[...])


def _ffn_call(h2, w_gu, w_down, layer, x1, mod, lnw, lnb):
    t, d = x1.shape
    resident = pl.Buffered(1)
    return pl.pallas_call(
        _ffn_kernel,
        grid=(t // FFN_TM,),
        in_specs=[pl.BlockSpec((FFN_TM, d), lambda i: (i, 0)),
                  pl.BlockSpec((None, d, 2 * D_FF), lambda i: (layer, 0, 0), pipeline_mode=resident),
                  pl.BlockSpec((None, D_FF, d), lambda i: (layer, 0, 0), pipeline_mode=resident),
                  pl.BlockSpec((FFN_TM, d), lambda i: (i, 0)),
                  pl.BlockSpec((2, SUBLANES, d), lambda i: (0, 0, 0)),
                  pl.BlockSpec((1, d), lambda i: (0, 0)),
                  pl.BlockSpec((1, d), lambda i: (0, 0))],
        out_specs=pl.BlockSpec((FFN_TM, d), lambda i: (i, 0)),
        out_shape=jax.ShapeDtypeStruct((t, d), F32),
        compiler_params=_cparams(1),
        name="dense_ffn",
    )(h2, w_gu, w_down, x1, mod, lnw, lnb)


def _router_kernel(h_ref, wr_ref, br_ref, su_ref, route_ref, cnt_ref, base_scr):
    @pl.when(pl.program_id(0) == 0)
    def _():
        base_scr[...] = jnp.zeros_like(base_scr)

    lt = lax.dot_general(wr_ref[...], h_ref[...], (((1,), (1,)), ((), ())),
                         preferred_element_type=F32, precision=HIGHEST) + br_ref[...]
    idx = lax.broadcasted_iota(jnp.int32, lt.shape, 0)
    neg_inf = jnp.float32(-jnp.inf)
    m1 = jnp.max(lt, axis=0, keepdims=True)
    e1 = jnp.min(jnp.where(lt == m1, idx, N_EXPERTS), axis=0, keepdims=True)
    lt2 = jnp.where(idx == e1, neg_inf, lt)
    m2 = jnp.max(lt2, axis=0, keepdims=True)
    e2 = jnp.min(jnp.where(lt2 == m2, idx, N_EXPERTS), axis=0, keepdims=True)
    t2 = jnp.exp(m2 - m1)
    w1 = 1.0 / (1.0 + t2)
    w2 = t2 / (1.0 + t2)
    su = su_ref[...]
    base = base_scr[...][:, 0:1]
    oh1 = (idx == e1).astype(F32)
    cum1 = jnp.dot(oh1.astype(BF16), su, preferred_element_type=F32)
    rank1 = jnp.sum(oh1 * (base + cum1), axis=0, keepdims=True)
    base = base + jnp.sum(oh1, axis=1, keepdims=True)
    oh2 = (idx == e2).astype(F32)
    cum2 = jnp.dot(oh2.astype(BF16), su, preferred_element_type=F32)
    rank2 = jnp.sum(oh2 * (base + cum2), axis=0, keepdims=True)
    base = base + jnp.sum(oh2, axis=1, keepdims=True)
    base_scr[...] = jnp.broadcast_to(base, base_scr.shape)
    cnt_ref[...] = jnp.broadcast_to(base, cnt_ref.shape)
    zero = jnp.zeros_like(w1)
    route_ref[...] = jnp.concatenate(
        [e1.astype(F32), e2.astype(F32), w1, w2, rank1, rank2, zero, zero], axis=0)


def _router_call(h2, wr_t, br, su):
    t, d = h2.shape
    nb = t // TB
    return pl.pallas_call(
        _router_kernel,
        grid=(nb,),
        in_specs=[pl.BlockSpec((TB, d), lambda i: (i, 0)),
                  pl.BlockSpec((N_EXPERTS, d), lambda i: (0, 0)),
                  pl.BlockSpec((N_EXPERTS, 1), lambda i: (0, 0)),
                  pl.BlockSpec((TB, TB), lambda i: (0, 0))],
        out_specs=[pl.BlockSpec((SUBLANES, TB), lambda i: (0, i)),
                   pl.BlockSpec((N_EXPERTS, LANES), lambda i: (0, 0))],
        out_shape=[jax.ShapeDtypeStruct((SUBLANES, t), F32),
                   jax.ShapeDtypeStruct((N_EXPERTS, LANES), F32)],
        scratch_shapes=[pltpu.VMEM((N_EXPERTS, LANES), F32)],
        compiler_params=_cparams(1),
        name="moe_router",
    )(h2, wr_t, br, su)


ROW_TILE = D_MODEL // LANES
DMA_UNROLL = 8
ZERO_CHUNK = 64
CAST_STEPS = 64
CAST_PIECES = 4


def _store_row_tiles(ref, val):
    n = val.shape[0]
    for j in range(ROW_TILE):
        ref[pl.ds(j, n, stride=ROW_TILE), :] = val[:, j * LANES:(j + 1) * LANES]


def _load_row_tiles(ref, n, dtype=F32):
    return jnp.concatenate([ref[pl.ds(j, n, stride=ROW_TILE), :].astype(dtype) for j in range(ROW_TILE)], axis=1)


def _row(ref, idx):
    return ref.at[pl.ds(pl.multiple_of(idx * ROW_TILE, ROW_TILE), ROW_TILE)]


def _dispatch_kernel(s1_ref, s2_ref, pad_lo_ref, pad_hi_ref, h_ref, xs_ref, hr_scr, zero_scr, sem, zsem):
    i = pl.program_id(0)

    @pl.when(i == 0)
    def _():
        zero_scr[...] = jnp.zeros_like(zero_scr)

        def zero_rows(s, n):
            dst = xs_ref.at[pl.ds(pl.multiple_of(s * ROW_TILE, ROW_TILE), n * ROW_TILE)]
            return pltpu.make_async_copy(zero_scr.at[pl.ds(0, n * ROW_TILE)], dst, zsem)

        def fill(e, act):
            lo, hi = pad_lo_ref[e], pad_hi_ref[e]
            n_big = (hi - lo) // ZERO_CHUNK
            lax.fori_loop(0, n_big, lambda k, c: (act(zero_rows(lo + k * ZERO_CHUNK, ZERO_CHUNK)), c)[1], 0)
            lax.fori_loop(lo + n_big * ZERO_CHUNK, hi, lambda s, c: (act(zero_rows(s, 1)), c)[1], 0)

        for e in range(N_EXPERTS + 1):
            fill(e, lambda cp: cp.start())
        for e in range(N_EXPERTS + 1):
            fill(e, lambda cp: cp.wait())

    n = pl.num_programs(0)
    cur = i % 2

    def copy(blk, r, slot_ref):
        buf = blk % 2
        return pltpu.make_async_copy(_row(hr_scr.at[buf], r), _row(xs_ref, slot_ref[blk * TB + r]), sem.at[buf])

    def start(r, carry):
        copy(i, r, s1_ref).start()
        copy(i, r, s2_ref).start()
        return carry

    def wait_block(blk):
        def wait(r, carry):
            copy(blk, r, s1_ref).wait()
            copy(blk, r, s2_ref).wait()
            return carry
        lax.fori_loop(0, TB, wait, 0, unroll=DMA_UNROLL)

    _store_row_tiles(hr_scr.at[cur], h_ref[...])
    lax.fori_loop(0, TB, start, 0, unroll=DMA_UNROLL)

    @pl.when(i > 0)
    def _():
        wait_block(i - 1)

    @pl.when(i == n - 1)
    def _():
        wait_block(i)


def _dispatch_call(slot1, slot2, pad_lo, pad_hi, h2, n_slots):
    t, d = h2.shape
    grid_spec = pltpu.PrefetchScalarGridSpec(
        num_scalar_prefetch=4,
        grid=(t // TB,),
        in_specs=[pl.BlockSpec((TB, d), lambda i, *_: (i, 0))],
        out_specs=pl.BlockSpec(memory_space=pl.ANY),
        scratch_shapes=[pltpu.VMEM((2, TB * ROW_TILE, LANES), F32), pltpu.VMEM((ZERO_CHUNK * ROW_TILE, LANES), F32),
                        pltpu.SemaphoreType.DMA((2,)), pltpu.SemaphoreType.DMA(())],
    )
    return pl.pallas_call(
        _dispatch_kernel,
        grid_spec=grid_spec,
        out_shape=jax.ShapeDtypeStruct((n_slots * ROW_TILE, LANES), F32),
        compiler_params=_cparams(1),
        name="moe_dispatch",
    )(slot1, slot2, pad_lo, pad_hi, h2)


def _expert_kernel(be_ref, used_ref, x_ref, wg_ref, wu_ref, wd_ref, y_ref, xb_scr, acc_scr, *, n_f):
    b, f = pl.program_id(0), pl.program_id(1)
    live = b < used_ref[0]
    last = n_f - 1

    def swiglu_part():
        x = xb_scr[...]
        g = jnp.dot(x, wg_ref[...], preferred_element_type=F32)
        u = jnp.dot(x, wu_ref[...], preferred_element_type=F32)
        return jnp.dot((g * _sigmoid(g) * u).astype(BF16), wd_ref[...], preferred_element_type=F32)

    @pl.when(live & (f == 0))
    def _():
        for j in range(ROW_TILE):
            xb_scr[:, j * LANES:(j + 1) * LANES] = x_ref[pl.ds(j, MOE_G, stride=ROW_TILE), :].astype(BF16)
        acc_scr[...] = swiglu_part()

    if n_f > 2:
        @pl.when(live & (f > 0) & (f < last))
        def _():
            acc_scr[...] += swiglu_part()

    @pl.when(live & (f == last))
    def _():
        _store_row_tiles(y_ref, acc_scr[...] + swiglu_part())

    @pl.when(jnp.logical_not(live) & (f == last))
    def _():
        y_ref[...] = jnp.zeros_like(y_ref)


def _expert_call(block_e, used, xs, w_gu, w_down):
    d = D_MODEL
    ns = xs.shape[0] // ROW_TILE
    nblk = ns // MOE_G
    nf = D_EXPERT // MOE_TF
    rows = MOE_G * ROW_TILE

    def f_eff(b, f, used_ref):
        return jnp.where(b < used_ref[0], f, nf - 1)

    grid_spec = pltpu.PrefetchScalarGridSpec(
        num_scalar_prefetch=2,
        grid=(nblk, nf),
        in_specs=[pl.BlockSpec((rows, LANES), lambda b, f, be, us: (b, 0)),
                  pl.BlockSpec((None, d, MOE_TF), lambda b, f, be, us: (be[b], 0, f_eff(b, f, us))),
                  pl.BlockSpec((None, d, MOE_TF), lambda b, f, be, us: (be[b], 0, nf + f_eff(b, f, us))),
                  pl.BlockSpec((None, MOE_TF, d), lambda b, f, be, us: (be[b], f_eff(b, f, us), 0))],
        out_specs=pl.BlockSpec((rows, LANES), lambda b, f, be, us: (b, 0)),
        scratch_shapes=[pltpu.VMEM((MOE_G, d), BF16), pltpu.VMEM((MOE_G, d), F32)],
    )
    return pl.pallas_call(
        functools.partial(_expert_kernel, n_f=nf),
        grid_spec=grid_spec,
        out_shape=jax.ShapeDtypeStruct(xs.shape, F32),
        compiler_params=_cparams(2),
        name="moe_experts",
    )(block_e, used, xs, w_gu, w_gu, w_down)


def _combine_kernel(s1_ref, s2_ref, yb_ref, route_ref, x_ref, mod_ref, lnw_ref, lnb_ref, o_ref, buf1, buf2, sem):
    i = pl.program_id(0)
    n = pl.num_programs(0)
    cur = i % 2

    def copy(blk, r, slot_ref, buf):
        b = blk % 2
        return pltpu.make_async_copy(_row(yb_ref, slot_ref[blk * TB + r]), _row(buf.at[b], r), sem.at[b])

    def start_block(blk):
        def start(r, carry):
            copy(blk, r, s1_ref, buf1).start()
            copy(blk, r, s2_ref, buf2).start()
            return carry
        lax.fori_loop(0, TB, start, 0, unroll=DMA_UNROLL)

    def wait(r, carry):
        copy(i, r, s1_ref, buf1).wait()
        copy(i, r, s2_ref, buf2).wait()
        return carry

    @pl.when(i == 0)
    def _():
        start_block(i)

    @pl.when(i + 1 < n)
    def _():
        start_block(i + 1)

    lax.fori_loop(0, TB, wait, 0, unroll=DMA_UNROLL)
    rt = route_ref[...].T
    y = rt[:, 2:3] * _load_row_tiles(buf1.at[cur], TB) + rt[:, 3:4] * _load_row_tiles(buf2.at[cur], TB)
    m = mod_ref[...]
    o_ref[...] = _layer_norm(DEEPNORM_ALPHA * x_ref[...] + m[5:6] * y, lnw_ref[...], lnb_ref[...])


def _combine_call(slot1, slot2, yb, route, x1, mod, lnw, lnb, latent_only):
    t, d = x1.shape
    if latent_only:
        out_rows, out_map = t - CTX_LEN, lambda i, a, b: (jnp.maximum(i - 1, 0), 0)
    else:
        out_rows, out_map = t, lambda i, a, b: (i, 0)
    grid_spec = pltpu.PrefetchScalarGridSpec(
        num_scalar_prefetch=2,
        grid=(t // TB,),
        in_specs=[pl.BlockSpec(memory_space=pl.ANY),
                  pl.BlockSpec((SUBLANES, TB), lambda i, a, b: (0, i)),
                  pl.BlockSpec((TB, d), lambda i, a, b: (i, 0)),
                  pl.BlockSpec((None, SUBLANES, d), lambda i, a, b: (_who(i), 0, 0)),
                  pl.BlockSpec((1, d), lambda i, a, b: (0, 0)),
                  pl.BlockSpec((1, d), lambda i, a, b: (0, 0))],
        out_specs=pl.BlockSpec((TB, d), out_map),
        scratch_shapes=[pltpu.VMEM((2, TB * ROW_TILE, LANES), F32), pltpu.VMEM((2, TB * ROW_TILE, LANES), F32),
                        pltpu.SemaphoreType.DMA((2,))],
    )
    return pl.pallas_call(
        _combine_kernel,
        grid_spec=grid_spec,
        out_shape=jax.ShapeDtypeStruct((out_rows, d), F32),
        compiler_params=_cparams(1),
        name="moe_combine",
    )(slot1, slot2, yb, route, x1, mod, lnw, lnb)


def _moe_layer(h2, x1, mod, lnw, lnb, w_router, b_router, w_gu, w_down, su, latent_only):
    t, d = h2.shape
    route, cnt = _router_call(h2, w_router.T, b_router.reshape(N_EXPERTS, 1), su)
    counts = cnt[:, 0].astype(jnp.int32)
    padded = (counts + MOE_G - 1) // MOE_G * MOE_G
    pend = jnp.cumsum(padded)
    pstart = pend - padded
    e1, e2 = route[0].astype(jnp.int32), route[1].astype(jnp.int32)
    slot1 = pstart[e1] + route[4].astype(jnp.int32)
    slot2 = pstart[e2] + route[5].astype(jnp.int32)
    nblk = -(-(2 * t) // MOE_G) + N_EXPERTS
    blk_start = jnp.arange(nblk, dtype=jnp.int32) * MOE_G
    block_e = jnp.minimum(jnp.sum((pend[None, :] <= blk_start[:, None]).astype(jnp.int32), axis=1), N_EXPERTS - 1)
    used = (pend[-1:] // MOE_G).astype(jnp.int32)
    n_slots = nblk * MOE_G
    pad_lo = jnp.concatenate([pstart + counts, pend[-1:]])
    pad_hi = jnp.concatenate([pend, jnp.full((1,), n_slots, jnp.int32)])
    xs = _dispatch_call(slot1, slot2, pad_lo, pad_hi, h2, n_slots)
    yb = _expert_call(block_e, used, xs, w_gu, w_down)
    return _combine_call(slot1, slot2, yb, route, x1, mod, lnw, lnb, latent_only)


def _attn_head_perm():
    cols = []
    for m in range(ATTN_HEADS // ATTN_KV_HEADS):
        for hq in (m, m + ATTN_HEADS // ATTN_KV_HEADS):
            cols.extend(range(hq * HEAD_DIM, (hq + 1) * HEAD_DIM))
    return np.asarray(cols, np.int32)


def _in_col_perm():
    n_m = 4 * MLSTM_WIDTH
    mg = 2 * 2 * MLSTM_HEADS
    a0 = n_m + mg
    perm = list(range(n_m))
    perm += [a0 + int(j) for j in _attn_head_perm()]
    perm += list(range(a0 + ATTN_Q_WIDTH, a0 + ATTN_Q_WIDTH + 2 * ATTN_KV_WIDTH))
    g0 = a0 + ATTN_Q_WIDTH + 2 * ATTN_KV_WIDTH
    perm += list(range(g0, g0 + 2 * GLA_KEY_WIDTH + 2 * GLA_WIDTH))
    perm += list(range(n_m, n_m + mg))
    perm += list(range(g0 + 2 * GLA_KEY_WIDTH + 2 * GLA_WIDTH, g0 + 2 * GLA_KEY_WIDTH + 2 * GLA_WIDTH + 2 * GLA_RANK))
    return np.asarray(perm, np.int32)


def _out_row_perm():
    return np.concatenate([np.arange(MLSTM_WIDTH), MLSTM_WIDTH + _attn_head_perm(),
                           np.arange(MLSTM_WIDTH + ATTN_Q_WIDTH, D_MODEL)]).astype(np.int32)


def _column_runs(perm):
    runs, start = [], 0
    for j in range(1, len(perm) + 1):
        if j == len(perm) or perm[j] != perm[j - 1] + 1:
            runs.append((start, int(perm[start]), j - start))
            start = j
    return runs


def _rope_tables(seq):
    inv = ROPE_BASE ** (-jnp.arange(ROPE_PAIRS, dtype=F32) / ROPE_PAIRS)
    rows = seq // GRID_W
    ang_r = jnp.arange(rows).astype(F32)[:, None] * inv
    ang_c = jnp.arange(GRID_W).astype(F32)[:, None] * inv
    lane_pat = lambda a, b: jnp.tile(jnp.concatenate([a, b], -1), (1, LANES // (2 * ROPE_PAIRS)))
    cos_rows, sin_rows = lane_pat(jnp.cos(ang_r), jnp.cos(ang_r)), lane_pat(-jnp.sin(ang_r), jnp.sin(ang_r))
    cos_cols, sin_cols = lane_pat(jnp.cos(ang_c), jnp.cos(ang_c)), lane_pat(-jnp.sin(ang_c), jnp.sin(ang_c))
    row_part = (np.arange(LANES) % HEAD_DIM) < HEAD_DIM // 2
    per_token = lambda by_row, by_col: jnp.where(
        row_part, jnp.broadcast_to(by_row[:, None, :], (rows, GRID_W, LANES)),
        jnp.broadcast_to(by_col[None, :, :], (rows, GRID_W, LANES))).reshape(seq, LANES)
    cos_l, sin_l = per_token(cos_rows, cos_cols), per_token(sin_rows, sin_cols)
    cos_t = jnp.concatenate([jnp.ones((CTX_LEN, LANES), F32), cos_l], 0)
    sin_t = jnp.concatenate([jnp.zeros((CTX_LEN, LANES), F32), sin_l], 0)
    return cos_t, sin_t


def _block_tri(direction):
    r = np.arange(TB)[:, None]
    c = np.arange(TB)[None, :]
    same = (r // CHUNK) == (c // CHUNK)
    tri = (c <= r) if direction == 0 else (c >= r)
    return jnp.asarray(same & tri, BF16)


def kernel(x, c, ctx, c_ctx, w_ada, b_ada, w_in, mlstm_gate_b, mlstm_norm_w, attn_sink, gla_gate_up, gla_gate_b,
           gla_norm_w, w_out, ln_w, ln_b, ffn_w_gu, ffn_w_down, router_w, router_b, moe_w_gu, moe_w_down):
    seq, d = x.shape[1], x.shape[2]
    depth = w_in.shape[0]
    xt = jnp.concatenate([ctx[0], x[0]], axis=0)

    cvec = jnp.zeros((SUBLANES, d), F32).at[0].set(c_ctx).at[1].set(c[0])
    mods = _mod_call(cvec, w_ada, b_ada)[:, :2].reshape(depth, 2, 6, d)
    mods = jnp.pad(mods, ((0, 0), (0, 0), (0, SUBLANES - 6), (0, 0)))

    gate_bias = jnp.pad(mlstm_gate_b.reshape(depth, 1, -1), ((0, 0), (0, 0), (0, N_GATE - 4 * MLSTM_HEADS)))
    gup_pad = jnp.zeros((depth, 2, N_GATE, GLA_KEY_WIDTH), F32)
    for dr in range(2):
        lo = GLA_GATE_OFF + dr * GLA_RANK
        gup_pad = gup_pad.at[:, dr, lo:lo + GLA_RANK, :].set(gla_gate_up[:, dr])
    cos_t, sin_t = _rope_tables(seq)
    tri = [_block_tri(0), _block_tri(1)]
    su = jnp.asarray(np.arange(TB)[:, None] < np.arange(TB)[None, :], BF16)
    hh = np.arange(MLSTM_WIDTH) // MLSTM_DH
    avg = jnp.asarray((hh[:, None] == hh[None, :]) / MLSTM_DH, BF16)
    ffn_gu, ffn_dn = ffn_w_gu.astype(BF16), ffn_w_down.astype(BF16)
    n_moe, n_exp, _, n_gu = moe_w_gu.shape
    gu_2d, dn_2d = moe_w_gu.reshape(-1, n_gu), moe_w_down.reshape(-1, d)
    cast_src = {2 * m: (gu_2d, m, n_moe) for m in range(n_moe)}
    cast_src.update({2 * m + 1: (dn_2d, m, n_moe) for m in range(n_moe)})
    cast_dst = {}

    expand = []
    for dr in range(2):
        e = np.zeros((N_GATE, MLSTM_WIDTH), np.float32)
        for h in range(MLSTM_HEADS):
            e[dr * 2 * MLSTM_HEADS + h, h * MLSTM_DH:(h + 1) * MLSTM_DH] = 1.0
        expand.append(jnp.asarray(e, BF16))

    for l in range(depth):
        is_moe = l % 2 == 1
        last = l == depth - 1
        main, gates = _inproj_call(xt, mods[l], w_in, l, cos_t, sin_t)
        mf, mb, gf, gbk, *cast = _scans_call(main, gates, gate_bias[l], tri, expand, gup_pad[l], gla_gate_b[l],
                                             cast_src.get(l))
        if cast:
            cast_dst[l] = cast[0]
        attn = _attn_call(main, attn_sink[l], seq)
        x1, h2 = _outproj_call(mf, mb, main, attn, gf, gbk, mlstm_norm_w[l:l + 1], gla_norm_w[l:l + 1], avg,
                               w_out, l, xt, mods[l], ln_w[l, 0:1], ln_b[l, 0:1], F32 if is_moe else BF16)
        if is_moe:
            moe_gu = cast_dst[l - 1].reshape(n_exp, d, n_gu)
            moe_dn = cast_dst[l].reshape(n_exp, n_gu // 2, d)
            xt = _moe_layer(h2, x1, mods[l], ln_w[l, 1:2], ln_b[l, 1:2], router_w[l // 2], router_b[l // 2],
                            moe_gu, moe_dn, su, latent_only=last)
        else:
            xt = _ffn_call(h2, ffn_gu, ffn_dn, l // 2, x1, mods[l], ln_w[l, 1:2], ln_b[l, 1:2])
    return (xt if depth % 2 == 0 else xt[CTX_LEN:])[None]
```

```python
import functools

import jax
import jax.numpy as jnp
import numpy as np
from jax import lax
from jax.experimental import pallas as pl
from jax.experimental.pallas import tpu as pltpu

F32 = jnp.float32
BF16 = jnp.bfloat16
HIGHEST = lax.Precision.HIGHEST

D_MODEL = 1024
SEQ = 16384
DEPTH = 4
GRID_W = 64
CTX_LEN = 256
MLSTM_HEADS = 4
MLSTM_DH = 64
MLSTM_WIDTH = 256
HEAD_DIM = 64
ATTN_HEADS = 8
ATTN_KV_HEADS = 2
ATTN_Q_WIDTH = 512
ATTN_KV_WIDTH = 128
WINDOW = 128
ROPE_BASE = 10000.0
ROPE_PAIRS = 16
GLA_HEADS = 4
GLA_DK = 32
GLA_DV = 64
GLA_WIDTH = 256
GLA_KEY_WIDTH = 128
GLA_RANK = 16
GLA_TAU = 16.0
CHUNK = 64
D_FF = 2816
N_EXPERTS = 8
D_EXPERT = 3584
DEEPNORM_ALPHA = (2 * DEPTH) ** 0.25
LN_EPS = 1e-5

LANES = 128
SUBLANES = 8
VMEM_LIMIT = 56 * 1024 * 1024

TB = 256
N_CHUNKS = TB // CHUNK
ADA_TN = 1536
PROJ_TM = 640
FFN_TM = 640
MOE_G = 512
MOE_TF = 1792

C_MQ, C_MK, C_MV, C_MO = 0, 256, 512, 768
C_AQ, C_AK, C_AV = 1024, 1536, 1664
C_GQ, C_GK, C_GV, C_GO = 1792, 1920, 2048, 2304
N_MAIN = 2560
N_GATE = 128
N_PROJ = N_MAIN + N_GATE
GLA_GATE_OFF = 16


def _cparams(n_axes=1):
    return pltpu.CompilerParams(dimension_semantics=("arbitrary",) * n_axes,
                                vmem_limit_bytes=VMEM_LIMIT)


def _sigmoid(x):
    return 1.0 / (1.0 + jnp.exp(-x))


def _log_sigmoid(x):
    return jnp.minimum(x, 0.0) - jnp.log(1.0 + jnp.exp(-jnp.abs(x)))


def _split3(x):
    hi = x.astype(BF16)
    r1 = x - hi.astype(F32)
    mid = r1.astype(BF16)
    lo = (r1 - mid.astype(F32)).astype(BF16)
    return hi, mid, lo


def _dot_exact_rhs(x, m_bf16):
    hi, mid, lo = _split3(x)
    d = lambda a: jnp.dot(a, m_bf16, preferred_element_type=F32)
    return d(hi) + d(mid) + d(lo)


def _dot_exact_lhs(m_bf16, x):
    hi, mid, lo = _split3(x)
    d = lambda a: jnp.dot(m_bf16, a, preferred_element_type=F32)
    return d(hi) + d(mid) + d(lo)


def _dot_bf16x3(a, b):
    a_hi, b_hi = a.astype(BF16), b.astype(BF16)
    a_lo = (a - a_hi.astype(F32)).astype(BF16)
    b_lo = (b - b_hi.astype(F32)).astype(BF16)
    d = lambda x, y: jnp.dot(x, y, preferred_element_type=F32)
    return d(a_hi, b_hi) + d(a_hi, b_lo) + d(a_lo, b_hi)


def _layer_norm(v, w, b):
    mu = jnp.mean(v, axis=-1, keepdims=True)
    cen = v - mu
    var = jnp.mean(cen * cen, axis=-1, keepdims=True)
    return cen * lax.rsqrt(var + LN_EPS) * w + b


def _mod_kernel(ct_ref, w_ref, b_ref, o_ref):
    c = ct_ref[...]
    sc = c * _sigmoid(c)
    w = w_ref[...]
    rows = [jnp.sum(sc[:, r:r + 1] * w, axis=0, keepdims=True) for r in range(2)]
    pad = jnp.zeros((SUBLANES - 2, w.shape[1]), F32)
    o_ref[...] = jnp.concatenate(rows + [pad], axis=0) + b_ref[...]


def _mod_call(cvec_t, w_ada, b_ada):
    depth, d, n = w_ada.shape
    tn = ADA_TN
    return pl.pallas_call(
        _mod_kernel,
        grid=(depth, n // tn),
        in_specs=[pl.BlockSpec((d, SUBLANES), lambda l, j: (0, 0)),
                  pl.BlockSpec((None, d, tn), lambda l, j: (l, 0, j)),
                  pl.BlockSpec((None, 1, tn), lambda l, j: (l, 0, j))],
        out_specs=pl.BlockSpec((None, SUBLANES, tn), lambda l, j: (l, 0, j)),
        out_shape=jax.ShapeDtypeStruct((depth, SUBLANES, n), F32),
        compiler_params=_cparams(2),
        name="ada_mod",
    )(cvec_t, w_ada, b_ada.reshape(depth, 1, n))


def _who(i):
    return jnp.minimum(i, 1)


def _rope(x, cos, sin_signed, first_half):
    swapped = jnp.where(first_half, pltpu.roll(x, LANES - ROPE_PAIRS, 1), pltpu.roll(x, ROPE_PAIRS, 1))
    return x * cos + swapped * sin_signed


def _inproj_kernel(x_ref, mod_ref, w_ref, cos_ref, sin_ref, main_ref, gate_ref, wbf_scr):
    i = pl.program_id(0)
    tm = x_ref.shape[0]

    @pl.when(i == 0)
    def _():
        for dst, src, n in _column_runs(_in_col_perm()):
            wbf_scr[:, dst:dst + n] = w_ref[:, src:src + n].astype(BF16)
        wbf_scr[:, w_ref.shape[1]:] = jnp.zeros((w_ref.shape[0], N_PROJ - w_ref.shape[1]), BF16)

    h = x_ref[...] * (1.0 + _select_mod(mod_ref, 1, i, tm)) + _select_mod(mod_ref, 0, i, tm)
    p = jnp.dot(h.astype(BF16), wbf_scr[...], preferred_element_type=F32)
    cos, sin = cos_ref[...], sin_ref[...]
    lane = lax.broadcasted_iota(jnp.int32, (tm, LANES), 1)
    first_half = (lane % (2 * ROPE_PAIRS)) < ROPE_PAIRS
    main_ref[:, :C_AQ] = p[:, :C_AQ].astype(BF16)
    for j in range(ATTN_Q_WIDTH // LANES):
        lo = C_AQ + j * LANES
        main_ref[:, lo:lo + LANES] = (_rope(p[:, lo:lo + LANES], cos, sin, first_half) * HEAD_DIM ** -0.5).astype(BF16)
    main_ref[:, C_AK:C_AV] = _rope(p[:, C_AK:C_AV], cos, sin, first_half).astype(BF16)
    main_ref[:, C_AV:] = p[:, C_AV:N_MAIN].astype(BF16)
    gate_ref[...] = p[:, N_MAIN:]


def _inproj_call(x, mod, w_in, layer, cos_t, sin_t):
    t, d = x.shape
    n_in = w_in.shape[2]
    tm = PROJ_TM
    return pl.pallas_call(
        _inproj_kernel,
        grid=(t // tm,),
        in_specs=[pl.BlockSpec((tm, d), lambda i: (i, 0)),
                  pl.BlockSpec((2, SUBLANES, d), lambda i: (0, 0, 0)),
                  pl.BlockSpec((None, d, n_in), lambda i: (layer, 0, 0), pipeline_mode=pl.Buffered(1)),
                  pl.BlockSpec((tm, LANES), lambda i: (i, 0)),
                  pl.BlockSpec((tm, LANES), lambda i: (i, 0))],
        out_specs=[pl.BlockSpec((tm, N_MAIN), lambda i: (i, 0)),
                   pl.BlockSpec((tm, N_GATE), lambda i: (i, 0))],
        out_shape=[jax.ShapeDtypeStruct((t, N_MAIN), BF16),
                   jax.ShapeDtypeStruct((t, N_GATE), F32)],
        scratch_shapes=[pltpu.VMEM((d, N_PROJ), BF16)],
        compiler_params=_cparams(1),
        name="in_proj",
    )(x, mod, w_in, cos_t, sin_t)


def _scan_block(i, nb, direction):
    if direction == 0:
        return i
    return jnp.where(i == 0, 0, nb - i)


def _scans_kernel(*refs, cast_weights):
    (mq_f, mk_f, mv_f, gq_f, gk_f, gv_f, g_f, mq_b, mk_b, mv_b, gq_b, gk_b, gv_b, g_b,
     gbias, tri_f, tri_b, exp_f, exp_b, gup_f, gup_b, glb_f, glb_b) = refs[:23]
    refs = refs[23:]
    if cast_weights:
        w_src, refs = refs[0], refs[1:]
    om_f, om_b, og_f, og_b = refs[:4]
    refs = refs[4:]
    if cast_weights:
        w_dst, refs = refs[0], refs[1:]
    c_f, n_f, m_f, c_b, n_b, m_b, s_f, s_b = refs

    @pl.when(pl.program_id(0) == 0)
    def _():
        for scr in (c_f, n_f, m_f, c_b, n_b, m_b, s_f, s_b):
            scr[...] = jnp.zeros_like(scr)

    streams = [
        _mlstm_phases(mq_f, mk_f, mv_f, g_f, gbias, tri_f, exp_f, om_f, c_f, n_f, m_f, 0),
        _mlstm_phases(mq_b, mk_b, mv_b, g_b, gbias, tri_b, exp_b, om_b, c_b, n_b, m_b, 1),
        _gla_phases(gq_f, gk_f, gv_f, g_f, gup_f, glb_f, tri_f, og_f, s_f, 0),
        _gla_phases(gq_b, gk_b, gv_b, g_b, gup_b, glb_b, tri_b, og_b, s_b, 1),
    ]
    if cast_weights:
        rows = w_src.shape[0] // CAST_PIECES

        def cast_piece(k):
            def run():
                w_dst[k * rows:(k + 1) * rows, :] = w_src[k * rows:(k + 1) * rows, :].astype(BF16)
            return run
        streams.append([cast_piece(k) for k in range(CAST_PIECES)])
    _run_interleaved(streams)


def _run_interleaved(streams):
    for k in range(max(len(s) for s in streams)):
        for s in streams:
            if k < len(s):
                s[k]()


def _mlstm_phases(q_ref, k_ref, v_ref, g_ref, gb_ref, tri_ref, exp_ref, o_ref, c_scr, n_scr, m_scr, direction):
    fwd = direction == 0
    neg_inf = jnp.float32(-jnp.inf)
    order = list(range(N_CHUNKS)) if fwd else list(range(N_CHUNKS - 1, -1, -1))
    pairs = list(range(MLSTM_HEADS // 2))
    inst = [(p, c) for p in pairs for c in order]
    last_of = lambda c: c * CHUNK + (CHUNK - 1 if fwd else 0)
    ts_of = lambda c: slice(c * CHUNK, (c + 1) * CHUNK)
    ts2_of = lambda c: slice((c // 2) * 2 * CHUNK, (c // 2 + 1) * 2 * CHUNK)
    lp_of = lambda p: slice(p * LANES, (p + 1) * LANES)
    dn_nt = (((1,), (1,)), ((), ()))
    st = {}

    lane = lax.broadcasted_iota(jnp.int32, (CHUNK, LANES), 1)
    trow = lax.broadcasted_iota(jnp.int32, (CHUNK, LANES), 0)
    low = lane < MLSTM_DH
    tri_ok = (lane % MLSTM_DH <= trow) if fwd else (lane % MLSTM_DH >= trow)
    lane2 = lax.broadcasted_iota(jnp.int32, (LANES, LANES), 1)
    row2 = lax.broadcasted_iota(jnp.int32, (LANES, LANES), 0)
    same_head = (lane2 < MLSTM_DH) == (row2 < MLSTM_DH)
    bd_ones = jnp.where(same_head, 1.0, 0.0).astype(BF16)
    lane_row = lax.broadcasted_iota(jnp.int32, (1, LANES), 1)
    pair_half = lax.broadcasted_iota(jnp.int32, (2 * CHUNK, LANES), 0) // CHUNK

    def pair_blockdiag(x):
        zero = jnp.zeros_like(x)
        return jnp.concatenate([jnp.where(low, x, zero), jnp.where(low, zero, x)], axis=0)

    def ph_cumsum():
        st['g'] = g_ref[...] + gb_ref[...]
        st['b_f'] = _dot_exact_lhs(tri_ref[...], _log_sigmoid(st['g']))

    def ph_gates():
        b_i = pltpu.roll(st['b_f'], LANES - MLSTM_HEADS, 1)
        gr = st['g'] - b_i
        tok = lax.broadcasted_iota(jnp.int32, (TB, LANES), 0) % CHUNK
        aloc = gr
        for sh in (1, 2, 4, 8, 16, 32):
            if fwd:
                shifted, ok = pltpu.roll(aloc, sh, 0), tok >= sh
            else:
                shifted, ok = pltpu.roll(aloc, TB - sh, 0), tok < CHUNK - sh
            aloc = jnp.maximum(aloc, jnp.where(ok, shifted, neg_inf))
        chunk_max = jnp.concatenate(
            [jnp.broadcast_to(aloc[last_of(c):last_of(c) + 1], (CHUNK, LANES)) for c in range(N_CHUNKS)], axis=0)
        st['b_i'], st['aloc'] = b_i, aloc
        st['wloc'] = jnp.exp(gr - chunk_max)
        st['gr_t'] = gr.T

    def ph_expand():
        expand = exp_ref[...]
        st['aloc_x'] = _dot_exact_rhs(st['aloc'], expand)
        st['b_x'] = _dot_exact_rhs(st['b_i'], expand)
        st['wloc_x'] = _dot_exact_rhs(st['wloc'], expand)

    def ph_scores():
        for p, c in inst:
            kp = k_ref[ts_of(c), lp_of(p)] * MLSTM_DH ** -0.5
            st['s', p, c] = lax.dot_general(q_ref[ts_of(c), lp_of(p)], pair_blockdiag(kp), dn_nt,
                                            preferred_element_type=F32)
        for p in pairs:
            for c2 in range(N_CHUNKS // 2):
                st['kt2', p, c2] = (k_ref[ts2_of(2 * c2), lp_of(p)].astype(F32) * MLSTM_DH ** -0.5).T.astype(BF16)

    def ph_weights():
        for p, c in inst:
            ji0 = direction * 2 * MLSTM_HEADS + 2 * p
            ra = st['gr_t'][ji0:ji0 + 1, ts2_of(c)]
            rb = st['gr_t'][ji0 + 1:ji0 + 2, ts2_of(c)]
            if c % 2 == 0:
                g_row = jnp.where(lane_row < MLSTM_DH, ra, pltpu.roll(rb, MLSTM_DH, 1))
            else:
                g_row = jnp.where(lane_row < MLSTM_DH, pltpu.roll(ra, MLSTM_DH, 1), rb)
            al = st['aloc_x'][ts_of(c), lp_of(p)]
            s = st['s', p, c] * jnp.exp(jnp.where(tri_ok, g_row - al, neg_inf))
            s_hi = s.astype(BF16)
            st['s_hi', p, c] = s_hi
            st['s_lo', p, c] = (s - s_hi.astype(F32)).astype(BF16)
            wx2 = st['wloc_x'][ts2_of(c), lp_of(p)]
            in_chunk = pair_half == (c % 2)
            wv = jnp.where(in_chunk, wx2 * v_ref[ts2_of(c), lp_of(p)].astype(F32), 0.0)
            ww = jnp.where(in_chunk, wx2, 0.0)
            st['wvw', p, c] = jnp.concatenate([wv, ww], axis=1).astype(BF16)

    def ph_intra():
        for p, c in inst:
            vp = v_ref[ts_of(c), lp_of(p)]
            st['r1', p, c] = jnp.dot(st['s_hi', p, c], jnp.concatenate([pair_blockdiag(vp), bd_ones], axis=1),
                                     preferred_element_type=F32)
            st['den_lo', p, c] = jnp.dot(st['s_lo', p, c], bd_ones, preferred_element_type=F32)
            st['upd', p, c] = jnp.dot(st['kt2', p, c // 2], st['wvw', p, c], preferred_element_type=F32)

    def ph_state():
        for p in pairs:
            bdc, bdn = c_scr[p], n_scr[p]
            m_row = m_scr[p][0:1]
            for c in order:
                st['cn', p, c] = jnp.concatenate([bdc, bdn], axis=1).astype(BF16)
                st['m', p, c] = m_row
                last = last_of(c)
                aloc_last = st['aloc_x'][last:last + 1, lp_of(p)]
                a_last = jnp.maximum(m_row, aloc_last)
                decay = jnp.exp(m_row - a_last)
                scale = jnp.exp(aloc_last - a_last)
                upd = st['upd', p, c]
                bdc = decay * bdc + scale * jnp.where(same_head, upd[:, :LANES], 0.0)
                bdn = decay * bdn + scale * jnp.where(same_head, upd[:, LANES:], 0.0)
                m_row = st['b_x'][last:last + 1, lp_of(p)] + a_last
            c_scr[p] = bdc
            n_scr[p] = bdn
            m_scr[p] = jnp.broadcast_to(m_row, (SUBLANES, LANES))

    def ph_inter():
        for p, c in inst:
            st['r2', p, c] = jnp.dot(q_ref[ts_of(c), lp_of(p)], st['cn', p, c],
                                     preferred_element_type=F32)

    def ph_out():
        for p, c in inst:
            al = st['aloc_x'][ts_of(c), lp_of(p)]
            m_row, r1, r2 = st['m', p, c], st['r1', p, c], st['r2', p, c]
            a = jnp.maximum(m_row, al)
            corr = jnp.exp(al - a)
            w_inter = jnp.exp(m_row - a)
            num = corr * r1[:, :LANES] + w_inter * r2[:, :LANES]
            den = corr * (r1[:, LANES:] + st['den_lo', p, c]) + w_inter * r2[:, LANES:]
            o_ref[ts_of(c), lp_of(p)] = num / jnp.maximum(jnp.abs(den),
                                                          jnp.exp(-(st['b_x'][ts_of(c), lp_of(p)] + a)))

    return [ph_cumsum, ph_gates, ph_expand, ph_scores, ph_weights, ph_intra, ph_state, ph_inter, ph_out]


def _gla_phases(q_ref, k_ref, v_ref, g_ref, gup_ref, gb_ref, tri_ref, o_ref, s_scr, direction):
    fwd = direction == 0
    order = list(range(N_CHUNKS)) if fwd else list(range(N_CHUNKS - 1, -1, -1))
    ts_of = lambda c: slice(c * CHUNK, (c + 1) * CHUNK)
    ts2_of = lambda c: slice((c // 2) * 2 * CHUNK, (c // 2 + 1) * 2 * CHUNK)
    last_of = lambda c: c * CHUNK + (CHUNK - 1 if fwd else 0)
    dn_nt = (((1,), (1,)), ((), ()))
    st = {}

    khead = lax.broadcasted_iota(jnp.int32, (CHUNK, GLA_KEY_WIDTH), 1) // GLA_DK
    vlane = lax.broadcasted_iota(jnp.int32, (CHUNK, GLA_WIDTH), 1)
    vhead = vlane // GLA_DV
    trow = lax.broadcasted_iota(jnp.int32, (CHUNK, GLA_WIDTH), 0)
    tri_ok = (vlane % CHUNK <= trow) if fwd else (vlane % CHUNK >= trow)
    st_head_r = lax.broadcasted_iota(jnp.int32, (GLA_WIDTH, GLA_KEY_WIDTH), 0) // GLA_DV
    st_head_c = lax.broadcasted_iota(jnp.int32, (GLA_WIDTH, GLA_KEY_WIDTH), 1) // GLA_DK
    same_head = st_head_r == st_head_c
    pair_row = lax.broadcasted_iota(jnp.int32, (2 * CHUNK, GLA_KEY_WIDTH), 0) // CHUNK

    def stack_heads(x, head_of_lane):
        zero = jnp.zeros_like(x)
        return jnp.concatenate([jnp.where(head_of_lane == h, x, zero) for h in range(GLA_HEADS)], axis=0)

    def ph_gate_proj():
        st['z'] = _dot_bf16x3(g_ref[...], gup_ref[...]) + gb_ref[...]

    def ph_log_decay():
        st['lg'] = _log_sigmoid(st['z']) * (1.0 / GLA_TAU)

    def ph_cumsum():
        st['b'] = _dot_exact_lhs(tri_ref[...], st['lg'])

    def ph_prep():
        b = st['b']
        st['k'] = k_ref[...].astype(F32)
        st['qt'] = (q_ref[...].astype(F32) * GLA_DK ** -0.5 * jnp.exp(b)).astype(BF16)
        st['kt'] = (st['k'] * jnp.exp(-b)).astype(BF16)
        st['v_t'] = v_ref[...].astype(F32).T.astype(BF16)

    def ph_scores():
        for c in order:
            st['a', c] = lax.dot_general(st['qt'][ts_of(c)], stack_heads(st['kt'][ts_of(c)], khead), dn_nt,
                                         preferred_element_type=F32)

    def ph_mask():
        b, k = st['b'], st['k']
        for c in order:
            st['a', c] = jnp.where(tri_ok, st['a', c], 0.0).astype(BF16)
            last = last_of(c)
            st['kdec', c] = jnp.where(pair_row == c % 2, k[ts2_of(c)] * jnp.exp(b[last:last + 1] - b[ts2_of(c)]),
                                      0.0).astype(BF16)

    def ph_intra():
        for c in order:
            st['o', c] = jnp.dot(st['a', c], stack_heads(v_ref[ts_of(c), :], vhead), preferred_element_type=F32)
            st['upd', c] = jnp.dot(st['v_t'][:, ts2_of(c)], st['kdec', c], preferred_element_type=F32)

    def ph_state():
        s_t = s_scr[...]
        for c in order:
            st['s', c] = s_t.astype(BF16)
            last = last_of(c)
            s_t = jnp.exp(st['b'][last:last + 1]) * s_t + jnp.where(same_head, st['upd', c], 0.0)
        s_scr[...] = s_t

    def ph_inter():
        for c in order:
            st['oi', c] = lax.dot_general(st['qt'][ts_of(c)], st['s', c], dn_nt, preferred_element_type=F32)

    def ph_out():
        for c in order:
            o_ref[ts_of(c), :] = st['o', c] + st['oi', c]

    return [ph_gate_proj, ph_log_decay, ph_cumsum, ph_prep, ph_scores, ph_mask, ph_intra, ph_state, ph_inter, ph_out]


def _scans_call(main, gates, gate_bias, tri, expand, gup_pad, gla_b, w_cast=None):
    t = main.shape[0]
    nb = t // TB
    const = lambda i: (0, 0)
    extra_in, extra_out, extra_shape, extra_args = [], [], [], []
    if w_cast is not None:
        w_all, part, n_parts = w_cast
        rows, cols = w_all.shape[0] // n_parts, w_all.shape[1]
        blk_rows = rows // CAST_STEPS
        step = lambda i: jnp.minimum(i, CAST_STEPS - 1)
        extra_in = [pl.BlockSpec((blk_rows, cols), lambda i: (part * CAST_STEPS + step(i), 0))]
        extra_out = [pl.BlockSpec((blk_rows, cols), lambda i: (step(i), 0))]
        extra_shape, extra_args = [jax.ShapeDtypeStruct((rows, cols), BF16)], [w_all]

    def streams(direction):
        blk = lambda i: _scan_block(i, nb, direction)
        col = lambda c, w: (lambda i: (blk(i), c // w))
        return [pl.BlockSpec((TB, MLSTM_WIDTH), col(C_MQ, MLSTM_WIDTH)),
                pl.BlockSpec((TB, MLSTM_WIDTH), col(C_MK, MLSTM_WIDTH)),
                pl.BlockSpec((TB, MLSTM_WIDTH), col(C_MV, MLSTM_WIDTH)),
                pl.BlockSpec((TB, GLA_KEY_WIDTH), col(C_GQ, GLA_KEY_WIDTH)),
                pl.BlockSpec((TB, GLA_KEY_WIDTH), col(C_GK, GLA_KEY_WIDTH)),
                pl.BlockSpec((TB, GLA_WIDTH), col(C_GV, GLA_WIDTH)),
                pl.BlockSpec((TB, N_GATE), lambda i: (blk(i), 0))]

    consts = [pl.BlockSpec((1, N_GATE), const),
              pl.BlockSpec((TB, TB), const), pl.BlockSpec((TB, TB), const),
              pl.BlockSpec((N_GATE, MLSTM_WIDTH), const), pl.BlockSpec((N_GATE, MLSTM_WIDTH), const),
              pl.BlockSpec((N_GATE, GLA_KEY_WIDTH), const), pl.BlockSpec((N_GATE, GLA_KEY_WIDTH), const),
              pl.BlockSpec((1, GLA_KEY_WIDTH), const), pl.BlockSpec((1, GLA_KEY_WIDTH), const)]
    out_spec = lambda direction, w: pl.BlockSpec((TB, w), lambda i: (_scan_block(i, nb, direction), 0))
    n_pairs = MLSTM_HEADS // 2
    mlstm_state = [pltpu.VMEM((n_pairs, LANES, LANES), F32), pltpu.VMEM((n_pairs, LANES, LANES), F32),
                   pltpu.VMEM((n_pairs, SUBLANES, LANES), F32)]
    gla_state = [pltpu.VMEM((GLA_WIDTH, GLA_KEY_WIDTH), F32)]
    return pl.pallas_call(
        functools.partial(_scans_kernel, cast_weights=w_cast is not None),
        grid=(nb,),
        in_specs=streams(0) + streams(1) + consts + extra_in,
        out_specs=[out_spec(0, MLSTM_WIDTH), out_spec(1, MLSTM_WIDTH), out_spec(0, GLA_WIDTH),
                   out_spec(1, GLA_WIDTH)] + extra_out,
        out_shape=([jax.ShapeDtypeStruct((t, MLSTM_WIDTH), F32)] * 2 + [jax.ShapeDtypeStruct((t, GLA_WIDTH), F32)] * 2
                   + extra_shape),
        scratch_shapes=mlstm_state + mlstm_state + gla_state + gla_state,
        compiler_params=_cparams(1),
        name="scans",
    )(*([main] * 6 + [gates]) * 2, gate_bias, tri[0], tri[1], expand[0], expand[1],
      gup_pad[0], gup_pad[1], gla_b[0:1], gla_b[1:2], *extra_args)


def _attn_kernel(sink_ref, q_ref, kp_ref, kc_ref, kn_ref, kx_ref, vp_ref, vc_ref, vn_ref, vx_ref, o_ref, *, seq):
    i = pl.program_id(0)
    half = TB // 2
    n_loc = 2 * TB
    keys = jnp.concatenate([kp_ref[...], kc_ref[...], kn_ref[...], kx_ref[...]], axis=0)
    vals = jnp.concatenate([vp_ref[...], vc_ref[...], vn_ref[...], vx_ref[...]], axis=0)
    n_keys = keys.shape[0]
    lane = lax.broadcasted_iota(jnp.int32, (n_keys, LANES), 1)
    zero = jnp.zeros_like(keys)
    keys_g = [jnp.where(lane < HEAD_DIM, keys, zero), jnp.where(lane >= HEAD_DIM, keys, zero)]
    r = lax.broadcasted_iota(jnp.int32, (TB, n_keys), 0)
    c = lax.broadcasted_iota(jnp.int32, (TB, n_keys), 1)
    rel = c - half - r
    kpos = (i - 1) * TB + c - half
    local_ok = (jnp.abs(rel) <= WINDOW) & (kpos >= 0) & (kpos < seq) & (i > 0)
    valid = local_ok | (c >= n_loc)
    out_lane = lax.broadcasted_iota(jnp.int32, (TB, LANES), 1)
    neg_inf = jnp.float32(-jnp.inf)
    dn = (((1,), (1,)), ((), ()))
    heads = [(m, gidx) for m in range(ATTN_Q_WIDTH // LANES) for gidx in range(ATTN_KV_HEADS)]
    scores = [lax.dot_general(q_ref[:, m * LANES:(m + 1) * LANES], keys_g[gidx], dn, preferred_element_type=F32)
              for m, gidx in heads]
    probs, denoms = [], []
    for (m, gidx), s in zip(heads, scores):
        sink = sink_ref[gidx * (ATTN_HEADS // ATTN_KV_HEADS) + m]
        s = jnp.where(valid, s, neg_inf)
        mx = jnp.maximum(jnp.max(s, axis=1, keepdims=True), sink)
        p = jnp.exp(s - mx)
        denoms.append(jnp.sum(p, axis=1, keepdims=True) + jnp.exp(sink - mx))
        probs.append(p.astype(BF16))
    pvs = [jnp.dot(p, vals, preferred_element_type=F32) for p in probs]
    for m in range(ATTN_Q_WIDTH // LANES):
        r0 = pvs[2 * m] / denoms[2 * m]
        r1 = pvs[2 * m + 1] / denoms[2 * m + 1]
        o_ref[:, m * LANES:(m + 1) * LANES] = jnp.where(out_lane < HEAD_DIM, r0, r1).astype(BF16)


def _attn_call(main, sink, seq):
    t = main.shape[0]
    nb = t // TB
    half = TB // 2
    nhb = t // half
    kcol, vcol = C_AK // LANES, C_AV // LANES
    prev_i = lambda i: jnp.maximum(2 * i - 1, 0)
    next_i = lambda i: jnp.minimum(2 * i + 2, nhb - 1)
    grid_spec = pltpu.PrefetchScalarGridSpec(
        num_scalar_prefetch=1,
        grid=(nb,),
        in_specs=[pl.BlockSpec((TB, ATTN_Q_WIDTH), lambda i, s: (i, C_AQ // ATTN_Q_WIDTH)),
                  pl.BlockSpec((half, LANES), lambda i, s: (prev_i(i), kcol)),
                  pl.BlockSpec((TB, LANES), lambda i, s: (i, kcol)),
                  pl.BlockSpec((half, LANES), lambda i, s: (next_i(i), kcol)),
                  pl.BlockSpec((TB, LANES), lambda i, s: (0, kcol)),
                  pl.BlockSpec((half, LANES), lambda i, s: (prev_i(i), vcol)),
                  pl.BlockSpec((TB, LANES), lambda i, s: (i, vcol)),
                  pl.BlockSpec((half, LANES), lambda i, s: (next_i(i), vcol)),
                  pl.BlockSpec((TB, LANES), lambda i, s: (0, vcol))],
        out_specs=pl.BlockSpec((TB, ATTN_Q_WIDTH), lambda i, s: (i, 0)),
    )
    return pl.pallas_call(
        functools.partial(_attn_kernel, seq=seq),
        grid_spec=grid_spec,
        out_shape=jax.ShapeDtypeStruct((t, ATTN_Q_WIDTH), BF16),
        compiler_params=_cparams(1),
        name="window_attn",
    )(sink, main, main, main, main, main, main, main, main, main)


def _group_mean(x, avg_bf16):
    hi = x.astype(BF16)
    lo = (x - hi.astype(F32)).astype(BF16)
    return (jnp.dot(hi, avg_bf16, preferred_element_type=F32) + jnp.dot(lo, avg_bf16, preferred_element_type=F32))


def _head_norm(x, avg_bf16, w):
    cen = x - _group_mean(x, avg_bf16)
    var = _group_mean(cen * cen, avg_bf16)
    return cen * lax.rsqrt(var + LN_EPS) * w


def _outproj_kernel(mf_ref, mb_ref, mo_ref, at_ref, gf_ref, gbk_ref, go_ref, mnw_ref, gnw_ref, avg_ref,
                    w_ref, x_ref, mod_ref, lnw_ref, lnb_ref, x1_ref, h2_ref, wbf_scr):
    i = pl.program_id(0)
    tm = x_ref.shape[0]

    @pl.when(i == 0)
    def _():
        for dst, src, n in _column_runs(_out_row_perm()):
            wbf_scr[dst:dst + n, :] = w_ref[src:src + n, :].astype(BF16)

    avg = avg_ref[...]
    hm = _head_norm(mf_ref[...] + mb_ref[...], avg, mnw_ref[...]) * _sigmoid(mo_ref[...].astype(F32))
    gate = go_ref[...].astype(F32)
    hg = _head_norm(gf_ref[...] + gbk_ref[...], avg, gnw_ref[...]) * (gate * _sigmoid(gate))
    a0, a1, a2 = MLSTM_WIDTH, MLSTM_WIDTH + ATTN_Q_WIDTH, D_MODEL
    mix = (jnp.dot(hm.astype(BF16), wbf_scr[:a0], preferred_element_type=F32)
           + jnp.dot(at_ref[...], wbf_scr[a0:a1], preferred_element_type=F32)
           + jnp.dot(hg.astype(BF16), wbf_scr[a1:a2], preferred_element_type=F32))
    x1 = _layer_norm(DEEPNORM_ALPHA * x_ref[...] + _select_mod(mod_ref, 2, i, tm) * mix, lnw_ref[...], lnb_ref[...])
    x1_ref[...] = x1
    h2_ref[...] = (x1 * (1.0 + _select_mod(mod_ref, 4, i, tm)) + _select_mod(mod_ref, 3, i, tm)).astype(h2_ref.dtype)


def _outproj_call(mf, mb, main, attn, gf, gbk, mnw, gnw, avg, w_out, layer, x, mod, lnw, lnb, h2_dtype):
    t, d = x.shape
    tm = PROJ_TM
    row = lambda i: (i, 0)
    const = lambda i: (0, 0)
    return pl.pallas_call(
        _outproj_kernel,
        grid=(t // tm,),
        in_specs=[pl.BlockSpec((tm, MLSTM_WIDTH), row),
                  pl.BlockSpec((tm, MLSTM_WIDTH), row),
                  pl.BlockSpec((tm, MLSTM_WIDTH), lambda i: (i, C_MO // MLSTM_WIDTH)),
                  pl.BlockSpec((tm, ATTN_Q_WIDTH), row),
                  pl.BlockSpec((tm, GLA_WIDTH), row),
                  pl.BlockSpec((tm, GLA_WIDTH), row),
                  pl.BlockSpec((tm, GLA_WIDTH), lambda i: (i, C_GO // GLA_WIDTH)),
                  pl.BlockSpec((1, MLSTM_WIDTH), const),
                  pl.BlockSpec((1, GLA_WIDTH), const),
                  pl.BlockSpec((MLSTM_WIDTH, MLSTM_WIDTH), const),
                  pl.BlockSpec((None, d, d), lambda i: (layer, 0, 0), pipeline_mode=pl.Buffered(1)),
                  pl.BlockSpec((tm, d), row),
                  pl.BlockSpec((2, SUBLANES, d), lambda i: (0, 0, 0)),
                  pl.BlockSpec((1, d), const),
                  pl.BlockSpec((1, d), const)],
        out_specs=[pl.BlockSpec((tm, d), row), pl.BlockSpec((tm, d), row)],
        out_shape=[jax.ShapeDtypeStruct((t, d), F32), jax.ShapeDtypeStruct((t, d), h2_dtype)],
        scratch_shapes=[pltpu.VMEM((d, d), BF16)],
        compiler_params=_cparams(1),
        name="out_proj",
    )(mf, mb, main, attn, gf, gbk, main, mnw, gnw, avg, w_out, x, mod, lnw, lnb)


def _select_mod(mod_ref, k, i, tm):
    rows = i * tm + lax.broadcasted_iota(jnp.int32, (tm, 1), 0)
    return jnp.where(rows < CTX_LEN, mod_ref[0, k:k + 1, :], mod_ref[1, k:k + 1, :])


def _ffn_kernel(h_ref, wgu_ref, wd_ref, x_ref, mod_ref, lnw_ref, lnb_ref, o_ref):
    h = h_ref[...]
    g = jnp.dot(h, wgu_ref[:, :D_FF], preferred_element_type=F32)
    u = jnp.dot(h, wgu_ref[:, D_FF:], preferred_element_type=F32)
    y = jnp.dot((g * _sigmoid(g) * u).astype(BF16), wd_ref[...], preferred_element_type=F32)
    m5 = _select_mod(mod_ref, 5, pl.program_id(0), h_ref.shape[0])
    o_ref[...] = _layer_norm(DEEPNORM_ALPHA * x_ref[...] + m5 * y, lnw_ref[...], lnb_ref[...])


def _ffn_call(h2, w_gu, w_down, layer, x1, mod, lnw, lnb):
    t, d = x1.shape
    resident = pl.Buffered(1)
    return pl.pallas_call(
        _ffn_kernel,
        grid=(t // FFN_TM,),
        in_specs=[pl.BlockSpec((FFN_TM, d), lambda i: (i, 0)),
                  pl.BlockSpec((None, d, 2 * D_FF), lambda i: (layer, 0, 0), pipeline_mode=resident),
                  pl.BlockSpec((None, D_FF, d), lambda i: (layer, 0, 0), pipeline_mode=resident),
                  pl.BlockSpec((FFN_TM, d), lambda i: (i, 0)),
                  pl.BlockSpec((2, SUBLANES, d), lambda i: (0, 0, 0)),
                  pl.BlockSpec((1, d), lambda i: (0, 0)),
                  pl.BlockSpec((1, d), lambda i: (0, 0))],
        out_specs=pl.BlockSpec((FFN_TM, d), lambda i: (i, 0)),
        out_shape=jax.ShapeDtypeStruct((t, d), F32),
        compiler_params=_cparams(1),
        name="dense_ffn",
    )(h2, w_gu, w_down, x1, mod, lnw, lnb)


def _router_kernel(h_ref, wr_ref, br_ref, su_ref, route_ref, cnt_ref, base_scr):
    @pl.when(pl.program_id(0) == 0)
    def _():
        base_scr[...] = jnp.zeros_like(base_scr)

    lt = lax.dot_general(wr_ref[...], h_ref[...], (((1,), (1,)), ((), ())),
                         preferred_element_type=F32, precision=HIGHEST) + br_ref[...]
    idx = lax.broadcasted_iota(jnp.int32, lt.shape, 0)
    neg_inf = jnp.float32(-jnp.inf)
    m1 = jnp.max(lt, axis=0, keepdims=True)
    e1 = jnp.min(jnp.where(lt == m1, idx, N_EXPERTS), axis=0, keepdims=True)
    lt2 = jnp.where(idx == e1, neg_inf, lt)
    m2 = jnp.max(lt2, axis=0, keepdims=True)
    e2 = jnp.min(jnp.where(lt2 == m2, idx, N_EXPERTS), axis=0, keepdims=True)
    t2 = jnp.exp(m2 - m1)
    w1 = 1.0 / (1.0 + t2)
    w2 = t2 / (1.0 + t2)
    su = su_ref[...]
    base = base_scr[...][:, 0:1]
    oh1 = (idx == e1).astype(F32)
    cum1 = jnp.dot(oh1.astype(BF16), su, preferred_element_type=F32)
    rank1 = jnp.sum(oh1 * (base + cum1), axis=0, keepdims=True)
    base = base + jnp.sum(oh1, axis=1, keepdims=True)
    oh2 = (idx == e2).astype(F32)
    cum2 = jnp.dot(oh2.astype(BF16), su, preferred_element_type=F32)
    rank2 = jnp.sum(oh2 * (base + cum2), axis=0, keepdims=True)
    base = base + jnp.sum(oh2, axis=1, keepdims=True)
    base_scr[...] = jnp.broadcast_to(base, base_scr.shape)
    cnt_ref[...] = jnp.broadcast_to(base, cnt_ref.shape)
    zero = jnp.zeros_like(w1)
    route_ref[...] = jnp.concatenate(
        [e1.astype(F32), e2.astype(F32), w1, w2, rank1, rank2, zero, zero], axis=0)


def _router_call(h2, wr_t, br, su):
    t, d = h2.shape
    nb = t // TB
    return pl.pallas_call(
        _router_kernel,
        grid=(nb,),
        in_specs=[pl.BlockSpec((TB, d), lambda i: (i, 0)),
                  pl.BlockSpec((N_EXPERTS, d), lambda i: (0, 0)),
                  pl.BlockSpec((N_EXPERTS, 1), lambda i: (0, 0)),
                  pl.BlockSpec((TB, TB), lambda i: (0, 0))],
        out_specs=[pl.BlockSpec((SUBLANES, TB), lambda i: (0, i)),
                   pl.BlockSpec((N_EXPERTS, LANES), lambda i: (0, 0))],
        out_shape=[jax.ShapeDtypeStruct((SUBLANES, t), F32),
                   jax.ShapeDtypeStruct((N_EXPERTS, LANES), F32)],
        scratch_shapes=[pltpu.VMEM((N_EXPERTS, LANES), F32)],
        compiler_params=_cparams(1),
        name="moe_router",
    )(h2, wr_t, br, su)


ROW_TILE = D_MODEL // LANES
DMA_UNROLL = 8
ZERO_CHUNK = 64
CAST_STEPS = 64
CAST_PIECES = 4


def _store_row_tiles(ref, val):
    n = val.shape[0]
    for j in range(ROW_TILE):
        ref[pl.ds(j, n, stride=ROW_TILE), :] = val[:, j * LANES:(j + 1) * LANES]


def _load_row_tiles(ref, n, dtype=F32):
    return jnp.concatenate([ref[pl.ds(j, n, stride=ROW_TILE), :].astype(dtype) for j in range(ROW_TILE)], axis=1)


def _row(ref, idx):
    return ref.at[pl.ds(pl.multiple_of(idx * ROW_TILE, ROW_TILE), ROW_TILE)]


def _dispatch_kernel(s1_ref, s2_ref, pad_lo_ref, pad_hi_ref, h_ref, xs_ref, hr_scr, zero_scr, sem, zsem):
    i = pl.program_id(0)

    @pl.when(i == 0)
    def _():
        zero_scr[...] = jnp.zeros_like(zero_scr)

        def zero_rows(s, n):
            dst = xs_ref.at[pl.ds(pl.multiple_of(s * ROW_TILE, ROW_TILE), n * ROW_TILE)]
            return pltpu.make_async_copy(zero_scr.at[pl.ds(0, n * ROW_TILE)], dst, zsem)

        def fill(e, act):
            lo, hi = pad_lo_ref[e], pad_hi_ref[e]
            n_big = (hi - lo) // ZERO_CHUNK
            lax.fori_loop(0, n_big, lambda k, c: (act(zero_rows(lo + k * ZERO_CHUNK, ZERO_CHUNK)), c)[1], 0)
            lax.fori_loop(lo + n_big * ZERO_CHUNK, hi, lambda s, c: (act(zero_rows(s, 1)), c)[1], 0)

        for e in range(N_EXPERTS + 1):
            fill(e, lambda cp: cp.start())
        for e in range(N_EXPERTS + 1):
            fill(e, lambda cp: cp.wait())

    n = pl.num_programs(0)
    cur = i % 2

    def copy(blk, r, slot_ref):
        buf = blk % 2
        return pltpu.make_async_copy(_row(hr_scr.at[buf], r), _row(xs_ref, slot_ref[blk * TB + r]), sem.at[buf])

    def start(r, carry):
        copy(i, r, s1_ref).start()
        copy(i, r, s2_ref).start()
        return carry

    def wait_block(blk):
        def wait(r, carry):
            copy(blk, r, s1_ref).wait()
            copy(blk, r, s2_ref).wait()
            return carry
        lax.fori_loop(0, TB, wait, 0, unroll=DMA_UNROLL)

    _store_row_tiles(hr_scr.at[cur], h_ref[...])
    lax.fori_loop(0, TB, start, 0, unroll=DMA_UNROLL)

    @pl.when(i > 0)
    def _():
        wait_block(i - 1)

    @pl.when(i == n - 1)
    def _():
        wait_block(i)


def _dispatch_call(slot1, slot2, pad_lo, pad_hi, h2, n_slots):
    t, d = h2.shape
    grid_spec = pltpu.PrefetchScalarGridSpec(
        num_scalar_prefetch=4,
        grid=(t // TB,),
        in_specs=[pl.BlockSpec((TB, d), lambda i, *_: (i, 0))],
        out_specs=pl.BlockSpec(memory_space=pl.ANY),
        scratch_shapes=[pltpu.VMEM((2, TB * ROW_TILE, LANES), F32), pltpu.VMEM((ZERO_CHUNK * ROW_TILE, LANES), F32),
                        pltpu.SemaphoreType.DMA((2,)), pltpu.SemaphoreType.DMA(())],
    )
    return pl.pallas_call(
        _dispatch_kernel,
        grid_spec=grid_spec,
        out_shape=jax.ShapeDtypeStruct((n_slots * ROW_TILE, LANES), F32),
        compiler_params=_cparams(1),
        name="moe_dispatch",
    )(slot1, slot2, pad_lo, pad_hi, h2)


def _expert_kernel(be_ref, used_ref, x_ref, wg_ref, wu_ref, wd_ref, y_ref, xb_scr, acc_scr, *, n_f):
    b, f = pl.program_id(0), pl.program_id(1)
    live = b < used_ref[0]
    last = n_f - 1

    def swiglu_part():
        x = xb_scr[...]
        g = jnp.dot(x, wg_ref[...], preferred_element_type=F32)
        u = jnp.dot(x, wu_ref[...], preferred_element_type=F32)
        return jnp.dot((g * _sigmoid(g) * u).astype(BF16), wd_ref[...], preferred_element_type=F32)

    @pl.when(live & (f == 0))
    def _():
        for j in range(ROW_TILE):
            xb_scr[:, j * LANES:(j + 1) * LANES] = x_ref[pl.ds(j, MOE_G, stride=ROW_TILE), :].astype(BF16)
        acc_scr[...] = swiglu_part()

    if n_f > 2:
        @pl.when(live & (f > 0) & (f < last))
        def _():
            acc_scr[...] += swiglu_part()

    @pl.when(live & (f == last))
    def _():
        _store_row_tiles(y_ref, acc_scr[...] + swiglu_part())

    @pl.when(jnp.logical_not(live) & (f == last))
    def _():
        y_ref[...] = jnp.zeros_like(y_ref)


def _expert_call(block_e, used, xs, w_gu, w_down):
    d = D_MODEL
    ns = xs.shape[0] // ROW_TILE
    nblk = ns // MOE_G
    nf = D_EXPERT // MOE_TF
    rows = MOE_G * ROW_TILE

    def f_eff(b, f, used_ref):
        return jnp.where(b < used_ref[0], f, nf - 1)

    grid_spec = pltpu.PrefetchScalarGridSpec(
        num_scalar_prefetch=2,
        grid=(nblk, nf),
        in_specs=[pl.BlockSpec((rows, LANES), lambda b, f, be, us: (b, 0)),
                  pl.BlockSpec((None, d, MOE_TF), lambda b, f, be, us: (be[b], 0, f_eff(b, f, us))),
                  pl.BlockSpec((None, d, MOE_TF), lambda b, f, be, us: (be[b], 0, nf + f_eff(b, f, us))),
                  pl.BlockSpec((None, MOE_TF, d), lambda b, f, be, us: (be[b], f_eff(b, f, us), 0))],
        out_specs=pl.BlockSpec((rows, LANES), lambda b, f, be, us: (b, 0)),
        scratch_shapes=[pltpu.VMEM((MOE_G, d), BF16), pltpu.VMEM((MOE_G, d), F32)],
    )
    return pl.pallas_call(
        functools.partial(_expert_kernel, n_f=nf),
        grid_spec=grid_spec,
        out_shape=jax.ShapeDtypeStruct(xs.shape, F32),
        compiler_params=_cparams(2),
        name="moe_experts",
    )(block_e, used, xs, w_gu, w_gu, w_down)


def _combine_kernel(s1_ref, s2_ref, yb_ref, route_ref, x_ref, mod_ref, lnw_ref, lnb_ref, o_ref, buf1, buf2, sem):
    i = pl.program_id(0)
    n = pl.num_programs(0)
    cur = i % 2

    def copy(blk, r, slot_ref, buf):
        b = blk % 2
        return pltpu.make_async_copy(_row(yb_ref, slot_ref[blk * TB + r]), _row(buf.at[b], r), sem.at[b])

    def start_block(blk):
        def start(r, carry):
            copy(blk, r, s1_ref, buf1).start()
            copy(blk, r, s2_ref, buf2).start()
            return carry
        lax.fori_loop(0, TB, start, 0, unroll=DMA_UNROLL)

    def wait(r, carry):
        copy(i, r, s1_ref, buf1).wait()
        copy(i, r, s2_ref, buf2).wait()
        return carry

    @pl.when(i == 0)
    def _():
        start_block(i)

    @pl.when(i + 1 < n)
    def _():
        start_block(i + 1)

    lax.fori_loop(0, TB, wait, 0, unroll=DMA_UNROLL)
    rt = route_ref[...].T
    y = rt[:, 2:3] * _load_row_tiles(buf1.at[cur], TB) + rt[:, 3:4] * _load_row_tiles(buf2.at[cur], TB)
    m = mod_ref[...]
    o_ref[...] = _layer_norm(DEEPNORM_ALPHA * x_ref[...] + m[5:6] * y, lnw_ref[...], lnb_ref[...])


def _combine_call(slot1, slot2, yb, route, x1, mod, lnw, lnb, latent_only):
    t, d = x1.shape
    if latent_only:
        out_rows, out_map = t - CTX_LEN, lambda i, a, b: (jnp.maximum(i - 1, 0), 0)
    else:
        out_rows, out_map = t, lambda i, a, b: (i, 0)
    grid_spec = pltpu.PrefetchScalarGridSpec(
        num_scalar_prefetch=2,
        grid=(t // TB,),
        in_specs=[pl.BlockSpec(memory_space=pl.ANY),
                  pl.BlockSpec((SUBLANES, TB), lambda i, a, b: (0, i)),
                  pl.BlockSpec((TB, d), lambda i, a, b: (i, 0)),
                  pl.BlockSpec((None, SUBLANES, d), lambda i, a, b: (_who(i), 0, 0)),
                  pl.BlockSpec((1, d), lambda i, a, b: (0, 0)),
                  pl.BlockSpec((1, d), lambda i, a, b: (0, 0))],
        out_specs=pl.BlockSpec((TB, d), out_map),
        scratch_shapes=[pltpu.VMEM((2, TB * ROW_TILE, LANES), F32), pltpu.VMEM((2, TB * ROW_TILE, LANES), F32),
                        pltpu.SemaphoreType.DMA((2,))],
    )
    return pl.pallas_call(
        _combine_kernel,
        grid_spec=grid_spec,
        out_shape=jax.ShapeDtypeStruct((out_rows, d), F32),
        compiler_params=_cparams(1),
        name="moe_combine",
    )(slot1, slot2, yb, route, x1, mod, lnw, lnb)


def _moe_layer(h2, x1, mod, lnw, lnb, w_router, b_router, w_gu, w_down, su, latent_only):
    t, d = h2.shape
    route, cnt = _router_call(h2, w_router.T, b_router.reshape(N_EXPERTS, 1), su)
    counts = cnt[:, 0].astype(jnp.int32)
    padded = (counts + MOE_G - 1) // MOE_G * MOE_G
    pend = jnp.cumsum(padded)
    pstart = pend - padded
    e1, e2 = route[0].astype(jnp.int32), route[1].astype(jnp.int32)
    slot1 = pstart[e1] + route[4].astype(jnp.int32)
    slot2 = pstart[e2] + route[5].astype(jnp.int32)
    nblk = -(-(2 * t) // MOE_G) + N_EXPERTS
    blk_start = jnp.arange(nblk, dtype=jnp.int32) * MOE_G
    block_e = jnp.minimum(jnp.sum((pend[None, :] <= blk_start[:, None]).astype(jnp.int32), axis=1), N_EXPERTS - 1)
    used = (pend[-1:] // MOE_G).astype(jnp.int32)
    n_slots = nblk * MOE_G
    pad_lo = jnp.concatenate([pstart + counts, pend[-1:]])
    pad_hi = jnp.concatenate([pend, jnp.full((1,), n_slots, jnp.int32)])
    xs = _dispatch_call(slot1, slot2, pad_lo, pad_hi, h2, n_slots)
    yb = _expert_call(block_e, used, xs, w_gu, w_down)
    return _combine_call(slot1, slot2, yb, route, x1, mod, lnw, lnb, latent_only)


def _attn_head_perm():
    cols = []
    for m in range(ATTN_HEADS // ATTN_KV_HEADS):
        for hq in (m, m + ATTN_HEADS // ATTN_KV_HEADS):
            cols.extend(range(hq * HEAD_DIM, (hq + 1) * HEAD_DIM))
    return np.asarray(cols, np.int32)


def _in_col_perm():
    n_m = 4 * MLSTM_WIDTH
    mg = 2 * 2 * MLSTM_HEADS
    a0 = n_m + mg
    perm = list(range(n_m))
    perm += [a0 + int(j) for j in _attn_head_perm()]
    perm += list(range(a0 + ATTN_Q_WIDTH, a0 + ATTN_Q_WIDTH + 2 * ATTN_KV_WIDTH))
    g0 = a0 + ATTN_Q_WIDTH + 2 * ATTN_KV_WIDTH
    perm += list(range(g0, g0 + 2 * GLA_KEY_WIDTH + 2 * GLA_WIDTH))
    perm += list(range(n_m, n_m + mg))
    perm += list(range(g0 + 2 * GLA_KEY_WIDTH + 2 * GLA_WIDTH, g0 + 2 * GLA_KEY_WIDTH + 2 * GLA_WIDTH + 2 * GLA_RANK))
    return np.asarray(perm, np.int32)


def _out_row_perm():
    return np.concatenate([np.arange(MLSTM_WIDTH), MLSTM_WIDTH + _attn_head_perm(),
                           np.arange(MLSTM_WIDTH + ATTN_Q_WIDTH, D_MODEL)]).astype(np.int32)


def _column_runs(perm):
    runs, start = [], 0
    for j in range(1, len(perm) + 1):
        if j == len(perm) or perm[j] != perm[j - 1] + 1:
            runs.append((start, int(perm[start]), j - start))
            start = j
    return runs


def _rope_tables(seq):
    inv = ROPE_BASE ** (-jnp.arange(ROPE_PAIRS, dtype=F32) / ROPE_PAIRS)
    rows = seq // GRID_W
    ang_r = jnp.arange(rows).astype(F32)[:, None] * inv
    ang_c = jnp.arange(GRID_W).astype(F32)[:, None] * inv
    lane_pat = lambda a, b: jnp.tile(jnp.concatenate([a, b], -1), (1, LANES // (2 * ROPE_PAIRS)))
    cos_rows, sin_rows = lane_pat(jnp.cos(ang_r), jnp.cos(ang_r)), lane_pat(-jnp.sin(ang_r), jnp.sin(ang_r))
    cos_cols, sin_cols = lane_pat(jnp.cos(ang_c), jnp.cos(ang_c)), lane_pat(-jnp.sin(ang_c), jnp.sin(ang_c))
    row_part = (np.arange(LANES) % HEAD_DIM) < HEAD_DIM // 2
    per_token = lambda by_row, by_col: jnp.where(
        row_part, jnp.broadcast_to(by_row[:, None, :], (rows, GRID_W, LANES)),
        jnp.broadcast_to(by_col[None, :, :], (rows, GRID_W, LANES))).reshape(seq, LANES)
    cos_l, sin_l = per_token(cos_rows, cos_cols), per_token(sin_rows, sin_cols)
    cos_t = jnp.concatenate([jnp.ones((CTX_LEN, LANES), F32), cos_l], 0)
    sin_t = jnp.concatenate([jnp.zeros((CTX_LEN, LANES), F32), sin_l], 0)
    return cos_t, sin_t


def _block_tri(direction):
    r = np.arange(TB)[:, None]
    c = np.arange(TB)[None, :]
    same = (r // CHUNK) == (c // CHUNK)
    tri = (c <= r) if direction == 0 else (c >= r)
    return jnp.asarray(same & tri, BF16)


def kernel(x, c, ctx, c_ctx, w_ada, b_ada, w_in, mlstm_gate_b, mlstm_norm_w, attn_sink, gla_gate_up, gla_gate_b,
           gla_norm_w, w_out, ln_w, ln_b, ffn_w_gu, ffn_w_down, router_w, router_b, moe_w_gu, moe_w_down):
    seq, d = x.shape[1], x.shape[2]
    depth = w_in.shape[0]
    t = ctx.shape[1] + seq
    assert x.shape[0] == 1 and d == D_MODEL and ctx.shape[1] == CTX_LEN == TB and seq % TB == 0
    assert t % PROJ_TM == 0 and t % FFN_TM == 0
    xt = jnp.concatenate([ctx[0], x[0]], axis=0)

    cvec_t = jnp.zeros((d, SUBLANES), F32).at[:, 0].set(c_ctx).at[:, 1].set(c[0])
    mods = _mod_call(cvec_t, w_ada, b_ada)[:, :2].reshape(depth, 2, 6, d)
    mods = jnp.pad(mods, ((0, 0), (0, 0), (0, SUBLANES - 6), (0, 0)))

    gate_bias = jnp.pad(mlstm_gate_b.reshape(depth, 1, -1), ((0, 0), (0, 0), (0, N_GATE - 4 * MLSTM_HEADS)))
    gup_pad = jnp.zeros((depth, 2, N_GATE, GLA_KEY_WIDTH), F32)
    for dr in range(2):
        lo = GLA_GATE_OFF + dr * GLA_RANK
        gup_pad = gup_pad.at[:, dr, lo:lo + GLA_RANK, :].set(gla_gate_up[:, dr])
    cos_t, sin_t = _rope_tables(seq)
    tri = [_block_tri(0), _block_tri(1)]
    su = jnp.asarray(np.arange(TB)[:, None] < np.arange(TB)[None, :], BF16)
    hh = np.arange(MLSTM_WIDTH) // MLSTM_DH
    avg = jnp.asarray((hh[:, None] == hh[None, :]) / MLSTM_DH, BF16)
    ffn_gu, ffn_dn = ffn_w_gu.astype(BF16), ffn_w_down.astype(BF16)
    n_moe, n_exp, _, n_gu = moe_w_gu.shape
    gu_2d, dn_2d = moe_w_gu.reshape(-1, n_gu), moe_w_down.reshape(-1, d)
    cast_src = {2 * m: (gu_2d, m, n_moe) for m in range(n_moe)}
    cast_src.update({2 * m + 1: (dn_2d, m, n_moe) for m in range(n_moe)})
    cast_dst = {}

    expand = []
    for dr in range(2):
        e = np.zeros((N_GATE, MLSTM_WIDTH), np.float32)
        for h in range(MLSTM_HEADS):
            e[dr * 2 * MLSTM_HEADS + h, h * MLSTM_DH:(h + 1) * MLSTM_DH] = 1.0
        expand.append(jnp.asarray(e, BF16))

    for l in range(depth):
        is_moe = l % 2 == 1
        last = l == depth - 1
        main, gates = _inproj_call(xt, mods[l], w_in, l, cos_t, sin_t)
        mf, mb, gf, gbk, *cast = _scans_call(main, gates, gate_bias[l], tri, expand, gup_pad[l], gla_gate_b[l],
                                             cast_src.get(l))
        if cast:
            cast_dst[l] = cast[0]
        attn = _attn_call(main, attn_sink[l], seq)
        x1, h2 = _outproj_call(mf, mb, main, attn, gf, gbk, mlstm_norm_w[l:l + 1], gla_norm_w[l:l + 1], avg,
                               w_out, l, xt, mods[l], ln_w[l, 0:1], ln_b[l, 0:1], F32 if is_moe else BF16)
        if is_moe:
            moe_gu = cast_dst[l - 1].reshape(n_exp, d, n_gu)
            moe_dn = cast_dst[l].reshape(n_exp, n_gu // 2, d)
            xt = _moe_layer(h2, x1, mods[l], ln_w[l, 1:2], ln_b[l, 1:2], router_w[l // 2], router_b[l // 2],
                            moe_gu, moe_dn, su, latent_only=last)
        else:
            xt = _ffn_call(h2, ffn_gu, ffn_dn, l // 2, x1, mods[l], ln_w[l, 1:2], ln_b[l, 1:2])
    return (xt if depth % 2 == 0 else xt[CTX_LEN:])[None]
```

```python
import functools

import jax
import jax.numpy as jnp
import numpy as np
from jax import lax
from jax.experimental import pallas as pl
from jax.experimental.pallas import tpu as pltpu

F32 = jnp.float32
BF16 = jnp.bfloat16

D_MODEL = 1024
SEQ = 16384
DEPTH = 4
GRID_W = 64
CTX_LEN = 256
MLSTM_HEADS = 4
MLSTM_DH = 64
MLSTM_WIDTH = 256
HEAD_DIM = 64
ATTN_HEADS = 8
ATTN_KV_HEADS = 2
ATTN_Q_WIDTH = 512
ATTN_KV_WIDTH = 128
WINDOW = 128
ROPE_BASE = 10000.0
ROPE_PAIRS = 16
GLA_HEADS = 4
GLA_DK = 32
GLA_DV = 64
GLA_WIDTH = 256
GLA_KEY_WIDTH = 128
GLA_RANK = 16
GLA_TAU = 16.0
CHUNK = 64
D_FF = 2816
N_EXPERTS = 8
D_EXPERT = 3584
DEEPNORM_ALPHA = (2 * DEPTH) ** 0.25
LN_EPS = 1e-5

LANES = 128
SUBLANES = 8
VMEM_LIMIT = 56 * 1024 * 1024

TB = 256
N_CHUNKS = TB // CHUNK
ADA_TN = 1536
PROJ_TM = 640
FFN_TM = 640
MOE_G = 512
MOE_TF = 1792

C_MQ, C_MK, C_MV, C_MO = 0, 256, 512, 768
C_AQ, C_AK, C_AV = 1024, 1536, 1664
C_GQ, C_GK, C_GV, C_GO = 1792, 1920, 2048, 2304
N_MAIN = 2560
N_GATE = 128
N_PROJ = N_MAIN + N_GATE
GLA_GATE_OFF = 16


def _cparams(n_axes=1):
    return pltpu.CompilerParams(dimension_semantics=("arbitrary",) * n_axes,
                                vmem_limit_bytes=VMEM_LIMIT)


def _sigmoid(x):
    return 1.0 / (1.0 + jnp.exp(-x))


def _log_sigmoid(x):
    return jnp.minimum(x, 0.0) - jnp.log(1.0 + jnp.exp(-jnp.abs(x)))


def _split3(x):
    hi = x.astype(BF16)
    r1 = x - hi.astype(F32)
    mid = r1.astype(BF16)
    lo = (r1 - mid.astype(F32)).astype(BF16)
    return hi, mid, lo


def _dot_exact_rhs(x, m_bf16):
    hi, mid, lo = _split3(x)
    d = lambda a: jnp.dot(a, m_bf16, preferred_element_type=F32)
    return d(hi) + d(mid) + d(lo)


def _dot_exact_lhs(m_bf16, x):
    hi, mid, lo = _split3(x)
    d = lambda a: jnp.dot(m_bf16, a, preferred_element_type=F32)
    return d(hi) + d(mid) + d(lo)


def _dot_bf16x3(a, b, dims=(((1,), (0,)), ((), ()))):
    a_hi, b_hi = a.astype(BF16), b.astype(BF16)
    a_lo = (a - a_hi.astype(F32)).astype(BF16)
    b_lo = (b - b_hi.astype(F32)).astype(BF16)
    d = lambda x, y: lax.dot_general(x, y, dims, preferred_element_type=F32)
    return d(a_hi, b_hi) + d(a_hi, b_lo) + d(a_lo, b_hi)


def _layer_norm(v, w, b):
    mu = jnp.mean(v, axis=-1, keepdims=True)
    cen = v - mu
    var = jnp.mean(cen * cen, axis=-1, keepdims=True)
    return cen * lax.rsqrt(var + LN_EPS) * w + b


def _mod_kernel(ct_ref, w_ref, b_ref, o_ref):
    c = ct_ref[...]
    sc = c * _sigmoid(c)
    w = w_ref[...]
    rows = [jnp.sum(sc[:, r:r + 1] * w, axis=0, keepdims=True) for r in range(2)]
    pad = jnp.zeros((SUBLANES - 2, w.shape[1]), F32)
    o_ref[...] = jnp.concatenate(rows + [pad], axis=0) + b_ref[...]


def _mod_call(cvec_t, w_ada, b_ada):
    depth, d, n = w_ada.shape
    tn = ADA_TN
    return pl.pallas_call(
        _mod_kernel,
        grid=(depth, n // tn),
        in_specs=[pl.BlockSpec((d, SUBLANES), lambda l, j: (0, 0)),
                  pl.BlockSpec((None, d, tn), lambda l, j: (l, 0, j)),
                  pl.BlockSpec((None, 1, tn), lambda l, j: (l, 0, j))],
        out_specs=pl.BlockSpec((None, SUBLANES, tn), lambda l, j: (l, 0, j)),
        out_shape=jax.ShapeDtypeStruct((depth, SUBLANES, n), F32),
        compiler_params=_cparams(2),
        name="ada_mod",
    )(cvec_t, w_ada, b_ada.reshape(depth, 1, n))


def _who(i):
    return jnp.minimum(i, 1)


def _rope(x, cos, sin_signed, first_half):
    swapped = jnp.where(first_half, pltpu.roll(x, LANES - ROPE_PAIRS, 1), pltpu.roll(x, ROPE_PAIRS, 1))
    return x * cos + swapped * sin_signed


def _inproj_kernel(x_ref, mod_ref, w_ref, cos_ref, sin_ref, main_ref, gate_ref, wbf_scr):
    i = pl.program_id(0)
    tm = x_ref.shape[0]

    @pl.when(i == 0)
    def _():
        for dst, src, n in _column_runs(_in_col_perm()):
            wbf_scr[:, dst:dst + n] = w_ref[:, src:src + n].astype(BF16)
        wbf_scr[:, w_ref.shape[1]:] = jnp.zeros((w_ref.shape[0], N_PROJ - w_ref.shape[1]), BF16)

    h = x_ref[...] * (1.0 + _select_mod(mod_ref, 1, i, tm)) + _select_mod(mod_ref, 0, i, tm)
    p = jnp.dot(h.astype(BF16), wbf_scr[...], preferred_element_type=F32)
    cos, sin = cos_ref[...], sin_ref[...]
    lane = lax.broadcasted_iota(jnp.int32, (tm, LANES), 1)
    first_half = (lane % (2 * ROPE_PAIRS)) < ROPE_PAIRS
    main_ref[:, :C_AQ] = p[:, :C_AQ].astype(BF16)
    for j in range(ATTN_Q_WIDTH // LANES):
        lo = C_AQ + j * LANES
        main_ref[:, lo:lo + LANES] = (_rope(p[:, lo:lo + LANES], cos, sin, first_half) * HEAD_DIM ** -0.5).astype(BF16)
    main_ref[:, C_AK:C_AV] = _rope(p[:, C_AK:C_AV], cos, sin, first_half).astype(BF16)
    main_ref[:, C_AV:] = p[:, C_AV:N_MAIN].astype(BF16)
    gate_ref[...] = p[:, N_MAIN:]


def _inproj_call(x, mod, w_in, layer, cos_t, sin_t):
    t, d = x.shape
    n_in = w_in.shape[2]
    tm = PROJ_TM
    return pl.pallas_call(
        _inproj_kernel,
        grid=(t // tm,),
        in_specs=[pl.BlockSpec((tm, d), lambda i: (i, 0)),
                  pl.BlockSpec((2, SUBLANES, d), lambda i: (0, 0, 0)),
                  pl.BlockSpec((None, d, n_in), lambda i: (layer, 0, 0), pipeline_mode=pl.Buffered(1)),
                  pl.BlockSpec((tm, LANES), lambda i: (i, 0)),
                  pl.BlockSpec((tm, LANES), lambda i: (i, 0))],
        out_specs=[pl.BlockSpec((tm, N_MAIN), lambda i: (i, 0)),
                   pl.BlockSpec((tm, N_GATE), lambda i: (i, 0))],
        out_shape=[jax.ShapeDtypeStruct((t, N_MAIN), BF16),
                   jax.ShapeDtypeStruct((t, N_GATE), F32)],
        scratch_shapes=[pltpu.VMEM((d, N_PROJ), BF16)],
        compiler_params=_cparams(1),
        name="in_proj",
    )(x, mod, w_in, cos_t, sin_t)


def _scan_block(i, nb, direction):
    if direction == 0:
        return i
    return jnp.where(i == 0, 0, nb - i)


def _scans_kernel(*refs, cast_weights):
    (mq_f, mk_f, mv_f, gq_f, gk_f, gv_f, g_f, mq_b, mk_b, mv_b, gq_b, gk_b, gv_b, g_b,
     gbias, tri_f, tri_b, exp_f, exp_b, gup_f, gup_b, glb_f, glb_b) = refs[:23]
    refs = refs[23:]
    if cast_weights:
        w_src, refs = refs[0], refs[1:]
    om_f, om_b, og_f, og_b = refs[:4]
    refs = refs[4:]
    if cast_weights:
        w_dst, refs = refs[0], refs[1:]
    c_f, n_f, m_f, c_b, n_b, m_b, s_f, s_b = refs

    @pl.when(pl.program_id(0) == 0)
    def _():
        for scr in (c_f, n_f, m_f, c_b, n_b, m_b, s_f, s_b):
            scr[...] = jnp.zeros_like(scr)

    streams = [
        _mlstm_phases(mq_f, mk_f, mv_f, g_f, gbias, tri_f, exp_f, om_f, c_f, n_f, m_f, 0),
        _mlstm_phases(mq_b, mk_b, mv_b, g_b, gbias, tri_b, exp_b, om_b, c_b, n_b, m_b, 1),
        _gla_phases(gq_f, gk_f, gv_f, g_f, gup_f, glb_f, tri_f, og_f, s_f, 0),
        _gla_phases(gq_b, gk_b, gv_b, g_b, gup_b, glb_b, tri_b, og_b, s_b, 1),
    ]
    if cast_weights:
        rows = w_src.shape[0] // CAST_PIECES

        def cast_piece(k):
            def run():
                w_dst[k * rows:(k + 1) * rows, :] = w_src[k * rows:(k + 1) * rows, :].astype(BF16)
            return run
        streams.append([cast_piece(k) for k in range(CAST_PIECES)])
    _run_interleaved(streams)


def _run_interleaved(streams):
    for k in range(max(len(s) for s in streams)):
        for s in streams:
            if k < len(s):
                s[k]()


def _mlstm_phases(q_ref, k_ref, v_ref, g_ref, gb_ref, tri_ref, exp_ref, o_ref, c_scr, n_scr, m_scr, direction):
    fwd = direction == 0
    neg_inf = jnp.float32(-jnp.inf)
    order = list(range(N_CHUNKS)) if fwd else list(range(N_CHUNKS - 1, -1, -1))
    pairs = list(range(MLSTM_HEADS // 2))
    inst = [(p, c) for p in pairs for c in order]
    last_of = lambda c: c * CHUNK + (CHUNK - 1 if fwd else 0)
    ts_of = lambda c: slice(c * CHUNK, (c + 1) * CHUNK)
    ts2_of = lambda c: slice((c // 2) * 2 * CHUNK, (c // 2 + 1) * 2 * CHUNK)
    lp_of = lambda p: slice(p * LANES, (p + 1) * LANES)
    dn_nt = (((1,), (1,)), ((), ()))
    st = {}

    lane = lax.broadcasted_iota(jnp.int32, (CHUNK, LANES), 1)
    trow = lax.broadcasted_iota(jnp.int32, (CHUNK, LANES), 0)
    low = lane < MLSTM_DH
    tri_ok = (lane % MLSTM_DH <= trow) if fwd else (lane % MLSTM_DH >= trow)
    lane2 = lax.broadcasted_iota(jnp.int32, (LANES, LANES), 1)
    row2 = lax.broadcasted_iota(jnp.int32, (LANES, LANES), 0)
    same_head = (lane2 < MLSTM_DH) == (row2 < MLSTM_DH)
    bd_ones = jnp.where(same_head, 1.0, 0.0).astype(BF16)
    lane_row = lax.broadcasted_iota(jnp.int32, (1, LANES), 1)
    pair_half = lax.broadcasted_iota(jnp.int32, (2 * CHUNK, LANES), 0) // CHUNK

    def pair_blockdiag(x):
        zero = jnp.zeros_like(x)
        return jnp.concatenate([jnp.where(low, x, zero), jnp.where(low, zero, x)], axis=0)

    def ph_cumsum():
        st['g'] = g_ref[...] + gb_ref[...]
        st['b_f'] = _dot_exact_lhs(tri_ref[...], _log_sigmoid(st['g']))

    def ph_gates():
        b_i = pltpu.roll(st['b_f'], LANES - MLSTM_HEADS, 1)
        gr = st['g'] - b_i
        tok = lax.broadcasted_iota(jnp.int32, (TB, LANES), 0) % CHUNK
        aloc = gr
        for sh in (1, 2, 4, 8, 16, 32):
            if fwd:
                shifted, ok = pltpu.roll(aloc, sh, 0), tok >= sh
            else:
                shifted, ok = pltpu.roll(aloc, TB - sh, 0), tok < CHUNK - sh
            aloc = jnp.maximum(aloc, jnp.where(ok, shifted, neg_inf))
        chunk_max = jnp.concatenate(
            [jnp.broadcast_to(aloc[last_of(c):last_of(c) + 1], (CHUNK, LANES)) for c in range(N_CHUNKS)], axis=0)
        st['b_i'], st['aloc'] = b_i, aloc
        st['wloc'] = jnp.exp(gr - chunk_max)
        st['gr_t'] = gr.T

    def ph_expand():
        expand = exp_ref[...]
        st['aloc_x'] = _dot_exact_rhs(st['aloc'], expand)
        st['b_x'] = _dot_exact_rhs(st['b_i'], expand)
        st['wloc_x'] = _dot_exact_rhs(st['wloc'], expand)

    def ph_scores():
        for p, c in inst:
            kp = k_ref[ts_of(c), lp_of(p)] * MLSTM_DH ** -0.5
            st['s', p, c] = lax.dot_general(q_ref[ts_of(c), lp_of(p)], pair_blockdiag(kp), dn_nt,
                                            preferred_element_type=F32)
        for p in pairs:
            for c2 in range(N_CHUNKS // 2):
                st['kt2', p, c2] = (k_ref[ts2_of(2 * c2), lp_of(p)].astype(F32) * MLSTM_DH ** -0.5).T.astype(BF16)

    def ph_weights():
        for p, c in inst:
            ji0 = direction * 2 * MLSTM_HEADS + 2 * p
            ra = st['gr_t'][ji0:ji0 + 1, ts2_of(c)]
            rb = st['gr_t'][ji0 + 1:ji0 + 2, ts2_of(c)]
            if c % 2 == 0:
                g_row = jnp.where(lane_row < MLSTM_DH, ra, pltpu.roll(rb, MLSTM_DH, 1))
            else:
                g_row = jnp.where(lane_row < MLSTM_DH, pltpu.roll(ra, MLSTM_DH, 1), rb)
            al = st['aloc_x'][ts_of(c), lp_of(p)]
            s = st['s', p, c] * jnp.exp(jnp.where(tri_ok, g_row - al, neg_inf))
            s_hi = s.astype(BF16)
            st['s_hi', p, c] = s_hi
            st['s_lo', p, c] = (s - s_hi.astype(F32)).astype(BF16)
            wx2 = st['wloc_x'][ts2_of(c), lp_of(p)]
            in_chunk = pair_half == (c % 2)
            wv = jnp.where(in_chunk, wx2 * v_ref[ts2_of(c), lp_of(p)].astype(F32), 0.0)
            ww = jnp.where(in_chunk, wx2, 0.0)
            st['wvw', p, c] = jnp.concatenate([wv, ww], axis=1).astype(BF16)

    def ph_intra():
        for p, c in inst:
            vp = v_ref[ts_of(c), lp_of(p)]
            st['r1', p, c] = jnp.dot(st['s_hi', p, c], jnp.concatenate([pair_blockdiag(vp), bd_ones], axis=1),
                                     preferred_element_type=F32)
            st['den_lo', p, c] = jnp.dot(st['s_lo', p, c], bd_ones, preferred_element_type=F32)
            st['upd', p, c] = jnp.dot(st['kt2', p, c // 2], st['wvw', p, c], preferred_element_type=F32)

    def ph_state():
        for p in pairs:
            bdc, bdn = c_scr[p], n_scr[p]
            m_row = m_scr[p][0:1]
            for c in order:
                st['cn', p, c] = jnp.concatenate([bdc, bdn], axis=1).astype(BF16)
                st['m', p, c] = m_row
                last = last_of(c)
                aloc_last = st['aloc_x'][last:last + 1, lp_of(p)]
                a_last = jnp.maximum(m_row, aloc_last)
                decay = jnp.exp(m_row - a_last)
                scale = jnp.exp(aloc_last - a_last)
                upd = st['upd', p, c]
                bdc = decay * bdc + scale * jnp.where(same_head, upd[:, :LANES], 0.0)
                bdn = decay * bdn + scale * jnp.where(same_head, upd[:, LANES:], 0.0)
                m_row = st['b_x'][last:last + 1, lp_of(p)] + a_last
            c_scr[p] = bdc
            n_scr[p] = bdn
            m_scr[p] = jnp.broadcast_to(m_row, (SUBLANES, LANES))

    def ph_inter():
        for p, c in inst:
            st['r2', p, c] = jnp.dot(q_ref[ts_of(c), lp_of(p)], st['cn', p, c],
                                     preferred_element_type=F32)

    def ph_out():
        for p, c in inst:
            al = st['aloc_x'][ts_of(c), lp_of(p)]
            m_row, r1, r2 = st['m', p, c], st['r1', p, c], st['r2', p, c]
            a = jnp.maximum(m_row, al)
            corr = jnp.exp(al - a)
            w_inter = jnp.exp(m_row - a)
            num = corr * r1[:, :LANES] + w_inter * r2[:, :LANES]
            den = corr * (r1[:, LANES:] + st['den_lo', p, c]) + w_inter * r2[:, LANES:]
            o_ref[ts_of(c), lp_of(p)] = num / jnp.maximum(jnp.abs(den),
                                                          jnp.exp(-(st['b_x'][ts_of(c), lp_of(p)] + a)))

    return [ph_cumsum, ph_gates, ph_expand, ph_scores, ph_weights, ph_intra, ph_state, ph_inter, ph_out]


def _gla_phases(q_ref, k_ref, v_ref, g_ref, gup_ref, gb_ref, tri_ref, o_ref, s_scr, direction):
    fwd = direction == 0
    order = list(range(N_CHUNKS)) if fwd else list(range(N_CHUNKS - 1, -1, -1))
    ts_of = lambda c: slice(c * CHUNK, (c + 1) * CHUNK)
    ts2_of = lambda c: slice((c // 2) * 2 * CHUNK, (c // 2 + 1) * 2 * CHUNK)
    last_of = lambda c: c * CHUNK + (CHUNK - 1 if fwd else 0)
    dn_nt = (((1,), (1,)), ((), ()))
    st = {}

    khead = lax.broadcasted_iota(jnp.int32, (CHUNK, GLA_KEY_WIDTH), 1) // GLA_DK
    vlane = lax.broadcasted_iota(jnp.int32, (CHUNK, GLA_WIDTH), 1)
    vhead = vlane // GLA_DV
    trow = lax.broadcasted_iota(jnp.int32, (CHUNK, GLA_WIDTH), 0)
    tri_ok = (vlane % CHUNK <= trow) if fwd else (vlane % CHUNK >= trow)
    st_head_r = lax.broadcasted_iota(jnp.int32, (GLA_WIDTH, GLA_KEY_WIDTH), 0) // GLA_DV
    st_head_c = lax.broadcasted_iota(jnp.int32, (GLA_WIDTH, GLA_KEY_WIDTH), 1) // GLA_DK
    same_head = st_head_r == st_head_c
    pair_row = lax.broadcasted_iota(jnp.int32, (2 * CHUNK, GLA_KEY_WIDTH), 0) // CHUNK

    def stack_heads(x, head_of_lane):
        zero = jnp.zeros_like(x)
        return jnp.concatenate([jnp.where(head_of_lane == h, x, zero) for h in range(GLA_HEADS)], axis=0)

    def ph_gate_proj():
        st['z'] = _dot_bf16x3(g_ref[...], gup_ref[...]) + gb_ref[...]

    def ph_log_decay():
        st['lg'] = _log_sigmoid(st['z']) * (1.0 / GLA_TAU)

    def ph_cumsum():
        st['b'] = _dot_exact_lhs(tri_ref[...], st['lg'])

    def ph_prep():
        b = st['b']
        st['k'] = k_ref[...].astype(F32)
        st['qt'] = (q_ref[...].astype(F32) * GLA_DK ** -0.5 * jnp.exp(b)).astype(BF16)
        st['kt'] = (st['k'] * jnp.exp(-b)).astype(BF16)
        st['v_t'] = v_ref[...].astype(F32).T.astype(BF16)

    def ph_scores():
        for c in order:
            st['a', c] = lax.dot_general(st['qt'][ts_of(c)], stack_heads(st['kt'][ts_of(c)], khead), dn_nt,
                                         preferred_element_type=F32)

    def ph_mask():
        b, k = st['b'], st['k']
        for c in order:
            st['a', c] = jnp.where(tri_ok, st['a', c], 0.0).astype(BF16)
            last = last_of(c)
            st['kdec', c] = jnp.where(pair_row == c % 2, k[ts2_of(c)] * jnp.exp(b[last:last + 1] - b[ts2_of(c)]),
                                      0.0).astype(BF16)

    def ph_intra():
        for c in order:
            st['o', c] = jnp.dot(st['a', c], stack_heads(v_ref[ts_of(c), :], vhead), preferred_element_type=F32)
            st['upd', c] = jnp.dot(st['v_t'][:, ts2_of(c)], st['kdec', c], preferred_element_type=F32)

    def ph_state():
        s_t = s_scr[...]
        for c in order:
            st['s', c] = s_t.astype(BF16)
            last = last_of(c)
            s_t = jnp.exp(st['b'][last:last + 1]) * s_t + jnp.where(same_head, st['upd', c], 0.0)
        s_scr[...] = s_t

    def ph_inter():
        for c in order:
            st['oi', c] = lax.dot_general(st['qt'][ts_of(c)], st['s', c], dn_nt, preferred_element_type=F32)

    def ph_out():
        for c in order:
            o_ref[ts_of(c), :] = st['o', c] + st['oi', c]

    return [ph_gate_proj, ph_log_decay, ph_cumsum, ph_prep, ph_scores, ph_mask, ph_intra, ph_state, ph_inter, ph_out]


def _scans_call(main, gates, gate_bias, tri, expand, gup_pad, gla_b, w_cast=None):
    t = main.shape[0]
    nb = t // TB
    const = lambda i: (0, 0)
    extra_in, extra_out, extra_shape, extra_args = [], [], [], []
    if w_cast is not None:
        w_all, part, n_parts = w_cast
        rows, cols = w_all.shape[0] // n_parts, w_all.shape[1]
        blk_rows = rows // CAST_STEPS
        step = lambda i: jnp.minimum(i, CAST_STEPS - 1)
        extra_in = [pl.BlockSpec((blk_rows, cols), lambda i: (part * CAST_STEPS + step(i), 0))]
        extra_out = [pl.BlockSpec((blk_rows, cols), lambda i: (step(i), 0))]
        extra_shape, extra_args = [jax.ShapeDtypeStruct((rows, cols), BF16)], [w_all]

    def streams(direction):
        blk = lambda i: _scan_block(i, nb, direction)
        col = lambda c, w: (lambda i: (blk(i), c // w))
        return [pl.BlockSpec((TB, MLSTM_WIDTH), col(C_MQ, MLSTM_WIDTH)),
                pl.BlockSpec((TB, MLSTM_WIDTH), col(C_MK, MLSTM_WIDTH)),
                pl.BlockSpec((TB, MLSTM_WIDTH), col(C_MV, MLSTM_WIDTH)),
                pl.BlockSpec((TB, GLA_KEY_WIDTH), col(C_GQ, GLA_KEY_WIDTH)),
                pl.BlockSpec((TB, GLA_KEY_WIDTH), col(C_GK, GLA_KEY_WIDTH)),
                pl.BlockSpec((TB, GLA_WIDTH), col(C_GV, GLA_WIDTH)),
                pl.BlockSpec((TB, N_GATE), lambda i: (blk(i), 0))]

    consts = [pl.BlockSpec((1, N_GATE), const),
              pl.BlockSpec((TB, TB), const), pl.BlockSpec((TB, TB), const),
              pl.BlockSpec((N_GATE, MLSTM_WIDTH), const), pl.BlockSpec((N_GATE, MLSTM_WIDTH), const),
              pl.BlockSpec((N_GATE, GLA_KEY_WIDTH), const), pl.BlockSpec((N_GATE, GLA_KEY_WIDTH), const),
              pl.BlockSpec((1, GLA_KEY_WIDTH), const), pl.BlockSpec((1, GLA_KEY_WIDTH), const)]
    out_spec = lambda direction, w: pl.BlockSpec((TB, w), lambda i: (_scan_block(i, nb, direction), 0))
    n_pairs = MLSTM_HEADS // 2
    mlstm_state = [pltpu.VMEM((n_pairs, LANES, LANES), F32), pltpu.VMEM((n_pairs, LANES, LANES), F32),
                   pltpu.VMEM((n_pairs, SUBLANES, LANES), F32)]
    gla_state = [pltpu.VMEM((GLA_WIDTH, GLA_KEY_WIDTH), F32)]
    return pl.pallas_call(
        functools.partial(_scans_kernel, cast_weights=w_cast is not None),
        grid=(nb,),
        in_specs=streams(0) + streams(1) + consts + extra_in,
        out_specs=[out_spec(0, MLSTM_WIDTH), out_spec(1, MLSTM_WIDTH), out_spec(0, GLA_WIDTH),
                   out_spec(1, GLA_WIDTH)] + extra_out,
        out_shape=([jax.ShapeDtypeStruct((t, MLSTM_WIDTH), F32)] * 2 + [jax.ShapeDtypeStruct((t, GLA_WIDTH), F32)] * 2
                   + extra_shape),
        scratch_shapes=mlstm_state + mlstm_state + gla_state + gla_state,
        compiler_params=_cparams(1),
        name="scans",
    )(*([main] * 6 + [gates]) * 2, gate_bias, tri[0], tri[1], expand[0], expand[1],
      gup_pad[0], gup_pad[1], gla_b[0:1], gla_b[1:2], *extra_args)


def _attn_kernel(sink_ref, q_ref, kp_ref, kc_ref, kn_ref, kx_ref, vp_ref, vc_ref, vn_ref, vx_ref, o_ref, *, seq):
    i = pl.program_id(0)
    half = TB // 2
    n_loc = 2 * TB
    keys = jnp.concatenate([kp_ref[...], kc_ref[...], kn_ref[...], kx_ref[...]], axis=0)
    vals = jnp.concatenate([vp_ref[...], vc_ref[...], vn_ref[...], vx_ref[...]], axis=0)
    n_keys = keys.shape[0]
    lane = lax.broadcasted_iota(jnp.int32, (n_keys, LANES), 1)
    zero = jnp.zeros_like(keys)
    keys_g = [jnp.where(lane < HEAD_DIM, keys, zero), jnp.where(lane >= HEAD_DIM, keys, zero)]
    r = lax.broadcasted_iota(jnp.int32, (TB, n_keys), 0)
    c = lax.broadcasted_iota(jnp.int32, (TB, n_keys), 1)
    rel = c - half - r
    kpos = (i - 1) * TB + c - half
    local_ok = (jnp.abs(rel) <= WINDOW) & (kpos >= 0) & (kpos < seq) & (i > 0)
    valid = local_ok | (c >= n_loc)
    out_lane = lax.broadcasted_iota(jnp.int32, (TB, LANES), 1)
    neg_inf = jnp.float32(-jnp.inf)
    dn = (((1,), (1,)), ((), ()))
    heads = [(m, gidx) for m in range(ATTN_Q_WIDTH // LANES) for gidx in range(ATTN_KV_HEADS)]
    scores = [lax.dot_general(q_ref[:, m * LANES:(m + 1) * LANES], keys_g[gidx], dn, preferred_element_type=F32)
              for m, gidx in heads]
    probs, denoms = [], []
    for (m, gidx), s in zip(heads, scores):
        sink = sink_ref[gidx * (ATTN_HEADS // ATTN_KV_HEADS) + m]
        s = jnp.where(valid, s, neg_inf)
        mx = jnp.maximum(jnp.max(s, axis=1, keepdims=True), sink)
        p = jnp.exp(s - mx)
        denoms.append(jnp.sum(p, axis=1, keepdims=True) + jnp.exp(sink - mx))
        probs.append(p.astype(BF16))
    pvs = [jnp.dot(p, vals, preferred_element_type=F32) for p in probs]
    for m in range(ATTN_Q_WIDTH // LANES):
        r0 = pvs[2 * m] / denoms[2 * m]
        r1 = pvs[2 * m + 1] / denoms[2 * m + 1]
        o_ref[:, m * LANES:(m + 1) * LANES] = jnp.where(out_lane < HEAD_DIM, r0, r1).astype(BF16)


def _attn_call(main, sink, seq):
    t = main.shape[0]
    nb = t // TB
    half = TB // 2
    nhb = t // half
    kcol, vcol = C_AK // LANES, C_AV // LANES
    prev_i = lambda i: jnp.maximum(2 * i - 1, 0)
    next_i = lambda i: jnp.minimum(2 * i + 2, nhb - 1)
    grid_spec = pltpu.PrefetchScalarGridSpec(
        num_scalar_prefetch=1,
        grid=(nb,),
        in_specs=[pl.BlockSpec((TB, ATTN_Q_WIDTH), lambda i, s: (i, C_AQ // ATTN_Q_WIDTH)),
                  pl.BlockSpec((half, LANES), lambda i, s: (prev_i(i), kcol)),
                  pl.BlockSpec((TB, LANES), lambda i, s: (i, kcol)),
                  pl.BlockSpec((half, LANES), lambda i, s: (next_i(i), kcol)),
                  pl.BlockSpec((TB, LANES), lambda i, s: (0, kcol)),
                  pl.BlockSpec((half, LANES), lambda i, s: (prev_i(i), vcol)),
                  pl.BlockSpec((TB, LANES), lambda i, s: (i, vcol)),
                  pl.BlockSpec((half, LANES), lambda i, s: (next_i(i), vcol)),
                  pl.BlockSpec((TB, LANES), lambda i, s: (0, vcol))],
        out_specs=pl.BlockSpec((TB, ATTN_Q_WIDTH), lambda i, s: (i, 0)),
    )
    return pl.pallas_call(
        functools.partial(_attn_kernel, seq=seq),
        grid_spec=grid_spec,
        out_shape=jax.ShapeDtypeStruct((t, ATTN_Q_WIDTH), BF16),
        compiler_params=_cparams(1),
        name="window_attn",
    )(sink, main, main, main, main, main, main, main, main, main)


def _group_mean(x, avg_bf16):
    hi = x.astype(BF16)
    lo = (x - hi.astype(F32)).astype(BF16)
    return (jnp.dot(hi, avg_bf16, preferred_element_type=F32) + jnp.dot(lo, avg_bf16, preferred_element_type=F32))


def _head_norm(x, avg_bf16, w):
    cen = x - _group_mean(x, avg_bf16)
    var = _group_mean(cen * cen, avg_bf16)
    return cen * lax.rsqrt(var + LN_EPS) * w


def _outproj_kernel(mf_ref, mb_ref, mo_ref, at_ref, gf_ref, gbk_ref, go_ref, mnw_ref, gnw_ref, avg_ref,
                    w_ref, x_ref, mod_ref, lnw_ref, lnb_ref, x1_ref, h2_ref, wbf_scr):
    i = pl.program_id(0)
    tm = x_ref.shape[0]

    @pl.when(i == 0)
    def _():
        for dst, src, n in _column_runs(_out_row_perm()):
            wbf_scr[dst:dst + n, :] = w_ref[src:src + n, :].astype(BF16)

    avg = avg_ref[...]
    hm = _head_norm(mf_ref[...] + mb_ref[...], avg, mnw_ref[...]) * _sigmoid(mo_ref[...].astype(F32))
    gate = go_ref[...].astype(F32)
    hg = _head_norm(gf_ref[...] + gbk_ref[...], avg, gnw_ref[...]) * (gate * _sigmoid(gate))
    a0, a1, a2 = MLSTM_WIDTH, MLSTM_WIDTH + ATTN_Q_WIDTH, D_MODEL
    mix = (jnp.dot(hm.astype(BF16), wbf_scr[:a0], preferred_element_type=F32)
           + jnp.dot(at_ref[...], wbf_scr[a0:a1], preferred_element_type=F32)
           + jnp.dot(hg.astype(BF16), wbf_scr[a1:a2], preferred_element_type=F32))
    x1 = _layer_norm(DEEPNORM_ALPHA * x_ref[...] + _select_mod(mod_ref, 2, i, tm) * mix, lnw_ref[...], lnb_ref[...])
    x1_ref[...] = x1
    h2_ref[...] = (x1 * (1.0 + _select_mod(mod_ref, 4, i, tm)) + _select_mod(mod_ref, 3, i, tm)).astype(h2_ref.dtype)


def _outproj_call(mf, mb, main, attn, gf, gbk, mnw, gnw, avg, w_out, layer, x, mod, lnw, lnb, h2_dtype):
    t, d = x.shape
    tm = PROJ_TM
    row = lambda i: (i, 0)
    const = lambda i: (0, 0)
    return pl.pallas_call(
        _outproj_kernel,
        grid=(t // tm,),
        in_specs=[pl.BlockSpec((tm, MLSTM_WIDTH), row),
                  pl.BlockSpec((tm, MLSTM_WIDTH), row),
                  pl.BlockSpec((tm, MLSTM_WIDTH), lambda i: (i, C_MO // MLSTM_WIDTH)),
                  pl.BlockSpec((tm, ATTN_Q_WIDTH), row),
                  pl.BlockSpec((tm, GLA_WIDTH), row),
                  pl.BlockSpec((tm, GLA_WIDTH), row),
                  pl.BlockSpec((tm, GLA_WIDTH), lambda i: (i, C_GO // GLA_WIDTH)),
                  pl.BlockSpec((1, MLSTM_WIDTH), const),
                  pl.BlockSpec((1, GLA_WIDTH), const),
                  pl.BlockSpec((MLSTM_WIDTH, MLSTM_WIDTH), const),
                  pl.BlockSpec((None, d, d), lambda i: (layer, 0, 0), pipeline_mode=pl.Buffered(1)),
                  pl.BlockSpec((tm, d), row),
                  pl.BlockSpec((2, SUBLANES, d), lambda i: (0, 0, 0)),
                  pl.BlockSpec((1, d), const),
                  pl.BlockSpec((1, d), const)],
        out_specs=[pl.BlockSpec((tm, d), row), pl.BlockSpec((tm, d), row)],
        out_shape=[jax.ShapeDtypeStruct((t, d), F32), jax.ShapeDtypeStruct((t, d), h2_dtype)],
        scratch_shapes=[pltpu.VMEM((d, d), BF16)],
        compiler_params=_cparams(1),
        name="out_proj",
    )(mf, mb, main, attn, gf, gbk, main, mnw, gnw, avg, w_out, x, mod, lnw, lnb)


def _select_mod(mod_ref, k, i, tm):
    rows = i * tm + lax.broadcasted_iota(jnp.int32, (tm, 1), 0)
    return jnp.where(rows < CTX_LEN, mod_ref[0, k:k + 1, :], mod_ref[1, k:k + 1, :])


def _ffn_kernel(h_ref, wgu_ref, wd_ref, x_ref, mod_ref, lnw_ref, lnb_ref, o_ref):
    h = h_ref[...]
    g = jnp.dot(h, wgu_ref[:, :D_FF], preferred_element_type=F32)
    u = jnp.dot(h, wgu_ref[:, D_FF:], preferred_element_type=F32)
    y = jnp.dot((g * _sigmoid(g) * u).astype(BF16), wd_ref[...], preferred_element_type=F32)
    m5 = _select_mod(mod_ref, 5, pl.program_id(0), h_ref.shape[0])
    o_ref[...] = _layer_norm(DEEPNORM_ALPHA * x_ref[...] + m5 * y, lnw_ref[...], lnb_ref[...])


def _ffn_call(h2, w_gu, w_down, layer, x1, mod, lnw, lnb):
    t, d = x1.shape
    resident = pl.Buffered(1)
    return pl.pallas_call(
        _ffn_kernel,
        grid=(t // FFN_TM,),
        in_specs=[pl.BlockSpec((FFN_TM, d), lambda i: (i, 0)),
                  pl.BlockSpec((None, d, 2 * D_FF), lambda i: (layer, 0, 0), pipeline_mode=resident),
                  pl.BlockSpec((None, D_FF, d), lambda i: (layer, 0, 0), pipeline_mode=resident),
                  pl.BlockSpec((FFN_TM, d), lambda i: (i, 0)),
                  pl.BlockSpec((2, SUBLANES, d), lambda i: (0, 0, 0)),
                  pl.BlockSpec((1, d), lambda i: (0, 0)),
                  pl.BlockSpec((1, d), lambda i: (0, 0))],
        out_specs=pl.BlockSpec((FFN_TM, d), lambda i: (i, 0)),
        out_shape=jax.ShapeDtypeStruct((t, d), F32),
        compiler_params=_cparams(1),
        name="dense_ffn",
    )(h2, w_gu, w_down, x1, mod, lnw, lnb)


def _router_kernel(h_ref, wr_ref, br_ref, su_ref, route_ref, cnt_ref, base_scr):
    @pl.when(pl.program_id(0) == 0)
    def _():
        base_scr[...] = jnp.zeros_like(base_scr)

    lt = _dot_bf16x3(wr_ref[...], h_ref[...], (((1,), (1,)), ((), ()))) + br_ref[...]
    idx = lax.broadcasted_iota(jnp.int32, lt.shape, 0)
    neg_inf = jnp.float32(-jnp.inf)
    m1 = jnp.max(lt, axis=0, keepdims=True)
    e1 = jnp.min(jnp.where(lt == m1, idx, N_EXPERTS), axis=0, keepdims=True)
    lt2 = jnp.where(idx == e1, neg_inf, lt)
    m2 = jnp.max(lt2, axis=0, keepdims=True)
    e2 = jnp.min(jnp.where(lt2 == m2, idx, N_EXPERTS), axis=0, keepdims=True)
    t2 = jnp.exp(m2 - m1)
    w1 = 1.0 / (1.0 + t2)
    w2 = t2 / (1.0 + t2)
    su = su_ref[...]
    base = base_scr[...][:, 0:1]
    oh1 = (idx == e1).astype(F32)
    cum1 = jnp.dot(oh1.astype(BF16), su, preferred_element_type=F32)
    rank1 = jnp.sum(oh1 * (base + cum1), axis=0, keepdims=True)
    base = base + jnp.sum(oh1, axis=1, keepdims=True)
    oh2 = (idx == e2).astype(F32)
    cum2 = jnp.dot(oh2.astype(BF16), su, preferred_element_type=F32)
    rank2 = jnp.sum(oh2 * (base + cum2), axis=0, keepdims=True)
    base = base + jnp.sum(oh2, axis=1, keepdims=True)
    base_scr[...] = jnp.broadcast_to(base, base_scr.shape)
    cnt_ref[...] = jnp.broadcast_to(base, cnt_ref.shape)
    zero = jnp.zeros_like(w1)
    route_ref[...] = jnp.concatenate(
        [e1.astype(F32), e2.astype(F32), w1, w2, rank1, rank2, zero, zero], axis=0)


def _router_call(h2, wr_t, br, su):
    t, d = h2.shape
    nb = t // TB
    return pl.pallas_call(
        _router_kernel,
        grid=(nb,),
        in_specs=[pl.BlockSpec((TB, d), lambda i: (i, 0)),
                  pl.BlockSpec((N_EXPERTS, d), lambda i: (0, 0)),
                  pl.BlockSpec((N_EXPERTS, 1), lambda i: (0, 0)),
                  pl.BlockSpec((TB, TB), lambda i: (0, 0))],
        out_specs=[pl.BlockSpec((SUBLANES, TB), lambda i: (0, i)),
                   pl.BlockSpec((N_EXPERTS, LANES), lambda i: (0, 0))],
        out_shape=[jax.ShapeDtypeStruct((SUBLANES, t), F32),
                   jax.ShapeDtypeStruct((N_EXPERTS, LANES), F32)],
        scratch_shapes=[pltpu.VMEM((N_EXPERTS, LANES), F32)],
        compiler_params=_cparams(1),
        name="moe_router",
    )(h2, wr_t, br, su)


ROW_TILE = D_MODEL // LANES
DMA_UNROLL = 8
ZERO_CHUNK = 64
CAST_STEPS = 64
CAST_PIECES = 4


def _store_row_tiles(ref, val):
    n = val.shape[0]
    for j in range(ROW_TILE):
        ref[pl.ds(j, n, stride=ROW_TILE), :] = val[:, j * LANES:(j + 1) * LANES]


def _load_row_tiles(ref, n, dtype=F32):
    return jnp.concatenate([ref[pl.ds(j, n, stride=ROW_TILE), :].astype(dtype) for j in range(ROW_TILE)], axis=1)


def _row(ref, idx):
    return ref.at[pl.ds(pl.multiple_of(idx * ROW_TILE, ROW_TILE), ROW_TILE)]


def _dispatch_kernel(s1_ref, s2_ref, pad_lo_ref, pad_hi_ref, h_ref, xs_ref, hr_scr, zero_scr, sem, zsem):
    i = pl.program_id(0)

    @pl.when(i == 0)
    def _():
        zero_scr[...] = jnp.zeros_like(zero_scr)

        def zero_rows(s, n):
            dst = xs_ref.at[pl.ds(pl.multiple_of(s * ROW_TILE, ROW_TILE), n * ROW_TILE)]
            return pltpu.make_async_copy(zero_scr.at[pl.ds(0, n * ROW_TILE)], dst, zsem)

        def fill(e, act):
            lo, hi = pad_lo_ref[e], pad_hi_ref[e]
            n_big = (hi - lo) // ZERO_CHUNK
            lax.fori_loop(0, n_big, lambda k, c: (act(zero_rows(lo + k * ZERO_CHUNK, ZERO_CHUNK)), c)[1], 0)
            lax.fori_loop(lo + n_big * ZERO_CHUNK, hi, lambda s, c: (act(zero_rows(s, 1)), c)[1], 0)

        for e in range(N_EXPERTS + 1):
            fill(e, lambda cp: cp.start())
        for e in range(N_EXPERTS + 1):
            fill(e, lambda cp: cp.wait())

    n = pl.num_programs(0)
    cur = i % 2

    def copy(blk, r, slot_ref):
        buf = blk % 2
        return pltpu.make_async_copy(_row(hr_scr.at[buf], r), _row(xs_ref, slot_ref[blk * TB + r]), sem.at[buf])

    def start(r, carry):
        copy(i, r, s1_ref).start()
        copy(i, r, s2_ref).start()
        return carry

    def wait_block(blk):
        def wait(r, carry):
            copy(blk, r, s1_ref).wait()
            copy(blk, r, s2_ref).wait()
            return carry
        lax.fori_loop(0, TB, wait, 0, unroll=DMA_UNROLL)

    _store_row_tiles(hr_scr.at[cur], h_ref[...])
    lax.fori_loop(0, TB, start, 0, unroll=DMA_UNROLL)

    @pl.when(i > 0)
    def _():
        wait_block(i - 1)

    @pl.when(i == n - 1)
    def _():
        wait_block(i)


def _dispatch_call(slot1, slot2, pad_lo, pad_hi, h2, n_slots):
    t, d = h2.shape
    grid_spec = pltpu.PrefetchScalarGridSpec(
        num_scalar_prefetch=4,
        grid=(t // TB,),
        in_specs=[pl.BlockSpec((TB, d), lambda i, *_: (i, 0))],
        out_specs=pl.BlockSpec(memory_space=pl.ANY),
        scratch_shapes=[pltpu.VMEM((2, TB * ROW_TILE, LANES), F32), pltpu.VMEM((ZERO_CHUNK * ROW_TILE, LANES), F32),
                        pltpu.SemaphoreType.DMA((2,)), pltpu.SemaphoreType.DMA(())],
    )
    return pl.pallas_call(
        _dispatch_kernel,
        grid_spec=grid_spec,
        out_shape=jax.ShapeDtypeStruct((n_slots * ROW_TILE, LANES), F32),
        compiler_params=_cparams(1),
        name="moe_dispatch",
    )(slot1, slot2, pad_lo, pad_hi, h2)


def _expert_kernel(be_ref, used_ref, x_ref, wg_ref, wu_ref, wd_ref, y_ref, xb_scr, acc_scr, *, n_f):
    b, f = pl.program_id(0), pl.program_id(1)
    live = b < used_ref[0]
    last = n_f - 1

    def swiglu_part():
        x = xb_scr[...]
        g = jnp.dot(x, wg_ref[...], preferred_element_type=F32)
        u = jnp.dot(x, wu_ref[...], preferred_element_type=F32)
        return jnp.dot((g * _sigmoid(g) * u).astype(BF16), wd_ref[...], preferred_element_type=F32)

    @pl.when(live & (f == 0))
    def _():
        for j in range(ROW_TILE):
            xb_scr[:, j * LANES:(j + 1) * LANES] = x_ref[pl.ds(j, MOE_G, stride=ROW_TILE), :].astype(BF16)
        acc_scr[...] = swiglu_part()

    if n_f > 2:
        @pl.when(live & (f > 0) & (f < last))
        def _():
            acc_scr[...] += swiglu_part()

    @pl.when(live & (f == last))
    def _():
        _store_row_tiles(y_ref, acc_scr[...] + swiglu_part())

    @pl.when(jnp.logical_not(live) & (f == last))
    def _():
        y_ref[...] = jnp.zeros_like(y_ref)


def _expert_call(block_e, used, xs, w_gu, w_down):
    d = D_MODEL
    ns = xs.shape[0] // ROW_TILE
    nblk = ns // MOE_G
    nf = D_EXPERT // MOE_TF
    rows = MOE_G * ROW_TILE

    def f_eff(b, f, used_ref):
        return jnp.where(b < used_ref[0], f, nf - 1)

    grid_spec = pltpu.PrefetchScalarGridSpec(
        num_scalar_prefetch=2,
        grid=(nblk, nf),
        in_specs=[pl.BlockSpec((rows, LANES), lambda b, f, be, us: (b, 0)),
                  pl.BlockSpec((None, d, MOE_TF), lambda b, f, be, us: (be[b], 0, f_eff(b, f, us))),
                  pl.BlockSpec((None, d, MOE_TF), lambda b, f, be, us: (be[b], 0, nf + f_eff(b, f, us))),
                  pl.BlockSpec((None, MOE_TF, d), lambda b, f, be, us: (be[b], f_eff(b, f, us), 0))],
        out_specs=pl.BlockSpec((rows, LANES), lambda b, f, be, us: (b, 0)),
        scratch_shapes=[pltpu.VMEM((MOE_G, d), BF16), pltpu.VMEM((MOE_G, d), F32)],
    )
    return pl.pallas_call(
        functools.partial(_expert_kernel, n_f=nf),
        grid_spec=grid_spec,
        out_shape=jax.ShapeDtypeStruct(xs.shape, F32),
        compiler_params=_cparams(2),
        name="moe_experts",
    )(block_e, used, xs, w_gu, w_gu, w_down)


def _combine_kernel(s1_ref, s2_ref, yb_ref, route_ref, x_ref, mod_ref, lnw_ref, lnb_ref, o_ref, buf1, buf2, sem):
    i = pl.program_id(0)
    n = pl.num_programs(0)
    cur = i % 2

    def copy(blk, r, slot_ref, buf):
        b = blk % 2
        return pltpu.make_async_copy(_row(yb_ref, slot_ref[blk * TB + r]), _row(buf.at[b], r), sem.at[b])

    def start_block(blk):
        def start(r, carry):
            copy(blk, r, s1_ref, buf1).start()
            copy(blk, r, s2_ref, buf2).start()
            return carry
        lax.fori_loop(0, TB, start, 0, unroll=DMA_UNROLL)

    def wait(r, carry):
        copy(i, r, s1_ref, buf1).wait()
        copy(i, r, s2_ref, buf2).wait()
        return carry

    @pl.when(i == 0)
    def _():
        start_block(i)

    @pl.when(i + 1 < n)
    def _():
        start_block(i + 1)

    lax.fori_loop(0, TB, wait, 0, unroll=DMA_UNROLL)
    rt = route_ref[...].T
    y = rt[:, 2:3] * _load_row_tiles(buf1.at[cur], TB) + rt[:, 3:4] * _load_row_tiles(buf2.at[cur], TB)
    m = mod_ref[...]
    o_ref[...] = _layer_norm(DEEPNORM_ALPHA * x_ref[...] + m[5:6] * y, lnw_ref[...], lnb_ref[...])


def _combine_call(slot1, slot2, yb, route, x1, mod, lnw, lnb, latent_only):
    t, d = x1.shape
    if latent_only:
        out_rows, out_map = t - CTX_LEN, lambda i, a, b: (jnp.maximum(i - 1, 0), 0)
    else:
        out_rows, out_map = t, lambda i, a, b: (i, 0)
    grid_spec = pltpu.PrefetchScalarGridSpec(
        num_scalar_prefetch=2,
        grid=(t // TB,),
        in_specs=[pl.BlockSpec(memory_space=pl.ANY),
                  pl.BlockSpec((SUBLANES, TB), lambda i, a, b: (0, i)),
                  pl.BlockSpec((TB, d), lambda i, a, b: (i, 0)),
                  pl.BlockSpec((None, SUBLANES, d), lambda i, a, b: (_who(i), 0, 0)),
                  pl.BlockSpec((1, d), lambda i, a, b: (0, 0)),
                  pl.BlockSpec((1, d), lambda i, a, b: (0, 0))],
        out_specs=pl.BlockSpec((TB, d), out_map),
        scratch_shapes=[pltpu.VMEM((2, TB * ROW_TILE, LANES), F32), pltpu.VMEM((2, TB * ROW_TILE, LANES), F32),
                        pltpu.SemaphoreType.DMA((2,))],
    )
    return pl.pallas_call(
        _combine_kernel,
        grid_spec=grid_spec,
        out_shape=jax.ShapeDtypeStruct((out_rows, d), F32),
        compiler_params=_cparams(1),
        name="moe_combine",
    )(slot1, slot2, yb, route, x1, mod, lnw, lnb)


def _moe_layer(h2, x1, mod, lnw, lnb, w_router, b_router, w_gu, w_down, su, latent_only):
    t, d = h2.shape
    route, cnt = _router_call(h2, w_router.T, b_router.reshape(N_EXPERTS, 1), su)
    counts = cnt[:, 0].astype(jnp.int32)
    padded = (counts + MOE_G - 1) // MOE_G * MOE_G
    pend = jnp.cumsum(padded)
    pstart = pend - padded
    e1, e2 = route[0].astype(jnp.int32), route[1].astype(jnp.int32)
    slot1 = pstart[e1] + route[4].astype(jnp.int32)
    slot2 = pstart[e2] + route[5].astype(jnp.int32)
    nblk = -(-(2 * t) // MOE_G) + N_EXPERTS
    blk_start = jnp.arange(nblk, dtype=jnp.int32) * MOE_G
    block_e = jnp.minimum(jnp.sum((pend[None, :] <= blk_start[:, None]).astype(jnp.int32), axis=1), N_EXPERTS - 1)
    used = (pend[-1:] // MOE_G).astype(jnp.int32)
    n_slots = nblk * MOE_G
    pad_lo = jnp.concatenate([pstart + counts, pend[-1:]])
    pad_hi = jnp.concatenate([pend, jnp.full((1,), n_slots, jnp.int32)])
    xs = _dispatch_call(slot1, slot2, pad_lo, pad_hi, h2, n_slots)
    yb = _expert_call(block_e, used, xs, w_gu, w_down)
    return _combine_call(slot1, slot2, yb, route, x1, mod, lnw, lnb, latent_only)


def _attn_head_perm():
    cols = []
    for m in range(ATTN_HEADS // ATTN_KV_HEADS):
        for hq in (m, m + ATTN_HEADS // ATTN_KV_HEADS):
            cols.extend(range(hq * HEAD_DIM, (hq + 1) * HEAD_DIM))
    return np.asarray(cols, np.int32)


def _in_col_perm():
    n_m = 4 * MLSTM_WIDTH
    mg = 2 * 2 * MLSTM_HEADS
    a0 = n_m + mg
    perm = list(range(n_m))
    perm += [a0 + int(j) for j in _attn_head_perm()]
    perm += list(range(a0 + ATTN_Q_WIDTH, a0 + ATTN_Q_WIDTH + 2 * ATTN_KV_WIDTH))
    g0 = a0 + ATTN_Q_WIDTH + 2 * ATTN_KV_WIDTH
    perm += list(range(g0, g0 + 2 * GLA_KEY_WIDTH + 2 * GLA_WIDTH))
    perm += list(range(n_m, n_m + mg))
    perm += list(range(g0 + 2 * GLA_KEY_WIDTH + 2 * GLA_WIDTH, g0 + 2 * GLA_KEY_WIDTH + 2 * GLA_WIDTH + 2 * GLA_RANK))
    return np.asarray(perm, np.int32)


def _out_row_perm():
    return np.concatenate([np.arange(MLSTM_WIDTH), MLSTM_WIDTH + _attn_head_perm(),
                           np.arange(MLSTM_WIDTH + ATTN_Q_WIDTH, D_MODEL)]).astype(np.int32)


def _column_runs(perm):
    runs, start = [], 0
    for j in range(1, len(perm) + 1):
        if j == len(perm) or perm[j] != perm[j - 1] + 1:
            runs.append((start, int(perm[start]), j - start))
            start = j
    return runs


def _rope_tables(seq):
    inv = ROPE_BASE ** (-jnp.arange(ROPE_PAIRS, dtype=F32) / ROPE_PAIRS)
    rows = seq // GRID_W
    ang_r = jnp.arange(rows).astype(F32)[:, None] * inv
    ang_c = jnp.arange(GRID_W).astype(F32)[:, None] * inv
    lane_pat = lambda a, b: jnp.tile(jnp.concatenate([a, b], -1), (1, LANES // (2 * ROPE_PAIRS)))
    cos_rows, sin_rows = lane_pat(jnp.cos(ang_r), jnp.cos(ang_r)), lane_pat(-jnp.sin(ang_r), jnp.sin(ang_r))
    cos_cols, sin_cols = lane_pat(jnp.cos(ang_c), jnp.cos(ang_c)), lane_pat(-jnp.sin(ang_c), jnp.sin(ang_c))
    row_part = (np.arange(LANES) % HEAD_DIM) < HEAD_DIM // 2
    per_token = lambda by_row, by_col: jnp.where(
        row_part, jnp.broadcast_to(by_row[:, None, :], (rows, GRID_W, LANES)),
        jnp.broadcast_to(by_col[None, :, :], (rows, GRID_W, LANES))).reshape(seq, LANES)
    cos_l, sin_l = per_token(cos_rows, cos_cols), per_token(sin_rows, sin_cols)
    cos_t = jnp.concatenate([jnp.ones((CTX_LEN, LANES), F32), cos_l], 0)
    sin_t = jnp.concatenate([jnp.zeros((CTX_LEN, LANES), F32), sin_l], 0)
    return cos_t, sin_t


def _block_tri(direction):
    r = np.arange(TB)[:, None]
    c = np.arange(TB)[None, :]
    same = (r // CHUNK) == (c // CHUNK)
    tri = (c <= r) if direction == 0 else (c >= r)
    return jnp.asarray(same & tri, BF16)


def kernel(x, c, ctx, c_ctx, w_ada, b_ada, w_in, mlstm_gate_b, mlstm_norm_w, attn_sink, gla_gate_up, gla_gate_b,
           gla_norm_w, w_out, ln_w, ln_b, ffn_w_gu, ffn_w_down, router_w, router_b, moe_w_gu, moe_w_down):
    seq, d = x.shape[1], x.shape[2]
    depth = w_in.shape[0]
    t = ctx.shape[1] + seq
    assert x.shape[0] == 1 and d == D_MODEL and ctx.shape[1] == CTX_LEN == TB and seq % TB == 0
    assert t % PROJ_TM == 0 and t % FFN_TM == 0
    xt = jnp.concatenate([ctx[0], x[0]], axis=0)

    cvec_t = jnp.zeros((d, SUBLANES), F32).at[:, 0].set(c_ctx).at[:, 1].set(c[0])
    mods = _mod_call(cvec_t, w_ada, b_ada)[:, :2].reshape(depth, 2, 6, d)
    mods = jnp.pad(mods, ((0, 0), (0, 0), (0, SUBLANES - 6), (0, 0)))

    gate_bias = jnp.pad(mlstm_gate_b.reshape(depth, 1, -1), ((0, 0), (0, 0), (0, N_GATE - 4 * MLSTM_HEADS)))
    gup_pad = jnp.zeros((depth, 2, N_GATE, GLA_KEY_WIDTH), F32)
    for dr in range(2):
        lo = GLA_GATE_OFF + dr * GLA_RANK
        gup_pad = gup_pad.at[:, dr, lo:lo + GLA_RANK, :].set(gla_gate_up[:, dr])
    cos_t, sin_t = _rope_tables(seq)
    tri = [_block_tri(0), _block_tri(1)]
    su = jnp.asarray(np.arange(TB)[:, None] < np.arange(TB)[None, :], BF16)
    hh = np.arange(MLSTM_WIDTH) // MLSTM_DH
    avg = jnp.asarray((hh[:, None] == hh[None, :]) / MLSTM_DH, BF16)
    ffn_gu, ffn_dn = ffn_w_gu.astype(BF16), ffn_w_down.astype(BF16)
    n_moe, n_exp, _, n_gu = moe_w_gu.shape
    gu_2d, dn_2d = moe_w_gu.reshape(-1, n_gu), moe_w_down.reshape(-1, d)
    cast_src = {2 * m: (gu_2d, m, n_moe) for m in range(n_moe)}
    cast_src.update({2 * m + 1: (dn_2d, m, n_moe) for m in range(n_moe)})
    cast_dst = {}

    expand = []
    for dr in range(2):
        e = np.zeros((N_GATE, MLSTM_WIDTH), np.float32)
        for h in range(MLSTM_HEADS):
            e[dr * 2 * MLSTM_HEADS + h, h * MLSTM_DH:(h + 1) * MLSTM_DH] = 1.0
        expand.append(jnp.asarray(e, BF16))

    for l in range(depth):
        is_moe = l % 2 == 1
        last = l == depth - 1
        main, gates = _inproj_call(xt, mods[l], w_in, l, cos_t, sin_t)
        mf, mb, gf, gbk, *cast = _scans_call(main, gates, gate_bias[l], tri, expand, gup_pad[l], gla_gate_b[l],
                                             cast_src.get(l))
        if cast:
            cast_dst[l] = cast[0]
        attn = _attn_call(main, attn_sink[l], seq)
        x1, h2 = _outproj_call(mf, mb, main, attn, gf, gbk, mlstm_norm_w[l:l + 1], gla_norm_w[l:l + 1], avg,
                               w_out, l, xt, mods[l], ln_w[l, 0:1], ln_b[l, 0:1], F32 if is_moe else BF16)
        if is_moe:
            moe_gu = cast_dst[l - 1].reshape(n_exp, d, n_gu)
            moe_dn = cast_dst[l].reshape(n_exp, n_gu // 2, d)
            xt = _moe_layer(h2, x1, mods[l], ln_w[l, 1:2], ln_b[l, 1:2], router_w[l // 2], router_b[l // 2],
                            moe_gu, moe_dn, su, latent_only=last)
        else:
            xt = _ffn_call(h2, ffn_gu, ffn_dn, l // 2, x1, mods[l], ln_w[l, 1:2], ln_b[l, 1:2])
    return (xt if depth % 2 == 0 else xt[CTX_LEN:])[None]
```

```python
import functools

import jax
import jax.numpy as jnp
import numpy as np
from jax import lax
from jax.experimental import pallas as pl
from jax.experimental.pallas import tpu as pltpu

F32 = jnp.float32
BF16 = jnp.bfloat16

D_MODEL = 1024
SEQ = 16384
DEPTH = 4
GRID_W = 64
CTX_LEN = 256
MLSTM_HEADS = 4
MLSTM_DH = 64
MLSTM_WIDTH = 256
HEAD_DIM = 64
ATTN_HEADS = 8
ATTN_KV_HEADS = 2
ATTN_Q_WIDTH = 512
ATTN_KV_WIDTH = 128
WINDOW = 128
ROPE_BASE = 10000.0
ROPE_PAIRS = 16
GLA_HEADS = 4
GLA_DK = 32
GLA_DV = 64
GLA_WIDTH = 256
GLA_KEY_WIDTH = 128
GLA_RANK = 16
GLA_TAU = 16.0
CHUNK = 64
D_FF = 2816
N_EXPERTS = 8
D_EXPERT = 3584
DEEPNORM_ALPHA = (2 * DEPTH) ** 0.25
LN_EPS = 1e-5

LANES = 128
SUBLANES = 8
VMEM_LIMIT = 56 * 1024 * 1024

TB = 256
N_CHUNKS = TB // CHUNK
ADA_TN = 1536
PROJ_TM = 640
FFN_TM = 640
MOE_G = 512
MOE_TF = 1792

C_MQ, C_MK, C_MV, C_MO = 0, 256, 512, 768
C_AQ, C_AK, C_AV = 1024, 1536, 1664
C_GQ, C_GK, C_GV, C_GO = 1792, 1920, 2048, 2304
N_MAIN = 2560
N_GATE = 128
N_PROJ = N_MAIN + N_GATE
GLA_GATE_OFF = 16


def _cparams(n_axes=1):
    return pltpu.CompilerParams(dimension_semantics=("arbitrary",) * n_axes,
                                vmem_limit_bytes=VMEM_LIMIT)


def _sigmoid(x):
    return 1.0 / (1.0 + jnp.exp(-x))


def _log_sigmoid(x):
    return jnp.minimum(x, 0.0) - jnp.log(1.0 + jnp.exp(-jnp.abs(x)))


def _split3(x):
    hi = x.astype(BF16)
    r1 = x - hi.astype(F32)
    mid = r1.astype(BF16)
    lo = (r1 - mid.astype(F32)).astype(BF16)
    return hi, mid, lo


def _dot_exact_rhs(x, m_bf16):
    hi, mid, lo = _split3(x)
    d = lambda a: jnp.dot(a, m_bf16, preferred_element_type=F32)
    return d(hi) + d(mid) + d(lo)


def _dot_exact_lhs(m_bf16, x):
    hi, mid, lo = _split3(x)
    d = lambda a: jnp.dot(m_bf16, a, preferred_element_type=F32)
    return d(hi) + d(mid) + d(lo)


def _dot_bf16x3(a, b, dims=(((1,), (0,)), ((), ()))):
    a_hi, b_hi = a.astype(BF16), b.astype(BF16)
    a_lo = (a - a_hi.astype(F32)).astype(BF16)
    b_lo = (b - b_hi.astype(F32)).astype(BF16)
    d = lambda x, y: lax.dot_general(x, y, dims, preferred_element_type=F32)
    return d(a_hi, b_hi) + d(a_hi, b_lo) + d(a_lo, b_hi)


def _layer_norm(v, w, b):
    mu = jnp.mean(v, axis=-1, keepdims=True)
    cen = v - mu
    var = jnp.mean(cen * cen, axis=-1, keepdims=True)
    return cen * lax.rsqrt(var + LN_EPS) * w + b


def _mod_kernel(ct_ref, w_ref, b_ref, o_ref):
    c = ct_ref[...]
    sc = c * _sigmoid(c)
    w = w_ref[...]
    rows = [jnp.sum(sc[:, r:r + 1] * w, axis=0, keepdims=True) for r in range(2)]
    pad = jnp.zeros((SUBLANES - 2, w.shape[1]), F32)
    o_ref[...] = jnp.concatenate(rows + [pad], axis=0) + b_ref[...]


def _mod_call(cvec_t, w_ada, b_ada):
    depth, d, n = w_ada.shape
    tn = ADA_TN
    return pl.pallas_call(
        _mod_kernel,
        grid=(depth, n // tn),
        in_specs=[pl.BlockSpec((d, SUBLANES), lambda l, j: (0, 0)),
                  pl.BlockSpec((None, d, tn), lambda l, j: (l, 0, j)),
                  pl.BlockSpec((None, 1, tn), lambda l, j: (l, 0, j))],
        out_specs=pl.BlockSpec((None, SUBLANES, tn), lambda l, j: (l, 0, j)),
        out_shape=jax.ShapeDtypeStruct((depth, SUBLANES, n), F32),
        compiler_params=_cparams(2),
        name="ada_mod",
    )(cvec_t, w_ada, b_ada.reshape(depth, 1, n))


def _who(i):
    return jnp.minimum(i, 1)


def _rope(x, cos, sin_signed, first_half):
    swapped = jnp.where(first_half, pltpu.roll(x, LANES - ROPE_PAIRS, 1), pltpu.roll(x, ROPE_PAIRS, 1))
    return x * cos + swapped * sin_signed


def _inproj_kernel(x_ref, mod_ref, w_ref, cos_ref, sin_ref, main_ref, gate_ref, wbf_scr):
    i = pl.program_id(0)
    tm = x_ref.shape[0]

    @pl.when(i == 0)
    def _():
        for dst, src, n in _column_runs(_in_col_perm()):
            wbf_scr[:, dst:dst + n] = w_ref[:, src:src + n].astype(BF16)
        wbf_scr[:, w_ref.shape[1]:] = jnp.zeros((w_ref.shape[0], N_PROJ - w_ref.shape[1]), BF16)

    h = x_ref[...] * (1.0 + _select_mod(mod_ref, 1, i, tm)) + _select_mod(mod_ref, 0, i, tm)
    p = jnp.dot(h.astype(BF16), wbf_scr[...], preferred_element_type=F32)
    cos, sin = cos_ref[...], sin_ref[...]
    lane = lax.broadcasted_iota(jnp.int32, (tm, LANES), 1)
    first_half = (lane % (2 * ROPE_PAIRS)) < ROPE_PAIRS
    main_ref[:, :C_AQ] = p[:, :C_AQ].astype(BF16)
    for j in range(ATTN_Q_WIDTH // LANES):
        lo = C_AQ + j * LANES
        main_ref[:, lo:lo + LANES] = (_rope(p[:, lo:lo + LANES], cos, sin, first_half) * HEAD_DIM ** -0.5).astype(BF16)
    main_ref[:, C_AK:C_AV] = _rope(p[:, C_AK:C_AV], cos, sin, first_half).astype(BF16)
    main_ref[:, C_AV:] = p[:, C_AV:N_MAIN].astype(BF16)
    gate_ref[...] = p[:, N_MAIN:]


def _inproj_call(x, mod, w_in, layer, cos_t, sin_t):
    t, d = x.shape
    n_in = w_in.shape[2]
    tm = PROJ_TM
    return pl.pallas_call(
        _inproj_kernel,
        grid=(t // tm,),
        in_specs=[pl.BlockSpec((tm, d), lambda i: (i, 0)),
                  pl.BlockSpec((2, SUBLANES, d), lambda i: (0, 0, 0)),
                  pl.BlockSpec((None, d, n_in), lambda i: (layer, 0, 0), pipeline_mode=pl.Buffered(1)),
                  pl.BlockSpec((tm, LANES), lambda i: (i, 0)),
                  pl.BlockSpec((tm, LANES), lambda i: (i, 0))],
        out_specs=[pl.BlockSpec((tm, N_MAIN), lambda i: (i, 0)),
                   pl.BlockSpec((tm, N_GATE), lambda i: (i, 0))],
        out_shape=[jax.ShapeDtypeStruct((t, N_MAIN), BF16),
                   jax.ShapeDtypeStruct((t, N_GATE), F32)],
        scratch_shapes=[pltpu.VMEM((d, N_PROJ), BF16)],
        compiler_params=_cparams(1),
        name="in_proj",
    )(x, mod, w_in, cos_t, sin_t)


def _scan_block(i, nb, direction):
    if direction == 0:
        return i
    return jnp.where(i == 0, 0, nb - i)


def _scans_kernel(*refs, cast_weights):
    (mq_f, mk_f, mv_f, gq_f, gk_f, gv_f, g_f, mq_b, mk_b, mv_b, gq_b, gk_b, gv_b, g_b,
     gbias, tri_f, tri_b, exp_f, exp_b, gup_f, gup_b, glb_f, glb_b) = refs[:23]
    refs = refs[23:]
    if cast_weights:
        w_src, refs = refs[0], refs[1:]
    om_f, om_b, og_f, og_b = refs[:4]
    refs = refs[4:]
    if cast_weights:
        w_dst, refs = refs[0], refs[1:]
    c_f, n_f, m_f, c_b, n_b, m_b, s_f, s_b = refs

    @pl.when(pl.program_id(0) == 0)
    def _():
        for scr in (c_f, n_f, m_f, c_b, n_b, m_b, s_f, s_b):
            scr[...] = jnp.zeros_like(scr)

    streams = [
        _mlstm_phases(mq_f, mk_f, mv_f, g_f, gbias, tri_f, exp_f, om_f, c_f, n_f, m_f, 0),
        _mlstm_phases(mq_b, mk_b, mv_b, g_b, gbias, tri_b, exp_b, om_b, c_b, n_b, m_b, 1),
        _gla_phases(gq_f, gk_f, gv_f, g_f, gup_f, glb_f, tri_f, og_f, s_f, 0),
        _gla_phases(gq_b, gk_b, gv_b, g_b, gup_b, glb_b, tri_b, og_b, s_b, 1),
    ]
    if cast_weights:
        rows = w_src.shape[0] // CAST_PIECES

        def cast_piece(k):
            def run():
                w_dst[k * rows:(k + 1) * rows, :] = w_src[k * rows:(k + 1) * rows, :].astype(BF16)
            return run
        streams.append([cast_piece(k) for k in range(CAST_PIECES)])
    _run_interleaved(streams)


def _run_interleaved(streams):
    for k in range(max(len(s) for s in streams)):
        for s in streams:
            if k < len(s):
                s[k]()


def _mlstm_phases(q_ref, k_ref, v_ref, g_ref, gb_ref, tri_ref, exp_ref, o_ref, c_scr, n_scr, m_scr, direction):
    fwd = direction == 0
    neg_inf = jnp.float32(-jnp.inf)
    order = list(range(N_CHUNKS)) if fwd else list(range(N_CHUNKS - 1, -1, -1))
    pairs = list(range(MLSTM_HEADS // 2))
    inst = [(p, c) for p in pairs for c in order]
    last_of = lambda c: c * CHUNK + (CHUNK - 1 if fwd else 0)
    ts_of = lambda c: slice(c * CHUNK, (c + 1) * CHUNK)
    ts2_of = lambda c: slice((c // 2) * 2 * CHUNK, (c // 2 + 1) * 2 * CHUNK)
    lp_of = lambda p: slice(p * LANES, (p + 1) * LANES)
    dn_nt = (((1,), (1,)), ((), ()))
    st = {}

    lane = lax.broadcasted_iota(jnp.int32, (CHUNK, LANES), 1)
    trow = lax.broadcasted_iota(jnp.int32, (CHUNK, LANES), 0)
    low = lane < MLSTM_DH
    tri_ok = (lane % MLSTM_DH <= trow) if fwd else (lane % MLSTM_DH >= trow)
    lane2 = lax.broadcasted_iota(jnp.int32, (LANES, LANES), 1)
    row2 = lax.broadcasted_iota(jnp.int32, (LANES, LANES), 0)
    same_head = (lane2 < MLSTM_DH) == (row2 < MLSTM_DH)
    bd_ones = jnp.where(same_head, 1.0, 0.0).astype(BF16)
    lane_row = lax.broadcasted_iota(jnp.int32, (1, LANES), 1)
    pair_half = lax.broadcasted_iota(jnp.int32, (2 * CHUNK, LANES), 0) // CHUNK

    def pair_blockdiag(x):
        zero = jnp.zeros_like(x)
        return jnp.concatenate([jnp.where(low, x, zero), jnp.where(low, zero, x)], axis=0)

    def ph_cumsum():
        st['g'] = g_ref[...] + gb_ref[...]
        st['b_f'] = _dot_exact_lhs(tri_ref[...], _log_sigmoid(st['g']))

    def ph_gates():
        b_i = pltpu.roll(st['b_f'], LANES - MLSTM_HEADS, 1)
        gr = st['g'] - b_i
        tok = lax.broadcasted_iota(jnp.int32, (TB, LANES), 0) % CHUNK
        aloc = gr
        for sh in (1, 2, 4, 8, 16, 32):
            if fwd:
                shifted, ok = pltpu.roll(aloc, sh, 0), tok >= sh
            else:
                shifted, ok = pltpu.roll(aloc, TB - sh, 0), tok < CHUNK - sh
            aloc = jnp.maximum(aloc, jnp.where(ok, shifted, neg_inf))
        chunk_max = jnp.concatenate(
            [jnp.broadcast_to(aloc[last_of(c):last_of(c) + 1], (CHUNK, LANES)) for c in range(N_CHUNKS)], axis=0)
        st['b_i'], st['aloc'] = b_i, aloc
        st['wloc'] = jnp.exp(gr - chunk_max)
        st['gr_t'] = gr.T

    def ph_expand():
        expand = exp_ref[...]
        st['aloc_x'] = _dot_exact_rhs(st['aloc'], expand)
        st['b_x'] = _dot_exact_rhs(st['b_i'], expand)
        st['wloc_x'] = _dot_exact_rhs(st['wloc'], expand)

    def ph_scores():
        for p, c in inst:
            kp = k_ref[ts_of(c), lp_of(p)] * MLSTM_DH ** -0.5
            st['s', p, c] = lax.dot_general(q_ref[ts_of(c), lp_of(p)], pair_blockdiag(kp), dn_nt,
                                            preferred_element_type=F32)
        for p in pairs:
            for c2 in range(N_CHUNKS // 2):
                st['kt2', p, c2] = (k_ref[ts2_of(2 * c2), lp_of(p)].astype(F32) * MLSTM_DH ** -0.5).T.astype(BF16)

    def ph_weights():
        for p, c in inst:
            ji0 = direction * 2 * MLSTM_HEADS + 2 * p
            ra = st['gr_t'][ji0:ji0 + 1, ts2_of(c)]
            rb = st['gr_t'][ji0 + 1:ji0 + 2, ts2_of(c)]
            if c % 2 == 0:
                g_row = jnp.where(lane_row < MLSTM_DH, ra, pltpu.roll(rb, MLSTM_DH, 1))
            else:
                g_row = jnp.where(lane_row < MLSTM_DH, pltpu.roll(ra, MLSTM_DH, 1), rb)
            al = st['aloc_x'][ts_of(c), lp_of(p)]
            s = st['s', p, c] * jnp.exp(jnp.where(tri_ok, g_row - al, neg_inf))
            s_hi = s.astype(BF16)
            st['s_hi', p, c] = s_hi
            st['s_lo', p, c] = (s - s_hi.astype(F32)).astype(BF16)
            wx2 = st['wloc_x'][ts2_of(c), lp_of(p)]
            in_chunk = pair_half == (c % 2)
            wv = jnp.where(in_chunk, wx2 * v_ref[ts2_of(c), lp_of(p)].astype(F32), 0.0)
            ww = jnp.where(in_chunk, wx2, 0.0)
            st['wvw', p, c] = jnp.concatenate([wv, ww], axis=1).astype(BF16)

    def ph_intra():
        for p, c in inst:
            vp = v_ref[ts_of(c), lp_of(p)]
            st['r1', p, c] = jnp.dot(st['s_hi', p, c], jnp.concatenate([pair_blockdiag(vp), bd_ones], axis=1),
                                     preferred_element_type=F32)
            st['den_lo', p, c] = jnp.dot(st['s_lo', p, c], bd_ones, preferred_element_type=F32)
            st['upd', p, c] = jnp.dot(st['kt2', p, c // 2], st['wvw', p, c], preferred_element_type=F32)

    def ph_state():
        for p in pairs:
            bdc, bdn = c_scr[p], n_scr[p]
            m_row = m_scr[p][0:1]
            for c in order:
                st['cn', p, c] = jnp.concatenate([bdc, bdn], axis=1).astype(BF16)
                st['m', p, c] = m_row
                last = last_of(c)
                aloc_last = st['aloc_x'][last:last + 1, lp_of(p)]
                a_last = jnp.maximum(m_row, aloc_last)
                decay = jnp.exp(m_row - a_last)
                scale = jnp.exp(aloc_last - a_last)
                upd = st['upd', p, c]
                bdc = decay * bdc + scale * jnp.where(same_head, upd[:, :LANES], 0.0)
                bdn = decay * bdn + scale * jnp.where(same_head, upd[:, LANES:], 0.0)
                m_row = st['b_x'][last:last + 1, lp_of(p)] + a_last
            c_scr[p] = bdc
            n_scr[p] = bdn
            m_scr[p] = jnp.broadcast_to(m_row, (SUBLANES, LANES))

    def ph_inter():
        for p, c in inst:
            st['r2', p, c] = jnp.dot(q_ref[ts_of(c), lp_of(p)], st['cn', p, c],
                                     preferred_element_type=F32)

    def ph_out():
        for p, c in inst:
            al = st['aloc_x'][ts_of(c), lp_of(p)]
            m_row, r1, r2 = st['m', p, c], st['r1', p, c], st['r2', p, c]
            a = jnp.maximum(m_row, al)
            corr = jnp.exp(al - a)
            w_inter = jnp.exp(m_row - a)
            num = corr * r1[:, :LANES] + w_inter * r2[:, :LANES]
            den = corr * (r1[:, LANES:] + st['den_lo', p, c]) + w_inter * r2[:, LANES:]
            o_ref[ts_of(c), lp_of(p)] = num / jnp.maximum(jnp.abs(den),
                                                          jnp.exp(-(st['b_x'][ts_of(c), lp_of(p)] + a)))

    return [ph_cumsum, ph_gates, ph_expand, ph_scores, ph_weights, ph_intra, ph_state, ph_inter, ph_out]


def _gla_phases(q_ref, k_ref, v_ref, g_ref, gup_ref, gb_ref, tri_ref, o_ref, s_scr, direction):
    fwd = direction == 0
    order = list(range(N_CHUNKS)) if fwd else list(range(N_CHUNKS - 1, -1, -1))
    ts_of = lambda c: slice(c * CHUNK, (c + 1) * CHUNK)
    ts2_of = lambda c: slice((c // 2) * 2 * CHUNK, (c // 2 + 1) * 2 * CHUNK)
    last_of = lambda c: c * CHUNK + (CHUNK - 1 if fwd else 0)
    dn_nt = (((1,), (1,)), ((), ()))
    st = {}

    khead = lax.broadcasted_iota(jnp.int32, (CHUNK, GLA_KEY_WIDTH), 1) // GLA_DK
    vlane = lax.broadcasted_iota(jnp.int32, (CHUNK, GLA_WIDTH), 1)
    vhead = vlane // GLA_DV
    trow = lax.broadcasted_iota(jnp.int32, (CHUNK, GLA_WIDTH), 0)
    tri_ok = (vlane % CHUNK <= trow) if fwd else (vlane % CHUNK >= trow)
    st_head_r = lax.broadcasted_iota(jnp.int32, (GLA_WIDTH, GLA_KEY_WIDTH), 0) // GLA_DV
    st_head_c = lax.broadcasted_iota(jnp.int32, (GLA_WIDTH, GLA_KEY_WIDTH), 1) // GLA_DK
    same_head = st_head_r == st_head_c
    pair_row = lax.broadcasted_iota(jnp.int32, (2 * CHUNK, GLA_KEY_WIDTH), 0) // CHUNK

    def stack_heads(x, head_of_lane):
        zero = jnp.zeros_like(x)
        return jnp.concatenate([jnp.where(head_of_lane == h, x, zero) for h in range(GLA_HEADS)], axis=0)

    def ph_gate_proj():
        st['z'] = _dot_bf16x3(g_ref[...], gup_ref[...]) + gb_ref[...]

    def ph_log_decay():
        st['lg'] = _log_sigmoid(st['z']) * (1.0 / GLA_TAU)

    def ph_cumsum():
        st['b'] = _dot_exact_lhs(tri_ref[...], st['lg'])

    def ph_prep():
        b = st['b']
        st['k'] = k_ref[...].astype(F32)
        st['qt'] = (q_ref[...].astype(F32) * GLA_DK ** -0.5 * jnp.exp(b)).astype(BF16)
        st['kt'] = (st['k'] * jnp.exp(-b)).astype(BF16)
        st['v_t'] = v_ref[...].astype(F32).T.astype(BF16)

    def ph_scores():
        for c in order:
            st['a', c] = lax.dot_general(st['qt'][ts_of(c)], stack_heads(st['kt'][ts_of(c)], khead), dn_nt,
                                         preferred_element_type=F32)

    def ph_mask():
        b, k = st['b'], st['k']
        for c in order:
            st['a', c] = jnp.where(tri_ok, st['a', c], 0.0).astype(BF16)
            last = last_of(c)
            st['kdec', c] = jnp.where(pair_row == c % 2, k[ts2_of(c)] * jnp.exp(b[last:last + 1] - b[ts2_of(c)]),
                                      0.0).astype(BF16)

    def ph_intra():
        for c in order:
            st['o', c] = jnp.dot(st['a', c], stack_heads(v_ref[ts_of(c), :], vhead), preferred_element_type=F32)
            st['upd', c] = jnp.dot(st['v_t'][:, ts2_of(c)], st['kdec', c], preferred_element_type=F32)

    def ph_state():
        s_t = s_scr[...]
        for c in order:
            st['s', c] = s_t.astype(BF16)
            last = last_of(c)
            s_t = jnp.exp(st['b'][last:last + 1]) * s_t + jnp.where(same_head, st['upd', c], 0.0)
        s_scr[...] = s_t

    def ph_inter():
        for c in order:
            st['oi', c] = lax.dot_general(st['qt'][ts_of(c)], st['s', c], dn_nt, preferred_element_type=F32)

    def ph_out():
        for c in order:
            o_ref[ts_of(c), :] = st['o', c] + st['oi', c]

    return [ph_gate_proj, ph_log_decay, ph_cumsum, ph_prep, ph_scores, ph_mask, ph_intra, ph_state, ph_inter, ph_out]


def _scans_call(main, gates, gate_bias, tri, expand, gup_pad, gla_b, w_cast=None):
    t = main.shape[0]
    nb = t // TB
    const = lambda i: (0, 0)
    extra_in, extra_out, extra_shape, extra_args = [], [], [], []
    if w_cast is not None:
        w_all, part, n_parts = w_cast
        rows, cols = w_all.shape[0] // n_parts, w_all.shape[1]
        blk_rows = rows // CAST_STEPS
        step = lambda i: jnp.minimum(i, CAST_STEPS - 1)
        extra_in = [pl.BlockSpec((blk_rows, cols), lambda i: (part * CAST_STEPS + step(i), 0))]
        extra_out = [pl.BlockSpec((blk_rows, cols), lambda i: (step(i), 0))]
        extra_shape, extra_args = [jax.ShapeDtypeStruct((rows, cols), BF16)], [w_all]

    def streams(direction):
        blk = lambda i: _scan_block(i, nb, direction)
        col = lambda c, w: (lambda i: (blk(i), c // w))
        return [pl.BlockSpec((TB, MLSTM_WIDTH), col(C_MQ, MLSTM_WIDTH)),
                pl.BlockSpec((TB, MLSTM_WIDTH), col(C_MK, MLSTM_WIDTH)),
                pl.BlockSpec((TB, MLSTM_WIDTH), col(C_MV, MLSTM_WIDTH)),
                pl.BlockSpec((TB, GLA_KEY_WIDTH), col(C_GQ, GLA_KEY_WIDTH)),
                pl.BlockSpec((TB, GLA_KEY_WIDTH), col(C_GK, GLA_KEY_WIDTH)),
                pl.BlockSpec((TB, GLA_WIDTH), col(C_GV, GLA_WIDTH)),
                pl.BlockSpec((TB, N_GATE), lambda i: (blk(i), 0))]

    consts = [pl.BlockSpec((1, N_GATE), const),
              pl.BlockSpec((TB, TB), const), pl.BlockSpec((TB, TB), const),
              pl.BlockSpec((N_GATE, MLSTM_WIDTH), const), pl.BlockSpec((N_GATE, MLSTM_WIDTH), const),
              pl.BlockSpec((N_GATE, GLA_KEY_WIDTH), const), pl.BlockSpec((N_GATE, GLA_KEY_WIDTH), const),
              pl.BlockSpec((1, GLA_KEY_WIDTH), const), pl.BlockSpec((1, GLA_KEY_WIDTH), const)]
    out_spec = lambda direction, w: pl.BlockSpec((TB, w), lambda i: (_scan_block(i, nb, direction), 0))
    n_pairs = MLSTM_HEADS // 2
    mlstm_state = [pltpu.VMEM((n_pairs, LANES, LANES), F32), pltpu.VMEM((n_pairs, LANES, LANES), F32),
                   pltpu.VMEM((n_pairs, SUBLANES, LANES), F32)]
    gla_state = [pltpu.VMEM((GLA_WIDTH, GLA_KEY_WIDTH), F32)]
    return pl.pallas_call(
        functools.partial(_scans_kernel, cast_weights=w_cast is not None),
        grid=(nb,),
        in_specs=streams(0) + streams(1) + consts + extra_in,
        out_specs=[out_spec(0, MLSTM_WIDTH), out_spec(1, MLSTM_WIDTH), out_spec(0, GLA_WIDTH),
                   out_spec(1, GLA_WIDTH)] + extra_out,
        out_shape=([jax.ShapeDtypeStruct((t, MLSTM_WIDTH), F32)] * 2 + [jax.ShapeDtypeStruct((t, GLA_WIDTH), F32)] * 2
                   + extra_shape),
        scratch_shapes=mlstm_state + mlstm_state + gla_state + gla_state,
        compiler_params=_cparams(1),
        name="scans",
    )(*([main] * 6 + [gates]) * 2, gate_bias, tri[0], tri[1], expand[0], expand[1],
      gup_pad[0], gup_pad[1], gla_b[0:1], gla_b[1:2], *extra_args)


def _attn_kernel(sink_ref, q_ref, kp_ref, kc_ref, kn_ref, kx_ref, vp_ref, vc_ref, vn_ref, vx_ref, o_ref, *, seq):
    i = pl.program_id(0)
    half = TB // 2
    n_loc = 2 * TB
    keys = jnp.concatenate([kp_ref[...], kc_ref[...], kn_ref[...], kx_ref[...]], axis=0)
    vals = jnp.concatenate([vp_ref[...], vc_ref[...], vn_ref[...], vx_ref[...]], axis=0)
    n_keys = keys.shape[0]
    lane = lax.broadcasted_iota(jnp.int32, (n_keys, LANES), 1)
    zero = jnp.zeros_like(keys)
    keys_g = [jnp.where(lane < HEAD_DIM, keys, zero), jnp.where(lane >= HEAD_DIM, keys, zero)]
    r = lax.broadcasted_iota(jnp.int32, (TB, n_keys), 0)
    c = lax.broadcasted_iota(jnp.int32, (TB, n_keys), 1)
    rel = c - half - r
    kpos = (i - 1) * TB + c - half
    local_ok = (jnp.abs(rel) <= WINDOW) & (kpos >= 0) & (kpos < seq) & (i > 0)
    valid = local_ok | (c >= n_loc)
    out_lane = lax.broadcasted_iota(jnp.int32, (TB, LANES), 1)
    neg_inf = jnp.float32(-jnp.inf)
    dn = (((1,), (1,)), ((), ()))
    heads = [(m, gidx) for m in range(ATTN_Q_WIDTH // LANES) for gidx in range(ATTN_KV_HEADS)]
    scores = [lax.dot_general(q_ref[:, m * LANES:(m + 1) * LANES], keys_g[gidx], dn, preferred_element_type=F32)
              for m, gidx in heads]
    probs, denoms = [], []
    for (m, gidx), s in zip(heads, scores):
        sink = sink_ref[gidx * (ATTN_HEADS // ATTN_KV_HEADS) + m]
        s = jnp.where(valid, s, neg_inf)
        mx = jnp.maximum(jnp.max(s, axis=1, keepdims=True), sink)
        p = jnp.exp(s - mx)
        denoms.append(jnp.sum(p, axis=1, keepdims=True) + jnp.exp(sink - mx))
        probs.append(p.astype(BF16))
    pvs = [jnp.dot(p, vals, preferred_element_type=F32) for p in probs]
    for m in range(ATTN_Q_WIDTH // LANES):
        r0 = pvs[2 * m] / denoms[2 * m]
        r1 = pvs[2 * m + 1] / denoms[2 * m + 1]
        o_ref[:, m * LANES:(m + 1) * LANES] = jnp.where(out_lane < HEAD_DIM, r0, r1).astype(BF16)


def _attn_call(main, sink, seq):
    t = main.shape[0]
    nb = t // TB
    half = TB // 2
    nhb = t // half
    kcol, vcol = C_AK // LANES, C_AV // LANES
    prev_i = lambda i: jnp.maximum(2 * i - 1, 0)
    next_i = lambda i: jnp.minimum(2 * i + 2, nhb - 1)
    grid_spec = pltpu.PrefetchScalarGridSpec(
        num_scalar_prefetch=1,
        grid=(nb,),
        in_specs=[pl.BlockSpec((TB, ATTN_Q_WIDTH), lambda i, s: (i, C_AQ // ATTN_Q_WIDTH)),
                  pl.BlockSpec((half, LANES), lambda i, s: (prev_i(i), kcol)),
                  pl.BlockSpec((TB, LANES), lambda i, s: (i, kcol)),
                  pl.BlockSpec((half, LANES), lambda i, s: (next_i(i), kcol)),
                  pl.BlockSpec((TB, LANES), lambda i, s: (0, kcol)),
                  pl.BlockSpec((half, LANES), lambda i, s: (prev_i(i), vcol)),
                  pl.BlockSpec((TB, LANES), lambda i, s: (i, vcol)),
                  pl.BlockSpec((half, LANES), lambda i, s: (next_i(i), vcol)),
                  pl.BlockSpec((TB, LANES), lambda i, s: (0, vcol))],
        out_specs=pl.BlockSpec((TB, ATTN_Q_WIDTH), lambda i, s: (i, 0)),
    )
    return pl.pallas_call(
        functools.partial(_attn_kernel, seq=seq),
        grid_spec=grid_spec,
        out_shape=jax.ShapeDtypeStruct((t, ATTN_Q_WIDTH), BF16),
        compiler_params=_cparams(1),
        name="window_attn",
    )(sink, main, main, main, main, main, main, main, main, main)


def _group_mean(x, avg_bf16):
    hi = x.astype(BF16)
    lo = (x - hi.astype(F32)).astype(BF16)
    return (jnp.dot(hi, avg_bf16, preferred_element_type=F32) + jnp.dot(lo, avg_bf16, preferred_element_type=F32))


def _head_norm(x, avg_bf16, w):
    cen = x - _group_mean(x, avg_bf16)
    var = _group_mean(cen * cen, avg_bf16)
    return cen * lax.rsqrt(var + LN_EPS) * w


def _outproj_kernel(mf_ref, mb_ref, mo_ref, at_ref, gf_ref, gbk_ref, go_ref, mnw_ref, gnw_ref, avg_ref,
                    w_ref, x_ref, mod_ref, lnw_ref, lnb_ref, x1_ref, h2_ref, wbf_scr):
    i = pl.program_id(0)
    tm = x_ref.shape[0]

    @pl.when(i == 0)
    def _():
        for dst, src, n in _column_runs(_out_row_perm()):
            wbf_scr[dst:dst + n, :] = w_ref[src:src + n, :].astype(BF16)

    avg = avg_ref[...]
    hm = _head_norm(mf_ref[...] + mb_ref[...], avg, mnw_ref[...]) * _sigmoid(mo_ref[...].astype(F32))
    gate = go_ref[...].astype(F32)
    hg = _head_norm(gf_ref[...] + gbk_ref[...], avg, gnw_ref[...]) * (gate * _sigmoid(gate))
    a0, a1, a2 = MLSTM_WIDTH, MLSTM_WIDTH + ATTN_Q_WIDTH, D_MODEL
    mix = (jnp.dot(hm.astype(BF16), wbf_scr[:a0], preferred_element_type=F32)
           + jnp.dot(at_ref[...], wbf_scr[a0:a1], preferred_element_type=F32)
           + jnp.dot(hg.astype(BF16), wbf_scr[a1:a2], preferred_element_type=F32))
    x1 = _layer_norm(DEEPNORM_ALPHA * x_ref[...] + _select_mod(mod_ref, 2, i, tm) * mix, lnw_ref[...], lnb_ref[...])
    x1_ref[...] = x1
    h2_ref[...] = (x1 * (1.0 + _select_mod(mod_ref, 4, i, tm)) + _select_mod(mod_ref, 3, i, tm)).astype(h2_ref.dtype)


def _outproj_call(mf, mb, main, attn, gf, gbk, mnw, gnw, avg, w_out, layer, x, mod, lnw, lnb, h2_dtype):
    t, d = x.shape
    tm = PROJ_TM
    row = lambda i: (i, 0)
    const = lambda i: (0, 0)
    return pl.pallas_call(
        _outproj_kernel,
        grid=(t // tm,),
        in_specs=[pl.BlockSpec((tm, MLSTM_WIDTH), row),
                  pl.BlockSpec((tm, MLSTM_WIDTH), row),
                  pl.BlockSpec((tm, MLSTM_WIDTH), lambda i: (i, C_MO // MLSTM_WIDTH)),
                  pl.BlockSpec((tm, ATTN_Q_WIDTH), row),
                  pl.BlockSpec((tm, GLA_WIDTH), row),
                  pl.BlockSpec((tm, GLA_WIDTH), row),
                  pl.BlockSpec((tm, GLA_WIDTH), lambda i: (i, C_GO // GLA_WIDTH)),
                  pl.BlockSpec((1, MLSTM_WIDTH), const),
                  pl.BlockSpec((1, GLA_WIDTH), const),
                  pl.BlockSpec((MLSTM_WIDTH, MLSTM_WIDTH), const),
                  pl.BlockSpec((None, d, d), lambda i: (layer, 0, 0), pipeline_mode=pl.Buffered(1)),
                  pl.BlockSpec((tm, d), row),
                  pl.BlockSpec((2, SUBLANES, d), lambda i: (0, 0, 0)),
                  pl.BlockSpec((1, d), const),
                  pl.BlockSpec((1, d), const)],
        out_specs=[pl.BlockSpec((tm, d), row), pl.BlockSpec((tm, d), row)],
        out_shape=[jax.ShapeDtypeStruct((t, d), F32), jax.ShapeDtypeStruct((t, d), h2_dtype)],
        scratch_shapes=[pltpu.VMEM((d, d), BF16)],
        compiler_params=_cparams(1),
        name="out_proj",
    )(mf, mb, main, attn, gf, gbk, main, mnw, gnw, avg, w_out, x, mod, lnw, lnb)


def _select_mod(mod_ref, k, i, tm):
    rows = i * tm + lax.broadcasted_iota(jnp.int32, (tm, 1), 0)
    return jnp.where(rows < CTX_LEN, mod_ref[0, k:k + 1, :], mod_ref[1, k:k + 1, :])


def _ffn_kernel(h_ref, wgu_ref, wd_ref, x_ref, mod_ref, lnw_ref, lnb_ref, o_ref):
    h = h_ref[...]
    g = jnp.dot(h, wgu_ref[:, :D_FF], preferred_element_type=F32)
    u = jnp.dot(h, wgu_ref[:, D_FF:], preferred_element_type=F32)
    y = jnp.dot((g * _sigmoid(g) * u).astype(BF16), wd_ref[...], preferred_element_type=F32)
    m5 = _select_mod(mod_ref, 5, pl.program_id(0), h_ref.shape[0])
    o_ref[...] = _layer_norm(DEEPNORM_ALPHA * x_ref[...] + m5 * y, lnw_ref[...], lnb_ref[...])


def _ffn_call(h2, w_gu, w_down, layer, x1, mod, lnw, lnb):
    t, d = x1.shape
    resident = pl.Buffered(1)
    return pl.pallas_call(
        _ffn_kernel,
        grid=(t // FFN_TM,),
        in_specs=[pl.BlockSpec((FFN_TM, d), lambda i: (i, 0)),
                  pl.BlockSpec((None, d, 2 * D_FF), lambda i: (layer, 0, 0), pipeline_mode=resident),
                  pl.BlockSpec((None, D_FF, d), lambda i: (layer, 0, 0), pipeline_mode=resident),
                  pl.BlockSpec((FFN_TM, d), lambda i: (i, 0)),
                  pl.BlockSpec((2, SUBLANES, d), lambda i: (0, 0, 0)),
                  pl.BlockSpec((1, d), lambda i: (0, 0)),
                  pl.BlockSpec((1, d), lambda i: (0, 0))],
        out_specs=pl.BlockSpec((FFN_TM, d), lambda i: (i, 0)),
        out_shape=jax.ShapeDtypeStruct((t, d), F32),
        compiler_params=_cparams(1),
        name="dense_ffn",
    )(h2, w_gu, w_down, x1, mod, lnw, lnb)


def _router_kernel(h_ref, wr_ref, br_ref, su_ref, route_ref, cnt_ref, base_scr):
    @pl.when(pl.program_id(0) == 0)
    def _():
        base_scr[...] = jnp.zeros_like(base_scr)

    lt = _dot_bf16x3(wr_ref[...], h_ref[...], (((1,), (1,)), ((), ()))) + br_ref[...]
    idx = lax.broadcasted_iota(jnp.int32, lt.shape, 0)
    neg_inf = jnp.float32(-jnp.inf)
    m1 = jnp.max(lt, axis=0, keepdims=True)
    e1 = jnp.min(jnp.where(lt == m1, idx, N_EXPERTS), axis=0, keepdims=True)
    lt2 = jnp.where(idx == e1, neg_inf, lt)
    m2 = jnp.max(lt2, axis=0, keepdims=True)
    e2 = jnp.min(jnp.where(lt2 == m2, idx, N_EXPERTS), axis=0, keepdims=True)
    t2 = jnp.exp(m2 - m1)
    w1 = 1.0 / (1.0 + t2)
    w2 = t2 / (1.0 + t2)
    su = su_ref[...]
    base = base_scr[...][:, 0:1]
    oh1 = (idx == e1).astype(F32)
    cum1 = jnp.dot(oh1.astype(BF16), su, preferred_element_type=F32)
    rank1 = jnp.sum(oh1 * (base + cum1), axis=0, keepdims=True)
    base = base + jnp.sum(oh1, axis=1, keepdims=True)
    oh2 = (idx == e2).astype(F32)
    cum2 = jnp.dot(oh2.astype(BF16), su, preferred_element_type=F32)
    rank2 = jnp.sum(oh2 * (base + cum2), axis=0, keepdims=True)
    base = base + jnp.sum(oh2, axis=1, keepdims=True)
    base_scr[...] = jnp.broadcast_to(base, base_scr.shape)
    cnt_ref[...] = jnp.broadcast_to(base, cnt_ref.shape)
    zero = jnp.zeros_like(w1)
    route_ref[...] = jnp.concatenate(
        [e1.astype(F32), e2.astype(F32), w1, w2, rank1, rank2, zero, zero], axis=0)


def _router_call(h2, wr_t, br, su):
    t, d = h2.shape
    nb = t // TB
    return pl.pallas_call(
        _router_kernel,
        grid=(nb,),
        in_specs=[pl.BlockSpec((TB, d), lambda i: (i, 0)),
                  pl.BlockSpec((N_EXPERTS, d), lambda i: (0, 0)),
                  pl.BlockSpec((N_EXPERTS, 1), lambda i: (0, 0)),
                  pl.BlockSpec((TB, TB), lambda i: (0, 0))],
        out_specs=[pl.BlockSpec((SUBLANES, TB), lambda i: (0, i)),
                   pl.BlockSpec((N_EXPERTS, LANES), lambda i: (0, 0))],
        out_shape=[jax.ShapeDtypeStruct((SUBLANES, t), F32),
                   jax.ShapeDtypeStruct((N_EXPERTS, LANES), F32)],
        scratch_shapes=[pltpu.VMEM((N_EXPERTS, LANES), F32)],
        compiler_params=_cparams(1),
        name="moe_router",
    )(h2, wr_t, br, su)


ROW_TILE = D_MODEL // LANES
DMA_UNROLL = 8
ZERO_CHUNK = 64
CAST_STEPS = 64
CAST_PIECES = 4


def _store_row_tiles(ref, val):
    n = val.shape[0]
    for j in range(ROW_TILE):
        ref[pl.ds(j, n, stride=ROW_TILE), :] = val[:, j * LANES:(j + 1) * LANES]


def _load_row_tiles(ref, n, dtype=F32):
    return jnp.concatenate([ref[pl.ds(j, n, stride=ROW_TILE), :].astype(dtype) for j in range(ROW_TILE)], axis=1)


def _row(ref, idx):
    return ref.at[pl.ds(pl.multiple_of(idx * ROW_TILE, ROW_TILE), ROW_TILE)]


def _dispatch_kernel(s1_ref, s2_ref, pad_lo_ref, pad_hi_ref, h_ref, xs_ref, hr_scr, zero_scr, sem, zsem):
    i = pl.program_id(0)

    @pl.when(i == 0)
    def _():
        zero_scr[...] = jnp.zeros_like(zero_scr)

        def zero_rows(s, n):
            dst = xs_ref.at[pl.ds(pl.multiple_of(s * ROW_TILE, ROW_TILE), n * ROW_TILE)]
            return pltpu.make_async_copy(zero_scr.at[pl.ds(0, n * ROW_TILE)], dst, zsem)

        def fill(e, act):
            lo, hi = pad_lo_ref[e], pad_hi_ref[e]
            n_big = (hi - lo) // ZERO_CHUNK
            lax.fori_loop(0, n_big, lambda k, c: (act(zero_rows(lo + k * ZERO_CHUNK, ZERO_CHUNK)), c)[1], 0)
            lax.fori_loop(lo + n_big * ZERO_CHUNK, hi, lambda s, c: (act(zero_rows(s, 1)), c)[1], 0)

        for e in range(N_EXPERTS + 1):
            fill(e, lambda cp: cp.start())
        for e in range(N_EXPERTS + 1):
            fill(e, lambda cp: cp.wait())

    n = pl.num_programs(0)
    cur = i % 2

    def copy(blk, r, slot_ref):
        buf = blk % 2
        return pltpu.make_async_copy(_row(hr_scr.at[buf], r), _row(xs_ref, slot_ref[blk * TB + r]), sem.at[buf])

    def start(r, carry):
        copy(i, r, s1_ref).start(priority=0)
        copy(i, r, s2_ref).start(priority=1)
        return carry

    def wait_block(blk):
        def wait(r, carry):
            copy(blk, r, s1_ref).wait()
            copy(blk, r, s2_ref).wait()
            return carry
        lax.fori_loop(0, TB, wait, 0, unroll=DMA_UNROLL)

    _store_row_tiles(hr_scr.at[cur], h_ref[...])
    lax.fori_loop(0, TB, start, 0, unroll=DMA_UNROLL)

    @pl.when(i > 0)
    def _():
        wait_block(i - 1)

    @pl.when(i == n - 1)
    def _():
        wait_block(i)


def _dispatch_call(slot1, slot2, pad_lo, pad_hi, h2, n_slots):
    t, d = h2.shape
    grid_spec = pltpu.PrefetchScalarGridSpec(
        num_scalar_prefetch=4,
        grid=(t // TB,),
        in_specs=[pl.BlockSpec((TB, d), lambda i, *_: (i, 0))],
        out_specs=pl.BlockSpec(memory_space=pl.ANY),
        scratch_shapes=[pltpu.VMEM((2, TB * ROW_TILE, LANES), F32), pltpu.VMEM((ZERO_CHUNK * ROW_TILE, LANES), F32),
                        pltpu.SemaphoreType.DMA((2,)), pltpu.SemaphoreType.DMA(())],
    )
    return pl.pallas_call(
        _dispatch_kernel,
        grid_spec=grid_spec,
        out_shape=jax.ShapeDtypeStruct((n_slots * ROW_TILE, LANES), F32),
        compiler_params=_cparams(1),
        name="moe_dispatch",
    )(slot1, slot2, pad_lo, pad_hi, h2)


def _expert_kernel(be_ref, used_ref, x_ref, wg_ref, wu_ref, wd_ref, y_ref, xb_scr, acc_scr, *, n_f):
    b, f = pl.program_id(0), pl.program_id(1)
    live = b < used_ref[0]
    last = n_f - 1

    def swiglu_part():
        x = xb_scr[...]
        g = jnp.dot(x, wg_ref[...], preferred_element_type=F32)
        u = jnp.dot(x, wu_ref[...], preferred_element_type=F32)
        return jnp.dot((g * _sigmoid(g) * u).astype(BF16), wd_ref[...], preferred_element_type=F32)

    @pl.when(live & (f == 0))
    def _():
        for j in range(ROW_TILE):
            xb_scr[:, j * LANES:(j + 1) * LANES] = x_ref[pl.ds(j, MOE_G, stride=ROW_TILE), :].astype(BF16)
        acc_scr[...] = swiglu_part()

    if n_f > 2:
        @pl.when(live & (f > 0) & (f < last))
        def _():
            acc_scr[...] += swiglu_part()

    @pl.when(live & (f == last))
    def _():
        _store_row_tiles(y_ref, acc_scr[...] + swiglu_part())

    @pl.when(jnp.logical_not(live) & (f == last))
    def _():
        y_ref[...] = jnp.zeros_like(y_ref)


def _expert_call(block_e, used, xs, w_gu, w_down):
    d = D_MODEL
    ns = xs.shape[0] // ROW_TILE
    nblk = ns // MOE_G
    nf = D_EXPERT // MOE_TF
    rows = MOE_G * ROW_TILE

    def f_eff(b, f, used_ref):
        return jnp.where(b < used_ref[0], f, nf - 1)

    grid_spec = pltpu.PrefetchScalarGridSpec(
        num_scalar_prefetch=2,
        grid=(nblk, nf),
        in_specs=[pl.BlockSpec((rows, LANES), lambda b, f, be, us: (b, 0)),
                  pl.BlockSpec((None, d, MOE_TF), lambda b, f, be, us: (be[b], 0, f_eff(b, f, us))),
                  pl.BlockSpec((None, d, MOE_TF), lambda b, f, be, us: (be[b], 0, nf + f_eff(b, f, us))),
                  pl.BlockSpec((None, MOE_TF, d), lambda b, f, be, us: (be[b], f_eff(b, f, us), 0))],
        out_specs=pl.BlockSpec((rows, LANES), lambda b, f, be, us: (b, 0)),
        scratch_shapes=[pltpu.VMEM((MOE_G, d), BF16), pltpu.VMEM((MOE_G, d), F32)],
    )
    return pl.pallas_call(
        functools.partial(_expert_kernel, n_f=nf),
        grid_spec=grid_spec,
        out_shape=jax.ShapeDtypeStruct(xs.shape, F32),
        compiler_params=_cparams(2),
        name="moe_experts",
    )(block_e, used, xs, w_gu, w_gu, w_down)


def _combine_kernel(s1_ref, s2_ref, yb_ref, route_ref, x_ref, mod_ref, lnw_ref, lnb_ref, o_ref, buf1, buf2, sem):
    i = pl.program_id(0)
    n = pl.num_programs(0)
    cur = i % 2

    def copy(blk, r, slot_ref, buf):
        b = blk % 2
        return pltpu.make_async_copy(_row(yb_ref, slot_ref[blk * TB + r]), _row(buf.at[b], r), sem.at[b])

    def start_block(blk):
        def start(r, carry):
            copy(blk, r, s1_ref, buf1).start(priority=0)
            copy(blk, r, s2_ref, buf2).start(priority=1)
            return carry
        lax.fori_loop(0, TB, start, 0, unroll=DMA_UNROLL)

    def wait(r, carry):
        copy(i, r, s1_ref, buf1).wait()
        copy(i, r, s2_ref, buf2).wait()
        return carry

    @pl.when(i == 0)
    def _():
        start_block(i)

    @pl.when(i + 1 < n)
    def _():
        start_block(i + 1)

    lax.fori_loop(0, TB, wait, 0, unroll=DMA_UNROLL)
    rt = route_ref[...].T
    y = rt[:, 2:3] * _load_row_tiles(buf1.at[cur], TB) + rt[:, 3:4] * _load_row_tiles(buf2.at[cur], TB)
    m = mod_ref[...]
    o_ref[...] = _layer_norm(DEEPNORM_ALPHA * x_ref[...] + m[5:6] * y, lnw_ref[...], lnb_ref[...])


def _combine_call(slot1, slot2, yb, route, x1, mod, lnw, lnb, latent_only):
    t, d = x1.shape
    if latent_only:
        out_rows, out_map = t - CTX_LEN, lambda i, a, b: (jnp.maximum(i - 1, 0), 0)
    else:
        out_rows, out_map = t, lambda i, a, b: (i, 0)
    grid_spec = pltpu.PrefetchScalarGridSpec(
        num_scalar_prefetch=2,
        grid=(t // TB,),
        in_specs=[pl.BlockSpec(memory_space=pl.ANY),
                  pl.BlockSpec((SUBLANES, TB), lambda i, a, b: (0, i)),
                  pl.BlockSpec((TB, d), lambda i, a, b: (i, 0)),
                  pl.BlockSpec((None, SUBLANES, d), lambda i, a, b: (_who(i), 0, 0)),
                  pl.BlockSpec((1, d), lambda i, a, b: (0, 0)),
                  pl.BlockSpec((1, d), lambda i, a, b: (0, 0))],
        out_specs=pl.BlockSpec((TB, d), out_map),
        scratch_shapes=[pltpu.VMEM((2, TB * ROW_TILE, LANES), F32), pltpu.VMEM((2, TB * ROW_TILE, LANES), F32),
                        pltpu.SemaphoreType.DMA((2,))],
    )
    return pl.pallas_call(
        _combine_kernel,
        grid_spec=grid_spec,
        out_shape=jax.ShapeDtypeStruct((out_rows, d), F32),
        compiler_params=_cparams(1),
        name="moe_combine",
    )(slot1, slot2, yb, route, x1, mod, lnw, lnb)


def _moe_layer(h2, x1, mod, lnw, lnb, w_router, b_router, w_gu, w_down, su, latent_only):
    t, d = h2.shape
    route, cnt = _router_call(h2, w_router.T, b_router.reshape(N_EXPERTS, 1), su)
    counts = cnt[:, 0].astype(jnp.int32)
    padded = (counts + MOE_G - 1) // MOE_G * MOE_G
    pend = jnp.cumsum(padded)
    pstart = pend - padded
    e1, e2 = route[0].astype(jnp.int32), route[1].astype(jnp.int32)
    slot1 = pstart[e1] + route[4].astype(jnp.int32)
    slot2 = pstart[e2] + route[5].astype(jnp.int32)
    nblk = -(-(2 * t) // MOE_G) + N_EXPERTS
    blk_start = jnp.arange(nblk, dtype=jnp.int32) * MOE_G
    block_e = jnp.minimum(jnp.sum((pend[None, :] <= blk_start[:, None]).astype(jnp.int32), axis=1), N_EXPERTS - 1)
    used = (pend[-1:] // MOE_G).astype(jnp.int32)
    n_slots = nblk * MOE_G
    pad_lo = jnp.concatenate([pstart + counts, pend[-1:]])
    pad_hi = jnp.concatenate([pend, jnp.full((1,), n_slots, jnp.int32)])
    xs = _dispatch_call(slot1, slot2, pad_lo, pad_hi, h2, n_slots)
    yb = _expert_call(block_e, used, xs, w_gu, w_down)
    return _combine_call(slot1, slot2, yb, route, x1, mod, lnw, lnb, latent_only)


def _attn_head_perm():
    cols = []
    for m in range(ATTN_HEADS // ATTN_KV_HEADS):
        for hq in (m, m + ATTN_HEADS // ATTN_KV_HEADS):
            cols.extend(range(hq * HEAD_DIM, (hq + 1) * HEAD_DIM))
    return np.asarray(cols, np.int32)


def _in_col_perm():
    n_m = 4 * MLSTM_WIDTH
    mg = 2 * 2 * MLSTM_HEADS
    a0 = n_m + mg
    perm = list(range(n_m))
    perm += [a0 + int(j) for j in _attn_head_perm()]
    perm += list(range(a0 + ATTN_Q_WIDTH, a0 + ATTN_Q_WIDTH + 2 * ATTN_KV_WIDTH))
    g0 = a0 + ATTN_Q_WIDTH + 2 * ATTN_KV_WIDTH
    perm += list(range(g0, g0 + 2 * GLA_KEY_WIDTH + 2 * GLA_WIDTH))
    perm += list(range(n_m, n_m + mg))
    perm += list(range(g0 + 2 * GLA_KEY_WIDTH + 2 * GLA_WIDTH, g0 + 2 * GLA_KEY_WIDTH + 2 * GLA_WIDTH + 2 * GLA_RANK))
    return np.asarray(perm, np.int32)


def _out_row_perm():
    return np.concatenate([np.arange(MLSTM_WIDTH), MLSTM_WIDTH + _attn_head_perm(),
                           np.arange(MLSTM_WIDTH + ATTN_Q_WIDTH, D_MODEL)]).astype(np.int32)


def _column_runs(perm):
    runs, start = [], 0
    for j in range(1, len(perm) + 1):
        if j == len(perm) or perm[j] != perm[j - 1] + 1:
            runs.append((start, int(perm[start]), j - start))
            start = j
    return runs


def _rope_tables(seq):
    inv = ROPE_BASE ** (-jnp.arange(ROPE_PAIRS, dtype=F32) / ROPE_PAIRS)
    rows = seq // GRID_W
    ang_r = jnp.arange(rows).astype(F32)[:, None] * inv
    ang_c = jnp.arange(GRID_W).astype(F32)[:, None] * inv
    lane_pat = lambda a, b: jnp.tile(jnp.concatenate([a, b], -1), (1, LANES // (2 * ROPE_PAIRS)))
    cos_rows, sin_rows = lane_pat(jnp.cos(ang_r), jnp.cos(ang_r)), lane_pat(-jnp.sin(ang_r), jnp.sin(ang_r))
    cos_cols, sin_cols = lane_pat(jnp.cos(ang_c), jnp.cos(ang_c)), lane_pat(-jnp.sin(ang_c), jnp.sin(ang_c))
    row_part = (np.arange(LANES) % HEAD_DIM) < HEAD_DIM // 2
    per_token = lambda by_row, by_col: jnp.where(
        row_part, jnp.broadcast_to(by_row[:, None, :], (rows, GRID_W, LANES)),
        jnp.broadcast_to(by_col[None, :, :], (rows, GRID_W, LANES))).reshape(seq, LANES)
    cos_l, sin_l = per_token(cos_rows, cos_cols), per_token(sin_rows, sin_cols)
    cos_t = jnp.concatenate([jnp.ones((CTX_LEN, LANES), F32), cos_l], 0)
    sin_t = jnp.concatenate([jnp.zeros((CTX_LEN, LANES), F32), sin_l], 0)
    return cos_t, sin_t


def _block_tri(direction):
    r = np.arange(TB)[:, None]
    c = np.arange(TB)[None, :]
    same = (r // CHUNK) == (c // CHUNK)
    tri = (c <= r) if direction == 0 else (c >= r)
    return jnp.asarray(same & tri, BF16)


def kernel(x, c, ctx, c_ctx, w_ada, b_ada, w_in, mlstm_gate_b, mlstm_norm_w, attn_sink, gla_gate_up, gla_gate_b,
           gla_norm_w, w_out, ln_w, ln_b, ffn_w_gu, ffn_w_down, router_w, router_b, moe_w_gu, moe_w_down):
    seq, d = x.shape[1], x.shape[2]
    depth = w_in.shape[0]
    t = ctx.shape[1] + seq
    assert x.shape[0] == 1 and d == D_MODEL and ctx.shape[1] == CTX_LEN == TB and seq % TB == 0
    assert t % PROJ_TM == 0 and t % FFN_TM == 0
    xt = jnp.concatenate([ctx[0], x[0]], axis=0)

    cvec_t = jnp.zeros((d, SUBLANES), F32).at[:, 0].set(c_ctx).at[:, 1].set(c[0])
    mods = _mod_call(cvec_t, w_ada, b_ada)[:, :2].reshape(depth, 2, 6, d)
    mods = jnp.pad(mods, ((0, 0), (0, 0), (0, SUBLANES - 6), (0, 0)))

    gate_bias = jnp.pad(mlstm_gate_b.reshape(depth, 1, -1), ((0, 0), (0, 0), (0, N_GATE - 4 * MLSTM_HEADS)))
    gup_pad = jnp.zeros((depth, 2, N_GATE, GLA_KEY_WIDTH), F32)
    for dr in range(2):
        lo = GLA_GATE_OFF + dr * GLA_RANK
        gup_pad = gup_pad.at[:, dr, lo:lo + GLA_RANK, :].set(gla_gate_up[:, dr])
    cos_t, sin_t = _rope_tables(seq)
    tri = [_block_tri(0), _block_tri(1)]
    su = jnp.asarray(np.arange(TB)[:, None] < np.arange(TB)[None, :], BF16)
    hh = np.arange(MLSTM_WIDTH) // MLSTM_DH
    avg = jnp.asarray((hh[:, None] == hh[None, :]) / MLSTM_DH, BF16)
    ffn_gu, ffn_dn = ffn_w_gu.astype(BF16), ffn_w_down.astype(BF16)
    n_moe, n_exp, _, n_gu = moe_w_gu.shape
    gu_2d, dn_2d = moe_w_gu.reshape(-1, n_gu), moe_w_down.reshape(-1, d)
    cast_src = {2 * m: (gu_2d, m, n_moe) for m in range(n_moe)}
    cast_src.update({2 * m + 1: (dn_2d, m, n_moe) for m in range(n_moe)})
    cast_dst = {}

    expand = []
    for dr in range(2):
        e = np.zeros((N_GATE, MLSTM_WIDTH), np.float32)
        for h in range(MLSTM_HEADS):
            e[dr * 2 * MLSTM_HEADS + h, h * MLSTM_DH:(h + 1) * MLSTM_DH] = 1.0
        expand.append(jnp.asarray(e, BF16))

    for l in range(depth):
        is_moe = l % 2 == 1
        last = l == depth - 1
        main, gates = _inproj_call(xt, mods[l], w_in, l, cos_t, sin_t)
        mf, mb, gf, gbk, *cast = _scans_call(main, gates, gate_bias[l], tri, expand, gup_pad[l], gla_gate_b[l],
                                             cast_src.get(l))
        if cast:
            cast_dst[l] = cast[0]
        attn = _attn_call(main, attn_sink[l], seq)
        x1, h2 = _outproj_call(mf, mb, main, attn, gf, gbk, mlstm_norm_w[l:l + 1], gla_norm_w[l:l + 1], avg,
                               w_out, l, xt, mods[l], ln_w[l, 0:1], ln_b[l, 0:1], F32 if is_moe else BF16)
        if is_moe:
            moe_gu = cast_dst[l - 1].reshape(n_exp, d, n_gu)
            moe_dn = cast_dst[l].reshape(n_exp, n_gu // 2, d)
            xt = _moe_layer(h2, x1, mods[l], ln_w[l, 1:2], ln_b[l, 1:2], router_w[l // 2], router_b[l // 2],
                            moe_gu, moe_dn, su, latent_only=last)
        else:
            xt = _ffn_call(h2, ffn_gu, ffn_dn, l // 2, x1, mods[l], ln_w[l, 1:2], ln_b[l, 1:2])
    return (xt if depth % 2 == 0 else xt[CTX_LEN:])[None]
```
